```python
import jax, jax.numpy as jnp
from jax import lax
import numpy as np

D_MODEL = 1024
BATCH = 8
SEQ = 2048
DEPTH = 1
DEC_BATCH = 128
DEC_SEQ = 4
PAST_LEN = 16384
PAGE_SIZE = 128

HEAD_DIM = 64
N_Q_HEADS = 8
N_KV_HEADS = 2
GQA_GROUP = N_Q_HEADS // N_KV_HEADS
ATTN_WIDTH = N_Q_HEADS * HEAD_DIM
KV_WIDTH = N_KV_HEADS * HEAD_DIM
CONV_WIDTH = D_MODEL - ATTN_WIDTH
CONV_K = 3
WINDOW = 128
BLOCK = 128
ROPE_THETA = 10000.0
D_FF = 4 * D_MODEL
PLE_DIM = 256
EPS = 1e-6
NEG = -1e30
IN_WIDTH = ATTN_WIDTH + 2 * KV_WIDTH + 3 * CONV_WIDTH
SPLITS = [ATTN_WIDTH, ATTN_WIDTH + KV_WIDTH, ATTN_WIDTH + 2 * KV_WIDTH,
          ATTN_WIDTH + 2 * KV_WIDTH + CONV_WIDTH, ATTN_WIDTH + 2 * KV_WIDTH + 2 * CONV_WIDTH]

kernel_name = 'hybrid_swa_sink_shortconv_step'


def rmsnorm(x, g):
    xf = x.astype(jnp.float32)
    y = xf * lax.rsqrt(jnp.mean(xf * xf, axis=-1, keepdims=True) + EPS)
    return (y * g.astype(jnp.float32)).astype(x.dtype)


def rope(x, pos):
    inv_freq = ROPE_THETA ** (-jnp.arange(0, HEAD_DIM, 2, dtype=jnp.float32) / HEAD_DIM)
    ang = pos.astype(jnp.float32)[:, None] * inv_freq[None, :]
    cos = jnp.cos(ang)[:, None, :]
    sin = jnp.sin(ang)[:, None, :]
    xf = x.astype(jnp.float32)
    x1, x2 = xf[..., : HEAD_DIM // 2], xf[..., HEAD_DIM // 2:]
    return jnp.concatenate([x1 * cos - x2 * sin, x2 * cos + x1 * sin], axis=-1).astype(x.dtype)


def sink_attention(q, k, v, mask, sinks):
    s = jnp.einsum('...qhgd,...khd->...hgqk', q, k).astype(jnp.float32) * (HEAD_DIM ** -0.5)
    s = jnp.where(mask, s, NEG)
    sink = sinks.astype(jnp.float32).reshape(N_KV_HEADS, GQA_GROUP)[:, :, None, None]
    m = jnp.maximum(jnp.max(s, axis=-1, keepdims=True), sink)
    e = jnp.exp(s - m)
    probs = e / (jnp.sum(e, axis=-1, keepdims=True) + jnp.exp(sink - m))
    return jnp.einsum('...hgqk,...khd->...qhgd', probs.astype(v.dtype), v)


def banded_window_attention(q, k, v, sinks):
    bn, s_len = q.shape[0], q.shape[1]
    nb = s_len // BLOCK
    qb = q.reshape(bn, nb, BLOCK, N_KV_HEADS, GQA_GROUP, HEAD_DIM)

    def with_prev(t):
        tb = t.reshape(bn, nb, BLOCK, N_KV_HEADS, HEAD_DIM)
        prev = jnp.concatenate([jnp.zeros_like(tb[:, :1]), tb[:, :-1]], axis=1)
        return jnp.concatenate([prev, tb], axis=2)

    kb, vb = with_prev(k), with_prev(v)
    qi = jnp.arange(BLOCK)[:, None]
    kj = jnp.arange(2 * BLOCK)[None, :]
    diff = qi + BLOCK - kj
    band = (diff >= 0) & (diff < WINDOW)
    real = (jnp.arange(nb)[:, None, None] > 0) | (kj >= BLOCK)[None]
    mask = (band[None] & real)[:, None, None]
    o = sink_attention(qb, kb, vb, mask, sinks)
    return o.reshape(bn, s_len, ATTN_WIDTH)


def cached_window_attention(q, k, v, past_k, past_v, sinks):
    bn, t_len = q.shape[0], q.shape[1]
    w = past_k.shape[1]
    kk = jnp.concatenate([past_k, k], axis=1)
    vv = jnp.concatenate([past_v, v], axis=1)
    qg = q.reshape(bn, t_len, N_KV_HEADS, GQA_GROUP, HEAD_DIM)
    diff = jnp.arange(t_len)[:, None] + w - jnp.arange(w + t_len)[None, :]
    mask = (diff >= 0) & (diff < WINDOW)
    o = sink_attention(qg, kk, vv, mask, sinks)
    return o.reshape(bn, t_len, ATTN_WIDTH), kk[:, t_len:], vv[:, t_len:]


def decoder_layer(x, p, pos, past_k, past_v, past_conv, g_mix_norm, w_in, g_q, g_k, sinks, conv_w,
                  g_attn_out, g_conv_out, w_o, g_mlp_norm, w_up, w_down, g_ple_norm, w_ple_gate, w_ple):
    bn, s_len, _ = x.shape
    h = rmsnorm(x, g_mix_norm)
    z = h @ w_in
    q, k, v, b_gate, c_gate, hc = jnp.split(z, SPLITS, axis=-1)
    q = rope(rmsnorm(q.reshape(bn, s_len, N_Q_HEADS, HEAD_DIM), g_q), pos)
    k = rope(rmsnorm(k.reshape(bn, s_len, N_KV_HEADS, HEAD_DIM), g_k), pos)
    v = v.reshape(bn, s_len, N_KV_HEADS, HEAD_DIM)
    u = c_gate * hc
    if past_k is None:
        o_attn = banded_window_attention(q, k, v, sinks)
        wbuf = min(WINDOW, s_len)
        new_k, new_v = k[:, s_len - wbuf:], v[:, s_len - wbuf:]
        u_full = jnp.pad(u, ((0, 0), (CONV_K - 1, 0), (0, 0)))
    else:
        o_attn, new_k, new_v = cached_window_attention(q, k, v, past_k, past_v, sinks)
        u_full = jnp.concatenate([past_conv.astype(u.dtype), u], axis=1)
    new_conv = u_full[:, u_full.shape[1] - (CONV_K - 1):]
    conv = conv_w[0] * u_full[:, 0:s_len]
    for j in range(1, CONV_K):
        conv = conv + conv_w[j] * u_full[:, j:j + s_len]
    o_conv = b_gate * conv
    mixed = jnp.concatenate([rmsnorm(o_attn, g_attn_out), rmsnorm(o_conv, g_conv_out)], axis=-1)
    x = x + mixed @ w_o
    hm = rmsnorm(x, g_mlp_norm)
    x = x + jnp.square(jax.nn.relu(hm @ w_up)) @ w_down
    gate = jax.nn.sigmoid(rmsnorm(x, g_ple_norm) @ w_ple_gate)
    x = x + gate * (p @ w_ple)
    return x, new_k, new_v, new_conv


def setup_inputs(seed: int = 0) -> dict:
    key = jax.random.key(seed)
    ks = jax.random.split(key, 24)
    f32 = jnp.float32

    def nrm(k, shape, scale=1.0):
        return jax.random.normal(k, shape, f32) * scale

    def gain(k, shape):
        return 1.0 + 0.05 * jax.random.normal(k, shape, f32)

    wbuf = min(WINDOW, PAST_LEN)
    return {
        'x_prompt': nrm(ks[0], (BATCH, SEQ, D_MODEL)),
        'x_sample': nrm(ks[1], (DEC_BATCH, DEC_SEQ, D_MODEL)),
        'p_prompt': nrm(ks[2], (DEPTH, BATCH, SEQ, PLE_DIM)),
        'p_sample': nrm(ks[3], (DEPTH, DEC_BATCH, DEC_SEQ, PLE_DIM)),
        'cache_k': nrm(ks[4], (DEPTH, DEC_BATCH, wbuf, N_KV_HEADS, HEAD_DIM)),
        'cache_v': nrm(ks[5], (DEPTH, DEC_BATCH, wbuf, N_KV_HEADS, HEAD_DIM)),
        'state_conv': nrm(ks[6], (DEPTH, DEC_BATCH, CONV_K - 1, CONV_WIDTH)),
        'g_mix_norm': gain(ks[7], (DEPTH, D_MODEL)),
        'w_in': nrm(ks[8], (DEPTH, D_MODEL, IN_WIDTH), D_MODEL ** -0.5),
        'g_q': gain(ks[9], (DEPTH, HEAD_DIM)),
        'g_k': gain(ks[10], (DEPTH, HEAD_DIM)),
        'sinks': nrm(ks[11], (DEPTH, N_Q_HEADS), 0.5),
        'conv_w': nrm(ks[12], (DEPTH, CONV_K, CONV_WIDTH), CONV_K ** -0.5),
        'g_attn_out': gain(ks[13], (DEPTH, ATTN_WIDTH)),
        'g_conv_out': gain(ks[14], (DEPTH, CONV_WIDTH)),
        'w_o': nrm(ks[15], (DEPTH, D_MODEL, D_MODEL), D_MODEL ** -0.5),
        'g_mlp_norm': gain(ks[16], (DEPTH, D_MODEL)),
        'w_up': nrm(ks[17], (DEPTH, D_MODEL, D_FF), D_MODEL ** -0.5),
        'w_down': nrm(ks[18], (DEPTH, D_FF, D_MODEL), D_FF ** -0.5),
        'g_ple_norm': gain(ks[19], (DEPTH, D_MODEL)),
        'w_ple_gate': nrm(ks[20], (DEPTH, D_MODEL, D_MODEL), D_MODEL ** -0.5),
        'w_ple': nrm(ks[21], (DEPTH, PLE_DIM, D_MODEL), PLE_DIM ** -0.5),
    }


def reference(x_prompt, x_sample, p_prompt, p_sample, cache_k, cache_v, state_conv,
              g_mix_norm, w_in, g_q, g_k, sinks, conv_w, g_attn_out, g_conv_out, w_o,
              g_mlp_norm, w_up, w_down, g_ple_norm, w_ple_gate, w_ple):
    pos_p = jnp.arange(x_prompt.shape[1], dtype=jnp.int32)
    pos_s = PAST_LEN + jnp.arange(x_sample.shape[1], dtype=jnp.int32)
    yp, ys = x_prompt, x_sample
    kp_l, vp_l, cp_l, ks_l, vs_l, cs_l = [], [], [], [], [], []
    for i in range(DEPTH):
        w = (g_mix_norm[i], w_in[i], g_q[i], g_k[i], sinks[i], conv_w[i], g_attn_out[i],
             g_conv_out[i], w_o[i], g_mlp_norm[i], w_up[i], w_down[i], g_ple_norm[i],
             w_ple_gate[i], w_ple[i])
        yp, kp, vp, cp = decoder_layer(yp, p_prompt[i], pos_p, None, None, None, *w)
        ys, ksn, vsn, csn = decoder_layer(ys, p_sample[i], pos_s, cache_k[i], cache_v[i],
                                          state_conv[i], *w)
        kp_l.append(kp); vp_l.append(vp); cp_l.append(cp)
        ks_l.append(ksn); vs_l.append(vsn); cs_l.append(csn)
    k_prompt = jnp.stack(kp_l); v_prompt = jnp.stack(vp_l); conv_prompt = jnp.stack(cp_l)
    k_sample = jnp.stack(ks_l); v_sample = jnp.stack(vs_l); conv_sample = jnp.stack(cs_l)
    return (yp, ys, k_prompt, v_prompt, conv_prompt, k_sample, v_sample, conv_sample)
```

```python
import functools

import jax
import jax.numpy as jnp
import numpy as np
from jax import lax
from jax.experimental import pallas as pl
from jax.experimental.pallas import tpu as pltpu

D_MODEL = 1024
HEAD_DIM = 64
N_Q_HEADS = 8
N_KV_HEADS = 2
GQA_GROUP = N_Q_HEADS // N_KV_HEADS
ATTN_WIDTH = N_Q_HEADS * HEAD_DIM
KV_WIDTH = N_KV_HEADS * HEAD_DIM
CONV_WIDTH = D_MODEL - ATTN_WIDTH
CONV_K = 3
WINDOW = 128
BLOCK = 128
ROPE_THETA = 10000.0
D_FF = 4 * D_MODEL
PLE_DIM = 256
EPS = 1e-6
NEG = -1e30
PAST_LEN = 16384
IN_WIDTH = ATTN_WIDTH + 2 * KV_WIDTH + 3 * CONV_WIDTH

LANES = 128
SUBLANES = 8
N_QCOL = ATTN_WIDTH // LANES
SEQ_TILE = 256
FF_CHUNK = 1024
SAMPLE_STEP_BATCH = 16
VMEM_LIMIT = 56 * 1024 * 1024

O_K = ATTN_WIDTH
O_V = O_K + KV_WIDTH
O_B = O_V + KV_WIDTH
O_C = O_B + CONV_WIDTH
O_H = O_C + CONV_WIDTH

BF16 = jnp.bfloat16
F32 = jnp.float32


def _dot(a, b):
    return jnp.dot(a, b, preferred_element_type=F32)


def _dot_t(a, b):
    return lax.dot_general(a, b, (((1,), (1,)), ((), ())), preferred_element_type=F32)


def _rms(x, g):
    return x * lax.rsqrt(jnp.mean(x * x, axis=-1, keepdims=True) + EPS) * g


def _head_norm_rope(t, ones_bd, g, cos, sin):
    ssq = _dot((t * t).astype(BF16), ones_bd)
    t = t * lax.rsqrt(ssq * (1.0 / HEAD_DIM) + EPS)
    lane = lax.broadcasted_iota(jnp.int32, (t.shape[0], LANES), 1)
    first_half = (lane & (HEAD_DIM - 1)) < HEAD_DIM // 2
    cols = []
    for m in range(t.shape[1] // LANES):
        c = t[:, m * LANES:(m + 1) * LANES] * g
        up = pltpu.roll(c, LANES - HEAD_DIM // 2, axis=1)
        dn = pltpu.roll(c, HEAD_DIM // 2, axis=1)
        cols.append(c * cos + jnp.where(first_half, up, dn) * sin)
    return cols


def _front(x, refs):
    h = _rms(x, refs["g_mix"][...]).astype(BF16)
    z = _dot(h, refs["w_in"][...])
    cos = refs["cos"][...]
    sin = refs["sin"][...]
    qcols = _head_norm_rope(z[:, :O_K], refs["bd_q"][...], refs["g_q"][...], cos, sin)
    (k,) = _head_norm_rope(z[:, O_K:O_V], refs["bd_k"][...], refs["g_k"][...], cos, sin)
    v = z[:, O_V:O_B]
    b_gate = z[:, O_B:O_C]
    u = z[:, O_C:O_H] * z[:, O_H:]
    return qcols, k, v, b_gate, u


def _tail(x, o_attn, o_conv, p, refs):
    mixed = jnp.concatenate([_rms(o_attn, refs["g_attn"][...]), _rms(o_conv, refs["g_conv"][...])], axis=1)
    x = x + _dot(mixed.astype(BF16), refs["w_o"][...])
    hm = _rms(x, refs["g_mlp"][...]).astype(BF16)
    for c in range(D_FF // FF_CHUNK):
        up = _dot(hm, refs["w_up"][:, c * FF_CHUNK:(c + 1) * FF_CHUNK])
        act = jnp.square(jnp.maximum(up, 0.0)).astype(BF16)
        x = x + _dot(act, refs["w_down"][c * FF_CHUNK:(c + 1) * FF_CHUNK, :])
    gate = jax.nn.sigmoid(_dot(_rms(x, refs["g_ple"][...]).astype(BF16), refs["w_gate"][...]))
    return x + gate * _dot(p.astype(BF16), refs["w_ple"][...])


def _split_groups(col):
    lane = lax.broadcasted_iota(jnp.int32, col.shape, 1)
    lo = lane < HEAD_DIM
    return jnp.where(lo, col, 0.0).astype(BF16), jnp.where(lo, 0.0, col).astype(BF16)


def _sink_column(sinks_ref, rows_per_head):
    return jnp.concatenate(
        [jnp.full((rows_per_head, 1), sinks_ref[i], F32) for i in range(N_Q_HEADS)], axis=0)


def _merge_groups(pv, rows_per_head, m):
    lane = lax.broadcasted_iota(jnp.int32, (rows_per_head, LANES), 1)
    a = pv[(2 * m) * rows_per_head:(2 * m + 1) * rows_per_head]
    b = pv[(2 * m + 1) * rows_per_head:(2 * m + 2) * rows_per_head]
    return jnp.where(lane < HEAD_DIM, a, b)


WEIGHT_NAMES = ("g_mix", "w_in", "bd_q", "bd_k", "g_q", "g_k", "conv_w", "g_attn", "g_conv", "w_o",
                "g_mlp", "w_up", "w_down", "g_ple", "w_gate", "w_ple")


def _prompt_kernel(sinks_ref, x_ref, p_ref, cos_ref, sin_ref, *rest):
    nw = len(WEIGHT_NAMES)
    refs = dict(zip(WEIGHT_NAMES, rest[:nw]))
    refs["cos"], refs["sin"] = cos_ref, sin_ref
    y_ref, kout_ref, vout_ref, convout_ref, kbuf, vbuf, ubuf, obuf = rest[nw:]
    tm = x_ref.shape[0]
    j = pl.program_id(1)

    @pl.when(j == 0)
    def _():
        kbuf[0:BLOCK, :] = jnp.zeros((BLOCK, KV_WIDTH), BF16)
        vbuf[0:BLOCK, :] = jnp.zeros((BLOCK, KV_WIDTH), BF16)
        ubuf[0:SUBLANES, :] = jnp.zeros((SUBLANES, CONV_WIDTH), F32)

    x = x_ref[...]
    qcols, k, v, b_gate, u = _front(x, refs)
    kbuf[BLOCK:, :] = k.astype(BF16)
    vbuf[BLOCK:, :] = v.astype(BF16)

    qi = lax.broadcasted_iota(jnp.int32, (N_Q_HEADS * BLOCK, BLOCK), 0) & (BLOCK - 1)
    kj = lax.broadcasted_iota(jnp.int32, (N_Q_HEADS * BLOCK, BLOCK), 1)
    own = kj <= qi
    sink = _sink_column(sinks_ref, BLOCK)
    for i in range(tm // BLOCK):
        qs = []
        for m in range(N_QCOL):
            qs.extend(_split_groups(qcols[m][i * BLOCK:(i + 1) * BLOCK]))
        qstack = jnp.concatenate(qs, axis=0)
        s2 = _dot_t(qstack, kbuf[i * BLOCK:(i + 2) * BLOCK, :])
        s_prev = s2[:, :BLOCK]
        if i == 0:
            s_prev = jnp.where(j == 0, NEG, s_prev)
        s = jnp.where(own, s2[:, BLOCK:], s_prev)
        mx = jnp.maximum(jnp.max(s, axis=-1, keepdims=True), sink)
        e = jnp.exp(s - mx)
        denom = jnp.sum(e, axis=-1, keepdims=True) + jnp.exp(sink - mx)
        e2 = jnp.concatenate([jnp.where(own, 0.0, e), jnp.where(own, e, 0.0)], axis=1).astype(BF16)
        pv = _dot(e2, vbuf[i * BLOCK:(i + 2) * BLOCK, :]) * (1.0 / denom)
        for m in range(N_QCOL):
            obuf[i * BLOCK:(i + 1) * BLOCK, m * LANES:(m + 1) * LANES] = _merge_groups(pv, BLOCK, m)

    kbuf[0:BLOCK, :] = kbuf[tm:tm + BLOCK, :]
    vbuf[0:BLOCK, :] = vbuf[tm:tm + BLOCK, :]

    @pl.when(j == pl.num_programs(1) - 1)
    def _():
        kout_ref[...] = k[tm - BLOCK:, :]
        vout_ref[...] = v[tm - BLOCK:, :]
        convout_ref[...] = u[tm - (CONV_K - 1):, :]

    ubuf[SUBLANES:, :] = u
    cw = refs["conv_w"]
    conv = (cw[0:1, :] * ubuf[SUBLANES - 2:SUBLANES - 2 + tm, :]
            + cw[1:2, :] * ubuf[SUBLANES - 1:SUBLANES - 1 + tm, :]
            + cw[2:3, :] * u)
    ubuf[0:SUBLANES, :] = ubuf[tm:tm + SUBLANES, :]

    y_ref[...] = _tail(x, obuf[...], b_gate * conv, p_ref[...], refs)


def _sample_kernel(sinks_ref, x_ref, p_ref, cos_ref, sin_ref, st1_ref, st2_ref, ck_ref, cv_ref, *rest):
    nw = len(WEIGHT_NAMES)
    refs = dict(zip(WEIGHT_NAMES, rest[:nw]))
    refs["cos"], refs["sin"] = cos_ref, sin_ref
    y_ref, kout_ref, vout_ref, uout_ref, qbuf, kbuf, vbuf, knew, vnew, obuf, bgbuf = rest[nw:]
    n_tok = x_ref.shape[0]
    step = pl.program_id(0)
    n_steps = pl.num_programs(0)
    step_batch = ck_ref.shape[0]
    t_len = n_tok // (step_batch * n_steps)
    pair_rows = 2 * t_len
    chunk_pairs = BLOCK // pair_rows

    @pl.when(step == 0)
    def _():
        qcols, k, v, b_gate, u = _front(x_ref[...], refs)
        lo = lax.broadcasted_iota(jnp.int32, (n_tok, LANES), 1) < HEAD_DIM
        for m in range(N_QCOL):
            qbuf[2 * m] = jnp.where(lo, qcols[m], 0.0)
            qbuf[2 * m + 1] = jnp.where(lo, 0.0, qcols[m])
        kbuf[...] = k.astype(BF16)
        vbuf[...] = v.astype(BF16)
        knew[...] = k
        vnew[...] = v
        bgbuf[...] = b_gate
        uout_ref[...] = u

    n_rows = N_Q_HEADS * pair_rows
    row = lax.broadcasted_iota(jnp.int32, (n_rows, 3 * BLOCK), 0)
    col = lax.broadcasted_iota(jnp.int32, (n_rows, 3 * BLOCK), 1)
    row_b = (row % pair_rows) // t_len
    row_t = row % t_len
    cache_ok = (col < 2 * BLOCK) & (col // BLOCK == row_b) & ((col % BLOCK) > row_t)
    new_col = col - 2 * BLOCK
    sink = _sink_column(sinks_ref, pair_rows)

    def pair_step(i, carry):
        pair = step * (step_batch // 2) + i
        r0 = pl.multiple_of(pair * pair_rows, pair_rows)
        c0 = pl.multiple_of((pair // chunk_pairs) * BLOCK, BLOCK)
        q = qbuf[:, pl.ds(r0, pair_rows), :].reshape(n_rows, LANES).astype(BF16)
        ck = ck_ref[pl.ds(2 * i, 2)].reshape(2 * BLOCK, KV_WIDTH)
        cv = cv_ref[pl.ds(2 * i, 2)].reshape(2 * BLOCK, KV_WIDTH)
        keys = jnp.concatenate([ck.astype(BF16), kbuf[pl.ds(c0, BLOCK), :]], axis=0)
        vals = jnp.concatenate([cv.astype(BF16), vbuf[pl.ds(c0, BLOCK), :]], axis=0)
        s = _dot_t(q, keys)
        new_ok = ((col >= 2 * BLOCK) & (new_col // t_len == (pair % chunk_pairs) * 2 + row_b)
                  & (new_col % t_len <= row_t))
        ok = cache_ok | new_ok
        s = jnp.where(ok, s, NEG)
        mx = jnp.maximum(jnp.max(s, axis=-1, keepdims=True), sink)
        e = jnp.where(ok, jnp.exp(s - mx), 0.0)
        denom = jnp.sum(e, axis=-1, keepdims=True) + jnp.exp(sink - mx)
        pv = _dot(e.astype(BF16), vals) * (1.0 / denom)
        for m in range(N_QCOL):
            obuf[pl.ds(r0, pair_rows), m * LANES:(m + 1) * LANES] = _merge_groups(pv, pair_rows, m)
        k8 = knew[pl.ds(r0, pair_rows), :]
        v8 = vnew[pl.ds(r0, pair_rows), :]
        for bb in range(2):
            kout_ref[2 * i + bb, 0:BLOCK - t_len, :] = ck_ref[2 * i + bb, t_len:BLOCK, :]
            vout_ref[2 * i + bb, 0:BLOCK - t_len, :] = cv_ref[2 * i + bb, t_len:BLOCK, :]
            kout_ref[2 * i + bb, BLOCK - t_len:BLOCK, :] = k8[bb * t_len:(bb + 1) * t_len]
            vout_ref[2 * i + bb, BLOCK - t_len:BLOCK, :] = v8[bb * t_len:(bb + 1) * t_len]
        return carry

    lax.fori_loop(0, step_batch // 2, pair_step, 0)

    @pl.when(step == n_steps - 1)
    def _():
        u = uout_ref[...]
        tok = lax.broadcasted_iota(jnp.int32, u.shape, 0) % t_len
        um1 = jnp.where(tok >= 1, pltpu.roll(u, 1, axis=0), st1_ref[...])
        um2 = jnp.where(tok >= 2, pltpu.roll(u, 2, axis=0), st2_ref[...])
        cw = refs["conv_w"]
        conv = cw[0:1, :] * um2 + cw[1:2, :] * um1 + cw[2:3, :] * u
        y_ref[...] = _tail(x_ref[...], obuf[...], bgbuf[...] * conv, p_ref[...], refs)


def _rope_tables(pos):
    inv_freq = ROPE_THETA ** (-jnp.arange(0, HEAD_DIM, 2, dtype=F32) / HEAD_DIM)
    ang = pos.astype(F32)[:, None] * inv_freq[None, :]
    cos, sin = jnp.cos(ang), jnp.sin(ang)
    cos_h = jnp.concatenate([cos, cos], axis=1)
    sin_h = jnp.concatenate([-sin, sin], axis=1)
    return jnp.tile(cos_h, (1, LANES // HEAD_DIM)), jnp.tile(sin_h, (1, LANES // HEAD_DIM))


def _block_diag_ones(width):
    idx = np.arange(width) // HEAD_DIM
    return jnp.asarray(idx[:, None] == idx[None, :], dtype=BF16)


def _interleave_groups(a, axis):
    shape = a.shape
    a = a.reshape(shape[:axis] + (N_KV_HEADS, GQA_GROUP, HEAD_DIM) + shape[axis + 1:])
    a = jnp.swapaxes(a, axis, axis + 1)
    return a.reshape(shape)


def _prepare_weights(g_mix_norm, w_in, g_q, g_k, sinks, conv_w, g_attn_out, g_conv_out, w_o,
                     g_mlp_norm, w_up, w_down, g_ple_norm, w_ple_gate, w_ple):
    w_in_p = jnp.concatenate([_interleave_groups(w_in[:, :ATTN_WIDTH], 1), w_in[:, ATTN_WIDTH:]], axis=1)
    w_o_p = jnp.concatenate([_interleave_groups(w_o[:ATTN_WIDTH], 0), w_o[ATTN_WIDTH:]], axis=0)
    row = lambda g: g.reshape(1, -1).astype(F32)
    tile_head = lambda g: jnp.tile(g.reshape(1, HEAD_DIM).astype(F32), (1, LANES // HEAD_DIM))
    weights = dict(
        g_mix=row(g_mix_norm), w_in=w_in_p.astype(BF16),
        bd_q=_block_diag_ones(ATTN_WIDTH), bd_k=_block_diag_ones(KV_WIDTH),
        g_q=tile_head(g_q) * (HEAD_DIM ** -0.5),
        g_k=tile_head(g_k), conv_w=conv_w.astype(F32),
        g_attn=row(_interleave_groups(g_attn_out, 0)), g_conv=row(g_conv_out), w_o=w_o_p.astype(BF16),
        g_mlp=row(g_mlp_norm), w_up=w_up.astype(BF16), w_down=w_down.astype(BF16),
        g_ple=row(g_ple_norm), w_gate=w_ple_gate.astype(BF16), w_ple=w_ple.astype(BF16))
    sinks_p = sinks.reshape(N_KV_HEADS, GQA_GROUP).T.reshape(N_Q_HEADS).astype(F32)
    return sinks_p, [weights[n] for n in WEIGHT_NAMES]


def _resident(a, n_grid):
    zeros = (0,) * a.ndim
    index_map = (lambda b, j: zeros) if n_grid == 2 else (lambda i: zeros)
    return pl.BlockSpec(a.shape, index_map, pipeline_mode=pl.Buffered(1))


def _prompt_layer(x, p, sinks_p, weights):
    batch, seq, _ = x.shape
    tm = SEQ_TILE
    cos, sin = _rope_tables(jnp.arange(seq, dtype=jnp.int32))
    tile = lambda w: pl.BlockSpec((None, tm, w), lambda b, j: (b, j, 0))
    per_batch = lambda r, w: pl.BlockSpec((None, r, w), lambda b, j: (b, 0, 0))
    table = pl.BlockSpec((tm, LANES), lambda b, j: (j, 0))
    in_specs = ([pl.BlockSpec(memory_space=pltpu.SMEM), tile(D_MODEL), tile(PLE_DIM), table, table]
                + [_resident(w, 2) for w in weights])
    out_shape = (jax.ShapeDtypeStruct((batch, seq, D_MODEL), F32),
                 jax.ShapeDtypeStruct((batch, BLOCK, KV_WIDTH), F32),
                 jax.ShapeDtypeStruct((batch, BLOCK, KV_WIDTH), F32),
                 jax.ShapeDtypeStruct((batch, CONV_K - 1, CONV_WIDTH), F32))
    out_specs = (tile(D_MODEL), per_batch(BLOCK, KV_WIDTH), per_batch(BLOCK, KV_WIDTH),
                 per_batch(CONV_K - 1, CONV_WIDTH))
    scratch = [pltpu.VMEM((BLOCK + tm, KV_WIDTH), BF16), pltpu.VMEM((BLOCK + tm, KV_WIDTH), BF16),
               pltpu.VMEM((SUBLANES + tm, CONV_WIDTH), F32), pltpu.VMEM((tm, ATTN_WIDTH), F32)]
    return pl.pallas_call(
        _prompt_kernel, grid=(batch, seq // tm), in_specs=in_specs, out_specs=out_specs,
        out_shape=out_shape, scratch_shapes=scratch, name="prompt_layer",
        compiler_params=pltpu.CompilerParams(dimension_semantics=("arbitrary", "arbitrary"),
                                             vmem_limit_bytes=VMEM_LIMIT),
    )(sinks_p, x, p, cos, sin, *weights)


def _sample_layer(x, p, cache_k, cache_v, state_conv, sinks_p, weights):
    batch, t_len, _ = x.shape
    n_tok = batch * t_len
    cos, sin = _rope_tables(PAST_LEN + jnp.arange(t_len, dtype=jnp.int32))
    cos, sin = jnp.tile(cos, (batch, 1)), jnp.tile(sin, (batch, 1))
    st1 = jnp.pad(state_conv[:, 1:2], ((0, 0), (0, t_len - 1), (0, 0))).reshape(n_tok, CONV_WIDTH)
    st2 = jnp.pad(state_conv, ((0, 0), (0, t_len - (CONV_K - 1)), (0, 0))).reshape(n_tok, CONV_WIDTH)
    ck = cache_k.reshape(batch, WINDOW, KV_WIDTH)
    cv = cache_v.reshape(batch, WINDOW, KV_WIDTH)
    flat = [x.reshape(n_tok, D_MODEL), p.reshape(n_tok, PLE_DIM), cos, sin, st1, st2]
    cache_block = pl.BlockSpec((SAMPLE_STEP_BATCH, WINDOW, KV_WIDTH), lambda i: (i, 0, 0))
    in_specs = ([pl.BlockSpec(memory_space=pltpu.SMEM)] + [_resident(a, 1) for a in flat]
                + [cache_block, cache_block] + [_resident(w, 1) for w in weights])
    out_shape = (jax.ShapeDtypeStruct((n_tok, D_MODEL), F32),
                 jax.ShapeDtypeStruct((batch, WINDOW, KV_WIDTH), F32),
                 jax.ShapeDtypeStruct((batch, WINDOW, KV_WIDTH), F32),
                 jax.ShapeDtypeStruct((n_tok, CONV_WIDTH), F32))
    whole = lambda s: pl.BlockSpec(s.shape, lambda i: (0, 0))
    out_specs = (whole(out_shape[0]), cache_block, cache_block, whole(out_shape[3]))
    scratch = [pltpu.VMEM((N_Q_HEADS, n_tok, LANES), F32),
               pltpu.VMEM((n_tok, KV_WIDTH), BF16), pltpu.VMEM((n_tok, KV_WIDTH), BF16),
               pltpu.VMEM((n_tok, KV_WIDTH), F32), pltpu.VMEM((n_tok, KV_WIDTH), F32),
               pltpu.VMEM((n_tok, ATTN_WIDTH), F32), pltpu.VMEM((n_tok, CONV_WIDTH), F32)]
    y, k_new, v_new, u = pl.pallas_call(
        _sample_kernel, grid=(batch // SAMPLE_STEP_BATCH,), in_specs=in_specs, out_specs=out_specs,
        out_shape=out_shape, scratch_shapes=scratch, name="sample_layer",
        compiler_params=pltpu.CompilerParams(dimension_semantics=("arbitrary",),
                                             vmem_limit_bytes=VMEM_LIMIT),
    )(sinks_p, *flat, ck, cv, *weights)
    conv_new = u.reshape(batch, t_len, CONV_WIDTH)[:, t_len - (CONV_K - 1):]
    return y.reshape(batch, t_len, D_MODEL), k_new, v_new, conv_new


def kernel(x_prompt, x_sample, p_prompt, p_sample, cache_k, cache_v, state_conv, g_mix_norm, w_in, g_q, g_k,
           sinks, conv_w, g_attn_out, g_conv_out, w_o, g_mlp_norm, w_up, w_down, g_ple_norm, w_ple_gate, w_ple):
    depth = w_in.shape[0]
    yp, ys = x_prompt, x_sample
    outs = [[] for _ in range(6)]
    for i in range(depth):
        sinks_p, weights = _prepare_weights(
            g_mix_norm[i], w_in[i], g_q[i], g_k[i], sinks[i], conv_w[i], g_attn_out[i], g_conv_out[i],
            w_o[i], g_mlp_norm[i], w_up[i], w_down[i], g_ple_norm[i], w_ple_gate[i], w_ple[i])
        yp, kp, vp, cp = _prompt_layer(yp, p_prompt[i], sinks_p, weights)
        ys, ksn, vsn, csn = _sample_layer(ys, p_sample[i], cache_k[i], cache_v[i], state_conv[i],
                                          sinks_p, weights)
        batch = yp.shape[0]
        dec_batch = ys.shape[0]
        kv5 = lambda a, n: a.reshape(n, a.shape[1], N_KV_HEADS, HEAD_DIM)
        for lst, val in zip(outs, (kv5(kp, batch), kv5(vp, batch), cp,
                                   kv5(ksn, dec_batch), kv5(vsn, dec_batch), csn)):
            lst.append(val)
    return (yp, ys) + tuple(jnp.stack(o) for o in outs)
```

```python
import functools

import jax
import jax.numpy as jnp
import numpy as np
from jax import lax
from jax.experimental import pallas as pl
from jax.experimental.pallas import tpu as pltpu

D_MODEL = 1024
HEAD_DIM = 64
N_Q_HEADS = 8
N_KV_HEADS = 2
GQA_GROUP = N_Q_HEADS // N_KV_HEADS
ATTN_WIDTH = N_Q_HEADS * HEAD_DIM
KV_WIDTH = N_KV_HEADS * HEAD_DIM
CONV_WIDTH = D_MODEL - ATTN_WIDTH
CONV_K = 3
WINDOW = 128
BLOCK = 128
ROPE_THETA = 10000.0
D_FF = 4 * D_MODEL
PLE_DIM = 256
EPS = 1e-6
NEG = -1e30
PAST_LEN = 16384
IN_WIDTH = ATTN_WIDTH + 2 * KV_WIDTH + 3 * CONV_WIDTH

LANES = 128
SUBLANES = 8
N_QCOL = ATTN_WIDTH // LANES
SEQ_TILE = 256
FF_CHUNK = 1024
SAMPLE_STEP_BATCH = 16
VMEM_LIMIT = 56 * 1024 * 1024

O_K = ATTN_WIDTH
O_V = O_K + KV_WIDTH
O_B = O_V + KV_WIDTH
O_C = O_B + CONV_WIDTH
O_H = O_C + CONV_WIDTH

BF16 = jnp.bfloat16
F32 = jnp.float32


def _dot(a, b):
    return jnp.dot(a, b, preferred_element_type=F32)


def _dot_t(a, b):
    return lax.dot_general(a, b, (((1,), (1,)), ((), ())), preferred_element_type=F32)


def _rms(x, g):
    return x * lax.rsqrt(jnp.mean(x * x, axis=-1, keepdims=True) + EPS) * g


def _head_norm_rope(t, ones_bd, g, cos, sin):
    ssq = _dot((t * t).astype(BF16), ones_bd)
    t = t * lax.rsqrt(ssq * (1.0 / HEAD_DIM) + EPS)
    lane = lax.broadcasted_iota(jnp.int32, (t.shape[0], LANES), 1)
    first_half = (lane & (HEAD_DIM - 1)) < HEAD_DIM // 2
    cols = []
    for m in range(t.shape[1] // LANES):
        c = t[:, m * LANES:(m + 1) * LANES] * g
        up = pltpu.roll(c, LANES - HEAD_DIM // 2, axis=1)
        dn = pltpu.roll(c, HEAD_DIM // 2, axis=1)
        cols.append(c * cos + jnp.where(first_half, up, dn) * sin)
    return cols


def _front(x, refs):
    h = _rms(x, refs["g_mix"][...]).astype(BF16)
    z = _dot(h, refs["w_in"][...])
    cos = refs["cos"][...]
    sin = refs["sin"][...]
    qcols = _head_norm_rope(z[:, :O_K], refs["bd_q"][...], refs["g_q"][...], cos, sin)
    (k,) = _head_norm_rope(z[:, O_K:O_V], refs["bd_k"][...], refs["g_k"][...], cos, sin)
    v = z[:, O_V:O_B]
    b_gate = z[:, O_B:O_C]
    u = z[:, O_C:O_H] * z[:, O_H:]
    return qcols, k, v, b_gate, u


def _tail(x, o_attn, o_conv, p, refs):
    mixed = jnp.concatenate([_rms(o_attn, refs["g_attn"][...]), _rms(o_conv, refs["g_conv"][...])], axis=1)
    x = x + _dot(mixed.astype(BF16), refs["w_o"][...])
    hm = _rms(x, refs["g_mlp"][...]).astype(BF16)
    for c in range(D_FF // FF_CHUNK):
        up = _dot(hm, refs["w_up"][:, c * FF_CHUNK:(c + 1) * FF_CHUNK])
        act = jnp.square(jnp.maximum(up, 0.0)).astype(BF16)
        x = x + _dot(act, refs["w_down"][c * FF_CHUNK:(c + 1) * FF_CHUNK, :])
    gate = jax.nn.sigmoid(_dot(_rms(x, refs["g_ple"][...]).astype(BF16), refs["w_gate"][...]))
    return x + gate * _dot(p.astype(BF16), refs["w_ple"][...])


def _split_groups(col):
    lane = lax.broadcasted_iota(jnp.int32, col.shape, 1)
    lo = lane < HEAD_DIM
    return jnp.where(lo, col, 0.0).astype(BF16), jnp.where(lo, 0.0, col).astype(BF16)


def _sink_column(sinks_ref, rows_per_head):
    return jnp.concatenate(
        [jnp.full((rows_per_head, 1), sinks_ref[i], F32) for i in range(N_Q_HEADS)], axis=0)


def _merge_groups(pv, rows_per_head, m):
    lane = lax.broadcasted_iota(jnp.int32, (rows_per_head, LANES), 1)
    a = pv[(2 * m) * rows_per_head:(2 * m + 1) * rows_per_head]
    b = pv[(2 * m + 1) * rows_per_head:(2 * m + 2) * rows_per_head]
    return jnp.where(lane < HEAD_DIM, a, b)


WEIGHT_NAMES = ("g_mix", "w_in", "bd_q", "bd_k", "g_q", "g_k", "conv_w", "g_attn", "g_conv", "w_o",
                "g_mlp", "w_up", "w_down", "g_ple", "w_gate", "w_ple")


def _prompt_kernel(sinks_ref, x_ref, p_ref, cos_ref, sin_ref, *rest):
    nw = len(WEIGHT_NAMES)
    refs = dict(zip(WEIGHT_NAMES, rest[:nw]))
    refs["cos"], refs["sin"] = cos_ref, sin_ref
    y_ref, kout_ref, vout_ref, convout_ref, kbuf, vbuf, ubuf, obuf = rest[nw:]
    tm = x_ref.shape[0]
    j = pl.program_id(1)

    @pl.when(j == 0)
    def _():
        kbuf[0:BLOCK, :] = jnp.zeros((BLOCK, KV_WIDTH), BF16)
        vbuf[:, 0:BLOCK] = jnp.zeros((KV_WIDTH, BLOCK), BF16)
        ubuf[0:SUBLANES, :] = jnp.zeros((SUBLANES, CONV_WIDTH), F32)

    x = x_ref[...]
    qcols, k, v, b_gate, u = _front(x, refs)
    kbuf[BLOCK:, :] = k.astype(BF16)
    v_t = v.T
    vbuf[:, BLOCK:] = v_t.astype(BF16)

    kj = lax.broadcasted_iota(jnp.int32, (BLOCK, N_Q_HEADS * BLOCK), 0)
    qi = lax.broadcasted_iota(jnp.int32, (BLOCK, N_Q_HEADS * BLOCK), 1) & (BLOCK - 1)
    own = kj <= qi
    sink = jnp.concatenate([jnp.full((1, BLOCK), sinks_ref[i], F32) for i in range(N_Q_HEADS)], axis=1)
    for i in range(tm // BLOCK):
        qs = []
        for m in range(N_QCOL):
            qs.extend(_split_groups(qcols[m][i * BLOCK:(i + 1) * BLOCK]))
        qstack = jnp.concatenate(qs, axis=0)
        s2 = _dot_t(kbuf[i * BLOCK:(i + 2) * BLOCK, :], qstack)
        s_prev = s2[:BLOCK]
        if i == 0:
            s_prev = jnp.where(j == 0, NEG, s_prev)
        s = jnp.where(own, s2[BLOCK:], s_prev)
        mx = jnp.maximum(jnp.max(s, axis=0, keepdims=True), sink)
        e = jnp.exp(s - mx)
        denom = jnp.sum(e, axis=0, keepdims=True) + jnp.exp(sink - mx)
        e2 = jnp.concatenate([jnp.where(own, 0.0, e), jnp.where(own, e, 0.0)], axis=0).astype(BF16)
        pv = _dot(vbuf[:, i * BLOCK:(i + 2) * BLOCK], e2) * (1.0 / denom)
        for m in range(N_QCOL):
            col_t = jnp.concatenate([pv[:HEAD_DIM, (2 * m) * BLOCK:(2 * m + 1) * BLOCK],
                                     pv[HEAD_DIM:, (2 * m + 1) * BLOCK:(2 * m + 2) * BLOCK]], axis=0)
            obuf[i * BLOCK:(i + 1) * BLOCK, m * LANES:(m + 1) * LANES] = col_t.T

    kbuf[0:BLOCK, :] = kbuf[tm:tm + BLOCK, :]
    vbuf[:, 0:BLOCK] = vbuf[:, tm:tm + BLOCK]

    @pl.when(j == pl.num_programs(1) - 1)
    def _():
        kout_ref[...] = k[tm - BLOCK:, :].T
        vout_ref[...] = v_t[:, tm - BLOCK:]
        convout_ref[...] = u[tm - (CONV_K - 1):, :]

    ubuf[SUBLANES:, :] = u
    cw = refs["conv_w"]
    conv = (cw[0:1, :] * ubuf[SUBLANES - 2:SUBLANES - 2 + tm, :]
            + cw[1:2, :] * ubuf[SUBLANES - 1:SUBLANES - 1 + tm, :]
            + cw[2:3, :] * u)
    ubuf[0:SUBLANES, :] = ubuf[tm:tm + SUBLANES, :]

    y_ref[...] = _tail(x, obuf[...], b_gate * conv, p_ref[...], refs)


def _sample_kernel(sinks_ref, x_ref, p_ref, cos_ref, sin_ref, st1_ref, st2_ref, ck_ref, cv_ref, *rest):
    nw = len(WEIGHT_NAMES)
    refs = dict(zip(WEIGHT_NAMES, rest[:nw]))
    refs["cos"], refs["sin"] = cos_ref, sin_ref
    y_ref, kout_ref, vout_ref, uout_ref, qbuf, kbuf, vbuf, knew, vnew, obuf, bgbuf = rest[nw:]
    n_tok = x_ref.shape[0]
    step = pl.program_id(0)
    n_steps = pl.num_programs(0)
    step_batch = ck_ref.shape[0]
    t_len = n_tok // (step_batch * n_steps)
    pair_rows = 2 * t_len
    chunk_pairs = BLOCK // pair_rows

    @pl.when(step == 0)
    def _():
        qcols, k, v, b_gate, u = _front(x_ref[...], refs)
        lo = lax.broadcasted_iota(jnp.int32, (n_tok, LANES), 1) < HEAD_DIM
        for m in range(N_QCOL):
            qbuf[2 * m] = jnp.where(lo, qcols[m], 0.0)
            qbuf[2 * m + 1] = jnp.where(lo, 0.0, qcols[m])
        k_t, v_t = k.T, v.T
        kbuf[...] = k_t.astype(BF16)
        vbuf[...] = v_t.astype(BF16)
        knew[...] = k_t
        vnew[...] = v_t
        bgbuf[...] = b_gate
        uout_ref[...] = u

    n_rows = N_Q_HEADS * pair_rows
    row = lax.broadcasted_iota(jnp.int32, (n_rows, 3 * BLOCK), 0)
    col = lax.broadcasted_iota(jnp.int32, (n_rows, 3 * BLOCK), 1)
    row_b = (row % pair_rows) // t_len
    row_t = row % t_len
    cache_ok = (col < 2 * BLOCK) & (col // BLOCK == row_b) & ((col % BLOCK) > row_t)
    new_col = col - 2 * BLOCK
    sink = _sink_column(sinks_ref, pair_rows)

    def pair_step(i, carry):
        pair = step * (step_batch // 2) + i
        r0 = pl.multiple_of(pair * pair_rows, pair_rows)
        c0 = pl.multiple_of((pair // chunk_pairs) * BLOCK, BLOCK)
        q = qbuf[:, pl.ds(r0, pair_rows), :].reshape(n_rows, LANES).astype(BF16)
        ck = [ck_ref[2 * i + bb] for bb in range(2)]
        cv = [cv_ref[2 * i + bb] for bb in range(2)]
        keys = jnp.concatenate([c.astype(BF16) for c in ck] + [kbuf[:, pl.ds(c0, BLOCK)]], axis=1)
        vals = jnp.concatenate([c.astype(BF16) for c in cv] + [vbuf[:, pl.ds(c0, BLOCK)]], axis=1)
        s = _dot(q, keys)
        new_ok = ((col >= 2 * BLOCK) & (new_col // t_len == (pair % chunk_pairs) * 2 + row_b)
                  & (new_col % t_len <= row_t))
        ok = cache_ok | new_ok
        s = jnp.where(ok, s, NEG)
        mx = jnp.maximum(jnp.max(s, axis=-1, keepdims=True), sink)
        e = jnp.where(ok, jnp.exp(s - mx), 0.0)
        denom = jnp.sum(e, axis=-1, keepdims=True) + jnp.exp(sink - mx)
        pv = _dot_t(e.astype(BF16), vals) * (1.0 / denom)
        for m in range(N_QCOL):
            obuf[pl.ds(r0, pair_rows), m * LANES:(m + 1) * LANES] = _merge_groups(pv, pair_rows, m)
        k_chunk = knew[:, pl.ds(c0, BLOCK)]
        v_chunk = vnew[:, pl.ds(c0, BLOCK)]
        keep = lax.broadcasted_iota(jnp.int32, (KV_WIDTH, BLOCK), 1) < BLOCK - t_len
        for bb in range(2):
            to_tail = (BLOCK - t_len) - t_len * ((pair % chunk_pairs) * 2 + bb)
            kout_ref[2 * i + bb] = jnp.where(keep, pltpu.roll(ck[bb], BLOCK - t_len, axis=1),
                                             pltpu.roll(k_chunk, to_tail, axis=1))
            vout_ref[2 * i + bb] = jnp.where(keep, pltpu.roll(cv[bb], BLOCK - t_len, axis=1),
                                             pltpu.roll(v_chunk, to_tail, axis=1))
        return carry

    lax.fori_loop(0, step_batch // 2, pair_step, 0)

    @pl.when(step == n_steps - 1)
    def _():
        u = uout_ref[...]
        tok = lax.broadcasted_iota(jnp.int32, u.shape, 0) % t_len
        um1 = jnp.where(tok >= 1, pltpu.roll(u, 1, axis=0), st1_ref[...])
        um2 = jnp.where(tok >= 2, pltpu.roll(u, 2, axis=0), st2_ref[...])
        cw = refs["conv_w"]
        conv = cw[0:1, :] * um2 + cw[1:2, :] * um1 + cw[2:3, :] * u
        y_ref[...] = _tail(x_ref[...], obuf[...], bgbuf[...] * conv, p_ref[...], refs)


def _rope_tables(pos):
    inv_freq = ROPE_THETA ** (-jnp.arange(0, HEAD_DIM, 2, dtype=F32) / HEAD_DIM)
    ang = pos.astype(F32)[:, None] * inv_freq[None, :]
    cos, sin = jnp.cos(ang), jnp.sin(ang)
    cos_h = jnp.concatenate([cos, cos], axis=1)
    sin_h = jnp.concatenate([-sin, sin], axis=1)
    return jnp.tile(cos_h, (1, LANES // HEAD_DIM)), jnp.tile(sin_h, (1, LANES // HEAD_DIM))


def _cache_to_kernel(c):
    batch, keys = c.shape[0], c.shape[1]
    return jnp.transpose(c, (0, 2, 3, 1)).reshape(batch, KV_WIDTH, keys)


def _cache_from_kernel(c):
    batch, _, keys = c.shape
    return jnp.transpose(c.reshape(batch, N_KV_HEADS, HEAD_DIM, keys), (0, 3, 1, 2))


def _block_diag_ones(width):
    idx = np.arange(width) // HEAD_DIM
    return jnp.asarray(idx[:, None] == idx[None, :], dtype=BF16)


def _interleave_groups(a, axis):
    shape = a.shape
    a = a.reshape(shape[:axis] + (N_KV_HEADS, GQA_GROUP, HEAD_DIM) + shape[axis + 1:])
    a = jnp.swapaxes(a, axis, axis + 1)
    return a.reshape(shape)


def _prepare_weights(g_mix_norm, w_in, g_q, g_k, sinks, conv_w, g_attn_out, g_conv_out, w_o,
                     g_mlp_norm, w_up, w_down, g_ple_norm, w_ple_gate, w_ple):
    w_in_p = jnp.concatenate([_interleave_groups(w_in[:, :ATTN_WIDTH], 1), w_in[:, ATTN_WIDTH:]], axis=1)
    w_o_p = jnp.concatenate([_interleave_groups(w_o[:ATTN_WIDTH], 0), w_o[ATTN_WIDTH:]], axis=0)
    row = lambda g: g.reshape(1, -1).astype(F32)
    tile_head = lambda g: jnp.tile(g.reshape(1, HEAD_DIM).astype(F32), (1, LANES // HEAD_DIM))
    weights = dict(
        g_mix=row(g_mix_norm), w_in=w_in_p.astype(BF16),
        bd_q=_block_diag_ones(ATTN_WIDTH), bd_k=_block_diag_ones(KV_WIDTH),
        g_q=tile_head(g_q) * (HEAD_DIM ** -0.5),
        g_k=tile_head(g_k), conv_w=conv_w.astype(F32),
        g_attn=row(_interleave_groups(g_attn_out, 0)), g_conv=row(g_conv_out), w_o=w_o_p.astype(BF16),
        g_mlp=row(g_mlp_norm), w_up=w_up.astype(BF16), w_down=w_down.astype(BF16),
        g_ple=row(g_ple_norm), w_gate=w_ple_gate.astype(BF16), w_ple=w_ple.astype(BF16))
    sinks_p = sinks.reshape(N_KV_HEADS, GQA_GROUP).T.reshape(N_Q_HEADS).astype(F32)
    return sinks_p, [weights[n] for n in WEIGHT_NAMES]


def _resident(a, n_grid):
    zeros = (0,) * a.ndim
    index_map = (lambda b, j: zeros) if n_grid == 2 else (lambda i: zeros)
    return pl.BlockSpec(a.shape, index_map, pipeline_mode=pl.Buffered(1))


def _prompt_layer(x, p, sinks_p, weights):
    batch, seq, _ = x.shape
    tm = SEQ_TILE
    cos, sin = _rope_tables(jnp.arange(seq, dtype=jnp.int32))
    tile = lambda w: pl.BlockSpec((None, tm, w), lambda b, j: (b, j, 0))
    per_batch = lambda r, w: pl.BlockSpec((None, r, w), lambda b, j: (b, 0, 0))
    table = pl.BlockSpec((tm, LANES), lambda b, j: (j, 0))
    in_specs = ([pl.BlockSpec(memory_space=pltpu.SMEM), tile(D_MODEL), tile(PLE_DIM), table, table]
                + [_resident(w, 2) for w in weights])
    out_shape = (jax.ShapeDtypeStruct((batch, seq, D_MODEL), F32),
                 jax.ShapeDtypeStruct((batch, BLOCK, KV_WIDTH), F32),
                 jax.ShapeDtypeStruct((batch, BLOCK, KV_WIDTH), F32),
                 jax.ShapeDtypeStruct((batch, CONV_K - 1, CONV_WIDTH), F32))
    out_specs = (tile(D_MODEL), per_batch(BLOCK, KV_WIDTH), per_batch(BLOCK, KV_WIDTH),
                 per_batch(CONV_K - 1, CONV_WIDTH))
    scratch = [pltpu.VMEM((BLOCK + tm, KV_WIDTH), BF16), pltpu.VMEM((KV_WIDTH, BLOCK + tm), BF16),
               pltpu.VMEM((SUBLANES + tm, CONV_WIDTH), F32), pltpu.VMEM((tm, ATTN_WIDTH), F32)]
    return pl.pallas_call(
        _prompt_kernel, grid=(batch, seq // tm), in_specs=in_specs, out_specs=out_specs,
        out_shape=out_shape, scratch_shapes=scratch, name="prompt_layer",
        compiler_params=pltpu.CompilerParams(dimension_semantics=("arbitrary", "arbitrary"),
                                             vmem_limit_bytes=VMEM_LIMIT),
    )(sinks_p, x, p, cos, sin, *weights)


def _sample_layer(x, p, cache_k, cache_v, state_conv, sinks_p, weights):
    batch, t_len, _ = x.shape
    n_tok = batch * t_len
    cos, sin = _rope_tables(PAST_LEN + jnp.arange(t_len, dtype=jnp.int32))
    cos, sin = jnp.tile(cos, (batch, 1)), jnp.tile(sin, (batch, 1))
    st1 = jnp.pad(state_conv[:, 1:2], ((0, 0), (0, t_len - 1), (0, 0))).reshape(n_tok, CONV_WIDTH)
    st2 = jnp.pad(state_conv, ((0, 0), (0, t_len - (CONV_K - 1)), (0, 0))).reshape(n_tok, CONV_WIDTH)
    ck = _cache_to_kernel(cache_k)
    cv = _cache_to_kernel(cache_v)
    flat = [x.reshape(n_tok, D_MODEL), p.reshape(n_tok, PLE_DIM), cos, sin, st1, st2]
    cache_block = pl.BlockSpec((SAMPLE_STEP_BATCH, WINDOW, KV_WIDTH), lambda i: (i, 0, 0))
    in_specs = ([pl.BlockSpec(memory_space=pltpu.SMEM)] + [_resident(a, 1) for a in flat]
                + [cache_block, cache_block] + [_resident(w, 1) for w in weights])
    out_shape = (jax.ShapeDtypeStruct((n_tok, D_MODEL), F32),
                 jax.ShapeDtypeStruct((batch, WINDOW, KV_WIDTH), F32),
                 jax.ShapeDtypeStruct((batch, WINDOW, KV_WIDTH), F32),
                 jax.ShapeDtypeStruct((n_tok, CONV_WIDTH), F32))
    whole = lambda s: pl.BlockSpec(s.shape, lambda i: (0, 0))
    out_specs = (whole(out_shape[0]), cache_block, cache_block, whole(out_shape[3]))
    scratch = [pltpu.VMEM((N_Q_HEADS, n_tok, LANES), F32),
               pltpu.VMEM((KV_WIDTH, n_tok), BF16), pltpu.VMEM((KV_WIDTH, n_tok), BF16),
               pltpu.VMEM((KV_WIDTH, n_tok), F32), pltpu.VMEM((KV_WIDTH, n_tok), F32),
               pltpu.VMEM((n_tok, ATTN_WIDTH), F32), pltpu.VMEM((n_tok, CONV_WIDTH), F32)]
    y, k_new, v_new, u = pl.pallas_call(
        _sample_kernel, grid=(batch // SAMPLE_STEP_BATCH,), in_specs=in_specs, out_specs=out_specs,
        out_shape=out_shape, scratch_shapes=scratch, name="sample_layer",
        compiler_params=pltpu.CompilerParams(dimension_semantics=("arbitrary",),
                                             vmem_limit_bytes=VMEM_LIMIT),
    )(sinks_p, *flat, ck, cv, *weights)
    conv_new = u.reshape(batch, t_len, CONV_WIDTH)[:, t_len - (CONV_K - 1):]
    return y.reshape(batch, t_len, D_MODEL), k_new, v_new, conv_new


def kernel(x_prompt, x_sample, p_prompt, p_sample, cache_k, cache_v, state_conv, g_mix_norm, w_in, g_q, g_k,
           sinks, conv_w, g_attn_out, g_conv_out, w_o, g_mlp_norm, w_up, w_down, g_ple_norm, w_ple_gate, w_ple):
    depth = w_in.shape[0]
    yp, ys = x_prompt, x_sample
    outs = [[] for _ in range(6)]
    for i in range(depth):
        sinks_p, weights = _prepare_weights(
            g_mix_norm[i], w_in[i], g_q[i], g_k[i], sinks[i], conv_w[i], g_attn_out[i], g_conv_out[i],
            w_o[i], g_mlp_norm[i], w_up[i], w_down[i], g_ple_norm[i], w_ple_gate[i], w_ple[i])
        yp, kp, vp, cp = _prompt_layer(yp, p_prompt[i], sinks_p, weights)
        ys, ksn, vsn, csn = _sample_layer(ys, p_sample[i], cache_k[i], cache_v[i], state_conv[i],
                                          sinks_p, weights)
        batch = yp.shape[0]
        dec_batch = ys.shape[0]
        for lst, val in zip(outs, (_cache_from_kernel(kp), _cache_from_kernel(vp), cp,
                                   _cache_from_kernel(ksn), _cache_from_kernel(vsn), csn)):
            lst.append(val)
    return (yp, ys) + tuple(jnp.stack(o) for o in outs)
```

```python
import functools

import jax
import jax.numpy as jnp
import numpy as np
from jax import lax
from jax.experimental import pallas as pl
from jax.experimental.pallas import tpu as pltpu

D_MODEL = 1024
HEAD_DIM = 64
N_Q_HEADS = 8
N_KV_HEADS = 2
GQA_GROUP = N_Q_HEADS // N_KV_HEADS
ATTN_WIDTH = N_Q_HEADS * HEAD_DIM
KV_WIDTH = N_KV_HEADS * HEAD_DIM
CONV_WIDTH = D_MODEL - ATTN_WIDTH
CONV_K = 3
WINDOW = 128
BLOCK = 128
ROPE_THETA = 10000.0
D_FF = 4 * D_MODEL
PLE_DIM = 256
EPS = 1e-6
NEG = -1e30
PAST_LEN = 16384
IN_WIDTH = ATTN_WIDTH + 2 * KV_WIDTH + 3 * CONV_WIDTH

LANES = 128
SUBLANES = 8
N_QCOL = ATTN_WIDTH // LANES
SEQ_TILE = 512
SUB_TILE = 256
STAGE_LEAD = 2
FF_CHUNK = 1024
SAMPLE_STEP_BATCH = 16
VMEM_LIMIT = 56 * 1024 * 1024

O_K = ATTN_WIDTH
O_V = O_K + KV_WIDTH
O_B = O_V + KV_WIDTH
O_C = O_B + CONV_WIDTH
O_H = O_C + CONV_WIDTH

BF16 = jnp.bfloat16
F32 = jnp.float32


def _dot(a, b):
    return jnp.dot(a, b, preferred_element_type=F32)


def _dot_t(a, b):
    return lax.dot_general(a, b, (((1,), (1,)), ((), ())), preferred_element_type=F32)


def _rms(x, g):
    return x * lax.rsqrt(jnp.mean(x * x, axis=-1, keepdims=True) + EPS) * g


def _head_norm_rope(t, ones_bd, g, cos, sin):
    ssq = _dot((t * t).astype(BF16), ones_bd)
    t = t * lax.rsqrt(ssq * (1.0 / HEAD_DIM) + EPS)
    lane = lax.broadcasted_iota(jnp.int32, (t.shape[0], LANES), 1)
    first_half = (lane & (HEAD_DIM - 1)) < HEAD_DIM // 2
    cols = []
    for m in range(t.shape[1] // LANES):
        c = t[:, m * LANES:(m + 1) * LANES] * g
        up = pltpu.roll(c, LANES - HEAD_DIM // 2, axis=1)
        dn = pltpu.roll(c, HEAD_DIM // 2, axis=1)
        cols.append(c * cos + jnp.where(first_half, up, dn) * sin)
    return cols


def _front(x, cos, sin, refs):
    h = _rms(x, refs["g_mix"][...]).astype(BF16)
    z = _dot(h, refs["w_in"][...])
    qcols = _head_norm_rope(z[:, :O_K], refs["bd_q"][...], refs["g_q"][...], cos, sin)
    (k,) = _head_norm_rope(z[:, O_K:O_V], refs["bd_k"][...], refs["g_k"][...], cos, sin)
    v = z[:, O_V:O_B]
    b_gate = z[:, O_B:O_C]
    u = z[:, O_C:O_H] * z[:, O_H:]
    return qcols, k, v, b_gate, u


def _tail_steps(x, o_attn, o_conv, p, refs, store):
    mixed = jnp.concatenate([_rms(o_attn, refs["g_attn"][...]), _rms(o_conv, refs["g_conv"][...])], axis=1)
    x = x + _dot(mixed.astype(BF16), refs["w_o"][...])
    hm = _rms(x, refs["g_mlp"][...]).astype(BF16)
    yield
    for c in range(D_FF // FF_CHUNK):
        up = _dot(hm, refs["w_up"][:, c * FF_CHUNK:(c + 1) * FF_CHUNK])
        act = jnp.square(jnp.maximum(up, 0.0)).astype(BF16)
        x = x + _dot(act, refs["w_down"][c * FF_CHUNK:(c + 1) * FF_CHUNK, :])
        yield
    gate = jax.nn.sigmoid(_dot(_rms(x, refs["g_ple"][...]).astype(BF16), refs["w_gate"][...]))
    store(x + gate * _dot(p.astype(BF16), refs["w_ple"][...]))


def _run(steps):
    for _ in steps:
        pass


def _split_groups(col):
    lane = lax.broadcasted_iota(jnp.int32, col.shape, 1)
    lo = lane < HEAD_DIM
    return jnp.where(lo, col, 0.0).astype(BF16), jnp.where(lo, 0.0, col).astype(BF16)


def _sink_column(sinks_ref, rows_per_head):
    return jnp.concatenate(
        [jnp.full((rows_per_head, 1), sinks_ref[i], F32) for i in range(N_Q_HEADS)], axis=0)


def _merge_groups(pv, rows_per_head, m):
    lane = lax.broadcasted_iota(jnp.int32, (rows_per_head, LANES), 1)
    a = pv[(2 * m) * rows_per_head:(2 * m + 1) * rows_per_head]
    b = pv[(2 * m + 1) * rows_per_head:(2 * m + 2) * rows_per_head]
    return jnp.where(lane < HEAD_DIM, a, b)


WEIGHT_NAMES = ("g_mix", "w_in", "bd_q", "bd_k", "g_q", "g_k", "conv_w", "g_attn", "g_conv", "w_o",
                "g_mlp", "w_up", "w_down", "g_ple", "w_gate", "w_ple")


def _prompt_kernel(sinks_ref, x_ref, p_ref, cos_ref, sin_ref, *rest):
    nw = len(WEIGHT_NAMES)
    refs = dict(zip(WEIGHT_NAMES, rest[:nw]))
    y_ref, kout_ref, vout_ref, convout_ref, kbuf, vbuf, ubuf = rest[nw:]
    tm = x_ref.shape[0]
    j = pl.program_id(1)

    @pl.when(j == 0)
    def _():
        kbuf[0:BLOCK, :] = jnp.zeros((BLOCK, KV_WIDTH), BF16)
        vbuf[:, 0:BLOCK] = jnp.zeros((KV_WIDTH, BLOCK), BF16)
        ubuf[0:SUBLANES, :] = jnp.zeros((SUBLANES, CONV_WIDTH), F32)

    kj = lax.broadcasted_iota(jnp.int32, (BLOCK, N_Q_HEADS * BLOCK), 0)
    qi = lax.broadcasted_iota(jnp.int32, (BLOCK, N_Q_HEADS * BLOCK), 1) & (BLOCK - 1)
    own = kj <= qi
    sink = jnp.concatenate([jnp.full((1, BLOCK), sinks_ref[i], F32) for i in range(N_Q_HEADS)], axis=1)
    cw = refs["conv_w"]

    last = {}

    def sub_tile_steps(r0):
        rows = slice(r0, r0 + SUB_TILE)
        x = x_ref[rows, :]
        qcols, k, v, b_gate, u = _front(x, cos_ref[rows, :], sin_ref[rows, :], refs)
        kbuf[BLOCK + r0:BLOCK + r0 + SUB_TILE, :] = k.astype(BF16)
        v_t = v.T
        vbuf[:, BLOCK + r0:BLOCK + r0 + SUB_TILE] = v_t.astype(BF16)
        ubuf[SUBLANES + r0:SUBLANES + r0 + SUB_TILE, :] = u
        last.update(k=k, v_t=v_t, u=u)
        yield

        o_blocks = []
        for i in range(SUB_TILE // BLOCK):
            g = r0 // BLOCK + i
            qs = []
            for m in range(N_QCOL):
                qs.extend(_split_groups(qcols[m][i * BLOCK:(i + 1) * BLOCK]))
            qstack = jnp.concatenate(qs, axis=0)
            s2 = _dot_t(kbuf[g * BLOCK:(g + 2) * BLOCK, :], qstack)
            s_prev = s2[:BLOCK]
            if g == 0:
                s_prev = jnp.where(j == 0, NEG, s_prev)
            s = jnp.where(own, s2[BLOCK:], s_prev)
            mx = jnp.maximum(jnp.max(s, axis=0, keepdims=True), sink)
            e = jnp.exp(s - mx)
            denom = jnp.sum(e, axis=0, keepdims=True) + jnp.exp(sink - mx)
            e2 = jnp.concatenate([jnp.where(own, 0.0, e), jnp.where(own, e, 0.0)], axis=0).astype(BF16)
            pv = _dot(vbuf[:, g * BLOCK:(g + 2) * BLOCK], e2) * (1.0 / denom)
            cols = []
            for m in range(N_QCOL):
                col_t = jnp.concatenate([pv[:HEAD_DIM, (2 * m) * BLOCK:(2 * m + 1) * BLOCK],
                                         pv[HEAD_DIM:, (2 * m + 1) * BLOCK:(2 * m + 2) * BLOCK]], axis=0)
                cols.append(col_t.T)
            o_blocks.append(jnp.concatenate(cols, axis=1))
            yield
        o_attn = jnp.concatenate(o_blocks, axis=0)

        conv = (cw[0:1, :] * ubuf[SUBLANES - 2 + r0:SUBLANES - 2 + r0 + SUB_TILE, :]
                + cw[1:2, :] * ubuf[SUBLANES - 1 + r0:SUBLANES - 1 + r0 + SUB_TILE, :]
                + cw[2:3, :] * u)

        def store(y):
            y_ref[rows, :] = y

        yield from _tail_steps(x, o_attn, b_gate * conv, p_ref[rows, :], refs, store)

    tiles = [sub_tile_steps(r0) for r0 in range(0, tm, SUB_TILE)]
    done = [False] * len(tiles)
    rnd = 0
    while not all(done):
        for t, steps in enumerate(tiles):
            if rnd >= t * STAGE_LEAD and not done[t]:
                done[t] = next(steps, True) is True
        rnd += 1

    kout_ref[...] = last["k"][SUB_TILE - BLOCK:, :].T
    vout_ref[...] = last["v_t"][:, SUB_TILE - BLOCK:]
    convout_ref[...] = last["u"][SUB_TILE - (CONV_K - 1):, :]
    kbuf[0:BLOCK, :] = kbuf[tm:tm + BLOCK, :]
    vbuf[:, 0:BLOCK] = vbuf[:, tm:tm + BLOCK]
    ubuf[0:SUBLANES, :] = ubuf[tm:tm + SUBLANES, :]


def _sample_kernel(sinks_ref, x_ref, p_ref, cos_ref, sin_ref, st1_ref, st2_ref, ck_ref, cv_ref, *rest):
    nw = len(WEIGHT_NAMES)
    refs = dict(zip(WEIGHT_NAMES, rest[:nw]))
    y_ref, kout_ref, vout_ref, uout_ref, qbuf, kbuf, vbuf, knew, vnew, obuf, bgbuf = rest[nw:]
    n_tok = x_ref.shape[0]
    step = pl.program_id(0)
    n_steps = pl.num_programs(0)
    step_batch = ck_ref.shape[0]
    t_len = n_tok // (step_batch * n_steps)
    pair_rows = 2 * t_len
    chunk_pairs = BLOCK // pair_rows

    @pl.when(step == 0)
    def _():
        qcols, k, v, b_gate, u = _front(x_ref[...], cos_ref[...], sin_ref[...], refs)
        lo = lax.broadcasted_iota(jnp.int32, (n_tok, LANES), 1) < HEAD_DIM
        for m in range(N_QCOL):
            qbuf[2 * m] = jnp.where(lo, qcols[m], 0.0)
            qbuf[2 * m + 1] = jnp.where(lo, 0.0, qcols[m])
        k_t, v_t = k.T, v.T
        kbuf[...] = k_t.astype(BF16)
        vbuf[...] = v_t.astype(BF16)
        knew[...] = k_t
        vnew[...] = v_t
        bgbuf[...] = b_gate
        uout_ref[...] = u

    n_rows = N_Q_HEADS * pair_rows
    row = lax.broadcasted_iota(jnp.int32, (n_rows, 3 * BLOCK), 0)
    col = lax.broadcasted_iota(jnp.int32, (n_rows, 3 * BLOCK), 1)
    row_b = (row % pair_rows) // t_len
    row_t = row % t_len
    cache_ok = (col < 2 * BLOCK) & (col // BLOCK == row_b) & ((col % BLOCK) > row_t)
    new_col = col - 2 * BLOCK
    sink = _sink_column(sinks_ref, pair_rows)

    def pair_step(i, carry):
        pair = step * (step_batch // 2) + i
        r0 = pl.multiple_of(pair * pair_rows, pair_rows)
        c0 = pl.multiple_of((pair // chunk_pairs) * BLOCK, BLOCK)
        q = qbuf[:, pl.ds(r0, pair_rows), :].reshape(n_rows, LANES).astype(BF16)
        ck = [ck_ref[2 * i + bb] for bb in range(2)]
        cv = [cv_ref[2 * i + bb] for bb in range(2)]
        keys = jnp.concatenate([c.astype(BF16) for c in ck] + [kbuf[:, pl.ds(c0, BLOCK)]], axis=1)
        vals = jnp.concatenate([c.astype(BF16) for c in cv] + [vbuf[:, pl.ds(c0, BLOCK)]], axis=1)
        s = _dot(q, keys)
        new_ok = ((col >= 2 * BLOCK) & (new_col // t_len == (pair % chunk_pairs) * 2 + row_b)
                  & (new_col % t_len <= row_t))
        ok = cache_ok | new_ok
        s = jnp.where(ok, s, NEG)
        mx = jnp.maximum(jnp.max(s, axis=-1, keepdims=True), sink)
        e = jnp.where(ok, jnp.exp(s - mx), 0.0)
        denom = jnp.sum(e, axis=-1, keepdims=True) + jnp.exp(sink - mx)
        pv = _dot_t(e.astype(BF16), vals) * (1.0 / denom)
        for m in range(N_QCOL):
            obuf[pl.ds(r0, pair_rows), m * LANES:(m + 1) * LANES] = _merge_groups(pv, pair_rows, m)
        k_chunk = knew[:, pl.ds(c0, BLOCK)]
        v_chunk = vnew[:, pl.ds(c0, BLOCK)]
        keep = lax.broadcasted_iota(jnp.int32, (KV_WIDTH, BLOCK), 1) < BLOCK - t_len
        for bb in range(2):
            to_tail = (BLOCK - t_len) - t_len * ((pair % chunk_pairs) * 2 + bb)
            kout_ref[2 * i + bb] = jnp.where(keep, pltpu.roll(ck[bb], BLOCK - t_len, axis=1),
                                             pltpu.roll(k_chunk, to_tail, axis=1))
            vout_ref[2 * i + bb] = jnp.where(keep, pltpu.roll(cv[bb], BLOCK - t_len, axis=1),
                                             pltpu.roll(v_chunk, to_tail, axis=1))
        return carry

    for i in range(step_batch // 2):
        pair_step(i, 0)

    @pl.when(step == n_steps - 1)
    def _():
        u = uout_ref[...]
        tok = lax.broadcasted_iota(jnp.int32, u.shape, 0) % t_len
        um1 = jnp.where(tok >= 1, pltpu.roll(u, 1, axis=0), st1_ref[...])
        um2 = jnp.where(tok >= 2, pltpu.roll(u, 2, axis=0), st2_ref[...])
        cw = refs["conv_w"]
        conv = cw[0:1, :] * um2 + cw[1:2, :] * um1 + cw[2:3, :] * u
        def store(y):
            y_ref[...] = y

        _run(_tail_steps(x_ref[...], obuf[...], bgbuf[...] * conv, p_ref[...], refs, store))


def _rope_tables(pos):
    inv_freq = ROPE_THETA ** (-jnp.arange(0, HEAD_DIM, 2, dtype=F32) / HEAD_DIM)
    ang = pos.astype(F32)[:, None] * inv_freq[None, :]
    cos, sin = jnp.cos(ang), jnp.sin(ang)
    cos_h = jnp.concatenate([cos, cos], axis=1)
    sin_h = jnp.concatenate([-sin, sin], axis=1)
    return jnp.tile(cos_h, (1, LANES // HEAD_DIM)), jnp.tile(sin_h, (1, LANES // HEAD_DIM))


def _cache_to_kernel(c):
    batch, keys = c.shape[0], c.shape[1]
    return jnp.transpose(c, (0, 2, 3, 1)).reshape(batch, KV_WIDTH, keys)


def _cache_from_kernel(c):
    batch, _, keys = c.shape
    return jnp.transpose(c.reshape(batch, N_KV_HEADS, HEAD_DIM, keys), (0, 3, 1, 2))


def _block_diag_ones(width):
    idx = np.arange(width) // HEAD_DIM
    return jnp.asarray(idx[:, None] == idx[None, :], dtype=BF16)


def _interleave_groups(a, axis):
    shape = a.shape
    a = a.reshape(shape[:axis] + (N_KV_HEADS, GQA_GROUP, HEAD_DIM) + shape[axis + 1:])
    a = jnp.swapaxes(a, axis, axis + 1)
    return a.reshape(shape)


def _prepare_weights(g_mix_norm, w_in, g_q, g_k, sinks, conv_w, g_attn_out, g_conv_out, w_o,
                     g_mlp_norm, w_up, w_down, g_ple_norm, w_ple_gate, w_ple):
    w_in_p = jnp.concatenate([_interleave_groups(w_in[:, :ATTN_WIDTH], 1), w_in[:, ATTN_WIDTH:]], axis=1)
    w_o_p = jnp.concatenate([_interleave_groups(w_o[:ATTN_WIDTH], 0), w_o[ATTN_WIDTH:]], axis=0)
    row = lambda g: g.reshape(1, -1).astype(F32)
    tile_head = lambda g: jnp.tile(g.reshape(1, HEAD_DIM).astype(F32), (1, LANES // HEAD_DIM))
    weights = dict(
        g_mix=row(g_mix_norm), w_in=w_in_p.astype(BF16),
        bd_q=_block_diag_ones(ATTN_WIDTH), bd_k=_block_diag_ones(KV_WIDTH),
        g_q=tile_head(g_q) * (HEAD_DIM ** -0.5),
        g_k=tile_head(g_k), conv_w=conv_w.astype(F32),
        g_attn=row(_interleave_groups(g_attn_out, 0)), g_conv=row(g_conv_out), w_o=w_o_p.astype(BF16),
        g_mlp=row(g_mlp_norm), w_up=w_up.astype(BF16), w_down=w_down.astype(BF16),
        g_ple=row(g_ple_norm), w_gate=w_ple_gate.astype(BF16), w_ple=w_ple.astype(BF16))
    sinks_p = sinks.reshape(N_KV_HEADS, GQA_GROUP).T.reshape(N_Q_HEADS).astype(F32)
    return sinks_p, [weights[n] for n in WEIGHT_NAMES]


def _resident(a, n_grid):
    zeros = (0,) * a.ndim
    index_map = (lambda b, j: zeros) if n_grid == 2 else (lambda i: zeros)
    return pl.BlockSpec(a.shape, index_map, pipeline_mode=pl.Buffered(1))


def _prompt_layer(x, p, sinks_p, weights):
    batch, seq, _ = x.shape
    tm = SEQ_TILE
    cos, sin = _rope_tables(jnp.arange(seq, dtype=jnp.int32))
    tile = lambda w: pl.BlockSpec((None, tm, w), lambda b, j: (b, j, 0))
    per_batch = lambda r, w: pl.BlockSpec((None, r, w), lambda b, j: (b, 0, 0))
    table = pl.BlockSpec((tm, LANES), lambda b, j: (j, 0))
    in_specs = ([pl.BlockSpec(memory_space=pltpu.SMEM), tile(D_MODEL), tile(PLE_DIM), table, table]
                + [_resident(w, 2) for w in weights])
    out_shape = (jax.ShapeDtypeStruct((batch, seq, D_MODEL), F32),
                 jax.ShapeDtypeStruct((batch, BLOCK, KV_WIDTH), F32),
                 jax.ShapeDtypeStruct((batch, BLOCK, KV_WIDTH), F32),
                 jax.ShapeDtypeStruct((batch, CONV_K - 1, CONV_WIDTH), F32))
    out_specs = (tile(D_MODEL), per_batch(BLOCK, KV_WIDTH), per_batch(BLOCK, KV_WIDTH),
                 per_batch(CONV_K - 1, CONV_WIDTH))
    scratch = [pltpu.VMEM((BLOCK + tm, KV_WIDTH), BF16), pltpu.VMEM((KV_WIDTH, BLOCK + tm), BF16),
               pltpu.VMEM((SUBLANES + tm, CONV_WIDTH), F32)]
    return pl.pallas_call(
        _prompt_kernel, grid=(batch, seq // tm), in_specs=in_specs, out_specs=out_specs,
        out_shape=out_shape, scratch_shapes=scratch, name="prompt_layer",
        compiler_params=pltpu.CompilerParams(dimension_semantics=("arbitrary", "arbitrary"),
                                             vmem_limit_bytes=VMEM_LIMIT),
    )(sinks_p, x, p, cos, sin, *weights)


def _sample_layer(x, p, cache_k, cache_v, state_conv, sinks_p, weights):
    batch, t_len, _ = x.shape
    n_tok = batch * t_len
    cos, sin = _rope_tables(PAST_LEN + jnp.arange(t_len, dtype=jnp.int32))
    cos, sin = jnp.tile(cos, (batch, 1)), jnp.tile(sin, (batch, 1))
    st1 = jnp.pad(state_conv[:, 1:2], ((0, 0), (0, t_len - 1), (0, 0))).reshape(n_tok, CONV_WIDTH)
    st2 = jnp.pad(state_conv, ((0, 0), (0, t_len - (CONV_K - 1)), (0, 0))).reshape(n_tok, CONV_WIDTH)
    ck = _cache_to_kernel(cache_k)
    cv = _cache_to_kernel(cache_v)
    flat = [x.reshape(n_tok, D_MODEL), p.reshape(n_tok, PLE_DIM), cos, sin, st1, st2]
    cache_block = pl.BlockSpec((SAMPLE_STEP_BATCH, WINDOW, KV_WIDTH), lambda i: (i, 0, 0))
    in_specs = ([pl.BlockSpec(memory_space=pltpu.SMEM)] + [_resident(a, 1) for a in flat]
                + [cache_block, cache_block] + [_resident(w, 1) for w in weights])
    out_shape = (jax.ShapeDtypeStruct((n_tok, D_MODEL), F32),
                 jax.ShapeDtypeStruct((batch, WINDOW, KV_WIDTH), F32),
                 jax.ShapeDtypeStruct((batch, WINDOW, KV_WIDTH), F32),
                 jax.ShapeDtypeStruct((n_tok, CONV_WIDTH), F32))
    whole = lambda s: pl.BlockSpec(s.shape, lambda i: (0, 0))
    out_specs = (whole(out_shape[0]), cache_block, cache_block, whole(out_shape[3]))
    scratch = [pltpu.VMEM((N_Q_HEADS, n_tok, LANES), F32),
               pltpu.VMEM((KV_WIDTH, n_tok), BF16), pltpu.VMEM((KV_WIDTH, n_tok), BF16),
               pltpu.VMEM((KV_WIDTH, n_tok), F32), pltpu.VMEM((KV_WIDTH, n_tok), F32),
               pltpu.VMEM((n_tok, ATTN_WIDTH), F32), pltpu.VMEM((n_tok, CONV_WIDTH), F32)]
    y, k_new, v_new, u = pl.pallas_call(
        _sample_kernel, grid=(batch // SAMPLE_STEP_BATCH,), in_specs=in_specs, out_specs=out_specs,
        out_shape=out_shape, scratch_shapes=scratch, name="sample_layer",
        compiler_params=pltpu.CompilerParams(dimension_semantics=("arbitrary",),
                                             vmem_limit_bytes=VMEM_LIMIT),
    )(sinks_p, *flat, ck, cv, *weights)
    conv_new = u.reshape(batch, t_len, CONV_WIDTH)[:, t_len - (CONV_K - 1):]
    return y.reshape(batch, t_len, D_MODEL), k_new, v_new, conv_new


def kernel(x_prompt, x_sample, p_prompt, p_sample, cache_k, cache_v, state_conv, g_mix_norm, w_in, g_q, g_k,
           sinks, conv_w, g_attn_out, g_conv_out, w_o, g_mlp_norm, w_up, w_down, g_ple_norm, w_ple_gate, w_ple):
    depth = w_in.shape[0]
    yp, ys = x_prompt, x_sample
    outs = [[] for _ in range(6)]
    for i in range(depth):
        sinks_p, weights = _prepare_weights(
            g_mix_norm[i], w_in[i], g_q[i], g_k[i], sinks[i], conv_w[i], g_attn_out[i], g_conv_out[i],
            w_o[i], g_mlp_norm[i], w_up[i], w_down[i], g_ple_norm[i], w_ple_gate[i], w_ple[i])
        yp, kp, vp, cp = _prompt_layer(yp, p_prompt[i], sinks_p, weights)
        ys, ksn, vsn, csn = _sample_layer(ys, p_sample[i], cache_k[i], cache_v[i], state_conv[i],
                                          sinks_p, weights)
        batch = yp.shape[0]
        dec_batch = ys.shape[0]
        for lst, val in zip(outs, (_cache_from_kernel(kp), _cache_from_kernel(vp), cp,
                                   _cache_from_kernel(ksn), _cache_from_kernel(vsn), csn)):
            lst.append(val)
    return (yp, ys) + tuple(jnp.stack(o) for o in outs)
```

```python
import functools

import jax
import jax.numpy as jnp
import numpy as np
from jax import lax
from jax.experimental import pallas as pl
from jax.experimental.pallas import tpu as pltpu

D_MODEL = 1024
HEAD_DIM = 64
N_Q_HEADS = 8
N_KV_HEADS = 2
GQA_GROUP = N_Q_HEADS // N_KV_HEADS
ATTN_WIDTH = N_Q_HEADS * HEAD_DIM
KV_WIDTH = N_KV_HEADS * HEAD_DIM
CONV_WIDTH = D_MODEL - ATTN_WIDTH
CONV_K = 3
WINDOW = 128
BLOCK = 128
ROPE_THETA = 10000.0
D_FF = 4 * D_MODEL
PLE_DIM = 256
EPS = 1e-6
NEG = -1e30
PAST_LEN = 16384
IN_WIDTH = ATTN_WIDTH + 2 * KV_WIDTH + 3 * CONV_WIDTH

LANES = 128
SUBLANES = 8
N_QCOL = ATTN_WIDTH // LANES
SEQ_TILE = 512
SUB_TILE = 256
STAGE_LEAD = 2
FF_CHUNK = 1024
SAMPLE_STEP_BATCH = 16
VMEM_LIMIT = 56 * 1024 * 1024

O_K = ATTN_WIDTH
O_V = O_K + KV_WIDTH
O_B = O_V + KV_WIDTH
O_C = O_B + CONV_WIDTH
O_H = O_C + CONV_WIDTH

BF16 = jnp.bfloat16
F32 = jnp.float32


def _dot(a, b):
    return jnp.dot(a, b, preferred_element_type=F32)


def _dot_t(a, b):
    return lax.dot_general(a, b, (((1,), (1,)), ((), ())), preferred_element_type=F32)


def _rms(x, g):
    return x * lax.rsqrt(jnp.mean(x * x, axis=-1, keepdims=True) + EPS) * g


def _head_norm_rope(t, ones_bd, g, cos, sin):
    ssq = _dot((t * t).astype(BF16), ones_bd)
    t = t * lax.rsqrt(ssq * (1.0 / HEAD_DIM) + EPS)
    lane = lax.broadcasted_iota(jnp.int32, (t.shape[0], LANES), 1)
    first_half = (lane & (HEAD_DIM - 1)) < HEAD_DIM // 2
    cols = []
    for m in range(t.shape[1] // LANES):
        c = t[:, m * LANES:(m + 1) * LANES] * g
        up = pltpu.roll(c, LANES - HEAD_DIM // 2, axis=1)
        dn = pltpu.roll(c, HEAD_DIM // 2, axis=1)
        cols.append(c * cos + jnp.where(first_half, up, dn) * sin)
    return cols


def _front(x, cos, sin, refs):
    h = _rms(x, refs["g_mix"][...]).astype(BF16)
    z = _dot(h, refs["w_in"][...])
    qcols = _head_norm_rope(z[:, :O_K], refs["bd_q"][...], refs["g_q"][...], cos, sin)
    (k,) = _head_norm_rope(z[:, O_K:O_V], refs["bd_k"][...], refs["g_k"][...], cos, sin)
    v = z[:, O_V:O_B]
    b_gate = z[:, O_B:O_C]
    u = z[:, O_C:O_H] * z[:, O_H:]
    return qcols, k, v, b_gate, u


def _tail_steps(x, o_attn, o_conv, p, refs, store):
    mixed = jnp.concatenate([_rms(o_attn, refs["g_attn"][...]), _rms(o_conv, refs["g_conv"][...])], axis=1)
    x = x + _dot(mixed.astype(BF16), refs["w_o"][...])
    hm = _rms(x, refs["g_mlp"][...]).astype(BF16)
    yield
    for c in range(D_FF // FF_CHUNK):
        up = _dot(hm, refs["w_up"][:, c * FF_CHUNK:(c + 1) * FF_CHUNK])
        act = jnp.square(jnp.maximum(up, 0.0)).astype(BF16)
        x = x + _dot(act, refs["w_down"][c * FF_CHUNK:(c + 1) * FF_CHUNK, :])
        yield
    gate = jax.nn.sigmoid(_dot(_rms(x, refs["g_ple"][...]).astype(BF16), refs["w_gate"][...]))
    store(x + gate * _dot(p.astype(BF16), refs["w_ple"][...]))


def _run(steps):
    for _ in steps:
        pass


def _split_heads(col, c):
    group = (2 * c) // GQA_GROUP
    lo = lax.broadcasted_iota(jnp.int32, col.shape, 1) < HEAD_DIM
    swapped = pltpu.roll(col, HEAD_DIM, axis=1)
    if group == 0:
        return jnp.where(lo, col, 0.0), jnp.where(lo, swapped, 0.0)
    return jnp.where(lo, 0.0, swapped), jnp.where(lo, 0.0, col)


def _sink_column(sinks_ref, rows_per_head):
    return jnp.concatenate(
        [jnp.full((rows_per_head, 1), sinks_ref[i], F32) for i in range(N_Q_HEADS)], axis=0)


def _merge_heads(pv, rows_per_head, c):
    group = (2 * c) // GQA_GROUP
    lo = lax.broadcasted_iota(jnp.int32, (rows_per_head, LANES), 1) < HEAD_DIM
    a = pv[(2 * c) * rows_per_head:(2 * c + 1) * rows_per_head]
    b = pv[(2 * c + 1) * rows_per_head:(2 * c + 2) * rows_per_head]
    if group == 0:
        return jnp.where(lo, a, pltpu.roll(b, HEAD_DIM, axis=1))
    return jnp.where(lo, pltpu.roll(a, HEAD_DIM, axis=1), b)


WEIGHT_NAMES = ("g_mix", "w_in", "bd_q", "bd_k", "g_q", "g_k", "conv_w", "g_attn", "g_conv", "w_o",
                "g_mlp", "w_up", "w_down", "g_ple", "w_gate", "w_ple")


def _prompt_kernel(sinks_ref, x_ref, p_ref, cos_ref, sin_ref, *rest):
    nw = len(WEIGHT_NAMES)
    refs = dict(zip(WEIGHT_NAMES, rest[:nw]))
    y_ref, kout_ref, vout_ref, convout_ref, kbuf, vbuf, ubuf = rest[nw:]
    tm = x_ref.shape[0]
    j = pl.program_id(1)

    @pl.when(j == 0)
    def _():
        kbuf[0:BLOCK, :] = jnp.zeros((BLOCK, KV_WIDTH), BF16)
        vbuf[:, 0:BLOCK] = jnp.zeros((KV_WIDTH, BLOCK), BF16)
        ubuf[0:SUBLANES, :] = jnp.zeros((SUBLANES, CONV_WIDTH), F32)

    kj = lax.broadcasted_iota(jnp.int32, (BLOCK, N_Q_HEADS * BLOCK), 0)
    qi = lax.broadcasted_iota(jnp.int32, (BLOCK, N_Q_HEADS * BLOCK), 1) & (BLOCK - 1)
    own = kj <= qi
    sink = jnp.concatenate([jnp.full((1, BLOCK), sinks_ref[i], F32) for i in range(N_Q_HEADS)], axis=1)
    cw = refs["conv_w"]

    last = {}

    def sub_tile_steps(r0):
        rows = slice(r0, r0 + SUB_TILE)
        x = x_ref[rows, :]
        qcols, k, v, b_gate, u = _front(x, cos_ref[rows, :], sin_ref[rows, :], refs)
        kbuf[BLOCK + r0:BLOCK + r0 + SUB_TILE, :] = k.astype(BF16)
        v_t = v.T
        vbuf[:, BLOCK + r0:BLOCK + r0 + SUB_TILE] = v_t.astype(BF16)
        ubuf[SUBLANES + r0:SUBLANES + r0 + SUB_TILE, :] = u
        last.update(k=k, v_t=v_t, u=u)
        yield

        o_blocks = []
        for i in range(SUB_TILE // BLOCK):
            g = r0 // BLOCK + i
            qs = []
            for c in range(N_QCOL):
                qs.extend(_split_heads(qcols[c][i * BLOCK:(i + 1) * BLOCK], c))
            qstack = jnp.concatenate(qs, axis=0).astype(BF16)
            s2 = _dot_t(kbuf[g * BLOCK:(g + 2) * BLOCK, :], qstack)
            s_prev = s2[:BLOCK]
            if g == 0:
                s_prev = jnp.where(j == 0, NEG, s_prev)
            s = jnp.where(own, s2[BLOCK:], s_prev)
            mx = jnp.maximum(jnp.max(s, axis=0, keepdims=True), sink)
            e = jnp.exp(s - mx)
            denom = jnp.sum(e, axis=0, keepdims=True) + jnp.exp(sink - mx)
            e2 = jnp.concatenate([jnp.where(own, 0.0, e), jnp.where(own, e, 0.0)], axis=0).astype(BF16)
            pv = _dot(vbuf[:, g * BLOCK:(g + 2) * BLOCK], e2) * (1.0 / denom)
            cols = []
            for c in range(N_QCOL):
                d0 = (2 * c) // GQA_GROUP * HEAD_DIM
                col_t = jnp.concatenate([pv[d0:d0 + HEAD_DIM, (2 * c) * BLOCK:(2 * c + 1) * BLOCK],
                                         pv[d0:d0 + HEAD_DIM, (2 * c + 1) * BLOCK:(2 * c + 2) * BLOCK]], axis=0)
                cols.append(col_t.T)
            o_blocks.append(jnp.concatenate(cols, axis=1))
            yield
        o_attn = jnp.concatenate(o_blocks, axis=0)

        conv = (cw[0:1, :] * ubuf[SUBLANES - 2 + r0:SUBLANES - 2 + r0 + SUB_TILE, :]
                + cw[1:2, :] * ubuf[SUBLANES - 1 + r0:SUBLANES - 1 + r0 + SUB_TILE, :]
                + cw[2:3, :] * u)

        def store(y):
            y_ref[rows, :] = y

        yield from _tail_steps(x, o_attn, b_gate * conv, p_ref[rows, :], refs, store)

    tiles = [sub_tile_steps(r0) for r0 in range(0, tm, SUB_TILE)]
    done = [False] * len(tiles)
    rnd = 0
    while not all(done):
        for t, steps in enumerate(tiles):
            if rnd >= t * STAGE_LEAD and not done[t]:
                done[t] = next(steps, True) is True
        rnd += 1

    kout_ref[...] = last["k"][SUB_TILE - BLOCK:, :].T
    vout_ref[...] = last["v_t"][:, SUB_TILE - BLOCK:]
    convout_ref[...] = last["u"][SUB_TILE - (CONV_K - 1):, :]
    kbuf[0:BLOCK, :] = kbuf[tm:tm + BLOCK, :]
    vbuf[:, 0:BLOCK] = vbuf[:, tm:tm + BLOCK]
    ubuf[0:SUBLANES, :] = ubuf[tm:tm + SUBLANES, :]


def _sample_kernel(sinks_ref, x_ref, p_ref, cos_ref, sin_ref, st1_ref, st2_ref, ck_ref, cv_ref, *rest):
    nw = len(WEIGHT_NAMES)
    refs = dict(zip(WEIGHT_NAMES, rest[:nw]))
    y_ref, kout_ref, vout_ref, uout_ref, qbuf, kbuf, vbuf, knew, vnew, obuf, bgbuf = rest[nw:]
    t_len = x_ref.shape[1]
    n_tok = x_ref.shape[0] * t_len
    step = pl.program_id(0)
    n_steps = pl.num_programs(0)
    step_batch = ck_ref.shape[0]
    pair_rows = 2 * t_len
    chunk_pairs = BLOCK // pair_rows

    @pl.when(step == 0)
    def _():
        qcols, k, v, b_gate, u = _front(x_ref[...].reshape(n_tok, D_MODEL), cos_ref[...], sin_ref[...], refs)
        for c in range(N_QCOL):
            qbuf[2 * c], qbuf[2 * c + 1] = _split_heads(qcols[c], c)
        k_t, v_t = k.T, v.T
        kbuf[...] = k_t.astype(BF16)
        vbuf[...] = v_t.astype(BF16)
        knew[...] = k_t
        vnew[...] = v_t
        bgbuf[...] = b_gate
        uout_ref[...] = u

    n_rows = N_Q_HEADS * pair_rows
    row = lax.broadcasted_iota(jnp.int32, (n_rows, 3 * BLOCK), 0)
    col = lax.broadcasted_iota(jnp.int32, (n_rows, 3 * BLOCK), 1)
    row_b = (row % pair_rows) // t_len
    row_t = row % t_len
    cache_ok = (col < 2 * BLOCK) & (col // BLOCK == row_b) & ((col % BLOCK) > row_t)
    new_col = col - 2 * BLOCK
    sink = _sink_column(sinks_ref, pair_rows)

    def pair_step(i, carry):
        pair = step * (step_batch // 2) + i
        r0 = pl.multiple_of(pair * pair_rows, pair_rows)
        c0 = pl.multiple_of((pair // chunk_pairs) * BLOCK, BLOCK)
        q = qbuf[:, pl.ds(r0, pair_rows), :].reshape(n_rows, LANES).astype(BF16)
        ck = [ck_ref[2 * i + bb] for bb in range(2)]
        cv = [cv_ref[2 * i + bb] for bb in range(2)]
        keys = jnp.concatenate([c.astype(BF16) for c in ck] + [kbuf[:, pl.ds(c0, BLOCK)]], axis=1)
        vals = jnp.concatenate([c.astype(BF16) for c in cv] + [vbuf[:, pl.ds(c0, BLOCK)]], axis=1)
        s = _dot(q, keys)
        new_ok = ((col >= 2 * BLOCK) & (new_col // t_len == (pair % chunk_pairs) * 2 + row_b)
                  & (new_col % t_len <= row_t))
        ok = cache_ok | new_ok
        s = jnp.where(ok, s, NEG)
        mx = jnp.maximum(jnp.max(s, axis=-1, keepdims=True), sink)
        e = jnp.where(ok, jnp.exp(s - mx), 0.0)
        denom = jnp.sum(e, axis=-1, keepdims=True) + jnp.exp(sink - mx)
        pv = _dot_t(e.astype(BF16), vals) * (1.0 / denom)
        for c in range(N_QCOL):
            obuf[pl.ds(r0, pair_rows), c * LANES:(c + 1) * LANES] = _merge_heads(pv, pair_rows, c)
        k_chunk = knew[:, pl.ds(c0, BLOCK)]
        v_chunk = vnew[:, pl.ds(c0, BLOCK)]
        keep = lax.broadcasted_iota(jnp.int32, (KV_WIDTH, BLOCK), 1) < BLOCK - t_len
        for bb in range(2):
            to_tail = (BLOCK - t_len) - t_len * ((pair % chunk_pairs) * 2 + bb)
            kout_ref[2 * i + bb] = jnp.where(keep, pltpu.roll(ck[bb], BLOCK - t_len, axis=1),
                                             pltpu.roll(k_chunk, to_tail, axis=1))
            vout_ref[2 * i + bb] = jnp.where(keep, pltpu.roll(cv[bb], BLOCK - t_len, axis=1),
                                             pltpu.roll(v_chunk, to_tail, axis=1))
        return carry

    for i in range(step_batch // 2):
        pair_step(i, 0)

    @pl.when(step == n_steps - 1)
    def _():
        u = uout_ref[...]
        tok = lax.broadcasted_iota(jnp.int32, u.shape, 0) % t_len
        um1 = jnp.where(tok >= 1, pltpu.roll(u, 1, axis=0), st1_ref[...])
        um2 = jnp.where(tok >= 2, pltpu.roll(u, 2, axis=0), st2_ref[...])
        cw = refs["conv_w"]
        conv = cw[0:1, :] * um2 + cw[1:2, :] * um1 + cw[2:3, :] * u
        def store(y):
            y_ref[...] = y.reshape(y_ref.shape)

        _run(_tail_steps(x_ref[...].reshape(n_tok, D_MODEL), obuf[...], bgbuf[...] * conv,
                         p_ref[...].reshape(n_tok, PLE_DIM), refs, store))


def _rope_tables(pos):
    inv_freq = ROPE_THETA ** (-jnp.arange(0, HEAD_DIM, 2, dtype=F32) / HEAD_DIM)
    ang = pos.astype(F32)[:, None] * inv_freq[None, :]
    cos, sin = jnp.cos(ang), jnp.sin(ang)
    cos_h = jnp.concatenate([cos, cos], axis=1)
    sin_h = jnp.concatenate([-sin, sin], axis=1)
    return jnp.tile(cos_h, (1, LANES // HEAD_DIM)), jnp.tile(sin_h, (1, LANES // HEAD_DIM))


def _cache_to_kernel(c):
    batch, keys = c.shape[0], c.shape[1]
    return jnp.transpose(c, (0, 2, 3, 1)).reshape(batch, KV_WIDTH, keys)


def _cache_from_kernel(c):
    batch, _, keys = c.shape
    return jnp.transpose(c.reshape(batch, N_KV_HEADS, HEAD_DIM, keys), (0, 3, 1, 2))


def _block_diag_ones(width):
    idx = np.arange(width) // HEAD_DIM
    return jnp.asarray(idx[:, None] == idx[None, :], dtype=BF16)


def _prepare_weights(g_mix_norm, w_in, g_q, g_k, sinks, conv_w, g_attn_out, g_conv_out, w_o,
                     g_mlp_norm, w_up, w_down, g_ple_norm, w_ple_gate, w_ple):
    row = lambda g: g.reshape(1, -1).astype(F32)
    tile_head = lambda g: jnp.tile(g.reshape(1, HEAD_DIM).astype(F32), (1, LANES // HEAD_DIM))
    weights = dict(
        g_mix=row(g_mix_norm), w_in=w_in.astype(BF16),
        bd_q=_block_diag_ones(ATTN_WIDTH), bd_k=_block_diag_ones(KV_WIDTH),
        g_q=tile_head(g_q) * (HEAD_DIM ** -0.5),
        g_k=tile_head(g_k), conv_w=conv_w.astype(F32),
        g_attn=row(g_attn_out), g_conv=row(g_conv_out), w_o=w_o.astype(BF16),
        g_mlp=row(g_mlp_norm), w_up=w_up.astype(BF16), w_down=w_down.astype(BF16),
        g_ple=row(g_ple_norm), w_gate=w_ple_gate.astype(BF16), w_ple=w_ple.astype(BF16))
    return sinks.astype(F32), [weights[n] for n in WEIGHT_NAMES]


def _resident(a, n_grid):
    zeros = (0,) * a.ndim
    index_map = (lambda b, j: zeros) if n_grid == 2 else (lambda i: zeros)
    return pl.BlockSpec(a.shape, index_map, pipeline_mode=pl.Buffered(1))


def _prompt_layer(x, p, sinks_p, weights):
    batch, seq, _ = x.shape
    tm = SEQ_TILE
    cos, sin = _rope_tables(jnp.arange(seq, dtype=jnp.int32))
    tile = lambda w: pl.BlockSpec((None, tm, w), lambda b, j: (b, j, 0))
    per_batch = lambda r, w: pl.BlockSpec((None, r, w), lambda b, j: (b, 0, 0))
    table = pl.BlockSpec((tm, LANES), lambda b, j: (j, 0))
    in_specs = ([pl.BlockSpec(memory_space=pltpu.SMEM), tile(D_MODEL), tile(PLE_DIM), table, table]
                + [_resident(w, 2) for w in weights])
    out_shape = (jax.ShapeDtypeStruct((batch, seq, D_MODEL), F32),
                 jax.ShapeDtypeStruct((batch, BLOCK, KV_WIDTH), F32),
                 jax.ShapeDtypeStruct((batch, BLOCK, KV_WIDTH), F32),
                 jax.ShapeDtypeStruct((batch, CONV_K - 1, CONV_WIDTH), F32))
    out_specs = (tile(D_MODEL), per_batch(BLOCK, KV_WIDTH), per_batch(BLOCK, KV_WIDTH),
                 per_batch(CONV_K - 1, CONV_WIDTH))
    scratch = [pltpu.VMEM((BLOCK + tm, KV_WIDTH), BF16), pltpu.VMEM((KV_WIDTH, BLOCK + tm), BF16),
               pltpu.VMEM((SUBLANES + tm, CONV_WIDTH), F32)]
    return pl.pallas_call(
        _prompt_kernel, grid=(batch, seq // tm), in_specs=in_specs, out_specs=out_specs,
        out_shape=out_shape, scratch_shapes=scratch, name="prompt_layer",
        compiler_params=pltpu.CompilerParams(dimension_semantics=("arbitrary", "arbitrary"),
                                             vmem_limit_bytes=VMEM_LIMIT),
    )(sinks_p, x, p, cos, sin, *weights)


def _sample_layer(x, p, cache_k, cache_v, state_conv, sinks_p, weights):
    batch, t_len, _ = x.shape
    n_tok = batch * t_len
    cos, sin = _rope_tables(PAST_LEN + jnp.arange(t_len, dtype=jnp.int32))
    cos, sin = jnp.tile(cos, (batch, 1)), jnp.tile(sin, (batch, 1))
    st1 = jnp.pad(state_conv[:, 1:2], ((0, 0), (0, t_len - 1), (0, 0))).reshape(n_tok, CONV_WIDTH)
    st2 = jnp.pad(state_conv, ((0, 0), (0, t_len - (CONV_K - 1)), (0, 0))).reshape(n_tok, CONV_WIDTH)
    ck = _cache_to_kernel(cache_k)
    cv = _cache_to_kernel(cache_v)
    flat = [x, p, cos, sin, st1, st2]
    cache_block = pl.BlockSpec((SAMPLE_STEP_BATCH, WINDOW, KV_WIDTH), lambda i: (i, 0, 0))
    in_specs = ([pl.BlockSpec(memory_space=pltpu.SMEM)] + [_resident(a, 1) for a in flat]
                + [cache_block, cache_block] + [_resident(w, 1) for w in weights])
    out_shape = (jax.ShapeDtypeStruct((batch, t_len, D_MODEL), F32),
                 jax.ShapeDtypeStruct((batch, WINDOW, KV_WIDTH), F32),
                 jax.ShapeDtypeStruct((batch, WINDOW, KV_WIDTH), F32),
                 jax.ShapeDtypeStruct((n_tok, CONV_WIDTH), F32))
    whole = lambda s: pl.BlockSpec(s.shape, lambda i: (0,) * len(s.shape))
    out_specs = (whole(out_shape[0]), cache_block, cache_block, whole(out_shape[3]))
    scratch = [pltpu.VMEM((N_Q_HEADS, n_tok, LANES), F32),
               pltpu.VMEM((KV_WIDTH, n_tok), BF16), pltpu.VMEM((KV_WIDTH, n_tok), BF16),
               pltpu.VMEM((KV_WIDTH, n_tok), F32), pltpu.VMEM((KV_WIDTH, n_tok), F32),
               pltpu.VMEM((n_tok, ATTN_WIDTH), F32), pltpu.VMEM((n_tok, CONV_WIDTH), F32)]
    y, k_new, v_new, u = pl.pallas_call(
        _sample_kernel, grid=(batch // SAMPLE_STEP_BATCH,), in_specs=in_specs, out_specs=out_specs,
        out_shape=out_shape, scratch_shapes=scratch, name="sample_layer",
        compiler_params=pltpu.CompilerParams(dimension_semantics=("arbitrary",),
                                             vmem_limit_bytes=VMEM_LIMIT),
    )(sinks_p, *flat, ck, cv, *weights)
    conv_new = u.reshape(batch, t_len, CONV_WIDTH)[:, t_len - (CONV_K - 1):]
    return y, k_new, v_new, conv_new


def kernel(x_prompt, x_sample, p_prompt, p_sample, cache_k, cache_v, state_conv, g_mix_norm, w_in, g_q, g_k,
           sinks, conv_w, g_attn_out, g_conv_out, w_o, g_mlp_norm, w_up, w_down, g_ple_norm, w_ple_gate, w_ple):
    depth = w_in.shape[0]
    yp, ys = x_prompt, x_sample
    outs = [[] for _ in range(6)]
    for i in range(depth):
        sinks_p, weights = _prepare_weights(
            g_mix_norm[i], w_in[i], g_q[i], g_k[i], sinks[i], conv_w[i], g_attn_out[i], g_conv_out[i],
            w_o[i], g_mlp_norm[i], w_up[i], w_down[i], g_ple_norm[i], w_ple_gate[i], w_ple[i])
        yp, kp, vp, cp = _prompt_layer(yp, p_prompt[i], sinks_p, weights)
        ys, ksn, vsn, csn = _sample_layer(ys, p_sample[i], cache_k[i], cache_v[i], state_conv[i],
                                          sinks_p, weights)
        batch = yp.shape[0]
        dec_batch = ys.shape[0]
        for lst, val in zip(outs, (_cache_from_kernel(kp), _cache_from_kernel(vp), cp,
                                   _cache_from_kernel(ksn), _cache_from_kernel(vsn), csn)):
            lst.append(val)
    return (yp, ys) + tuple(jnp.stack(o) for o in outs)
```

```python
import jax
import jax.numpy as jnp
import numpy as np
from jax import lax
from jax.experimental import pallas as pl
from jax.experimental.pallas import tpu as pltpu

D_MODEL = 1024
HEAD_DIM = 64
N_Q_HEADS = 8
N_KV_HEADS = 2
GQA_GROUP = N_Q_HEADS // N_KV_HEADS
ATTN_WIDTH = N_Q_HEADS * HEAD_DIM
KV_WIDTH = N_KV_HEADS * HEAD_DIM
CONV_WIDTH = D_MODEL - ATTN_WIDTH
CONV_K = 3
WINDOW = 128
BLOCK = 128
ROPE_THETA = 10000.0
D_FF = 4 * D_MODEL
PLE_DIM = 256
EPS = 1e-6
NEG = -1e30
LOG2_E = 1.4426950408889634
PAST_LEN = 16384
IN_WIDTH = ATTN_WIDTH + 2 * KV_WIDTH + 3 * CONV_WIDTH

LANES = 128
SUBLANES = 8
N_QCOL = ATTN_WIDTH // LANES
SEQ_TILE = 512
SUB_TILE = 256
STAGE_LEAD = 2
FF_CHUNK = 1024
SAMPLE_STEP_BATCH = 16
VMEM_LIMIT = 56 * 1024 * 1024

O_K = ATTN_WIDTH
O_V = O_K + KV_WIDTH
O_B = O_V + KV_WIDTH
O_C = O_B + CONV_WIDTH
O_H = O_C + CONV_WIDTH

BF16 = jnp.bfloat16
F32 = jnp.float32


def _dot(a, b):
    return jnp.dot(a, b, preferred_element_type=F32)


def _dot_t(a, b):
    return lax.dot_general(a, b, (((1,), (1,)), ((), ())), preferred_element_type=F32)


def _rms(x):
    return x * lax.rsqrt(jnp.mean(x * x, axis=-1, keepdims=True) + EPS)


def _head_norm_rope(t, ones_bd, g, cos, sin):
    ssq = _dot((t * t).astype(BF16), ones_bd)
    t = t * lax.rsqrt(ssq * (1.0 / HEAD_DIM) + EPS)
    lane = lax.broadcasted_iota(jnp.int32, (t.shape[0], LANES), 1)
    first_half = (lane & (HEAD_DIM - 1)) < HEAD_DIM // 2
    cols = []
    for m in range(t.shape[1] // LANES):
        c = t[:, m * LANES:(m + 1) * LANES] * g
        up = pltpu.roll(c, LANES - HEAD_DIM // 2, axis=1)
        dn = pltpu.roll(c, HEAD_DIM // 2, axis=1)
        cols.append(c * cos + jnp.where(first_half, up, dn) * sin)
    return cols


def _front(x, cos, sin, refs):
    h = _rms(x).astype(BF16)
    z = _dot(h, refs["w_in"][...])
    qcols = _head_norm_rope(z[:, :O_K], refs["bd_q"][...], refs["g_q"][...], cos, sin)
    (k,) = _head_norm_rope(z[:, O_K:O_V], refs["bd_k"][...], refs["g_k"][...], cos, sin)
    v = z[:, O_V:O_B]
    b_gate = z[:, O_B:O_C]
    u = z[:, O_C:O_H] * z[:, O_H:]
    return qcols, k, v, b_gate, u


def _tail_steps(x, o_attn, o_conv, p, refs, store):
    mixed = jnp.concatenate([_rms(o_attn), _rms(o_conv)], axis=1)
    x = x + _dot(mixed.astype(BF16), refs["w_o"][...])
    hm = _rms(x).astype(BF16)
    yield
    for c in range(D_FF // FF_CHUNK):
        up = _dot(hm, refs["w_up"][:, c * FF_CHUNK:(c + 1) * FF_CHUNK])
        act = jnp.square(jnp.maximum(up.astype(BF16), 0.0))
        x = x + _dot(act, refs["w_down"][c * FF_CHUNK:(c + 1) * FF_CHUNK, :])
        yield
    gate = jax.nn.sigmoid(_dot(_rms(x).astype(BF16), refs["w_gate"][...]))
    store(x + gate * _dot(p.astype(BF16), refs["w_ple"][...]))


def _run(steps):
    for _ in steps:
        pass


def _split_heads(col, c):
    group = (2 * c) // GQA_GROUP
    lo = lax.broadcasted_iota(jnp.int32, col.shape, 1) < HEAD_DIM
    swapped = pltpu.roll(col, HEAD_DIM, axis=1)
    if group == 0:
        return jnp.where(lo, col, 0.0), jnp.where(lo, swapped, 0.0)
    return jnp.where(lo, 0.0, swapped), jnp.where(lo, 0.0, col)


def _sink_column(sinks_ref, rows_per_head):
    return jnp.concatenate(
        [jnp.full((rows_per_head, 1), sinks_ref[i] * LOG2_E, F32) for i in range(N_Q_HEADS)], axis=0)


def _merge_heads(pv, rows_per_head, c):
    group = (2 * c) // GQA_GROUP
    lo = lax.broadcasted_iota(jnp.int32, (rows_per_head, LANES), 1) < HEAD_DIM
    a = pv[(2 * c) * rows_per_head:(2 * c + 1) * rows_per_head]
    b = pv[(2 * c + 1) * rows_per_head:(2 * c + 2) * rows_per_head]
    if group == 0:
        return jnp.where(lo, a, pltpu.roll(b, HEAD_DIM, axis=1))
    return jnp.where(lo, pltpu.roll(a, HEAD_DIM, axis=1), b)


WEIGHT_NAMES = ("w_in", "bd_q", "bd_k", "g_q", "g_k", "conv_w", "w_o", "w_up", "w_down", "w_gate", "w_ple")


def _prompt_kernel(sinks_ref, x_ref, p_ref, cos_ref, sin_ref, *rest):
    nw = len(WEIGHT_NAMES)
    refs = dict(zip(WEIGHT_NAMES, rest[:nw]))
    y_ref, kout_ref, vout_ref, convout_ref, kbuf, vbuf, ubuf = rest[nw:]
    tm = x_ref.shape[0]
    j = pl.program_id(1)

    @pl.when(j == 0)
    def _():
        kbuf[0:BLOCK, :] = jnp.zeros((BLOCK, KV_WIDTH), BF16)
        vbuf[:, 0:BLOCK] = jnp.zeros((KV_WIDTH, BLOCK), BF16)
        ubuf[0:SUBLANES, :] = jnp.zeros((SUBLANES, CONV_WIDTH), F32)

    kj = lax.broadcasted_iota(jnp.int32, (BLOCK, N_Q_HEADS * BLOCK), 0)
    qi = lax.broadcasted_iota(jnp.int32, (BLOCK, N_Q_HEADS * BLOCK), 1) & (BLOCK - 1)
    own = kj <= qi
    sink = jnp.concatenate([jnp.full((1, BLOCK), sinks_ref[i] * LOG2_E, F32) for i in range(N_Q_HEADS)],
                           axis=1)
    cw = refs["conv_w"]

    last = {}

    def sub_tile_steps(r0):
        rows = slice(r0, r0 + SUB_TILE)
        x = x_ref[rows, :]
        qcols, k, v, b_gate, u = _front(x, cos_ref[rows, :], sin_ref[rows, :], refs)
        kbuf[BLOCK + r0:BLOCK + r0 + SUB_TILE, :] = k.astype(BF16)
        v_t = v.T
        vbuf[:, BLOCK + r0:BLOCK + r0 + SUB_TILE] = v_t.astype(BF16)
        ubuf[SUBLANES + r0:SUBLANES + r0 + SUB_TILE, :] = u
        last.update(k=k, v_t=v_t, u=u)
        yield

        o_blocks = []
        for i in range(SUB_TILE // BLOCK):
            g = r0 // BLOCK + i
            qs = []
            for c in range(N_QCOL):
                qs.extend(_split_heads(qcols[c][i * BLOCK:(i + 1) * BLOCK], c))
            qstack = jnp.concatenate(qs, axis=0).astype(BF16)
            s2 = _dot_t(kbuf[g * BLOCK:(g + 2) * BLOCK, :], qstack)
            s_prev = s2[:BLOCK]
            if g == 0:
                s_prev = jnp.where(j == 0, NEG, s_prev)
            s = jnp.where(own, s2[BLOCK:], s_prev)
            mx = jnp.maximum(jnp.max(s, axis=0, keepdims=True), sink)
            e = jnp.exp2(s - mx)
            denom = jnp.sum(e, axis=0, keepdims=True) + jnp.exp2(sink - mx)
            e2 = jnp.concatenate([jnp.where(own, 0.0, e), jnp.where(own, e, 0.0)], axis=0).astype(BF16)
            pv = _dot(vbuf[:, g * BLOCK:(g + 2) * BLOCK], e2) * (1.0 / denom)
            cols = []
            for c in range(N_QCOL):
                d0 = (2 * c) // GQA_GROUP * HEAD_DIM
                col_t = jnp.concatenate([pv[d0:d0 + HEAD_DIM, (2 * c) * BLOCK:(2 * c + 1) * BLOCK],
                                         pv[d0:d0 + HEAD_DIM, (2 * c + 1) * BLOCK:(2 * c + 2) * BLOCK]], axis=0)
                cols.append(col_t.T)
            o_blocks.append(jnp.concatenate(cols, axis=1))
            yield
        o_attn = jnp.concatenate(o_blocks, axis=0)

        conv = (cw[0:1, :] * ubuf[SUBLANES - 2 + r0:SUBLANES - 2 + r0 + SUB_TILE, :]
                + cw[1:2, :] * ubuf[SUBLANES - 1 + r0:SUBLANES - 1 + r0 + SUB_TILE, :]
                + cw[2:3, :] * u)

        def store(y):
            y_ref[rows, :] = y

        yield from _tail_steps(x, o_attn, b_gate * conv, p_ref[rows, :], refs, store)

    tiles = [sub_tile_steps(r0) for r0 in range(0, tm, SUB_TILE)]
    done = [False] * len(tiles)
    rnd = 0
    while not all(done):
        for t, steps in enumerate(tiles):
            if rnd >= t * STAGE_LEAD and not done[t]:
                done[t] = next(steps, True) is True
        rnd += 1

    kout_ref[...] = last["k"][SUB_TILE - BLOCK:, :].T
    vout_ref[...] = last["v_t"][:, SUB_TILE - BLOCK:]
    convout_ref[...] = last["u"][SUB_TILE - (CONV_K - 1):, :]
    kbuf[0:BLOCK, :] = kbuf[tm:tm + BLOCK, :]
    vbuf[:, 0:BLOCK] = vbuf[:, tm:tm + BLOCK]
    ubuf[0:SUBLANES, :] = ubuf[tm:tm + SUBLANES, :]


def _sample_kernel(sinks_ref, x_ref, p_ref, cos_ref, sin_ref, st1_ref, st2_ref, ck_ref, cv_ref, *rest):
    nw = len(WEIGHT_NAMES)
    refs = dict(zip(WEIGHT_NAMES, rest[:nw]))
    y_ref, kout_ref, vout_ref, uout_ref, qbuf, kbuf, vbuf, knew, vnew, obuf, bgbuf = rest[nw:]
    t_len = x_ref.shape[1]
    n_tok = x_ref.shape[0] * t_len
    step = pl.program_id(0)
    n_steps = pl.num_programs(0)
    step_batch = ck_ref.shape[0]
    pair_rows = 2 * t_len
    chunk_pairs = BLOCK // pair_rows

    @pl.when(step == 0)
    def _():
        qcols, k, v, b_gate, u = _front(x_ref[...].reshape(n_tok, D_MODEL), cos_ref[...], sin_ref[...], refs)
        for c in range(N_QCOL):
            qbuf[2 * c], qbuf[2 * c + 1] = _split_heads(qcols[c], c)
        k_t, v_t = k.T, v.T
        kbuf[...] = k_t.astype(BF16)
        vbuf[...] = v_t.astype(BF16)
        knew[...] = k_t
        vnew[...] = v_t
        bgbuf[...] = b_gate
        uout_ref[...] = u

    n_rows = N_Q_HEADS * pair_rows
    row = lax.broadcasted_iota(jnp.int32, (n_rows, 3 * BLOCK), 0)
    col = lax.broadcasted_iota(jnp.int32, (n_rows, 3 * BLOCK), 1)
    row_b = (row % pair_rows) // t_len
    row_t = row % t_len
    cache_ok = (col < 2 * BLOCK) & (col // BLOCK == row_b) & ((col % BLOCK) > row_t)
    new_col = col - 2 * BLOCK
    sink = _sink_column(sinks_ref, pair_rows)

    def pair_step(i, carry):
        pair = step * (step_batch // 2) + i
        r0 = pl.multiple_of(pair * pair_rows, pair_rows)
        c0 = pl.multiple_of((pair // chunk_pairs) * BLOCK, BLOCK)
        q = qbuf[:, pl.ds(r0, pair_rows), :].reshape(n_rows, LANES).astype(BF16)
        ck = [ck_ref[2 * i + bb] for bb in range(2)]
        cv = [cv_ref[2 * i + bb] for bb in range(2)]
        keys = jnp.concatenate([c.astype(BF16) for c in ck] + [kbuf[:, pl.ds(c0, BLOCK)]], axis=1)
        vals = jnp.concatenate([c.astype(BF16) for c in cv] + [vbuf[:, pl.ds(c0, BLOCK)]], axis=1)
        s = _dot(q, keys)
        new_ok = ((col >= 2 * BLOCK) & (new_col // t_len == (pair % chunk_pairs) * 2 + row_b)
                  & (new_col % t_len <= row_t))
        ok = cache_ok | new_ok
        s = jnp.where(ok, s, NEG)
        mx = jnp.maximum(jnp.max(s, axis=-1, keepdims=True), sink)
        e = jnp.where(ok, jnp.exp2(s - mx), 0.0)
        denom = jnp.sum(e, axis=-1, keepdims=True) + jnp.exp2(sink - mx)
        pv = _dot_t(e.astype(BF16), vals) * (1.0 / denom)
        for c in range(N_QCOL):
            obuf[pl.ds(r0, pair_rows), c * LANES:(c + 1) * LANES] = _merge_heads(pv, pair_rows, c)
        k_chunk = knew[:, pl.ds(c0, BLOCK)]
        v_chunk = vnew[:, pl.ds(c0, BLOCK)]
        keep = lax.broadcasted_iota(jnp.int32, (KV_WIDTH, BLOCK), 1) < BLOCK - t_len
        for bb in range(2):
            to_tail = (BLOCK - t_len) - t_len * ((pair % chunk_pairs) * 2 + bb)
            kout_ref[2 * i + bb] = jnp.where(keep, pltpu.roll(ck[bb], BLOCK - t_len, axis=1),
                                             pltpu.roll(k_chunk, to_tail, axis=1))
            vout_ref[2 * i + bb] = jnp.where(keep, pltpu.roll(cv[bb], BLOCK - t_len, axis=1),
                                             pltpu.roll(v_chunk, to_tail, axis=1))
        return carry

    for i in range(step_batch // 2):
        pair_step(i, 0)

    @pl.when(step == n_steps - 1)
    def _():
        u = uout_ref[...]
        tok = lax.broadcasted_iota(jnp.int32, u.shape, 0) % t_len
        um1 = jnp.where(tok >= 1, pltpu.roll(u, 1, axis=0), st1_ref[...])
        um2 = jnp.where(tok >= 2, pltpu.roll(u, 2, axis=0), st2_ref[...])
        cw = refs["conv_w"]
        conv = cw[0:1, :] * um2 + cw[1:2, :] * um1 + cw[2:3, :] * u

        def store(y):
            y_ref[...] = y.reshape(y_ref.shape)

        _run(_tail_steps(x_ref[...].reshape(n_tok, D_MODEL), obuf[...], bgbuf[...] * conv,
                         p_ref[...].reshape(n_tok, PLE_DIM), refs, store))


def _rope_tables(pos):
    inv_freq = ROPE_THETA ** (-jnp.arange(0, HEAD_DIM, 2, dtype=F32) / HEAD_DIM)
    ang = pos.astype(F32)[:, None] * inv_freq[None, :]
    cos, sin = jnp.cos(ang), jnp.sin(ang)
    cos_h = jnp.concatenate([cos, cos], axis=1)
    sin_h = jnp.concatenate([-sin, sin], axis=1)
    return jnp.tile(cos_h, (1, LANES // HEAD_DIM)), jnp.tile(sin_h, (1, LANES // HEAD_DIM))


def _cache_to_kernel(c):
    batch, keys = c.shape[0], c.shape[1]
    return jnp.transpose(c, (0, 2, 3, 1)).reshape(batch, KV_WIDTH, keys)


def _cache_from_kernel(c):
    batch, _, keys = c.shape
    return jnp.transpose(c.reshape(batch, N_KV_HEADS, HEAD_DIM, keys), (0, 3, 1, 2))


def _block_diag_ones(width):
    idx = np.arange(width) // HEAD_DIM
    return jnp.asarray(idx[:, None] == idx[None, :], dtype=BF16)


def _prepare_weights(g_mix_norm, w_in, g_q, g_k, sinks, conv_w, g_attn_out, g_conv_out, w_o,
                     g_mlp_norm, w_up, w_down, g_ple_norm, w_ple_gate, w_ple):
    fold = lambda g, w: (g.astype(F32)[:, None] * w).astype(BF16)
    tile_head = lambda g: jnp.tile(g.reshape(1, HEAD_DIM).astype(F32), (1, LANES // HEAD_DIM))
    weights = dict(
        w_in=fold(g_mix_norm, w_in),
        bd_q=_block_diag_ones(ATTN_WIDTH), bd_k=_block_diag_ones(KV_WIDTH),
        g_q=tile_head(g_q) * (HEAD_DIM ** -0.5 * LOG2_E), g_k=tile_head(g_k), conv_w=conv_w.astype(F32),
        w_o=fold(jnp.concatenate([g_attn_out, g_conv_out]), w_o),
        w_up=fold(g_mlp_norm, w_up), w_down=w_down.astype(BF16),
        w_gate=fold(g_ple_norm, w_ple_gate), w_ple=w_ple.astype(BF16))
    return sinks.astype(F32), [weights[n] for n in WEIGHT_NAMES]


def _resident(a, n_grid):
    zeros = (0,) * a.ndim
    index_map = (lambda b, j: zeros) if n_grid == 2 else (lambda i: zeros)
    return pl.BlockSpec(a.shape, index_map, pipeline_mode=pl.Buffered(1))


def _prompt_layer(x, p, sinks_p, weights):
    batch, seq, _ = x.shape
    tm = SEQ_TILE
    cos, sin = _rope_tables(jnp.arange(seq, dtype=jnp.int32))
    tile = lambda w: pl.BlockSpec((None, tm, w), lambda b, j: (b, j, 0))
    per_batch = lambda r, w: pl.BlockSpec((None, r, w), lambda b, j: (b, 0, 0))
    table = pl.BlockSpec((tm, LANES), lambda b, j: (j, 0))
    in_specs = ([pl.BlockSpec(memory_space=pltpu.SMEM), tile(D_MODEL), tile(PLE_DIM), table, table]
                + [_resident(w, 2) for w in weights])
    out_shape = (jax.ShapeDtypeStruct((batch, seq, D_MODEL), F32),
                 jax.ShapeDtypeStruct((batch, BLOCK, KV_WIDTH), F32),
                 jax.ShapeDtypeStruct((batch, BLOCK, KV_WIDTH), F32),
                 jax.ShapeDtypeStruct((batch, CONV_K - 1, CONV_WIDTH), F32))
    out_specs = (tile(D_MODEL), per_batch(BLOCK, KV_WIDTH), per_batch(BLOCK, KV_WIDTH),
                 per_batch(CONV_K - 1, CONV_WIDTH))
    scratch = [pltpu.VMEM((BLOCK + tm, KV_WIDTH), BF16), pltpu.VMEM((KV_WIDTH, BLOCK + tm), BF16),
               pltpu.VMEM((SUBLANES + tm, CONV_WIDTH), F32)]
    return pl.pallas_call(
        _prompt_kernel, grid=(batch, seq // tm), in_specs=in_specs, out_specs=out_specs,
        out_shape=out_shape, scratch_shapes=scratch, name="prompt_layer",
        compiler_params=pltpu.CompilerParams(dimension_semantics=("arbitrary", "arbitrary"),
                                             vmem_limit_bytes=VMEM_LIMIT),
    )(sinks_p, x, p, cos, sin, *weights)


def _sample_layer(x, p, cache_k, cache_v, state_conv, sinks_p, weights):
    batch, t_len, _ = x.shape
    n_tok = batch * t_len
    cos, sin = _rope_tables(PAST_LEN + jnp.arange(t_len, dtype=jnp.int32))
    cos, sin = jnp.tile(cos, (batch, 1)), jnp.tile(sin, (batch, 1))
    st1 = jnp.pad(state_conv[:, 1:2], ((0, 0), (0, t_len - 1), (0, 0))).reshape(n_tok, CONV_WIDTH)
    st2 = jnp.pad(state_conv, ((0, 0), (0, t_len - (CONV_K - 1)), (0, 0))).reshape(n_tok, CONV_WIDTH)
    ck = _cache_to_kernel(cache_k)
    cv = _cache_to_kernel(cache_v)
    flat = [x, p, cos, sin, st1, st2]
    cache_block = pl.BlockSpec((SAMPLE_STEP_BATCH, WINDOW, KV_WIDTH), lambda i: (i, 0, 0))
    in_specs = ([pl.BlockSpec(memory_space=pltpu.SMEM)] + [_resident(a, 1) for a in flat]
                + [cache_block, cache_block] + [_resident(w, 1) for w in weights])
    out_shape = (jax.ShapeDtypeStruct((batch, t_len, D_MODEL), F32),
                 jax.ShapeDtypeStruct((batch, WINDOW, KV_WIDTH), F32),
                 jax.ShapeDtypeStruct((batch, WINDOW, KV_WIDTH), F32),
                 jax.ShapeDtypeStruct((n_tok, CONV_WIDTH), F32))
    whole = lambda s: pl.BlockSpec(s.shape, lambda i: (0,) * len(s.shape))
    out_specs = (whole(out_shape[0]), cache_block, cache_block, whole(out_shape[3]))
    scratch = [pltpu.VMEM((N_Q_HEADS, n_tok, LANES), F32),
               pltpu.VMEM((KV_WIDTH, n_tok), BF16), pltpu.VMEM((KV_WIDTH, n_tok), BF16),
               pltpu.VMEM((KV_WIDTH, n_tok), F32), pltpu.VMEM((KV_WIDTH, n_tok), F32),
               pltpu.VMEM((n_tok, ATTN_WIDTH), F32), pltpu.VMEM((n_tok, CONV_WIDTH), F32)]
    y, k_new, v_new, u = pl.pallas_call(
        _sample_kernel, grid=(batch // SAMPLE_STEP_BATCH,), in_specs=in_specs, out_specs=out_specs,
        out_shape=out_shape, scratch_shapes=scratch, name="sample_layer",
        compiler_params=pltpu.CompilerParams(dimension_semantics=("arbitrary",),
                                             vmem_limit_bytes=VMEM_LIMIT),
    )(sinks_p, *flat, ck, cv, *weights)
    conv_new = u.reshape(batch, t_len, CONV_WIDTH)[:, t_len - (CONV_K - 1):]
    return y, k_new, v_new, conv_new


def kernel(x_prompt, x_sample, p_prompt, p_sample, cache_k, cache_v, state_conv, g_mix_norm, w_in, g_q, g_k,
           sinks, conv_w, g_attn_out, g_conv_out, w_o, g_mlp_norm, w_up, w_down, g_ple_norm, w_ple_gate, w_ple):
    depth = w_in.shape[0]
    yp, ys = x_prompt, x_sample
    outs = [[] for _ in range(6)]
    for i in range(depth):
        sinks_p, weights = _prepare_weights(
            g_mix_norm[i], w_in[i], g_q[i], g_k[i], sinks[i], conv_w[i], g_attn_out[i], g_conv_out[i],
            w_o[i], g_mlp_norm[i], w_up[i], w_down[i], g_ple_norm[i], w_ple_gate[i], w_ple[i])
        yp, kp, vp, cp = _prompt_layer(yp, p_prompt[i], sinks_p, weights)
        ys, ksn, vsn, csn = _sample_layer(ys, p_sample[i], cache_k[i], cache_v[i], state_conv[i],
                                          sinks_p, weights)
        for lst, val in zip(outs, (_cache_from_kernel(kp), _cache_from_kernel(vp), cp,
                                   _cache_from_kernel(ksn), _cache_from_kernel(vsn), csn)):
            lst.append(val)
    return (yp, ys) + tuple(jnp.stack(o) for o in outs)
```

```python
import jax
import jax.numpy as jnp
import numpy as np
from jax import lax
from jax.experimental import pallas as pl
from jax.experimental.pallas import tpu as pltpu

D_MODEL = 1024
HEAD_DIM = 64
N_Q_HEADS = 8
N_KV_HEADS = 2
GQA_GROUP = N_Q_HEADS // N_KV_HEADS
ATTN_WIDTH = N_Q_HEADS * HEAD_DIM
KV_WIDTH = N_KV_HEADS * HEAD_DIM
CONV_WIDTH = D_MODEL - ATTN_WIDTH
CONV_K = 3
WINDOW = 128
BLOCK = 128
ROPE_THETA = 10000.0
D_FF = 4 * D_MODEL
PLE_DIM = 256
EPS = 1e-6
NEG = -1e30
LOG2_E = 1.4426950408889634
PAST_LEN = 16384
IN_WIDTH = ATTN_WIDTH + 2 * KV_WIDTH + 3 * CONV_WIDTH

LANES = 128
SUBLANES = 8
N_QCOL = ATTN_WIDTH // LANES
SEQ_TILE = 1024
SUB_TILE = 256
FF_CHUNK = 1024
STAGE_LEAD = 3
SAMPLE_STEP_BATCH = 16
VMEM_LIMIT = 62 * 1024 * 1024

O_K = ATTN_WIDTH
O_V = O_K + KV_WIDTH
O_B = O_V + KV_WIDTH
O_C = O_B + CONV_WIDTH
O_H = O_C + CONV_WIDTH

BF16 = jnp.bfloat16
F32 = jnp.float32


def _dot(a, b):
    return jnp.dot(a, b, preferred_element_type=F32)


def _dot_t(a, b):
    return lax.dot_general(a, b, (((1,), (1,)), ((), ())), preferred_element_type=F32)


def _rms(x, g):
    return x * lax.rsqrt(jnp.mean(x * x, axis=-1, keepdims=True) + EPS) * g


def _head_norm_rope(t, ones_bd, g, cos, sin):
    ssq = _dot((t * t).astype(BF16), ones_bd)
    t = t * lax.rsqrt(ssq * (1.0 / HEAD_DIM) + EPS)
    lane = lax.broadcasted_iota(jnp.int32, (t.shape[0], LANES), 1)
    first_half = (lane & (HEAD_DIM - 1)) < HEAD_DIM // 2
    cols = []
    for m in range(t.shape[1] // LANES):
        c = t[:, m * LANES:(m + 1) * LANES] * g
        up = pltpu.roll(c, LANES - HEAD_DIM // 2, axis=1)
        dn = pltpu.roll(c, HEAD_DIM // 2, axis=1)
        cols.append(c * cos + jnp.where(first_half, up, dn) * sin)
    return cols


def _front(x, cos, sin, refs):
    h = _rms(x, refs["g_mix"][...]).astype(BF16)
    z = _dot(h, refs["w_in"][...])
    qcols = _head_norm_rope(z[:, :O_K], refs["bd_q"][...], refs["g_q"][...], cos, sin)
    (k,) = _head_norm_rope(z[:, O_K:O_V], refs["bd_k"][...], refs["g_k"][...], cos, sin)
    v = z[:, O_V:O_B]
    b_gate = z[:, O_B:O_C]
    u = z[:, O_C:O_H] * z[:, O_H:]
    return qcols, k, v, b_gate, u


def _tail_steps(x, o_attn, o_conv, p, refs, store):
    mixed = jnp.concatenate([_rms(o_attn, refs["g_attn"][...]), _rms(o_conv, refs["g_conv"][...])], axis=1)
    x = x + _dot(mixed.astype(BF16), refs["w_o"][...])
    hm = _rms(x, refs["g_mlp"][...]).astype(BF16)
    yield
    for c in range(D_FF // FF_CHUNK):
        up = _dot(hm, refs["w_up"][:, c * FF_CHUNK:(c + 1) * FF_CHUNK])
        act = jnp.square(jnp.maximum(up.astype(BF16), 0.0))
        x = x + _dot(act, refs["w_down"][c * FF_CHUNK:(c + 1) * FF_CHUNK, :])
        yield
    gate = jax.nn.sigmoid(_dot(_rms(x, refs["g_ple"][...]).astype(BF16), refs["w_gate"][...]))
    store(x + gate * _dot(p.astype(BF16), refs["w_ple"][...]))


def _run(steps):
    for _ in steps:
        pass


def _split_heads(col, c):
    group = (2 * c) // GQA_GROUP
    lo = lax.broadcasted_iota(jnp.int32, col.shape, 1) < HEAD_DIM
    swapped = pltpu.roll(col, HEAD_DIM, axis=1)
    if group == 0:
        return jnp.where(lo, col, 0.0), jnp.where(lo, swapped, 0.0)
    return jnp.where(lo, 0.0, swapped), jnp.where(lo, 0.0, col)


def _sink_column(sinks_ref, rows_per_head):
    return jnp.concatenate(
        [jnp.full((rows_per_head, 1), sinks_ref[i] * LOG2_E, F32) for i in range(N_Q_HEADS)], axis=0)


def _merge_heads(pv, rows_per_head, c):
    group = (2 * c) // GQA_GROUP
    lo = lax.broadcasted_iota(jnp.int32, (rows_per_head, LANES), 1) < HEAD_DIM
    a = pv[(2 * c) * rows_per_head:(2 * c + 1) * rows_per_head]
    b = pv[(2 * c + 1) * rows_per_head:(2 * c + 2) * rows_per_head]
    if group == 0:
        return jnp.where(lo, a, pltpu.roll(b, HEAD_DIM, axis=1))
    return jnp.where(lo, pltpu.roll(a, HEAD_DIM, axis=1), b)


WEIGHT_NAMES = ("g_mix", "w_in", "bd_q", "bd_k", "g_q", "g_k", "conv_w", "g_attn", "g_conv", "w_o",
                "g_mlp", "w_up", "w_down", "g_ple", "w_gate", "w_ple")


def _prompt_kernel(sinks_ref, x_ref, p_ref, cos_ref, sin_ref, *rest):
    nw = len(WEIGHT_NAMES)
    refs = dict(zip(WEIGHT_NAMES, rest[:nw]))
    y_ref, kout_ref, vout_ref, convout_ref, kbuf, vbuf, ubuf = rest[nw:]
    tm = x_ref.shape[0]
    j = pl.program_id(1)

    @pl.when(j == 0)
    def _():
        kbuf[0:BLOCK, :] = jnp.zeros((BLOCK, KV_WIDTH), BF16)
        vbuf[:, 0:BLOCK] = jnp.zeros((KV_WIDTH, BLOCK), BF16)
        ubuf[0:SUBLANES, :] = jnp.zeros((SUBLANES, CONV_WIDTH), F32)

    kj = lax.broadcasted_iota(jnp.int32, (BLOCK, N_Q_HEADS * BLOCK), 0)
    qi = lax.broadcasted_iota(jnp.int32, (BLOCK, N_Q_HEADS * BLOCK), 1) & (BLOCK - 1)
    own = kj <= qi
    sink = jnp.concatenate([jnp.full((1, BLOCK), sinks_ref[i] * LOG2_E, F32) for i in range(N_Q_HEADS)],
                           axis=1)
    cw = refs["conv_w"]

    last = {}

    def sub_tile_steps(r0):
        rows = slice(r0, r0 + SUB_TILE)
        x = x_ref[rows, :]
        qcols, k, v, b_gate, u = _front(x, cos_ref[rows, :], sin_ref[rows, :], refs)
        kbuf[BLOCK + r0:BLOCK + r0 + SUB_TILE, :] = k.astype(BF16)
        v_t = v.T
        vbuf[:, BLOCK + r0:BLOCK + r0 + SUB_TILE] = v_t.astype(BF16)
        ubuf[SUBLANES + r0:SUBLANES + r0 + SUB_TILE, :] = u
        last.update(k=k, v_t=v_t, u=u)
        yield

        o_blocks = []
        for i in range(SUB_TILE // BLOCK):
            g = r0 // BLOCK + i
            qs = []
            for c in range(N_QCOL):
                qs.extend(_split_heads(qcols[c][i * BLOCK:(i + 1) * BLOCK], c))
            qstack = jnp.concatenate(qs, axis=0).astype(BF16)
            s2 = _dot_t(kbuf[g * BLOCK:(g + 2) * BLOCK, :], qstack)
            s_prev = s2[:BLOCK]
            if g == 0:
                s_prev = jnp.where(j == 0, NEG, s_prev)
            s = jnp.where(own, s2[BLOCK:], s_prev)
            mx = jnp.maximum(jnp.max(s, axis=0, keepdims=True), sink)
            e = jnp.exp2(s - mx)
            denom = jnp.sum(e, axis=0, keepdims=True) + jnp.exp2(sink - mx)
            e2 = jnp.concatenate([jnp.where(own, 0.0, e), jnp.where(own, e, 0.0)], axis=0).astype(BF16)
            pv = _dot(vbuf[:, g * BLOCK:(g + 2) * BLOCK], e2) * (1.0 / denom)
            cols = []
            for c in range(N_QCOL):
                d0 = (2 * c) // GQA_GROUP * HEAD_DIM
                col_t = jnp.concatenate([pv[d0:d0 + HEAD_DIM, (2 * c) * BLOCK:(2 * c + 1) * BLOCK],
                                         pv[d0:d0 + HEAD_DIM, (2 * c + 1) * BLOCK:(2 * c + 2) * BLOCK]], axis=0)
                cols.append(col_t.T)
            o_blocks.append(jnp.concatenate(cols, axis=1))
            yield
        o_attn = jnp.concatenate(o_blocks, axis=0)

        conv = (cw[0:1, :] * ubuf[SUBLANES - 2 + r0:SUBLANES - 2 + r0 + SUB_TILE, :]
                + cw[1:2, :] * ubuf[SUBLANES - 1 + r0:SUBLANES - 1 + r0 + SUB_TILE, :]
                + cw[2:3, :] * u)

        def store(y):
            y_ref[rows, :] = y

        yield from _tail_steps(x, o_attn, b_gate * conv, p_ref[rows, :], refs, store)

    tiles = [sub_tile_steps(r0) for r0 in range(0, tm, SUB_TILE)]
    live = list(range(len(tiles)))
    rnd = 0
    while live:
        for t in reversed([t for t in live if rnd >= t * STAGE_LEAD]):
            if next(tiles[t], True) is True:
                live.remove(t)
        rnd += 1

    kout_ref[...] = last["k"][SUB_TILE - BLOCK:, :].T
    vout_ref[...] = last["v_t"][:, SUB_TILE - BLOCK:]
    convout_ref[...] = last["u"][SUB_TILE - (CONV_K - 1):, :]
    kbuf[0:BLOCK, :] = kbuf[tm:tm + BLOCK, :]
    vbuf[:, 0:BLOCK] = vbuf[:, tm:tm + BLOCK]
    ubuf[0:SUBLANES, :] = ubuf[tm:tm + SUBLANES, :]


def _sample_kernel(sinks_ref, x_ref, p_ref, cos_ref, sin_ref, st1_ref, st2_ref, ck_ref, cv_ref, *rest):
    nw = len(WEIGHT_NAMES)
    refs = dict(zip(WEIGHT_NAMES, rest[:nw]))
    y_ref, kout_ref, vout_ref, uout_ref, qbuf, kbuf, vbuf, knew, vnew, obuf, bgbuf = rest[nw:]
    t_len = x_ref.shape[1]
    n_tok = x_ref.shape[0] * t_len
    step = pl.program_id(0)
    n_steps = pl.num_programs(0)
    step_batch = ck_ref.shape[0]
    pair_rows = 2 * t_len
    chunk_pairs = BLOCK // pair_rows

    @pl.when(step == 0)
    def _():
        qcols, k, v, b_gate, u = _front(x_ref[...].reshape(n_tok, D_MODEL), cos_ref[...], sin_ref[...], refs)
        for c in range(N_QCOL):
            qbuf[2 * c], qbuf[2 * c + 1] = _split_heads(qcols[c], c)
        k_t, v_t = k.T, v.T
        kbuf[...] = k_t.astype(BF16)
        vbuf[...] = v_t.astype(BF16)
        knew[...] = k_t
        vnew[...] = v_t
        bgbuf[...] = b_gate
        uout_ref[...] = u

    n_rows = N_Q_HEADS * pair_rows
    row = lax.broadcasted_iota(jnp.int32, (n_rows, 3 * BLOCK), 0)
    col = lax.broadcasted_iota(jnp.int32, (n_rows, 3 * BLOCK), 1)
    row_b = (row % pair_rows) // t_len
    row_t = row % t_len
    cache_ok = (col < 2 * BLOCK) & (col // BLOCK == row_b) & ((col % BLOCK) > row_t)
    new_col = col - 2 * BLOCK
    sink = _sink_column(sinks_ref, pair_rows)

    def pair_step(i, carry):
        pair = step * (step_batch // 2) + i
        r0 = pl.multiple_of(pair * pair_rows, pair_rows)
        c0 = pl.multiple_of((pair // chunk_pairs) * BLOCK, BLOCK)
        q = qbuf[:, pl.ds(r0, pair_rows), :].reshape(n_rows, LANES).astype(BF16)
        ck = [ck_ref[2 * i + bb] for bb in range(2)]
        cv = [cv_ref[2 * i + bb] for bb in range(2)]
        keys = jnp.concatenate([c.astype(BF16) for c in ck] + [kbuf[:, pl.ds(c0, BLOCK)]], axis=1)
        vals = jnp.concatenate([c.astype(BF16) for c in cv] + [vbuf[:, pl.ds(c0, BLOCK)]], axis=1)
        s = _dot(q, keys)
        new_ok = ((col >= 2 * BLOCK) & (new_col // t_len == (pair % chunk_pairs) * 2 + row_b)
                  & (new_col % t_len <= row_t))
        ok = cache_ok | new_ok
        s = jnp.where(ok, s, NEG)
        mx = jnp.maximum(jnp.max(s, axis=-1, keepdims=True), sink)
        e = jnp.where(ok, jnp.exp2(s - mx), 0.0)
        denom = jnp.sum(e, axis=-1, keepdims=True) + jnp.exp2(sink - mx)
        pv = _dot_t(e.astype(BF16), vals) * (1.0 / denom)
        for c in range(N_QCOL):
            obuf[pl.ds(r0, pair_rows), c * LANES:(c + 1) * LANES] = _merge_heads(pv, pair_rows, c)
        k_chunk = knew[:, pl.ds(c0, BLOCK)]
        v_chunk = vnew[:, pl.ds(c0, BLOCK)]
        keep = lax.broadcasted_iota(jnp.int32, (KV_WIDTH, BLOCK), 1) < BLOCK - t_len
        for bb in range(2):
            to_tail = (BLOCK - t_len) - t_len * ((pair % chunk_pairs) * 2 + bb)
            kout_ref[2 * i + bb] = jnp.where(keep, pltpu.roll(ck[bb], BLOCK - t_len, axis=1),
                                             pltpu.roll(k_chunk, to_tail, axis=1))
            vout_ref[2 * i + bb] = jnp.where(keep, pltpu.roll(cv[bb], BLOCK - t_len, axis=1),
                                             pltpu.roll(v_chunk, to_tail, axis=1))
        return carry

    for i in range(step_batch // 2):
        pair_step(i, 0)

    @pl.when(step == n_steps - 1)
    def _():
        u = uout_ref[...]
        tok = lax.broadcasted_iota(jnp.int32, u.shape, 0) % t_len
        um1 = jnp.where(tok >= 1, pltpu.roll(u, 1, axis=0), st1_ref[...])
        um2 = jnp.where(tok >= 2, pltpu.roll(u, 2, axis=0), st2_ref[...])
        cw = refs["conv_w"]
        conv = cw[0:1, :] * um2 + cw[1:2, :] * um1 + cw[2:3, :] * u

        def store(y):
            y_ref[...] = y.reshape(y_ref.shape)

        _run(_tail_steps(x_ref[...].reshape(n_tok, D_MODEL), obuf[...], bgbuf[...] * conv,
                         p_ref[...].reshape(n_tok, PLE_DIM), refs, store))


def _rope_tables(pos):
    inv_freq = ROPE_THETA ** (-jnp.arange(0, HEAD_DIM, 2, dtype=F32) / HEAD_DIM)
    ang = pos.astype(F32)[:, None] * inv_freq[None, :]
    cos, sin = jnp.cos(ang), jnp.sin(ang)
    cos_h = jnp.concatenate([cos, cos], axis=1)
    sin_h = jnp.concatenate([-sin, sin], axis=1)
    return jnp.tile(cos_h, (1, LANES // HEAD_DIM)), jnp.tile(sin_h, (1, LANES // HEAD_DIM))


def _cache_to_kernel(c):
    batch, keys = c.shape[0], c.shape[1]
    return jnp.transpose(c, (0, 2, 3, 1)).reshape(batch, KV_WIDTH, keys)


def _cache_from_kernel(c):
    batch, _, keys = c.shape
    return jnp.transpose(c.reshape(batch, N_KV_HEADS, HEAD_DIM, keys), (0, 3, 1, 2))


def _block_diag_ones(width):
    idx = np.arange(width) // HEAD_DIM
    return jnp.asarray(idx[:, None] == idx[None, :], dtype=BF16)


def _prepare_weights(g_mix_norm, w_in, g_q, g_k, sinks, conv_w, g_attn_out, g_conv_out, w_o,
                     g_mlp_norm, w_up, w_down, g_ple_norm, w_ple_gate, w_ple):
    row = lambda g: g.reshape(1, -1).astype(F32)
    tile_head = lambda g: jnp.tile(g.reshape(1, HEAD_DIM).astype(F32), (1, LANES // HEAD_DIM))
    weights = dict(
        g_mix=row(g_mix_norm), w_in=w_in.astype(BF16),
        bd_q=_block_diag_ones(ATTN_WIDTH), bd_k=_block_diag_ones(KV_WIDTH),
        g_q=tile_head(g_q) * (HEAD_DIM ** -0.5 * LOG2_E), g_k=tile_head(g_k), conv_w=conv_w.astype(F32),
        g_attn=row(g_attn_out), g_conv=row(g_conv_out), w_o=w_o.astype(BF16),
        g_mlp=row(g_mlp_norm), w_up=w_up.astype(BF16), w_down=w_down.astype(BF16),
        g_ple=row(g_ple_norm), w_gate=w_ple_gate.astype(BF16), w_ple=w_ple.astype(BF16))
    return sinks.astype(F32), [weights[n] for n in WEIGHT_NAMES]


def _resident(a, n_grid):
    zeros = (0,) * a.ndim
    index_map = (lambda b, j: zeros) if n_grid == 2 else (lambda i: zeros)
    return pl.BlockSpec(a.shape, index_map, pipeline_mode=pl.Buffered(1))


def _prompt_layer(x, p, sinks_p, weights):
    batch, seq, _ = x.shape
    tm = SEQ_TILE
    cos, sin = _rope_tables(jnp.arange(seq, dtype=jnp.int32))
    tile = lambda w: pl.BlockSpec((None, tm, w), lambda b, j: (b, j, 0))
    per_batch = lambda r, w: pl.BlockSpec((None, r, w), lambda b, j: (b, 0, 0))
    table = pl.BlockSpec((tm, LANES), lambda b, j: (j, 0))
    in_specs = ([pl.BlockSpec(memory_space=pltpu.SMEM), tile(D_MODEL), tile(PLE_DIM), table, table]
                + [_resident(w, 2) for w in weights])
    out_shape = (jax.ShapeDtypeStruct((batch, seq, D_MODEL), F32),
                 jax.ShapeDtypeStruct((batch, BLOCK, KV_WIDTH), F32),
                 jax.ShapeDtypeStruct((batch, BLOCK, KV_WIDTH), F32),
                 jax.ShapeDtypeStruct((batch, CONV_K - 1, CONV_WIDTH), F32))
    out_specs = (tile(D_MODEL), per_batch(BLOCK, KV_WIDTH), per_batch(BLOCK, KV_WIDTH),
                 per_batch(CONV_K - 1, CONV_WIDTH))
    scratch = [pltpu.VMEM((BLOCK + tm, KV_WIDTH), BF16), pltpu.VMEM((KV_WIDTH, BLOCK + tm), BF16),
               pltpu.VMEM((SUBLANES + tm, CONV_WIDTH), F32)]
    return pl.pallas_call(
        _prompt_kernel, grid=(batch, seq // tm), in_specs=in_specs, out_specs=out_specs,
        out_shape=out_shape, scratch_shapes=scratch, name="prompt_layer",
        compiler_params=pltpu.CompilerParams(dimension_semantics=("arbitrary", "arbitrary"),
                                             vmem_limit_bytes=VMEM_LIMIT),
    )(sinks_p, x, p, cos, sin, *weights)


def _sample_layer(x, p, cache_k, cache_v, state_conv, sinks_p, weights):
    batch, t_len, _ = x.shape
    n_tok = batch * t_len
    cos, sin = _rope_tables(PAST_LEN + jnp.arange(t_len, dtype=jnp.int32))
    cos, sin = jnp.tile(cos, (batch, 1)), jnp.tile(sin, (batch, 1))
    st1 = jnp.pad(state_conv[:, 1:2], ((0, 0), (0, t_len - 1), (0, 0))).reshape(n_tok, CONV_WIDTH)
    st2 = jnp.pad(state_conv, ((0, 0), (0, t_len - (CONV_K - 1)), (0, 0))).reshape(n_tok, CONV_WIDTH)
    ck = _cache_to_kernel(cache_k)
    cv = _cache_to_kernel(cache_v)
    flat = [x, p, cos, sin, st1, st2]
    cache_block = pl.BlockSpec((SAMPLE_STEP_BATCH, WINDOW, KV_WIDTH), lambda i: (i, 0, 0))
    in_specs = ([pl.BlockSpec(memory_space=pltpu.SMEM)] + [_resident(a, 1) for a in flat]
                + [cache_block, cache_block] + [_resident(w, 1) for w in weights])
    out_shape = (jax.ShapeDtypeStruct((batch, t_len, D_MODEL), F32),
                 jax.ShapeDtypeStruct((batch, WINDOW, KV_WIDTH), F32),
                 jax.ShapeDtypeStruct((batch, WINDOW, KV_WIDTH), F32),
                 jax.ShapeDtypeStruct((n_tok, CONV_WIDTH), F32))
    whole = lambda s: pl.BlockSpec(s.shape, lambda i: (0,) * len(s.shape))
    out_specs = (whole(out_shape[0]), cache_block, cache_block, whole(out_shape[3]))
    scratch = [pltpu.VMEM((N_Q_HEADS, n_tok, LANES), F32),
               pltpu.VMEM((KV_WIDTH, n_tok), BF16), pltpu.VMEM((KV_WIDTH, n_tok), BF16),
               pltpu.VMEM((KV_WIDTH, n_tok), F32), pltpu.VMEM((KV_WIDTH, n_tok), F32),
               pltpu.VMEM((n_tok, ATTN_WIDTH), F32), pltpu.VMEM((n_tok, CONV_WIDTH), F32)]
    y, k_new, v_new, u = pl.pallas_call(
        _sample_kernel, grid=(batch // SAMPLE_STEP_BATCH,), in_specs=in_specs, out_specs=out_specs,
        out_shape=out_shape, scratch_shapes=scratch, name="sample_layer",
        compiler_params=pltpu.CompilerParams(dimension_semantics=("arbitrary",),
                                             vmem_limit_bytes=VMEM_LIMIT),
    )(sinks_p, *flat, ck, cv, *weights)
    conv_new = u.reshape(batch, t_len, CONV_WIDTH)[:, t_len - (CONV_K - 1):]
    return y, k_new, v_new, conv_new


def kernel(x_prompt, x_sample, p_prompt, p_sample, cache_k, cache_v, state_conv, g_mix_norm, w_in, g_q, g_k,
           sinks, conv_w, g_attn_out, g_conv_out, w_o, g_mlp_norm, w_up, w_down, g_ple_norm, w_ple_gate, w_ple):
    depth = w_in.shape[0]
    yp, ys = x_prompt, x_sample
    outs = [[] for _ in range(6)]
    for i in range(depth):
        sinks_p, weights = _prepare_weights(
            g_mix_norm[i], w_in[i], g_q[i], g_k[i], sinks[i], conv_w[i], g_attn_out[i], g_conv_out[i],
            w_o[i], g_mlp_norm[i], w_up[i], w_down[i], g_ple_norm[i], w_ple_gate[i], w_ple[i])
        yp, kp, vp, cp = _prompt_layer(yp, p_prompt[i], sinks_p, weights)
        ys, ksn, vsn, csn = _sample_layer(ys, p_sample[i], cache_k[i], cache_v[i], state_conv[i],
                                          sinks_p, weights)
        for lst, val in zip(outs, (_cache_from_kernel(kp), _cache_from_kernel(vp), cp,
                                   _cache_from_kernel(ksn), _cache_from_kernel(vsn), csn)):
            lst.append(val)
    return (yp, ys) + tuple(jnp.stack(o) for o in outs)
```

```python
import jax
import jax.numpy as jnp
import numpy as np
from jax import lax
from jax.experimental import pallas as pl
from jax.experimental.pallas import tpu as pltpu

D_MODEL = 1024
HEAD_DIM = 64
N_Q_HEADS = 8
N_KV_HEADS = 2
GQA_GROUP = N_Q_HEADS // N_KV_HEADS
ATTN_WIDTH = N_Q_HEADS * HEAD_DIM
KV_WIDTH = N_KV_HEADS * HEAD_DIM
CONV_WIDTH = D_MODEL - ATTN_WIDTH
CONV_K = 3
WINDOW = 128
BLOCK = 128
ROPE_THETA = 10000.0
D_FF = 4 * D_MODEL
PLE_DIM = 256
EPS = 1e-6
NEG = -1e30
LOG2_E = 1.4426950408889634
PAST_LEN = 16384
IN_WIDTH = ATTN_WIDTH + 2 * KV_WIDTH + 3 * CONV_WIDTH

LANES = 128
SUBLANES = 8
N_QCOL = ATTN_WIDTH // LANES
SEQ_TILE = 512
SUB_TILE = 256
FF_CHUNK = 1024
STAGE_LEAD = 3
SAMPLE_STEP_BATCH = 8
VMEM_LIMIT = 56 * 1024 * 1024

O_K = ATTN_WIDTH
O_V = O_K + KV_WIDTH
O_B = O_V + KV_WIDTH
O_C = O_B + CONV_WIDTH
O_H = O_C + CONV_WIDTH

BF16 = jnp.bfloat16
F32 = jnp.float32


def _dot(a, b):
    return jnp.dot(a, b, preferred_element_type=F32)


def _dot_t(a, b):
    return lax.dot_general(a, b, (((1,), (1,)), ((), ())), preferred_element_type=F32)


def _rms(x, g):
    return x * lax.rsqrt(jnp.mean(x * x, axis=-1, keepdims=True) + EPS) * g


def _head_norm_rope(t, ones_bd, g, cos, sin):
    ssq = _dot((t * t).astype(BF16), ones_bd)
    t = t * lax.rsqrt(ssq * (1.0 / HEAD_DIM) + EPS)
    lane = lax.broadcasted_iota(jnp.int32, (t.shape[0], LANES), 1)
    first_half = (lane & (HEAD_DIM - 1)) < HEAD_DIM // 2
    cols = []
    for m in range(t.shape[1] // LANES):
        c = t[:, m * LANES:(m + 1) * LANES] * g
        up = pltpu.roll(c, LANES - HEAD_DIM // 2, axis=1)
        dn = pltpu.roll(c, HEAD_DIM // 2, axis=1)
        cols.append(c * cos + jnp.where(first_half, up, dn) * sin)
    return cols


def _front(x, cos, sin, refs):
    h = _rms(x, refs["g_mix"][...]).astype(BF16)
    z = _dot(h, refs["w_in"][...])
    qcols = _head_norm_rope(z[:, :O_K], refs["bd_q"][...], refs["g_q"][...], cos, sin)
    (k,) = _head_norm_rope(z[:, O_K:O_V], refs["bd_k"][...], refs["g_k"][...], cos, sin)
    v = z[:, O_V:O_B]
    b_gate = z[:, O_B:O_C]
    u = z[:, O_C:O_H] * z[:, O_H:]
    return qcols, k, v, b_gate, u


def _tail_steps(x, o_attn, o_conv, p, refs, store):
    mixed = jnp.concatenate([_rms(o_attn, refs["g_attn"][...]), _rms(o_conv, refs["g_conv"][...])], axis=1)
    x = x + _dot(mixed.astype(BF16), refs["w_o"][...])
    hm = _rms(x, refs["g_mlp"][...]).astype(BF16)
    yield
    for c in range(D_FF // FF_CHUNK):
        up = _dot(hm, refs["w_up"][:, c * FF_CHUNK:(c + 1) * FF_CHUNK])
        act = jnp.square(jnp.maximum(up.astype(BF16), 0.0))
        x = x + _dot(act, refs["w_down"][c * FF_CHUNK:(c + 1) * FF_CHUNK, :])
        yield
    gate = jax.nn.sigmoid(_dot(_rms(x, refs["g_ple"][...]).astype(BF16), refs["w_gate"][...]))
    store(x + gate * _dot(p.astype(BF16), refs["w_ple"][...]))


def _run(steps):
    for _ in steps:
        pass


def _split_heads(col, c):
    group = (2 * c) // GQA_GROUP
    lo = lax.broadcasted_iota(jnp.int32, col.shape, 1) < HEAD_DIM
    swapped = pltpu.roll(col, HEAD_DIM, axis=1)
    if group == 0:
        return jnp.where(lo, col, 0.0), jnp.where(lo, swapped, 0.0)
    return jnp.where(lo, 0.0, swapped), jnp.where(lo, 0.0, col)


def _sink_column(sinks_ref, rows_per_head):
    return jnp.concatenate(
        [jnp.full((rows_per_head, 1), sinks_ref[i] * LOG2_E, F32) for i in range(N_Q_HEADS)], axis=0)


def _merge_heads(pv, rows_per_head, c):
    group = (2 * c) // GQA_GROUP
    lo = lax.broadcasted_iota(jnp.int32, (rows_per_head, LANES), 1) < HEAD_DIM
    a = pv[(2 * c) * rows_per_head:(2 * c + 1) * rows_per_head]
    b = pv[(2 * c + 1) * rows_per_head:(2 * c + 2) * rows_per_head]
    if group == 0:
        return jnp.where(lo, a, pltpu.roll(b, HEAD_DIM, axis=1))
    return jnp.where(lo, pltpu.roll(a, HEAD_DIM, axis=1), b)


WEIGHT_NAMES = ("g_mix", "w_in", "bd_q", "bd_k", "g_q", "g_k", "conv_w", "g_attn", "g_conv", "w_o",
                "g_mlp", "w_up", "w_down", "g_ple", "w_gate", "w_ple")
MATMUL_WEIGHTS = ("w_in", "w_o", "w_up", "w_down", "w_gate", "w_ple")
LATE_WEIGHTS = MATMUL_WEIGHTS[1:]
SMALL_WEIGHTS = tuple(n for n in WEIGHT_NAMES if n not in MATMUL_WEIGHTS)
W_IN_CHUNK = 128


def _prompt_kernel(sinks_ref, x_ref, p_ref, cos_ref, sin_ref, *rest):
    nw = len(WEIGHT_NAMES)
    refs = dict(zip(WEIGHT_NAMES, rest[:nw]))
    y_ref, kout_ref, vout_ref, convout_ref, kbuf, vbuf, ubuf = rest[nw:]
    tm = x_ref.shape[0]
    j = pl.program_id(1)

    @pl.when(j == 0)
    def _():
        kbuf[0:BLOCK, :] = jnp.zeros((BLOCK, KV_WIDTH), BF16)
        vbuf[:, 0:BLOCK] = jnp.zeros((KV_WIDTH, BLOCK), BF16)
        ubuf[0:SUBLANES, :] = jnp.zeros((SUBLANES, CONV_WIDTH), F32)

    kj = lax.broadcasted_iota(jnp.int32, (BLOCK, N_Q_HEADS * BLOCK), 0)
    qi = lax.broadcasted_iota(jnp.int32, (BLOCK, N_Q_HEADS * BLOCK), 1) & (BLOCK - 1)
    own = kj <= qi
    sink = jnp.concatenate([jnp.full((1, BLOCK), sinks_ref[i] * LOG2_E, F32) for i in range(N_Q_HEADS)],
                           axis=1)
    cw = refs["conv_w"]

    last = {}

    def sub_tile_steps(r0):
        rows = slice(r0, r0 + SUB_TILE)
        x = x_ref[rows, :]
        qcols, k, v, b_gate, u = _front(x, cos_ref[rows, :], sin_ref[rows, :], refs)
        kbuf[BLOCK + r0:BLOCK + r0 + SUB_TILE, :] = k.astype(BF16)
        v_t = v.T
        vbuf[:, BLOCK + r0:BLOCK + r0 + SUB_TILE] = v_t.astype(BF16)
        ubuf[SUBLANES + r0:SUBLANES + r0 + SUB_TILE, :] = u
        last.update(k=k, v_t=v_t, u=u)
        yield

        o_blocks = []
        for i in range(SUB_TILE // BLOCK):
            g = r0 // BLOCK + i
            qs = []
            for c in range(N_QCOL):
                qs.extend(_split_heads(qcols[c][i * BLOCK:(i + 1) * BLOCK], c))
            qstack = jnp.concatenate(qs, axis=0).astype(BF16)
            s2 = _dot_t(kbuf[g * BLOCK:(g + 2) * BLOCK, :], qstack)
            s_prev = s2[:BLOCK]
            if g == 0:
                s_prev = jnp.where(j == 0, NEG, s_prev)
            s = jnp.where(own, s2[BLOCK:], s_prev)
            mx = jnp.maximum(jnp.max(s, axis=0, keepdims=True), sink)
            e = jnp.exp2(s - mx)
            denom = jnp.sum(e, axis=0, keepdims=True) + jnp.exp2(sink - mx)
            e2 = jnp.concatenate([jnp.where(own, 0.0, e), jnp.where(own, e, 0.0)], axis=0).astype(BF16)
            pv = _dot(vbuf[:, g * BLOCK:(g + 2) * BLOCK], e2) * (1.0 / denom)
            cols = []
            for c in range(N_QCOL):
                d0 = (2 * c) // GQA_GROUP * HEAD_DIM
                col_t = jnp.concatenate([pv[d0:d0 + HEAD_DIM, (2 * c) * BLOCK:(2 * c + 1) * BLOCK],
                                         pv[d0:d0 + HEAD_DIM, (2 * c + 1) * BLOCK:(2 * c + 2) * BLOCK]], axis=0)
                cols.append(col_t.T)
            o_blocks.append(jnp.concatenate(cols, axis=1))
            yield
        o_attn = jnp.concatenate(o_blocks, axis=0)

        conv = (cw[0:1, :] * ubuf[SUBLANES - 2 + r0:SUBLANES - 2 + r0 + SUB_TILE, :]
                + cw[1:2, :] * ubuf[SUBLANES - 1 + r0:SUBLANES - 1 + r0 + SUB_TILE, :]
                + cw[2:3, :] * u)

        def store(y):
            y_ref[rows, :] = y

        yield from _tail_steps(x, o_attn, b_gate * conv, p_ref[rows, :], refs, store)

    tiles = [sub_tile_steps(r0) for r0 in range(0, tm, SUB_TILE)]
    live = list(range(len(tiles)))
    rnd = 0
    while live:
        for t in reversed([t for t in live if rnd >= t * STAGE_LEAD]):
            if next(tiles[t], True) is True:
                live.remove(t)
        rnd += 1

    kout_ref[...] = last["k"][SUB_TILE - BLOCK:, :].T
    vout_ref[...] = last["v_t"][:, SUB_TILE - BLOCK:]
    convout_ref[...] = last["u"][SUB_TILE - (CONV_K - 1):, :]
    kbuf[0:BLOCK, :] = kbuf[tm:tm + BLOCK, :]
    vbuf[:, 0:BLOCK] = vbuf[:, tm:tm + BLOCK]
    ubuf[0:SUBLANES, :] = ubuf[tm:tm + SUBLANES, :]


def _sample_kernel(sinks_ref, x_ref, p_ref, cos_ref, sin_ref, st1_ref, st2_ref, ck_ref, cv_ref, *rest):
    ns, nm, nl = len(SMALL_WEIGHTS), len(MATMUL_WEIGHTS), len(LATE_WEIGHTS)
    refs = dict(zip(SMALL_WEIGHTS, rest[:ns]))
    w_f32 = dict(zip(MATMUL_WEIGHTS, rest[ns:ns + nm]))
    y_ref, kout_ref, vout_ref, uout_ref = rest[ns + nm:ns + nm + 4]
    w_out = dict(zip(MATMUL_WEIGHTS, rest[ns + nm + 4:ns + 2 * nm + 4]))
    scratch = rest[ns + 2 * nm + 4:]
    qbuf, kbuf, vbuf, knew, vnew, obuf, bgbuf = scratch[:7]
    w_vmem = dict(zip(MATMUL_WEIGHTS, scratch[7:7 + nm]))
    in_stage = scratch[7 + nm]
    late_stage = dict(zip(LATE_WEIGHTS, scratch[8 + nm:8 + nm + nl]))
    in_sem, late_sem, out_sem = scratch[8 + nm + nl:]
    refs.update(w_vmem)
    t_len = x_ref.shape[1]
    n_tok = x_ref.shape[0] * t_len
    step = pl.program_id(0)
    n_steps = pl.num_programs(0)
    step_batch = ck_ref.shape[0]
    pair_rows = 2 * t_len
    chunk_pairs = BLOCK // pair_rows

    def in_copy(i):
        return pltpu.make_async_copy(w_f32["w_in"].at[pl.ds(i * W_IN_CHUNK, W_IN_CHUNK), :],
                                     in_stage.at[i % 2], in_sem.at[i % 2])

    def late_copy(n, name):
        rows = late_stage[name].shape[0]
        return pltpu.make_async_copy(w_f32[name].at[pl.ds(step * rows, rows), :], late_stage[name],
                                     late_sem.at[n])

    def out_copy(n, name):
        return pltpu.make_async_copy(w_vmem[name], w_out[name], out_sem.at[n])

    for n, name in enumerate(LATE_WEIGHTS):
        late_copy(n, name).start()

    @pl.when(step == 0)
    def _():
        n_chunks = D_MODEL // W_IN_CHUNK
        in_copy(0).start()
        in_copy(1).start()
        for i in range(n_chunks):
            in_copy(i).wait()
            w_vmem["w_in"][i * W_IN_CHUNK:(i + 1) * W_IN_CHUNK, :] = in_stage[i % 2].astype(BF16)
            if i + 2 < n_chunks:
                in_copy(i + 2).start()
        out_copy(0, "w_in").start()
        qcols, k, v, b_gate, u = _front(x_ref[...].reshape(n_tok, D_MODEL), cos_ref[...], sin_ref[...], refs)
        for c in range(N_QCOL):
            qbuf[2 * c], qbuf[2 * c + 1] = _split_heads(qcols[c], c)
        k_t, v_t = k.T, v.T
        kbuf[...] = k_t.astype(BF16)
        vbuf[...] = v_t.astype(BF16)
        knew[...] = k_t
        vnew[...] = v_t
        bgbuf[...] = b_gate
        uout_ref[...] = u

    n_rows = N_Q_HEADS * pair_rows
    row = lax.broadcasted_iota(jnp.int32, (n_rows, 3 * BLOCK), 0)
    col = lax.broadcasted_iota(jnp.int32, (n_rows, 3 * BLOCK), 1)
    row_b = (row % pair_rows) // t_len
    row_t = row % t_len
    cache_ok = (col < 2 * BLOCK) & (col // BLOCK == row_b) & ((col % BLOCK) > row_t)
    new_col = col - 2 * BLOCK
    sink = _sink_column(sinks_ref, pair_rows)

    def pair_step(i, carry):
        pair = step * (step_batch // 2) + i
        r0 = pl.multiple_of(pair * pair_rows, pair_rows)
        c0 = pl.multiple_of((pair // chunk_pairs) * BLOCK, BLOCK)
        q = qbuf[:, pl.ds(r0, pair_rows), :].reshape(n_rows, LANES).astype(BF16)
        ck = [ck_ref[2 * i + bb] for bb in range(2)]
        cv = [cv_ref[2 * i + bb] for bb in range(2)]
        keys = jnp.concatenate([c.astype(BF16) for c in ck] + [kbuf[:, pl.ds(c0, BLOCK)]], axis=1)
        vals = jnp.concatenate([c.astype(BF16) for c in cv] + [vbuf[:, pl.ds(c0, BLOCK)]], axis=1)
        s = _dot(q, keys)
        new_ok = ((col >= 2 * BLOCK) & (new_col // t_len == (pair % chunk_pairs) * 2 + row_b)
                  & (new_col % t_len <= row_t))
        ok = cache_ok | new_ok
        s = jnp.where(ok, s, NEG)
        mx = jnp.maximum(jnp.max(s, axis=-1, keepdims=True), sink)
        e = jnp.where(ok, jnp.exp2(s - mx), 0.0)
        denom = jnp.sum(e, axis=-1, keepdims=True) + jnp.exp2(sink - mx)
        pv = _dot_t(e.astype(BF16), vals) * (1.0 / denom)
        for c in range(N_QCOL):
            obuf[pl.ds(r0, pair_rows), c * LANES:(c + 1) * LANES] = _merge_heads(pv, pair_rows, c)
        k_chunk = knew[:, pl.ds(c0, BLOCK)]
        v_chunk = vnew[:, pl.ds(c0, BLOCK)]
        keep = lax.broadcasted_iota(jnp.int32, (KV_WIDTH, BLOCK), 1) < BLOCK - t_len
        for bb in range(2):
            to_tail = (BLOCK - t_len) - t_len * ((pair % chunk_pairs) * 2 + bb)
            kout_ref[2 * i + bb] = jnp.where(keep, pltpu.roll(ck[bb], BLOCK - t_len, axis=1),
                                             pltpu.roll(k_chunk, to_tail, axis=1))
            vout_ref[2 * i + bb] = jnp.where(keep, pltpu.roll(cv[bb], BLOCK - t_len, axis=1),
                                             pltpu.roll(v_chunk, to_tail, axis=1))
        return carry

    for i in range(step_batch // 2):
        pair_step(i, 0)

    for n, name in enumerate(LATE_WEIGHTS):
        rows = late_stage[name].shape[0]
        late_copy(n, name).wait()
        r0 = pl.multiple_of(step * rows, rows)
        w_vmem[name][pl.ds(r0, rows), :] = late_stage[name][...].astype(BF16)

    @pl.when(step == n_steps - 1)
    def _():
        for n, name in enumerate(LATE_WEIGHTS):
            out_copy(n + 1, name).start()
        u = uout_ref[...]
        tok = lax.broadcasted_iota(jnp.int32, u.shape, 0) % t_len
        um1 = jnp.where(tok >= 1, pltpu.roll(u, 1, axis=0), st1_ref[...])
        um2 = jnp.where(tok >= 2, pltpu.roll(u, 2, axis=0), st2_ref[...])
        cw = refs["conv_w"]
        conv = cw[0:1, :] * um2 + cw[1:2, :] * um1 + cw[2:3, :] * u

        def store(y):
            y_ref[...] = y.reshape(y_ref.shape)

        _run(_tail_steps(x_ref[...].reshape(n_tok, D_MODEL), obuf[...], bgbuf[...] * conv,
                         p_ref[...].reshape(n_tok, PLE_DIM), refs, store))
        for n, name in enumerate(MATMUL_WEIGHTS):
            out_copy(n, name).wait()


def _rope_tables(pos):
    inv_freq = ROPE_THETA ** (-jnp.arange(0, HEAD_DIM, 2, dtype=F32) / HEAD_DIM)
    ang = pos.astype(F32)[:, None] * inv_freq[None, :]
    cos, sin = jnp.cos(ang), jnp.sin(ang)
    cos_h = jnp.concatenate([cos, cos], axis=1)
    sin_h = jnp.concatenate([-sin, sin], axis=1)
    return jnp.tile(cos_h, (1, LANES // HEAD_DIM)), jnp.tile(sin_h, (1, LANES // HEAD_DIM))


def _cache_to_kernel(c):
    batch, keys = c.shape[0], c.shape[1]
    return jnp.transpose(c, (0, 2, 3, 1)).reshape(batch, KV_WIDTH, keys)


def _cache_from_kernel(c):
    batch, _, keys = c.shape
    return jnp.transpose(c.reshape(batch, N_KV_HEADS, HEAD_DIM, keys), (0, 3, 1, 2))


def _block_diag_ones(width):
    idx = np.arange(width) // HEAD_DIM
    return jnp.asarray(idx[:, None] == idx[None, :], dtype=BF16)


def _prepare_weights(g_mix_norm, w_in, g_q, g_k, sinks, conv_w, g_attn_out, g_conv_out, w_o,
                     g_mlp_norm, w_up, w_down, g_ple_norm, w_ple_gate, w_ple):
    row = lambda g: g.reshape(1, -1).astype(F32)
    tile_head = lambda g: jnp.tile(g.reshape(1, HEAD_DIM).astype(F32), (1, LANES // HEAD_DIM))
    weights = dict(
        g_mix=row(g_mix_norm), w_in=w_in,
        bd_q=_block_diag_ones(ATTN_WIDTH), bd_k=_block_diag_ones(KV_WIDTH),
        g_q=tile_head(g_q) * (HEAD_DIM ** -0.5 * LOG2_E), g_k=tile_head(g_k), conv_w=conv_w.astype(F32),
        g_attn=row(g_attn_out), g_conv=row(g_conv_out), w_o=w_o,
        g_mlp=row(g_mlp_norm), w_up=w_up, w_down=w_down,
        g_ple=row(g_ple_norm), w_gate=w_ple_gate, w_ple=w_ple)
    return sinks.astype(F32), weights


def _resident(a, n_grid):
    zeros = (0,) * a.ndim
    index_map = (lambda b, j: zeros) if n_grid == 2 else (lambda i: zeros)
    return pl.BlockSpec(a.shape, index_map, pipeline_mode=pl.Buffered(1))


def _prompt_layer(x, p, sinks_p, weights):
    batch, seq, _ = x.shape
    tm = SEQ_TILE
    cos, sin = _rope_tables(jnp.arange(seq, dtype=jnp.int32))
    tile = lambda w: pl.BlockSpec((None, tm, w), lambda b, j: (b, j, 0))
    per_batch = lambda r, w: pl.BlockSpec((None, r, w), lambda b, j: (b, 0, 0))
    table = pl.BlockSpec((tm, LANES), lambda b, j: (j, 0))
    in_specs = ([pl.BlockSpec(memory_space=pltpu.SMEM), tile(D_MODEL), tile(PLE_DIM), table, table]
                + [_resident(w, 2) for w in weights])
    out_shape = (jax.ShapeDtypeStruct((batch, seq, D_MODEL), F32),
                 jax.ShapeDtypeStruct((batch, BLOCK, KV_WIDTH), F32),
                 jax.ShapeDtypeStruct((batch, BLOCK, KV_WIDTH), F32),
                 jax.ShapeDtypeStruct((batch, CONV_K - 1, CONV_WIDTH), F32))
    out_specs = (tile(D_MODEL), per_batch(BLOCK, KV_WIDTH), per_batch(BLOCK, KV_WIDTH),
                 per_batch(CONV_K - 1, CONV_WIDTH))
    scratch = [pltpu.VMEM((BLOCK + tm, KV_WIDTH), BF16), pltpu.VMEM((KV_WIDTH, BLOCK + tm), BF16),
               pltpu.VMEM((SUBLANES + tm, CONV_WIDTH), F32)]
    return pl.pallas_call(
        _prompt_kernel, grid=(batch, seq // tm), in_specs=in_specs, out_specs=out_specs,
        out_shape=out_shape, scratch_shapes=scratch, name="prompt_layer",
        compiler_params=pltpu.CompilerParams(dimension_semantics=("arbitrary", "arbitrary"),
                                             vmem_limit_bytes=VMEM_LIMIT),
    )(sinks_p, x, p, cos, sin, *weights)


def _sample_layer(x, p, cache_k, cache_v, state_conv, sinks_p, weights):
    batch, t_len, _ = x.shape
    n_tok = batch * t_len
    cos, sin = _rope_tables(PAST_LEN + jnp.arange(t_len, dtype=jnp.int32))
    cos, sin = jnp.tile(cos, (batch, 1)), jnp.tile(sin, (batch, 1))
    st1 = jnp.pad(state_conv[:, 1:2], ((0, 0), (0, t_len - 1), (0, 0))).reshape(n_tok, CONV_WIDTH)
    st2 = jnp.pad(state_conv, ((0, 0), (0, t_len - (CONV_K - 1)), (0, 0))).reshape(n_tok, CONV_WIDTH)
    ck = _cache_to_kernel(cache_k)
    cv = _cache_to_kernel(cache_v)
    flat = [x, p, cos, sin, st1, st2]
    n_steps = batch // SAMPLE_STEP_BATCH
    small = [weights[n] for n in SMALL_WEIGHTS]
    big = [weights[n] for n in MATMUL_WEIGHTS]
    hbm = pl.BlockSpec(memory_space=pl.ANY)
    cache_block = pl.BlockSpec((SAMPLE_STEP_BATCH, WINDOW, KV_WIDTH), lambda i: (i, 0, 0))
    in_specs = ([pl.BlockSpec(memory_space=pltpu.SMEM)] + [_resident(a, 1) for a in flat]
                + [cache_block, cache_block] + [_resident(w, 1) for w in small] + [hbm] * len(big))
    out_shape = (jax.ShapeDtypeStruct((batch, t_len, D_MODEL), F32),
                 jax.ShapeDtypeStruct((batch, WINDOW, KV_WIDTH), F32),
                 jax.ShapeDtypeStruct((batch, WINDOW, KV_WIDTH), F32),
                 jax.ShapeDtypeStruct((n_tok, CONV_WIDTH), F32)
                 ) + tuple(jax.ShapeDtypeStruct(w.shape, BF16) for w in big)
    whole = lambda s: pl.BlockSpec(s.shape, lambda i: (0,) * len(s.shape), pipeline_mode=pl.Buffered(1))
    out_specs = (whole(out_shape[0]), cache_block, cache_block, whole(out_shape[3])) + (hbm,) * len(big)
    late = [weights[n] for n in LATE_WEIGHTS]
    scratch = ([pltpu.VMEM((N_Q_HEADS, n_tok, LANES), F32),
                pltpu.VMEM((KV_WIDTH, n_tok), BF16), pltpu.VMEM((KV_WIDTH, n_tok), BF16),
                pltpu.VMEM((KV_WIDTH, n_tok), F32), pltpu.VMEM((KV_WIDTH, n_tok), F32),
                pltpu.VMEM((n_tok, ATTN_WIDTH), F32), pltpu.VMEM((n_tok, CONV_WIDTH), F32)]
               + [pltpu.VMEM(w.shape, BF16) for w in big]
               + [pltpu.VMEM((2, W_IN_CHUNK, IN_WIDTH), F32)]
               + [pltpu.VMEM((w.shape[0] // n_steps, w.shape[1]), F32) for w in late]
               + [pltpu.SemaphoreType.DMA((2,)), pltpu.SemaphoreType.DMA((len(late),)),
                  pltpu.SemaphoreType.DMA((len(big),))])
    y, k_new, v_new, u, *w_bf16 = pl.pallas_call(
        _sample_kernel, grid=(n_steps,), in_specs=in_specs, out_specs=out_specs,
        out_shape=out_shape, scratch_shapes=scratch, name="sample_layer",
        compiler_params=pltpu.CompilerParams(dimension_semantics=("arbitrary",),
                                             vmem_limit_bytes=VMEM_LIMIT),
    )(sinks_p, *flat, ck, cv, *small, *big)
    conv_new = u.reshape(batch, t_len, CONV_WIDTH)[:, t_len - (CONV_K - 1):]
    return y, k_new, v_new, conv_new, dict(zip(MATMUL_WEIGHTS, w_bf16))


def kernel(x_prompt, x_sample, p_prompt, p_sample, cache_k, cache_v, state_conv, g_mix_norm, w_in, g_q, g_k,
           sinks, conv_w, g_attn_out, g_conv_out, w_o, g_mlp_norm, w_up, w_down, g_ple_norm, w_ple_gate, w_ple):
    depth = w_in.shape[0]
    yp, ys = x_prompt, x_sample
    outs = [[] for _ in range(6)]
    for i in range(depth):
        sinks_p, weights = _prepare_weights(
            g_mix_norm[i], w_in[i], g_q[i], g_k[i], sinks[i], conv_w[i], g_attn_out[i], g_conv_out[i],
            w_o[i], g_mlp_norm[i], w_up[i], w_down[i], g_ple_norm[i], w_ple_gate[i], w_ple[i])
        ys, ksn, vsn, csn, w_bf16 = _sample_layer(ys, p_sample[i], cache_k[i], cache_v[i], state_conv[i],
                                                  sinks_p, weights)
        yp, kp, vp, cp = _prompt_layer(yp, p_prompt[i], sinks_p,
                                       [w_bf16.get(n, weights[n]) for n in WEIGHT_NAMES])
        for lst, val in zip(outs, (_cache_from_kernel(kp), _cache_from_kernel(vp), cp,
                                   _cache_from_kernel(ksn), _cache_from_kernel(vsn), csn)):
            lst.append(val)
    return (yp, ys) + tuple(jnp.stack(o) for o in outs)
```

```python
import jax
import jax.numpy as jnp
import numpy as np
from jax import lax
from jax.experimental import pallas as pl
from jax.experimental.pallas import tpu as pltpu

D_MODEL = 1024
HEAD_DIM = 64
N_Q_HEADS = 8
N_KV_HEADS = 2
GQA_GROUP = N_Q_HEADS // N_KV_HEADS
ATTN_WIDTH = N_Q_HEADS * HEAD_DIM
KV_WIDTH = N_KV_HEADS * HEAD_DIM
CONV_WIDTH = D_MODEL - ATTN_WIDTH
CONV_K = 3
WINDOW = 128
BLOCK = 128
ROPE_THETA = 10000.0
D_FF = 4 * D_MODEL
PLE_DIM = 256
EPS = 1e-6
NEG = -1e30
LOG2_E = 1.4426950408889634
PAST_LEN = 16384
IN_WIDTH = ATTN_WIDTH + 2 * KV_WIDTH + 3 * CONV_WIDTH

LANES = 128
SUBLANES = 8
N_QCOL = ATTN_WIDTH // LANES
SEQ_TILE = 512
SUB_TILES = (256, 256)
FF_CHUNK = 1024
SAMPLE_STEP_BATCH = 8
VMEM_LIMIT = 56 * 1024 * 1024

O_K = ATTN_WIDTH
O_V = O_K + KV_WIDTH
O_B = O_V + KV_WIDTH
O_C = O_B + CONV_WIDTH
O_H = O_C + CONV_WIDTH

BF16 = jnp.bfloat16
F32 = jnp.float32


def _dot(a, b):
    return jnp.dot(a, b, preferred_element_type=F32)


def _dot_t(a, b):
    return lax.dot_general(a, b, (((1,), (1,)), ((), ())), preferred_element_type=F32)


def _rms(x, g):
    return x * lax.rsqrt(jnp.mean(x * x, axis=-1, keepdims=True) + EPS) * g


def _head_norm_rope(t, ones_bd, g, cos, sin):
    ssq = _dot((t * t).astype(BF16), ones_bd)
    t = t * lax.rsqrt(ssq * (1.0 / HEAD_DIM) + EPS)
    lane = lax.broadcasted_iota(jnp.int32, (t.shape[0], LANES), 1)
    first_half = (lane & (HEAD_DIM - 1)) < HEAD_DIM // 2
    cols = []
    for m in range(t.shape[1] // LANES):
        c = t[:, m * LANES:(m + 1) * LANES] * g
        up = pltpu.roll(c, LANES - HEAD_DIM // 2, axis=1)
        dn = pltpu.roll(c, HEAD_DIM // 2, axis=1)
        cols.append(c * cos + jnp.where(first_half, up, dn) * sin)
    return cols


def _front(x, cos, sin, refs):
    h = _rms(x, refs["g_mix"][...]).astype(BF16)
    z = _dot(h, refs["w_in"][...])
    qcols = _head_norm_rope(z[:, :O_K], refs["bd_q"][...], refs["g_q"][...], cos, sin)
    (k,) = _head_norm_rope(z[:, O_K:O_V], refs["bd_k"][...], refs["g_k"][...], cos, sin)
    v = z[:, O_V:O_B]
    b_gate = z[:, O_B:O_C]
    u = z[:, O_C:O_H] * z[:, O_H:]
    return qcols, k, v, b_gate, u


def _tail_steps(x, o_attn, o_conv, p, refs, store):
    mixed = jnp.concatenate([_rms(o_attn, refs["g_attn"][...]), _rms(o_conv, refs["g_conv"][...])], axis=1)
    x = x + _dot(mixed.astype(BF16), refs["w_o"][...])
    hm = _rms(x, refs["g_mlp"][...]).astype(BF16)
    yield
    for c in range(D_FF // FF_CHUNK):
        up = _dot(hm, refs["w_up"][:, c * FF_CHUNK:(c + 1) * FF_CHUNK])
        act = jnp.square(jnp.maximum(up.astype(BF16), 0.0))
        x = x + _dot(act, refs["w_down"][c * FF_CHUNK:(c + 1) * FF_CHUNK, :])
        yield
    gate = jax.nn.sigmoid(_dot(_rms(x, refs["g_ple"][...]).astype(BF16), refs["w_gate"][...]))
    store(x + gate * _dot(p.astype(BF16), refs["w_ple"][...]))


def _run(steps):
    for _ in steps:
        pass


def _split_heads(col, c):
    group = (2 * c) // GQA_GROUP
    lo = lax.broadcasted_iota(jnp.int32, col.shape, 1) < HEAD_DIM
    swapped = pltpu.roll(col, HEAD_DIM, axis=1)
    if group == 0:
        return jnp.where(lo, col, 0.0), jnp.where(lo, swapped, 0.0)
    return jnp.where(lo, 0.0, swapped), jnp.where(lo, 0.0, col)


def _sink_column(sinks_ref, rows_per_head):
    return jnp.concatenate(
        [jnp.full((rows_per_head, 1), sinks_ref[i] * LOG2_E, F32) for i in range(N_Q_HEADS)], axis=0)


def _merge_heads(pv, rows_per_head, c):
    group = (2 * c) // GQA_GROUP
    lo = lax.broadcasted_iota(jnp.int32, (rows_per_head, LANES), 1) < HEAD_DIM
    a = pv[(2 * c) * rows_per_head:(2 * c + 1) * rows_per_head]
    b = pv[(2 * c + 1) * rows_per_head:(2 * c + 2) * rows_per_head]
    if group == 0:
        return jnp.where(lo, a, pltpu.roll(b, HEAD_DIM, axis=1))
    return jnp.where(lo, pltpu.roll(a, HEAD_DIM, axis=1), b)


WEIGHT_NAMES = ("g_mix", "w_in", "bd_q", "bd_k", "g_q", "g_k", "conv_w", "g_attn", "g_conv", "w_o",
                "g_mlp", "w_up", "w_down", "g_ple", "w_gate", "w_ple")
MATMUL_WEIGHTS = ("w_in", "w_o", "w_up", "w_down", "w_gate", "w_ple")
LATE_WEIGHTS = MATMUL_WEIGHTS[1:]
SMALL_WEIGHTS = tuple(n for n in WEIGHT_NAMES if n not in MATMUL_WEIGHTS)
W_IN_CHUNK = 64
W_IN_SLOTS = 4


def _prompt_kernel(sinks_ref, x_ref, p_ref, cos_ref, sin_ref, *rest):
    nw = len(WEIGHT_NAMES)
    refs = dict(zip(WEIGHT_NAMES, rest[:nw]))
    y_ref, kout_ref, vout_ref, convout_ref, kbuf, vbuf, ubuf = rest[nw:]
    tm = x_ref.shape[0]
    j = pl.program_id(1)

    @pl.when(j == 0)
    def _():
        kbuf[0:BLOCK, :] = jnp.zeros((BLOCK, KV_WIDTH), BF16)
        vbuf[:, 0:BLOCK] = jnp.zeros((KV_WIDTH, BLOCK), BF16)
        ubuf[0:SUBLANES, :] = jnp.zeros((SUBLANES, CONV_WIDTH), F32)

    kj = lax.broadcasted_iota(jnp.int32, (BLOCK, N_Q_HEADS * BLOCK), 0)
    qi = lax.broadcasted_iota(jnp.int32, (BLOCK, N_Q_HEADS * BLOCK), 1) & (BLOCK - 1)
    own = kj <= qi
    sink = jnp.concatenate([jnp.full((1, BLOCK), sinks_ref[i] * LOG2_E, F32) for i in range(N_Q_HEADS)],
                           axis=1)
    cw = refs["conv_w"]

    last = {}

    def sub_tile_steps(r0, n):
        rows = slice(r0, r0 + n)
        x = x_ref[rows, :]
        qcols, k, v, b_gate, u = _front(x, cos_ref[rows, :], sin_ref[rows, :], refs)
        kbuf[BLOCK + r0:BLOCK + r0 + n, :] = k.astype(BF16)
        v_t = v.T
        vbuf[:, BLOCK + r0:BLOCK + r0 + n] = v_t.astype(BF16)
        ubuf[SUBLANES + r0:SUBLANES + r0 + n, :] = u
        last.update(k=k, v_t=v_t, u=u, n=n)
        yield

        o_blocks = []
        for i in range(n // BLOCK):
            g = r0 // BLOCK + i
            qs = []
            for c in range(N_QCOL):
                qs.extend(_split_heads(qcols[c][i * BLOCK:(i + 1) * BLOCK], c))
            qstack = jnp.concatenate(qs, axis=0).astype(BF16)
            s2 = _dot_t(kbuf[g * BLOCK:(g + 2) * BLOCK, :], qstack)
            s_prev = s2[:BLOCK]
            if g == 0:
                s_prev = jnp.where(j == 0, NEG, s_prev)
            s = jnp.where(own, s2[BLOCK:], s_prev)
            mx = jnp.maximum(jnp.max(s, axis=0, keepdims=True), sink)
            e = jnp.exp2(s - mx)
            denom = jnp.sum(e, axis=0, keepdims=True) + jnp.exp2(sink - mx)
            e2 = jnp.concatenate([jnp.where(own, 0.0, e), jnp.where(own, e, 0.0)], axis=0).astype(BF16)
            pv = _dot(vbuf[:, g * BLOCK:(g + 2) * BLOCK], e2) * (1.0 / denom)
            cols = []
            for c in range(N_QCOL):
                d0 = (2 * c) // GQA_GROUP * HEAD_DIM
                col_t = jnp.concatenate([pv[d0:d0 + HEAD_DIM, (2 * c) * BLOCK:(2 * c + 1) * BLOCK],
                                         pv[d0:d0 + HEAD_DIM, (2 * c + 1) * BLOCK:(2 * c + 2) * BLOCK]], axis=0)
                cols.append(col_t.T)
            o_blocks.append(jnp.concatenate(cols, axis=1))
            yield
        o_attn = jnp.concatenate(o_blocks, axis=0)

        conv = (cw[0:1, :] * ubuf[SUBLANES - 2 + r0:SUBLANES - 2 + r0 + n, :]
                + cw[1:2, :] * ubuf[SUBLANES - 1 + r0:SUBLANES - 1 + r0 + n, :]
                + cw[2:3, :] * u)

        def store(y):
            y_ref[rows, :] = y

        yield from _tail_steps(x, o_attn, b_gate * conv, p_ref[rows, :], refs, store)

    assert sum(SUB_TILES) == tm
    starts = [sum(SUB_TILES[:t]) for t in range(len(SUB_TILES))]
    tiles = [sub_tile_steps(r0, n) for r0, n in zip(starts, SUB_TILES)]
    begin = [sum(1 + n // BLOCK for n in SUB_TILES[:t]) for t in range(len(SUB_TILES))]
    live = list(range(len(tiles)))
    rnd = 0
    while live:
        for t in reversed([t for t in live if rnd >= begin[t]]):
            if next(tiles[t], True) is True:
                live.remove(t)
        rnd += 1

    n_last = last["n"]
    kout_ref[...] = last["k"][n_last - BLOCK:, :].T
    vout_ref[...] = last["v_t"][:, n_last - BLOCK:]
    convout_ref[...] = last["u"][n_last - (CONV_K - 1):, :]
    kbuf[0:BLOCK, :] = kbuf[tm:tm + BLOCK, :]
    vbuf[:, 0:BLOCK] = vbuf[:, tm:tm + BLOCK]
    ubuf[0:SUBLANES, :] = ubuf[tm:tm + SUBLANES, :]


def _sample_kernel(sinks_ref, x_ref, p_ref, cos_ref, sin_ref, st1_ref, st2_ref, ck_ref, cv_ref, *rest):
    ns, nm, nl = len(SMALL_WEIGHTS), len(MATMUL_WEIGHTS), len(LATE_WEIGHTS)
    refs = dict(zip(SMALL_WEIGHTS, rest[:ns]))
    w_f32 = dict(zip(MATMUL_WEIGHTS, rest[ns:ns + nm]))
    y_ref, kout_ref, vout_ref, uout_ref = rest[ns + nm:ns + nm + 4]
    w_out = dict(zip(MATMUL_WEIGHTS, rest[ns + nm + 4:ns + 2 * nm + 4]))
    scratch = rest[ns + 2 * nm + 4:]
    qbuf, kbuf, vbuf, knew, vnew, obuf, bgbuf = scratch[:7]
    w_vmem = dict(zip(MATMUL_WEIGHTS, scratch[7:7 + nm]))
    in_stage = scratch[7 + nm]
    late_stage = dict(zip(LATE_WEIGHTS, scratch[8 + nm:8 + nm + nl]))
    in_sem, late_sem, out_sem = scratch[8 + nm + nl:]
    refs.update(w_vmem)
    t_len = x_ref.shape[1]
    n_tok = x_ref.shape[0] * t_len
    step = pl.program_id(0)
    n_steps = pl.num_programs(0)
    step_batch = ck_ref.shape[0]
    pair_rows = 2 * t_len
    chunk_pairs = BLOCK // pair_rows

    n_slots = in_stage.shape[0]

    def in_copy(i):
        return pltpu.make_async_copy(w_f32["w_in"].at[pl.ds(i * W_IN_CHUNK, W_IN_CHUNK), :],
                                     in_stage.at[i % n_slots], in_sem.at[i % n_slots])

    def late_copy(n, name):
        rows = late_stage[name].shape[0]
        return pltpu.make_async_copy(w_f32[name].at[pl.ds(step * rows, rows), :], late_stage[name],
                                     late_sem.at[n])

    def out_copy(n, name):
        return pltpu.make_async_copy(w_vmem[name], w_out[name], out_sem.at[n])

    for n, name in enumerate(LATE_WEIGHTS):
        late_copy(n, name).start()

    @pl.when(step == 0)
    def _():
        n_chunks = D_MODEL // W_IN_CHUNK
        for i in range(n_slots):
            in_copy(i).start()
        for i in range(n_chunks):
            in_copy(i).wait()
            w_vmem["w_in"][i * W_IN_CHUNK:(i + 1) * W_IN_CHUNK, :] = in_stage[i % n_slots].astype(BF16)
            if i + n_slots < n_chunks:
                in_copy(i + n_slots).start()
        out_copy(0, "w_in").start()
        qcols, k, v, b_gate, u = _front(x_ref[...].reshape(n_tok, D_MODEL), cos_ref[...], sin_ref[...], refs)
        for c in range(N_QCOL):
            qbuf[2 * c], qbuf[2 * c + 1] = _split_heads(qcols[c], c)
        k_t, v_t = k.T, v.T
        kbuf[...] = k_t.astype(BF16)
        vbuf[...] = v_t.astype(BF16)
        knew[...] = k_t
        vnew[...] = v_t
        bgbuf[...] = b_gate
        uout_ref[...] = u

    n_rows = N_Q_HEADS * pair_rows
    row = lax.broadcasted_iota(jnp.int32, (n_rows, 3 * BLOCK), 0)
    col = lax.broadcasted_iota(jnp.int32, (n_rows, 3 * BLOCK), 1)
    row_b = (row % pair_rows) // t_len
    row_t = row % t_len
    cache_ok = (col < 2 * BLOCK) & (col // BLOCK == row_b) & ((col % BLOCK) > row_t)
    new_col = col - 2 * BLOCK
    sink = _sink_column(sinks_ref, pair_rows)

    def pair_step(i, carry):
        pair = step * (step_batch // 2) + i
        r0 = pl.multiple_of(pair * pair_rows, pair_rows)
        c0 = pl.multiple_of((pair // chunk_pairs) * BLOCK, BLOCK)
        q = qbuf[:, pl.ds(r0, pair_rows), :].reshape(n_rows, LANES).astype(BF16)
        ck = [ck_ref[2 * i + bb] for bb in range(2)]
        cv = [cv_ref[2 * i + bb] for bb in range(2)]
        keys = jnp.concatenate([c.astype(BF16) for c in ck] + [kbuf[:, pl.ds(c0, BLOCK)]], axis=1)
        vals = jnp.concatenate([c.astype(BF16) for c in cv] + [vbuf[:, pl.ds(c0, BLOCK)]], axis=1)
        s = _dot(q, keys)
        new_ok = ((col >= 2 * BLOCK) & (new_col // t_len == (pair % chunk_pairs) * 2 + row_b)
                  & (new_col % t_len <= row_t))
        ok = cache_ok | new_ok
        s = jnp.where(ok, s, NEG)
        mx = jnp.maximum(jnp.max(s, axis=-1, keepdims=True), sink)
        e = jnp.where(ok, jnp.exp2(s - mx), 0.0)
        denom = jnp.sum(e, axis=-1, keepdims=True) + jnp.exp2(sink - mx)
        pv = _dot_t(e.astype(BF16), vals) * (1.0 / denom)
        for c in range(N_QCOL):
            obuf[pl.ds(r0, pair_rows), c * LANES:(c + 1) * LANES] = _merge_heads(pv, pair_rows, c)
        k_chunk = knew[:, pl.ds(c0, BLOCK)]
        v_chunk = vnew[:, pl.ds(c0, BLOCK)]
        keep = lax.broadcasted_iota(jnp.int32, (KV_WIDTH, BLOCK), 1) < BLOCK - t_len
        for bb in range(2):
            to_tail = (BLOCK - t_len) - t_len * ((pair % chunk_pairs) * 2 + bb)
            kout_ref[2 * i + bb] = jnp.where(keep, pltpu.roll(ck[bb], BLOCK - t_len, axis=1),
                                             pltpu.roll(k_chunk, to_tail, axis=1))
            vout_ref[2 * i + bb] = jnp.where(keep, pltpu.roll(cv[bb], BLOCK - t_len, axis=1),
                                             pltpu.roll(v_chunk, to_tail, axis=1))
        return carry

    for i in range(step_batch // 2):
        pair_step(i, 0)

    for n, name in enumerate(LATE_WEIGHTS):
        rows = late_stage[name].shape[0]
        late_copy(n, name).wait()
        r0 = pl.multiple_of(step * rows, rows)
        w_vmem[name][pl.ds(r0, rows), :] = late_stage[name][...].astype(BF16)

    @pl.when(step == n_steps - 1)
    def _():
        for n, name in enumerate(LATE_WEIGHTS):
            out_copy(n + 1, name).start()
        u = uout_ref[...]
        tok = lax.broadcasted_iota(jnp.int32, u.shape, 0) % t_len
        um1 = jnp.where(tok >= 1, pltpu.roll(u, 1, axis=0), st1_ref[...])
        um2 = jnp.where(tok >= 2, pltpu.roll(u, 2, axis=0), st2_ref[...])
        cw = refs["conv_w"]
        conv = cw[0:1, :] * um2 + cw[1:2, :] * um1 + cw[2:3, :] * u

        def store(y):
            y_ref[...] = y.reshape(y_ref.shape)

        _run(_tail_steps(x_ref[...].reshape(n_tok, D_MODEL), obuf[...], bgbuf[...] * conv,
                         p_ref[...].reshape(n_tok, PLE_DIM), refs, store))
        for n, name in enumerate(MATMUL_WEIGHTS):
            out_copy(n, name).wait()


def _rope_tables(pos):
    inv_freq = ROPE_THETA ** (-jnp.arange(0, HEAD_DIM, 2, dtype=F32) / HEAD_DIM)
    ang = pos.astype(F32)[:, None] * inv_freq[None, :]
    cos, sin = jnp.cos(ang), jnp.sin(ang)
    cos_h = jnp.concatenate([cos, cos], axis=1)
    sin_h = jnp.concatenate([-sin, sin], axis=1)
    return jnp.tile(cos_h, (1, LANES // HEAD_DIM)), jnp.tile(sin_h, (1, LANES // HEAD_DIM))


def _cache_to_kernel(c):
    batch, keys = c.shape[0], c.shape[1]
    return jnp.transpose(c, (0, 2, 3, 1)).reshape(batch, KV_WIDTH, keys)


def _cache_from_kernel(c):
    batch, _, keys = c.shape
    return jnp.transpose(c.reshape(batch, N_KV_HEADS, HEAD_DIM, keys), (0, 3, 1, 2))


def _block_diag_ones(width):
    idx = np.arange(width) // HEAD_DIM
    return jnp.asarray(idx[:, None] == idx[None, :], dtype=BF16)


def _prepare_weights(g_mix_norm, w_in, g_q, g_k, sinks, conv_w, g_attn_out, g_conv_out, w_o,
                     g_mlp_norm, w_up, w_down, g_ple_norm, w_ple_gate, w_ple):
    row = lambda g: g.reshape(1, -1).astype(F32)
    tile_head = lambda g: jnp.tile(g.reshape(1, HEAD_DIM).astype(F32), (1, LANES // HEAD_DIM))
    weights = dict(
        g_mix=row(g_mix_norm), w_in=w_in,
        bd_q=_block_diag_ones(ATTN_WIDTH), bd_k=_block_diag_ones(KV_WIDTH),
        g_q=tile_head(g_q) * (HEAD_DIM ** -0.5 * LOG2_E), g_k=tile_head(g_k), conv_w=conv_w.astype(F32),
        g_attn=row(g_attn_out), g_conv=row(g_conv_out), w_o=w_o,
        g_mlp=row(g_mlp_norm), w_up=w_up, w_down=w_down,
        g_ple=row(g_ple_norm), w_gate=w_ple_gate, w_ple=w_ple)
    return sinks.astype(F32), weights


def _resident(a, n_grid):
    zeros = (0,) * a.ndim
    index_map = (lambda b, j: zeros) if n_grid == 2 else (lambda i: zeros)
    return pl.BlockSpec(a.shape, index_map, pipeline_mode=pl.Buffered(1))


def _prompt_layer(x, p, sinks_p, weights):
    batch, seq, _ = x.shape
    tm = SEQ_TILE
    cos, sin = _rope_tables(jnp.arange(seq, dtype=jnp.int32))
    tile = lambda w: pl.BlockSpec((None, tm, w), lambda b, j: (b, j, 0))
    per_batch = lambda r, w: pl.BlockSpec((None, r, w), lambda b, j: (b, 0, 0))
    table = pl.BlockSpec((tm, LANES), lambda b, j: (j, 0))
    in_specs = ([pl.BlockSpec(memory_space=pltpu.SMEM), tile(D_MODEL), tile(PLE_DIM), table, table]
                + [_resident(w, 2) for w in weights])
    out_shape = (jax.ShapeDtypeStruct((batch, seq, D_MODEL), F32),
                 jax.ShapeDtypeStruct((batch, BLOCK, KV_WIDTH), F32),
                 jax.ShapeDtypeStruct((batch, BLOCK, KV_WIDTH), F32),
                 jax.ShapeDtypeStruct((batch, CONV_K - 1, CONV_WIDTH), F32))
    out_specs = (tile(D_MODEL), per_batch(BLOCK, KV_WIDTH), per_batch(BLOCK, KV_WIDTH),
                 per_batch(CONV_K - 1, CONV_WIDTH))
    scratch = [pltpu.VMEM((BLOCK + tm, KV_WIDTH), BF16), pltpu.VMEM((KV_WIDTH, BLOCK + tm), BF16),
               pltpu.VMEM((SUBLANES + tm, CONV_WIDTH), F32)]
    return pl.pallas_call(
        _prompt_kernel, grid=(batch, seq // tm), in_specs=in_specs, out_specs=out_specs,
        out_shape=out_shape, scratch_shapes=scratch, name="prompt_layer",
        compiler_params=pltpu.CompilerParams(dimension_semantics=("arbitrary", "arbitrary"),
                                             vmem_limit_bytes=VMEM_LIMIT),
    )(sinks_p, x, p, cos, sin, *weights)


def _sample_layer(x, p, cache_k, cache_v, state_conv, sinks_p, weights):
    batch, t_len, _ = x.shape
    n_tok = batch * t_len
    cos, sin = _rope_tables(PAST_LEN + jnp.arange(t_len, dtype=jnp.int32))
    cos, sin = jnp.tile(cos, (batch, 1)), jnp.tile(sin, (batch, 1))
    st1 = jnp.pad(state_conv[:, 1:2], ((0, 0), (0, t_len - 1), (0, 0))).reshape(n_tok, CONV_WIDTH)
    st2 = jnp.pad(state_conv, ((0, 0), (0, t_len - (CONV_K - 1)), (0, 0))).reshape(n_tok, CONV_WIDTH)
    ck = _cache_to_kernel(cache_k)
    cv = _cache_to_kernel(cache_v)
    flat = [x, p, cos, sin, st1, st2]
    n_steps = batch // SAMPLE_STEP_BATCH
    small = [weights[n] for n in SMALL_WEIGHTS]
    big = [weights[n] for n in MATMUL_WEIGHTS]
    hbm = pl.BlockSpec(memory_space=pl.ANY)
    cache_block = pl.BlockSpec((SAMPLE_STEP_BATCH, WINDOW, KV_WIDTH), lambda i: (i, 0, 0))
    in_specs = ([pl.BlockSpec(memory_space=pltpu.SMEM)] + [_resident(a, 1) for a in flat]
                + [cache_block, cache_block] + [_resident(w, 1) for w in small] + [hbm] * len(big))
    out_shape = (jax.ShapeDtypeStruct((batch, t_len, D_MODEL), F32),
                 jax.ShapeDtypeStruct((batch, WINDOW, KV_WIDTH), F32),
                 jax.ShapeDtypeStruct((batch, WINDOW, KV_WIDTH), F32),
                 jax.ShapeDtypeStruct((n_tok, CONV_WIDTH), F32)
                 ) + tuple(jax.ShapeDtypeStruct(w.shape, BF16) for w in big)
    whole = lambda s: pl.BlockSpec(s.shape, lambda i: (0,) * len(s.shape), pipeline_mode=pl.Buffered(1))
    out_specs = (whole(out_shape[0]), cache_block, cache_block, whole(out_shape[3])) + (hbm,) * len(big)
    late = [weights[n] for n in LATE_WEIGHTS]
    scratch = ([pltpu.VMEM((N_Q_HEADS, n_tok, LANES), F32),
                pltpu.VMEM((KV_WIDTH, n_tok), BF16), pltpu.VMEM((KV_WIDTH, n_tok), BF16),
                pltpu.VMEM((KV_WIDTH, n_tok), F32), pltpu.VMEM((KV_WIDTH, n_tok), F32),
                pltpu.VMEM((n_tok, ATTN_WIDTH), F32), pltpu.VMEM((n_tok, CONV_WIDTH), F32)]
               + [pltpu.VMEM(w.shape, BF16) for w in big]
               + [pltpu.VMEM((W_IN_SLOTS, W_IN_CHUNK, IN_WIDTH), F32)]
               + [pltpu.VMEM((w.shape[0] // n_steps, w.shape[1]), F32) for w in late]
               + [pltpu.SemaphoreType.DMA((W_IN_SLOTS,)), pltpu.SemaphoreType.DMA((len(late),)),
                  pltpu.SemaphoreType.DMA((len(big),))])
    y, k_new, v_new, u, *w_bf16 = pl.pallas_call(
        _sample_kernel, grid=(n_steps,), in_specs=in_specs, out_specs=out_specs,
        out_shape=out_shape, scratch_shapes=scratch, name="sample_layer",
        compiler_params=pltpu.CompilerParams(dimension_semantics=("arbitrary",),
                                             vmem_limit_bytes=VMEM_LIMIT),
    )(sinks_p, *flat, ck, cv, *small, *big)
    conv_new = u.reshape(batch, t_len, CONV_WIDTH)[:, t_len - (CONV_K - 1):]
    return y, k_new, v_new, conv_new, dict(zip(MATMUL_WEIGHTS, w_bf16))


def kernel(x_prompt, x_sample, p_prompt, p_sample, cache_k, cache_v, state_conv, g_mix_norm, w_in, g_q, g_k,
           sinks, conv_w, g_attn_out, g_conv_out, w_o, g_mlp_norm, w_up, w_down, g_ple_norm, w_ple_gate, w_ple):
    depth = w_in.shape[0]
    yp, ys = x_prompt, x_sample
    outs = [[] for _ in range(6)]
    for i in range(depth):
        sinks_p, weights = _prepare_weights(
            g_mix_norm[i], w_in[i], g_q[i], g_k[i], sinks[i], conv_w[i], g_attn_out[i], g_conv_out[i],
            w_o[i], g_mlp_norm[i], w_up[i], w_down[i], g_ple_norm[i], w_ple_gate[i], w_ple[i])
        ys, ksn, vsn, csn, w_bf16 = _sample_layer(ys, p_sample[i], cache_k[i], cache_v[i], state_conv[i],
                                                  sinks_p, weights)
        yp, kp, vp, cp = _prompt_layer(yp, p_prompt[i], sinks_p,
                                       [w_bf16.get(n, weights[n]) for n in WEIGHT_NAMES])
        for lst, val in zip(outs, (_cache_from_kernel(kp), _cache_from_kernel(vp), cp,
                                   _cache_from_kernel(ksn), _cache_from_kernel(vsn), csn)):
            lst.append(val)
    return (yp, ys) + tuple(jnp.stack(o) for o in outs)
```

```python
import jax
import jax.numpy as jnp
import numpy as np
from jax import lax
from jax.experimental import pallas as pl
from jax.experimental.pallas import tpu as pltpu

D_MODEL = 1024
HEAD_DIM = 64
N_Q_HEADS = 8
N_KV_HEADS = 2
GQA_GROUP = N_Q_HEADS // N_KV_HEADS
ATTN_WIDTH = N_Q_HEADS * HEAD_DIM
KV_WIDTH = N_KV_HEADS * HEAD_DIM
CONV_WIDTH = D_MODEL - ATTN_WIDTH
CONV_K = 3
WINDOW = 128
BLOCK = 128
ROPE_THETA = 10000.0
D_FF = 4 * D_MODEL
PLE_DIM = 256
EPS = 1e-6
NEG = -1e30
LOG2_E = 1.4426950408889634
PAST_LEN = 16384
IN_WIDTH = ATTN_WIDTH + 2 * KV_WIDTH + 3 * CONV_WIDTH

LANES = 128
SUBLANES = 8
N_QCOL = ATTN_WIDTH // LANES
SEQ_TILE = 512
SUB_TILES = (256, 256)
FF_CHUNK = 1024
SAMPLE_STEP_BATCH = 8
VMEM_LIMIT = 56 * 1024 * 1024

O_K = ATTN_WIDTH
O_V = O_K + KV_WIDTH
O_B = O_V + KV_WIDTH
O_C = O_B + CONV_WIDTH
O_H = O_C + CONV_WIDTH

BF16 = jnp.bfloat16
F32 = jnp.float32


def _dot(a, b):
    return jnp.dot(a, b, preferred_element_type=F32)


def _dot_t(a, b):
    return lax.dot_general(a, b, (((1,), (1,)), ((), ())), preferred_element_type=F32)


def _rms(x, g):
    return x * lax.rsqrt(jnp.mean(x * x, axis=-1, keepdims=True) + EPS) * g


def _head_norm_rope(t, ones_bd, g, cos, sin):
    ssq = _dot((t * t).astype(BF16), ones_bd)
    t = t * lax.rsqrt(ssq * (1.0 / HEAD_DIM) + EPS)
    lane = lax.broadcasted_iota(jnp.int32, (t.shape[0], LANES), 1)
    first_half = (lane & (HEAD_DIM - 1)) < HEAD_DIM // 2
    cols = []
    for m in range(t.shape[1] // LANES):
        c = t[:, m * LANES:(m + 1) * LANES] * g
        up = pltpu.roll(c, LANES - HEAD_DIM // 2, axis=1)
        dn = pltpu.roll(c, HEAD_DIM // 2, axis=1)
        cols.append(c * cos + jnp.where(first_half, up, dn) * sin)
    return cols


def _front(x, cos, sin, refs):
    h = _rms(x, refs["g_mix"][...]).astype(BF16)
    z = _dot(h, refs["w_in"][...])
    qcols = _head_norm_rope(z[:, :O_K], refs["bd_q"][...], refs["g_q"][...], cos, sin)
    (k,) = _head_norm_rope(z[:, O_K:O_V], refs["bd_k"][...], refs["g_k"][...], cos, sin)
    v = z[:, O_V:O_B]
    b_gate = z[:, O_B:O_C]
    u = z[:, O_C:O_H] * z[:, O_H:]
    return qcols, k, v, b_gate, u


def _tail_steps(x, o_attn, o_conv, p, refs, store):
    mixed = jnp.concatenate([_rms(o_attn, refs["g_attn"][...]), _rms(o_conv, refs["g_conv"][...])], axis=1)
    x = x + _dot(mixed.astype(BF16), refs["w_o"][...])
    hm = _rms(x, refs["g_mlp"][...]).astype(BF16)
    yield
    for c in range(D_FF // FF_CHUNK):
        up = _dot(hm, refs["w_up"][:, c * FF_CHUNK:(c + 1) * FF_CHUNK])
        act = jnp.square(jnp.maximum(up.astype(BF16), 0.0))
        x = x + _dot(act, refs["w_down"][c * FF_CHUNK:(c + 1) * FF_CHUNK, :])
        yield
    gate = jax.nn.sigmoid(_dot(_rms(x, refs["g_ple"][...]).astype(BF16), refs["w_gate"][...]))
    store(x + gate * _dot(p.astype(BF16), refs["w_ple"][...]))


def _run(steps):
    for _ in steps:
        pass


def _split_heads(col, c):
    group = (2 * c) // GQA_GROUP
    lo = lax.broadcasted_iota(jnp.int32, col.shape, 1) < HEAD_DIM
    swapped = pltpu.roll(col, HEAD_DIM, axis=1)
    if group == 0:
        return jnp.where(lo, col, 0.0), jnp.where(lo, swapped, 0.0)
    return jnp.where(lo, 0.0, swapped), jnp.where(lo, 0.0, col)


def _sink_column(sinks_ref, rows_per_head):
    return jnp.concatenate(
        [jnp.full((rows_per_head, 1), sinks_ref[i] * LOG2_E, F32) for i in range(N_Q_HEADS)], axis=0)


def _merge_heads(pv, rows_per_head, c):
    group = (2 * c) // GQA_GROUP
    lo = lax.broadcasted_iota(jnp.int32, (rows_per_head, LANES), 1) < HEAD_DIM
    a = pv[(2 * c) * rows_per_head:(2 * c + 1) * rows_per_head]
    b = pv[(2 * c + 1) * rows_per_head:(2 * c + 2) * rows_per_head]
    if group == 0:
        return jnp.where(lo, a, pltpu.roll(b, HEAD_DIM, axis=1))
    return jnp.where(lo, pltpu.roll(a, HEAD_DIM, axis=1), b)


WEIGHT_NAMES = ("g_mix", "w_in", "bd_q", "bd_k", "g_q", "g_k", "conv_w", "g_attn", "g_conv", "w_o",
                "g_mlp", "w_up", "w_down", "g_ple", "w_gate", "w_ple")
MATMUL_WEIGHTS = ("w_in", "w_o", "w_up", "w_down", "w_gate", "w_ple")
LATE_WEIGHTS = MATMUL_WEIGHTS[1:]
SMALL_WEIGHTS = tuple(n for n in WEIGHT_NAMES if n not in MATMUL_WEIGHTS)
W_IN_CHUNK = 64
W_IN_SLOTS = 4


def _prompt_kernel(sinks_ref, x_ref, p_ref, cos_ref, sin_ref, *rest):
    nw = len(WEIGHT_NAMES)
    refs = dict(zip(WEIGHT_NAMES, rest[:nw]))
    y_ref, kout_ref, vout_ref, convout_ref, kbuf, vbuf, ubuf = rest[nw:]
    tm = x_ref.shape[0]
    j = pl.program_id(1)

    @pl.when(j == 0)
    def _():
        kbuf[0:BLOCK, :] = jnp.zeros((BLOCK, KV_WIDTH), BF16)
        vbuf[:, 0:BLOCK] = jnp.zeros((KV_WIDTH, BLOCK), BF16)
        ubuf[0:SUBLANES, :] = jnp.zeros((SUBLANES, CONV_WIDTH), F32)

    kj = lax.broadcasted_iota(jnp.int32, (BLOCK, N_Q_HEADS * BLOCK), 0)
    qi = lax.broadcasted_iota(jnp.int32, (BLOCK, N_Q_HEADS * BLOCK), 1) & (BLOCK - 1)
    own = kj <= qi
    sink = jnp.concatenate([jnp.full((1, BLOCK), sinks_ref[i] * LOG2_E, F32) for i in range(N_Q_HEADS)],
                           axis=1)
    cw = refs["conv_w"]

    last = {}

    def sub_tile_steps(r0, n):
        rows = slice(r0, r0 + n)
        x = x_ref[rows, :]
        qcols, k, v, b_gate, u = _front(x, cos_ref[rows, :], sin_ref[rows, :], refs)
        kbuf[BLOCK + r0:BLOCK + r0 + n, :] = k.astype(BF16)
        v_t = v.T
        vbuf[:, BLOCK + r0:BLOCK + r0 + n] = v_t.astype(BF16)
        ubuf[SUBLANES + r0:SUBLANES + r0 + n, :] = u
        last.update(k=k, v_t=v_t, u=u, n=n)
        yield

        o_blocks = []
        for i in range(n // BLOCK):
            g = r0 // BLOCK + i
            qs = []
            for c in range(N_QCOL):
                qs.extend(_split_heads(qcols[c][i * BLOCK:(i + 1) * BLOCK], c))
            qstack = jnp.concatenate(qs, axis=0).astype(BF16)
            s2 = _dot_t(kbuf[g * BLOCK:(g + 2) * BLOCK, :], qstack)
            s_prev = s2[:BLOCK]
            if g == 0:
                s_prev = jnp.where(j == 0, NEG, s_prev)
            s = jnp.where(own, s2[BLOCK:], s_prev)
            mx = jnp.maximum(jnp.max(s, axis=0, keepdims=True), sink)
            e = jnp.exp2(s - mx)
            denom = jnp.sum(e, axis=0, keepdims=True) + jnp.exp2(sink - mx)
            e2 = jnp.concatenate([jnp.where(own, 0.0, e), jnp.where(own, e, 0.0)], axis=0).astype(BF16)
            pv = _dot(vbuf[:, g * BLOCK:(g + 2) * BLOCK], e2) * (1.0 / denom)
            cols = []
            for c in range(N_QCOL):
                d0 = (2 * c) // GQA_GROUP * HEAD_DIM
                col_t = jnp.concatenate([pv[d0:d0 + HEAD_DIM, (2 * c) * BLOCK:(2 * c + 1) * BLOCK],
                                         pv[d0:d0 + HEAD_DIM, (2 * c + 1) * BLOCK:(2 * c + 2) * BLOCK]], axis=0)
                cols.append(col_t.T)
            o_blocks.append(jnp.concatenate(cols, axis=1))
            yield
        o_attn = jnp.concatenate(o_blocks, axis=0)

        conv = (cw[0:1, :] * ubuf[SUBLANES - 2 + r0:SUBLANES - 2 + r0 + n, :]
                + cw[1:2, :] * ubuf[SUBLANES - 1 + r0:SUBLANES - 1 + r0 + n, :]
                + cw[2:3, :] * u)

        def store(y):
            y_ref[rows, :] = y

        yield from _tail_steps(x, o_attn, b_gate * conv, p_ref[rows, :], refs, store)

    assert sum(SUB_TILES) == tm
    starts = [sum(SUB_TILES[:t]) for t in range(len(SUB_TILES))]
    tiles = [sub_tile_steps(r0, n) for r0, n in zip(starts, SUB_TILES)]
    begin = [sum(1 + n // BLOCK for n in SUB_TILES[:t]) for t in range(len(SUB_TILES))]
    live = list(range(len(tiles)))
    rnd = 0
    while live:
        for t in reversed([t for t in live if rnd >= begin[t]]):
            if next(tiles[t], True) is True:
                live.remove(t)
        rnd += 1

    n_last = last["n"]
    kout_ref[...] = last["k"][n_last - BLOCK:, :].T
    vout_ref[...] = last["v_t"][:, n_last - BLOCK:]
    convout_ref[...] = last["u"][n_last - (CONV_K - 1):, :]
    kbuf[0:BLOCK, :] = kbuf[tm:tm + BLOCK, :]
    vbuf[:, 0:BLOCK] = vbuf[:, tm:tm + BLOCK]
    ubuf[0:SUBLANES, :] = ubuf[tm:tm + SUBLANES, :]


def _sample_kernel(sinks_ref, x_ref, p_ref, cos_ref, sin_ref, st1_ref, st2_ref, ck_ref, cv_ref, *rest):
    ns, nm, nl = len(SMALL_WEIGHTS), len(MATMUL_WEIGHTS), len(LATE_WEIGHTS)
    refs = dict(zip(SMALL_WEIGHTS, rest[:ns]))
    w_f32 = dict(zip(MATMUL_WEIGHTS, rest[ns:ns + nm]))
    y_ref, kout_ref, vout_ref, uout_ref = rest[ns + nm:ns + nm + 4]
    w_out = dict(zip(MATMUL_WEIGHTS, rest[ns + nm + 4:ns + 2 * nm + 4]))
    scratch = rest[ns + 2 * nm + 4:]
    qbuf, kbuf, vbuf, knew, vnew, obuf, bgbuf = scratch[:7]
    w_vmem = dict(zip(MATMUL_WEIGHTS, scratch[7:7 + nm]))
    in_stage = scratch[7 + nm]
    late_stage = dict(zip(LATE_WEIGHTS, scratch[8 + nm:8 + nm + nl]))
    in_sem, late_sem, out_sem = scratch[8 + nm + nl:]
    refs.update(w_vmem)
    t_len = x_ref.shape[1]
    n_tok = x_ref.shape[0] * t_len
    step = pl.program_id(0)
    n_steps = pl.num_programs(0)
    step_batch = ck_ref.shape[0]
    pair_rows = 2 * t_len
    chunk_pairs = BLOCK // pair_rows

    n_slots = in_stage.shape[0]

    def in_copy(i):
        return pltpu.make_async_copy(w_f32["w_in"].at[pl.ds(i * W_IN_CHUNK, W_IN_CHUNK), :],
                                     in_stage.at[i % n_slots], in_sem.at[i % n_slots])

    def late_copy(n, name):
        rows = late_stage[name].shape[0]
        return pltpu.make_async_copy(w_f32[name].at[pl.ds(step * rows, rows), :], late_stage[name],
                                     late_sem.at[n])

    def out_copy(n, name):
        return pltpu.make_async_copy(w_vmem[name], w_out[name], out_sem.at[n])

    for n, name in enumerate(LATE_WEIGHTS):
        late_copy(n, name).start()

    @pl.when(step == 0)
    def _():
        n_chunks = D_MODEL // W_IN_CHUNK
        for i in range(n_slots):
            in_copy(i).start()
        for i in range(n_chunks):
            in_copy(i).wait()
            w_vmem["w_in"][i * W_IN_CHUNK:(i + 1) * W_IN_CHUNK, :] = in_stage[i % n_slots].astype(BF16)
            if i + n_slots < n_chunks:
                in_copy(i + n_slots).start()
        out_copy(0, "w_in").start()
        qcols, k, v, b_gate, u = _front(x_ref[...].reshape(n_tok, D_MODEL), cos_ref[...], sin_ref[...], refs)
        for c in range(N_QCOL):
            qbuf[2 * c], qbuf[2 * c + 1] = _split_heads(qcols[c], c)
        k_t, v_t = k.T, v.T
        kbuf[...] = k_t.astype(BF16)
        vbuf[...] = v_t.astype(BF16)
        knew[...] = k_t
        vnew[...] = v_t
        bgbuf[...] = b_gate
        uout_ref[...] = u

    n_rows = N_Q_HEADS * pair_rows
    row = lax.broadcasted_iota(jnp.int32, (n_rows, 3 * BLOCK), 0)
    col = lax.broadcasted_iota(jnp.int32, (n_rows, 3 * BLOCK), 1)
    row_b = (row % pair_rows) // t_len
    row_t = row % t_len
    cache_ok = (col < 2 * BLOCK) & (col // BLOCK == row_b) & ((col % BLOCK) > row_t)
    new_col = col - 2 * BLOCK
    sink = _sink_column(sinks_ref, pair_rows)

    def pair_stages(i):
        pair = step * (step_batch // 2) + i
        r0 = pl.multiple_of(pair * pair_rows, pair_rows)
        c0 = pl.multiple_of((pair // chunk_pairs) * BLOCK, BLOCK)
        q = qbuf[:, pl.ds(r0, pair_rows), :].reshape(n_rows, LANES).astype(BF16)
        ck = [ck_ref[2 * i + bb] for bb in range(2)]
        cv = [cv_ref[2 * i + bb] for bb in range(2)]
        keys = jnp.concatenate([c.astype(BF16) for c in ck] + [kbuf[:, pl.ds(c0, BLOCK)]], axis=1)
        vals = jnp.concatenate([c.astype(BF16) for c in cv] + [vbuf[:, pl.ds(c0, BLOCK)]], axis=1)
        s = _dot(q, keys)
        yield
        new_ok = ((col >= 2 * BLOCK) & (new_col // t_len == (pair % chunk_pairs) * 2 + row_b)
                  & (new_col % t_len <= row_t))
        ok = cache_ok | new_ok
        s = jnp.where(ok, s, NEG)
        mx = jnp.maximum(jnp.max(s, axis=-1, keepdims=True), sink)
        e = jnp.where(ok, jnp.exp2(s - mx), 0.0)
        denom = jnp.sum(e, axis=-1, keepdims=True) + jnp.exp2(sink - mx)
        yield
        pv = _dot_t(e.astype(BF16), vals) * (1.0 / denom)
        for c in range(N_QCOL):
            obuf[pl.ds(r0, pair_rows), c * LANES:(c + 1) * LANES] = _merge_heads(pv, pair_rows, c)
        yield
        k_chunk = knew[:, pl.ds(c0, BLOCK)]
        v_chunk = vnew[:, pl.ds(c0, BLOCK)]
        keep = lax.broadcasted_iota(jnp.int32, (KV_WIDTH, BLOCK), 1) < BLOCK - t_len
        for bb in range(2):
            to_tail = (BLOCK - t_len) - t_len * ((pair % chunk_pairs) * 2 + bb)
            kout_ref[2 * i + bb] = jnp.where(keep, pltpu.roll(ck[bb], BLOCK - t_len, axis=1),
                                             pltpu.roll(k_chunk, to_tail, axis=1))
            vout_ref[2 * i + bb] = jnp.where(keep, pltpu.roll(cv[bb], BLOCK - t_len, axis=1),
                                             pltpu.roll(v_chunk, to_tail, axis=1))

    pairs = [pair_stages(i) for i in range(step_batch // 2)]
    for _ in range(4):
        for stages in pairs:
            next(stages, None)

    for n, name in enumerate(LATE_WEIGHTS):
        rows = late_stage[name].shape[0]
        late_copy(n, name).wait()
        r0 = pl.multiple_of(step * rows, rows)
        w_vmem[name][pl.ds(r0, rows), :] = late_stage[name][...].astype(BF16)

    @pl.when(step == n_steps - 1)
    def _():
        for n, name in enumerate(LATE_WEIGHTS):
            out_copy(n + 1, name).start()
        u = uout_ref[...]
        tok = lax.broadcasted_iota(jnp.int32, u.shape, 0) % t_len
        um1 = jnp.where(tok >= 1, pltpu.roll(u, 1, axis=0), st1_ref[...])
        um2 = jnp.where(tok >= 2, pltpu.roll(u, 2, axis=0), st2_ref[...])
        cw = refs["conv_w"]
        conv = cw[0:1, :] * um2 + cw[1:2, :] * um1 + cw[2:3, :] * u

        def store(y):
            y_ref[...] = y.reshape(y_ref.shape)

        _run(_tail_steps(x_ref[...].reshape(n_tok, D_MODEL), obuf[...], bgbuf[...] * conv,
                         p_ref[...].reshape(n_tok, PLE_DIM), refs, store))
        for n, name in enumerate(MATMUL_WEIGHTS):
            out_copy(n, name).wait()


def _rope_tables(pos):
    inv_freq = ROPE_THETA ** (-jnp.arange(0, HEAD_DIM, 2, dtype=F32) / HEAD_DIM)
    ang = pos.astype(F32)[:, None] * inv_freq[None, :]
    cos, sin = jnp.cos(ang), jnp.sin(ang)
    cos_h = jnp.concatenate([cos, cos], axis=1)
    sin_h = jnp.concatenate([-sin, sin], axis=1)
    return jnp.tile(cos_h, (1, LANES // HEAD_DIM)), jnp.tile(sin_h, (1, LANES // HEAD_DIM))


def _cache_to_kernel(c):
    batch, keys = c.shape[0], c.shape[1]
    return jnp.transpose(c, (0, 2, 3, 1)).reshape(batch, KV_WIDTH, keys)


def _cache_from_kernel(c):
    batch, _, keys = c.shape
    return jnp.transpose(c.reshape(batch, N_KV_HEADS, HEAD_DIM, keys), (0, 3, 1, 2))


def _block_diag_ones(width):
    idx = np.arange(width) // HEAD_DIM
    return jnp.asarray(idx[:, None] == idx[None, :], dtype=BF16)


def _prepare_weights(g_mix_norm, w_in, g_q, g_k, sinks, conv_w, g_attn_out, g_conv_out, w_o,
                     g_mlp_norm, w_up, w_down, g_ple_norm, w_ple_gate, w_ple):
    row = lambda g: g.reshape(1, -1).astype(F32)
    tile_head = lambda g: jnp.tile(g.reshape(1, HEAD_DIM).astype(F32), (1, LANES // HEAD_DIM))
    weights = dict(
        g_mix=row(g_mix_norm), w_in=w_in,
        bd_q=_block_diag_ones(ATTN_WIDTH), bd_k=_block_diag_ones(KV_WIDTH),
        g_q=tile_head(g_q) * (HEAD_DIM ** -0.5 * LOG2_E), g_k=tile_head(g_k), conv_w=conv_w.astype(F32),
        g_attn=row(g_attn_out), g_conv=row(g_conv_out), w_o=w_o,
        g_mlp=row(g_mlp_norm), w_up=w_up, w_down=w_down,
        g_ple=row(g_ple_norm), w_gate=w_ple_gate, w_ple=w_ple)
    return sinks.astype(F32), weights


def _resident(a, n_grid):
    zeros = (0,) * a.ndim
    index_map = (lambda b, j: zeros) if n_grid == 2 else (lambda i: zeros)
    return pl.BlockSpec(a.shape, index_map, pipeline_mode=pl.Buffered(1))


def _prompt_layer(x, p, sinks_p, weights):
    batch, seq, _ = x.shape
    tm = SEQ_TILE
    cos, sin = _rope_tables(jnp.arange(seq, dtype=jnp.int32))
    tile = lambda w: pl.BlockSpec((None, tm, w), lambda b, j: (b, j, 0))
    per_batch = lambda r, w: pl.BlockSpec((None, r, w), lambda b, j: (b, 0, 0))
    table = pl.BlockSpec((tm, LANES), lambda b, j: (j, 0))
    in_specs = ([pl.BlockSpec(memory_space=pltpu.SMEM), tile(D_MODEL), tile(PLE_DIM), table, table]
                + [_resident(w, 2) for w in weights])
    out_shape = (jax.ShapeDtypeStruct((batch, seq, D_MODEL), F32),
                 jax.ShapeDtypeStruct((batch, BLOCK, KV_WIDTH), F32),
                 jax.ShapeDtypeStruct((batch, BLOCK, KV_WIDTH), F32),
                 jax.ShapeDtypeStruct((batch, CONV_K - 1, CONV_WIDTH), F32))
    out_specs = (tile(D_MODEL), per_batch(BLOCK, KV_WIDTH), per_batch(BLOCK, KV_WIDTH),
                 per_batch(CONV_K - 1, CONV_WIDTH))
    scratch = [pltpu.VMEM((BLOCK + tm, KV_WIDTH), BF16), pltpu.VMEM((KV_WIDTH, BLOCK + tm), BF16),
               pltpu.VMEM((SUBLANES + tm, CONV_WIDTH), F32)]
    return pl.pallas_call(
        _prompt_kernel, grid=(batch, seq // tm), in_specs=in_specs, out_specs=out_specs,
        out_shape=out_shape, scratch_shapes=scratch, name="prompt_layer",
        compiler_params=pltpu.CompilerParams(dimension_semantics=("arbitrary", "arbitrary"),
                                             vmem_limit_bytes=VMEM_LIMIT),
    )(sinks_p, x, p, cos, sin, *weights)


def _sample_layer(x, p, cache_k, cache_v, state_conv, sinks_p, weights):
    batch, t_len, _ = x.shape
    n_tok = batch * t_len
    cos, sin = _rope_tables(PAST_LEN + jnp.arange(t_len, dtype=jnp.int32))
    cos, sin = jnp.tile(cos, (batch, 1)), jnp.tile(sin, (batch, 1))
    st1 = jnp.pad(state_conv[:, 1:2], ((0, 0), (0, t_len - 1), (0, 0))).reshape(n_tok, CONV_WIDTH)
    st2 = jnp.pad(state_conv, ((0, 0), (0, t_len - (CONV_K - 1)), (0, 0))).reshape(n_tok, CONV_WIDTH)
    ck = _cache_to_kernel(cache_k)
    cv = _cache_to_kernel(cache_v)
    flat = [x, p, cos, sin, st1, st2]
    n_steps = batch // SAMPLE_STEP_BATCH
    small = [weights[n] for n in SMALL_WEIGHTS]
    big = [weights[n] for n in MATMUL_WEIGHTS]
    hbm = pl.BlockSpec(memory_space=pl.ANY)
    cache_block = pl.BlockSpec((SAMPLE_STEP_BATCH, WINDOW, KV_WIDTH), lambda i: (i, 0, 0))
    in_specs = ([pl.BlockSpec(memory_space=pltpu.SMEM)] + [_resident(a, 1) for a in flat]
                + [cache_block, cache_block] + [_resident(w, 1) for w in small] + [hbm] * len(big))
    out_shape = (jax.ShapeDtypeStruct((batch, t_len, D_MODEL), F32),
                 jax.ShapeDtypeStruct((batch, WINDOW, KV_WIDTH), F32),
                 jax.ShapeDtypeStruct((batch, WINDOW, KV_WIDTH), F32),
                 jax.ShapeDtypeStruct((n_tok, CONV_WIDTH), F32)
                 ) + tuple(jax.ShapeDtypeStruct(w.shape, BF16) for w in big)
    whole = lambda s: pl.BlockSpec(s.shape, lambda i: (0,) * len(s.shape), pipeline_mode=pl.Buffered(1))
    out_specs = (whole(out_shape[0]), cache_block, cache_block, whole(out_shape[3])) + (hbm,) * len(big)
    late = [weights[n] for n in LATE_WEIGHTS]
    scratch = ([pltpu.VMEM((N_Q_HEADS, n_tok, LANES), F32),
                pltpu.VMEM((KV_WIDTH, n_tok), BF16), pltpu.VMEM((KV_WIDTH, n_tok), BF16),
                pltpu.VMEM((KV_WIDTH, n_tok), F32), pltpu.VMEM((KV_WIDTH, n_tok), F32),
                pltpu.VMEM((n_tok, ATTN_WIDTH), F32), pltpu.VMEM((n_tok, CONV_WIDTH), F32)]
               + [pltpu.VMEM(w.shape, BF16) for w in big]
               + [pltpu.VMEM((W_IN_SLOTS, W_IN_CHUNK, IN_WIDTH), F32)]
               + [pltpu.VMEM((w.shape[0] // n_steps, w.shape[1]), F32) for w in late]
               + [pltpu.SemaphoreType.DMA((W_IN_SLOTS,)), pltpu.SemaphoreType.DMA((len(late),)),
                  pltpu.SemaphoreType.DMA((len(big),))])
    y, k_new, v_new, u, *w_bf16 = pl.pallas_call(
        _sample_kernel, grid=(n_steps,), in_specs=in_specs, out_specs=out_specs,
        out_shape=out_shape, scratch_shapes=scratch, name="sample_layer",
        compiler_params=pltpu.CompilerParams(dimension_semantics=("arbitrary",),
                                             vmem_limit_bytes=VMEM_LIMIT),
    )(sinks_p, *flat, ck, cv, *small, *big)
    conv_new = u.reshape(batch, t_len, CONV_WIDTH)[:, t_len - (CONV_K - 1):]
    return y, k_new, v_new, conv_new, dict(zip(MATMUL_WEIGHTS, w_bf16))


def kernel(x_prompt, x_sample, p_prompt, p_sample, cache_k, cache_v, state_conv, g_mix_norm, w_in, g_q, g_k,
           sinks, conv_w, g_attn_out, g_conv_out, w_o, g_mlp_norm, w_up, w_down, g_ple_norm, w_ple_gate, w_ple):
    depth = w_in.shape[0]
    yp, ys = x_prompt, x_sample
    outs = [[] for _ in range(6)]
    for i in range(depth):
        sinks_p, weights = _prepare_weights(
            g_mix_norm[i], w_in[i], g_q[i], g_k[i], sinks[i], conv_w[i], g_attn_out[i], g_conv_out[i],
            w_o[i], g_mlp_norm[i], w_up[i], w_down[i], g_ple_norm[i], w_ple_gate[i], w_ple[i])
        ys, ksn, vsn, csn, w_bf16 = _sample_layer(ys, p_sample[i], cache_k[i], cache_v[i], state_conv[i],
                                                  sinks_p, weights)
        yp, kp, vp, cp = _prompt_layer(yp, p_prompt[i], sinks_p,
                                       [w_bf16.get(n, weights[n]) for n in WEIGHT_NAMES])
        for lst, val in zip(outs, (_cache_from_kernel(kp), _cache_from_kernel(vp), cp,
                                   _cache_from_kernel(ksn), _cache_from_kernel(vsn), csn)):
            lst.append(val)
    return (yp, ys) + tuple(jnp.stack(o) for o in outs)
```

```python
import jax
import jax.numpy as jnp
import numpy as np
from jax import lax
from jax.experimental import pallas as pl
from jax.experimental.pallas import tpu as pltpu

D_MODEL = 1024
HEAD_DIM = 64
N_Q_HEADS = 8
N_KV_HEADS = 2
GQA_GROUP = N_Q_HEADS // N_KV_HEADS
ATTN_WIDTH = N_Q_HEADS * HEAD_DIM
KV_WIDTH = N_KV_HEADS * HEAD_DIM
CONV_WIDTH = D_MODEL - ATTN_WIDTH
CONV_K = 3
WINDOW = 128
BLOCK = 128
ROPE_THETA = 10000.0
D_FF = 4 * D_MODEL
PLE_DIM = 256
EPS = 1e-6
NEG = -1e30
LOG2_E = 1.4426950408889634
PAST_LEN = 16384
IN_WIDTH = ATTN_WIDTH + 2 * KV_WIDTH + 3 * CONV_WIDTH

LANES = 128
SUBLANES = 8
N_QCOL = ATTN_WIDTH // LANES
SEQ_TILE = 512
SUB_TILES = (256, 256)
FF_CHUNK = 1024
SAMPLE_STEP_BATCH = 8
VMEM_LIMIT = 56 * 1024 * 1024
SAMPLE_VMEM_LIMIT = 60 * 1024 * 1024

O_K = ATTN_WIDTH
O_V = O_K + KV_WIDTH
O_B = O_V + KV_WIDTH
O_C = O_B + CONV_WIDTH
O_H = O_C + CONV_WIDTH

BF16 = jnp.bfloat16
F32 = jnp.float32


def _dot(a, b):
    return jnp.dot(a, b, preferred_element_type=F32)


def _dot_t(a, b):
    return lax.dot_general(a, b, (((1,), (1,)), ((), ())), preferred_element_type=F32)


def _rms(x, g):
    return x * lax.rsqrt(jnp.mean(x * x, axis=-1, keepdims=True) + EPS) * g


def _head_norm_rope(t, ones_bd, g, cos, sin):
    ssq = _dot((t * t).astype(BF16), ones_bd)
    t = t * lax.rsqrt(ssq * (1.0 / HEAD_DIM) + EPS)
    lane = lax.broadcasted_iota(jnp.int32, (t.shape[0], LANES), 1)
    first_half = (lane & (HEAD_DIM - 1)) < HEAD_DIM // 2
    cols = []
    for m in range(t.shape[1] // LANES):
        c = t[:, m * LANES:(m + 1) * LANES] * g
        up = pltpu.roll(c, LANES - HEAD_DIM // 2, axis=1)
        dn = pltpu.roll(c, HEAD_DIM // 2, axis=1)
        cols.append(c * cos + jnp.where(first_half, up, dn) * sin)
    return cols


def _front(x, cos, sin, refs):
    h = _rms(x, refs["g_mix"][...]).astype(BF16)
    z = _dot(h, refs["w_in"][...])
    qcols = _head_norm_rope(z[:, :O_K], refs["bd_q"][...], refs["g_q"][...], cos, sin)
    (k,) = _head_norm_rope(z[:, O_K:O_V], refs["bd_k"][...], refs["g_k"][...], cos, sin)
    v = z[:, O_V:O_B]
    b_gate = z[:, O_B:O_C]
    u = z[:, O_C:O_H] * z[:, O_H:]
    return qcols, k, v, b_gate, u


def _tail_steps(x, o_attn, o_conv, p, refs, store):
    mixed = jnp.concatenate([_rms(o_attn, refs["g_attn"][...]), _rms(o_conv, refs["g_conv"][...])], axis=1)
    x = x + _dot(mixed.astype(BF16), refs["w_o"][...])
    hm = _rms(x, refs["g_mlp"][...]).astype(BF16)
    yield
    for c in range(D_FF // FF_CHUNK):
        up = _dot(hm, refs["w_up"][:, c * FF_CHUNK:(c + 1) * FF_CHUNK])
        act = jnp.square(jnp.maximum(up.astype(BF16), 0.0))
        x = x + _dot(act, refs["w_down"][c * FF_CHUNK:(c + 1) * FF_CHUNK, :])
        yield
    gate = jax.nn.sigmoid(_dot(_rms(x, refs["g_ple"][...]).astype(BF16), refs["w_gate"][...]))
    store(x + gate * _dot(p.astype(BF16), refs["w_ple"][...]))


def _run(steps):
    for _ in steps:
        pass


def _split_heads(col, c):
    group = (2 * c) // GQA_GROUP
    lo = lax.broadcasted_iota(jnp.int32, col.shape, 1) < HEAD_DIM
    swapped = pltpu.roll(col, HEAD_DIM, axis=1)
    if group == 0:
        return jnp.where(lo, col, 0.0), jnp.where(lo, swapped, 0.0)
    return jnp.where(lo, 0.0, swapped), jnp.where(lo, 0.0, col)


def _sink_column(sinks_ref, rows_per_head):
    return jnp.concatenate(
        [jnp.full((rows_per_head, 1), sinks_ref[i] * LOG2_E, F32) for i in range(N_Q_HEADS)], axis=0)


def _merge_heads(pv, rows_per_head, c):
    group = (2 * c) // GQA_GROUP
    lo = lax.broadcasted_iota(jnp.int32, (rows_per_head, LANES), 1) < HEAD_DIM
    a = pv[(2 * c) * rows_per_head:(2 * c + 1) * rows_per_head]
    b = pv[(2 * c + 1) * rows_per_head:(2 * c + 2) * rows_per_head]
    if group == 0:
        return jnp.where(lo, a, pltpu.roll(b, HEAD_DIM, axis=1))
    return jnp.where(lo, pltpu.roll(a, HEAD_DIM, axis=1), b)


WEIGHT_NAMES = ("g_mix", "w_in", "bd_q", "bd_k", "g_q", "g_k", "conv_w", "g_attn", "g_conv", "w_o",
                "g_mlp", "w_up", "w_down", "g_ple", "w_gate", "w_ple")
MATMUL_WEIGHTS = ("w_in", "w_o", "w_up", "w_down", "w_gate", "w_ple")
LATE_WEIGHTS = MATMUL_WEIGHTS[1:]
SMALL_WEIGHTS = tuple(n for n in WEIGHT_NAMES if n not in MATMUL_WEIGHTS)
W_IN_CHUNK = 64
W_IN_SLOTS = 4


def _prompt_kernel(sinks_ref, x_ref, p_ref, cos_ref, sin_ref, *rest):
    nw = len(WEIGHT_NAMES)
    refs = dict(zip(WEIGHT_NAMES, rest[:nw]))
    y_ref, kout_ref, vout_ref, convout_ref, kbuf, vbuf, ubuf = rest[nw:]
    tm = x_ref.shape[0]
    j = pl.program_id(1)

    @pl.when(j == 0)
    def _():
        kbuf[0:BLOCK, :] = jnp.zeros((BLOCK, KV_WIDTH), BF16)
        vbuf[:, 0:BLOCK] = jnp.zeros((KV_WIDTH, BLOCK), BF16)
        ubuf[0:SUBLANES, :] = jnp.zeros((SUBLANES, CONV_WIDTH), F32)

    kj = lax.broadcasted_iota(jnp.int32, (BLOCK, N_Q_HEADS * BLOCK), 0)
    qi = lax.broadcasted_iota(jnp.int32, (BLOCK, N_Q_HEADS * BLOCK), 1) & (BLOCK - 1)
    own = kj <= qi
    sink = jnp.concatenate([jnp.full((1, BLOCK), sinks_ref[i] * LOG2_E, F32) for i in range(N_Q_HEADS)],
                           axis=1)
    cw = refs["conv_w"]

    last = {}

    def sub_tile_steps(r0, n):
        rows = slice(r0, r0 + n)
        x = x_ref[rows, :]
        qcols, k, v, b_gate, u = _front(x, cos_ref[rows, :], sin_ref[rows, :], refs)
        kbuf[BLOCK + r0:BLOCK + r0 + n, :] = k.astype(BF16)
        v_t = v.T
        vbuf[:, BLOCK + r0:BLOCK + r0 + n] = v_t.astype(BF16)
        ubuf[SUBLANES + r0:SUBLANES + r0 + n, :] = u
        last.update(k=k, v_t=v_t, u=u, n=n)
        yield

        o_blocks = []
        for i in range(n // BLOCK):
            g = r0 // BLOCK + i
            qs = []
            for c in range(N_QCOL):
                qs.extend(_split_heads(qcols[c][i * BLOCK:(i + 1) * BLOCK], c))
            qstack = jnp.concatenate(qs, axis=0).astype(BF16)
            s2 = _dot_t(kbuf[g * BLOCK:(g + 2) * BLOCK, :], qstack)
            s_prev = s2[:BLOCK]
            if g == 0:
                s_prev = jnp.where(j == 0, NEG, s_prev)
            s = jnp.where(own, s2[BLOCK:], s_prev)
            mx = jnp.maximum(jnp.max(s, axis=0, keepdims=True), sink)
            e = jnp.exp2(s - mx)
            denom = jnp.sum(e, axis=0, keepdims=True) + jnp.exp2(sink - mx)
            e2 = jnp.concatenate([jnp.where(own, 0.0, e), jnp.where(own, e, 0.0)], axis=0).astype(BF16)
            pv = _dot(vbuf[:, g * BLOCK:(g + 2) * BLOCK], e2) * (1.0 / denom)
            cols = []
            for c in range(N_QCOL):
                d0 = (2 * c) // GQA_GROUP * HEAD_DIM
                col_t = jnp.concatenate([pv[d0:d0 + HEAD_DIM, (2 * c) * BLOCK:(2 * c + 1) * BLOCK],
                                         pv[d0:d0 + HEAD_DIM, (2 * c + 1) * BLOCK:(2 * c + 2) * BLOCK]], axis=0)
                cols.append(col_t.T)
            o_blocks.append(jnp.concatenate(cols, axis=1))
            yield
        o_attn = jnp.concatenate(o_blocks, axis=0)

        conv = (cw[0:1, :] * ubuf[SUBLANES - 2 + r0:SUBLANES - 2 + r0 + n, :]
                + cw[1:2, :] * ubuf[SUBLANES - 1 + r0:SUBLANES - 1 + r0 + n, :]
                + cw[2:3, :] * u)

        def store(y):
            y_ref[rows, :] = y

        yield from _tail_steps(x, o_attn, b_gate * conv, p_ref[rows, :], refs, store)

    assert sum(SUB_TILES) == tm
    starts = [sum(SUB_TILES[:t]) for t in range(len(SUB_TILES))]
    tiles = [sub_tile_steps(r0, n) for r0, n in zip(starts, SUB_TILES)]
    begin = [sum(1 + n // BLOCK for n in SUB_TILES[:t]) for t in range(len(SUB_TILES))]
    live = list(range(len(tiles)))
    rnd = 0
    while live:
        for t in reversed([t for t in live if rnd >= begin[t]]):
            if next(tiles[t], True) is True:
                live.remove(t)
        rnd += 1

    n_last = last["n"]
    kout_ref[...] = last["k"][n_last - BLOCK:, :].T
    vout_ref[...] = last["v_t"][:, n_last - BLOCK:]
    convout_ref[...] = last["u"][n_last - (CONV_K - 1):, :]
    kbuf[0:BLOCK, :] = kbuf[tm:tm + BLOCK, :]
    vbuf[:, 0:BLOCK] = vbuf[:, tm:tm + BLOCK]
    ubuf[0:SUBLANES, :] = ubuf[tm:tm + SUBLANES, :]


def _sample_kernel(sinks_ref, x_ref, p_ref, cos_ref, sin_ref, st1_ref, st2_ref, ck_ref, cv_ref, *rest):
    ns, nm, nl = len(SMALL_WEIGHTS), len(MATMUL_WEIGHTS), len(LATE_WEIGHTS)
    refs = dict(zip(SMALL_WEIGHTS, rest[:ns]))
    w_f32 = dict(zip(MATMUL_WEIGHTS, rest[ns:ns + nm]))
    y_ref, kout_ref, vout_ref, uout_ref = rest[ns + nm:ns + nm + 4]
    w_out = dict(zip(MATMUL_WEIGHTS, rest[ns + nm + 4:ns + 2 * nm + 4]))
    scratch = rest[ns + 2 * nm + 4:]
    qbuf, kbuf, vbuf, knew, vnew, obuf, bgbuf = scratch[:7]
    w_vmem = dict(zip(MATMUL_WEIGHTS, scratch[7:7 + nm]))
    in_stage = scratch[7 + nm]
    late_stage = dict(zip(LATE_WEIGHTS, scratch[8 + nm:8 + nm + nl]))
    in_sem, late_sem, out_sem = scratch[8 + nm + nl:]
    refs.update(w_vmem)
    t_len = x_ref.shape[1]
    n_tok = x_ref.shape[0] * t_len
    step = pl.program_id(0)
    n_steps = pl.num_programs(0)
    step_batch = ck_ref.shape[0]
    pair_rows = 2 * t_len
    chunk_pairs = BLOCK // pair_rows

    n_slots = in_stage.shape[0]

    def in_copy(i):
        return pltpu.make_async_copy(w_f32["w_in"].at[pl.ds(i * W_IN_CHUNK, W_IN_CHUNK), :],
                                     in_stage.at[i % n_slots], in_sem.at[i % n_slots])

    def late_copy(n, name, s):
        rows = late_stage[name].shape[1]
        return pltpu.make_async_copy(w_f32[name].at[pl.ds(s * rows, rows), :], late_stage[name].at[s % 2],
                                     late_sem.at[n, s % 2])

    def out_copy(n, name):
        return pltpu.make_async_copy(w_vmem[name], w_out[name], out_sem.at[n])

    @pl.when(step + 1 < n_steps)
    def _():
        for n, name in enumerate(LATE_WEIGHTS):
            late_copy(n, name, step + 1).start()

    @pl.when(step == 0)
    def _():
        for n, name in enumerate(LATE_WEIGHTS):
            late_copy(n, name, step).start()
        n_chunks = D_MODEL // W_IN_CHUNK
        for i in range(n_slots):
            in_copy(i).start()
        for i in range(n_chunks):
            in_copy(i).wait()
            w_vmem["w_in"][i * W_IN_CHUNK:(i + 1) * W_IN_CHUNK, :] = in_stage[i % n_slots].astype(BF16)
            if i + n_slots < n_chunks:
                in_copy(i + n_slots).start()
        out_copy(0, "w_in").start()
        qcols, k, v, b_gate, u = _front(x_ref[...].reshape(n_tok, D_MODEL), cos_ref[...], sin_ref[...], refs)
        for c in range(N_QCOL):
            qbuf[2 * c], qbuf[2 * c + 1] = _split_heads(qcols[c], c)
        k_t, v_t = k.T, v.T
        kbuf[...] = k_t.astype(BF16)
        vbuf[...] = v_t.astype(BF16)
        knew[...] = k_t
        vnew[...] = v_t
        bgbuf[...] = b_gate
        uout_ref[...] = u

    n_rows = N_Q_HEADS * pair_rows
    row = lax.broadcasted_iota(jnp.int32, (n_rows, 3 * BLOCK), 0)
    col = lax.broadcasted_iota(jnp.int32, (n_rows, 3 * BLOCK), 1)
    row_b = (row % pair_rows) // t_len
    row_t = row % t_len
    cache_ok = (col < 2 * BLOCK) & (col // BLOCK == row_b) & ((col % BLOCK) > row_t)
    new_col = col - 2 * BLOCK
    sink = _sink_column(sinks_ref, pair_rows)

    def pair_stages(i):
        pair = step * (step_batch // 2) + i
        r0 = pl.multiple_of(pair * pair_rows, pair_rows)
        c0 = pl.multiple_of((pair // chunk_pairs) * BLOCK, BLOCK)
        q = qbuf[:, pl.ds(r0, pair_rows), :].reshape(n_rows, LANES).astype(BF16)
        ck = [ck_ref[2 * i + bb] for bb in range(2)]
        cv = [cv_ref[2 * i + bb] for bb in range(2)]
        keys = jnp.concatenate([c.astype(BF16) for c in ck] + [kbuf[:, pl.ds(c0, BLOCK)]], axis=1)
        vals = jnp.concatenate([c.astype(BF16) for c in cv] + [vbuf[:, pl.ds(c0, BLOCK)]], axis=1)
        s = _dot(q, keys)
        yield
        new_ok = ((col >= 2 * BLOCK) & (new_col // t_len == (pair % chunk_pairs) * 2 + row_b)
                  & (new_col % t_len <= row_t))
        ok = cache_ok | new_ok
        s = jnp.where(ok, s, NEG)
        mx = jnp.maximum(jnp.max(s, axis=-1, keepdims=True), sink)
        e = jnp.where(ok, jnp.exp2(s - mx), 0.0)
        denom = jnp.sum(e, axis=-1, keepdims=True) + jnp.exp2(sink - mx)
        yield
        pv = _dot_t(e.astype(BF16), vals) * (1.0 / denom)
        for c in range(N_QCOL):
            obuf[pl.ds(r0, pair_rows), c * LANES:(c + 1) * LANES] = _merge_heads(pv, pair_rows, c)
        yield
        k_chunk = knew[:, pl.ds(c0, BLOCK)]
        v_chunk = vnew[:, pl.ds(c0, BLOCK)]
        keep = lax.broadcasted_iota(jnp.int32, (KV_WIDTH, BLOCK), 1) < BLOCK - t_len
        for bb in range(2):
            to_tail = (BLOCK - t_len) - t_len * ((pair % chunk_pairs) * 2 + bb)
            kout_ref[2 * i + bb] = jnp.where(keep, pltpu.roll(ck[bb], BLOCK - t_len, axis=1),
                                             pltpu.roll(k_chunk, to_tail, axis=1))
            vout_ref[2 * i + bb] = jnp.where(keep, pltpu.roll(cv[bb], BLOCK - t_len, axis=1),
                                             pltpu.roll(v_chunk, to_tail, axis=1))

    pairs = [pair_stages(i) for i in range(step_batch // 2)]
    for _ in range(4):
        for stages in pairs:
            next(stages, None)

    for n, name in enumerate(LATE_WEIGHTS):
        rows = late_stage[name].shape[1]
        late_copy(n, name, step).wait()
        r0 = pl.multiple_of(step * rows, rows)
        w_vmem[name][pl.ds(r0, rows), :] = late_stage[name][step % 2].astype(BF16)

    @pl.when(step == n_steps - 1)
    def _():
        for n, name in enumerate(LATE_WEIGHTS):
            out_copy(n + 1, name).start()
        u = uout_ref[...]
        tok = lax.broadcasted_iota(jnp.int32, u.shape, 0) % t_len
        um1 = jnp.where(tok >= 1, pltpu.roll(u, 1, axis=0), st1_ref[...])
        um2 = jnp.where(tok >= 2, pltpu.roll(u, 2, axis=0), st2_ref[...])
        cw = refs["conv_w"]
        conv = cw[0:1, :] * um2 + cw[1:2, :] * um1 + cw[2:3, :] * u

        def store(y):
            y_ref[...] = y.reshape(y_ref.shape)

        _run(_tail_steps(x_ref[...].reshape(n_tok, D_MODEL), obuf[...], bgbuf[...] * conv,
                         p_ref[...].reshape(n_tok, PLE_DIM), refs, store))
        for n, name in enumerate(MATMUL_WEIGHTS):
            out_copy(n, name).wait()


def _rope_tables(pos):
    inv_freq = ROPE_THETA ** (-jnp.arange(0, HEAD_DIM, 2, dtype=F32) / HEAD_DIM)
    ang = pos.astype(F32)[:, None] * inv_freq[None, :]
    cos, sin = jnp.cos(ang), jnp.sin(ang)
    cos_h = jnp.concatenate([cos, cos], axis=1)
    sin_h = jnp.concatenate([-sin, sin], axis=1)
    return jnp.tile(cos_h, (1, LANES // HEAD_DIM)), jnp.tile(sin_h, (1, LANES // HEAD_DIM))


def _cache_to_kernel(c):
    batch, keys = c.shape[0], c.shape[1]
    return jnp.transpose(c, (0, 2, 3, 1)).reshape(batch, KV_WIDTH, keys)


def _cache_from_kernel(c):
    batch, _, keys = c.shape
    return jnp.transpose(c.reshape(batch, N_KV_HEADS, HEAD_DIM, keys), (0, 3, 1, 2))


def _block_diag_ones(width):
    idx = np.arange(width) // HEAD_DIM
    return jnp.asarray(idx[:, None] == idx[None, :], dtype=BF16)


def _prepare_weights(g_mix_norm, w_in, g_q, g_k, sinks, conv_w, g_attn_out, g_conv_out, w_o,
                     g_mlp_norm, w_up, w_down, g_ple_norm, w_ple_gate, w_ple):
    row = lambda g: g.reshape(1, -1).astype(F32)
    tile_head = lambda g: jnp.tile(g.reshape(1, HEAD_DIM).astype(F32), (1, LANES // HEAD_DIM))
    weights = dict(
        g_mix=row(g_mix_norm), w_in=w_in,
        bd_q=_block_diag_ones(ATTN_WIDTH), bd_k=_block_diag_ones(KV_WIDTH),
        g_q=tile_head(g_q) * (HEAD_DIM ** -0.5 * LOG2_E), g_k=tile_head(g_k), conv_w=conv_w.astype(F32),
        g_attn=row(g_attn_out), g_conv=row(g_conv_out), w_o=w_o,
        g_mlp=row(g_mlp_norm), w_up=w_up, w_down=w_down,
        g_ple=row(g_ple_norm), w_gate=w_ple_gate, w_ple=w_ple)
    return sinks.astype(F32), weights


def _resident(a, n_grid):
    zeros = (0,) * a.ndim
    index_map = (lambda b, j: zeros) if n_grid == 2 else (lambda i: zeros)
    return pl.BlockSpec(a.shape, index_map, pipeline_mode=pl.Buffered(1))


def _prompt_layer(x, p, sinks_p, weights):
    batch, seq, _ = x.shape
    tm = SEQ_TILE
    cos, sin = _rope_tables(jnp.arange(seq, dtype=jnp.int32))
    tile = lambda w: pl.BlockSpec((None, tm, w), lambda b, j: (b, j, 0))
    per_batch = lambda r, w: pl.BlockSpec((None, r, w), lambda b, j: (b, 0, 0))
    table = pl.BlockSpec((tm, LANES), lambda b, j: (j, 0))
    in_specs = ([pl.BlockSpec(memory_space=pltpu.SMEM), tile(D_MODEL), tile(PLE_DIM), table, table]
                + [_resident(w, 2) for w in weights])
    out_shape = (jax.ShapeDtypeStruct((batch, seq, D_MODEL), F32),
                 jax.ShapeDtypeStruct((batch, BLOCK, KV_WIDTH), F32),
                 jax.ShapeDtypeStruct((batch, BLOCK, KV_WIDTH), F32),
                 jax.ShapeDtypeStruct((batch, CONV_K - 1, CONV_WIDTH), F32))
    out_specs = (tile(D_MODEL), per_batch(BLOCK, KV_WIDTH), per_batch(BLOCK, KV_WIDTH),
                 per_batch(CONV_K - 1, CONV_WIDTH))
    scratch = [pltpu.VMEM((BLOCK + tm, KV_WIDTH), BF16), pltpu.VMEM((KV_WIDTH, BLOCK + tm), BF16),
               pltpu.VMEM((SUBLANES + tm, CONV_WIDTH), F32)]
    return pl.pallas_call(
        _prompt_kernel, grid=(batch, seq // tm), in_specs=in_specs, out_specs=out_specs,
        out_shape=out_shape, scratch_shapes=scratch, name="prompt_layer",
        compiler_params=pltpu.CompilerParams(dimension_semantics=("arbitrary", "arbitrary"),
                                             vmem_limit_bytes=VMEM_LIMIT),
    )(sinks_p, x, p, cos, sin, *weights)


def _sample_layer(x, p, cache_k, cache_v, state_conv, sinks_p, weights):
    batch, t_len, _ = x.shape
    n_tok = batch * t_len
    cos, sin = _rope_tables(PAST_LEN + jnp.arange(t_len, dtype=jnp.int32))
    cos, sin = jnp.tile(cos, (batch, 1)), jnp.tile(sin, (batch, 1))
    st1 = jnp.pad(state_conv[:, 1:2], ((0, 0), (0, t_len - 1), (0, 0))).reshape(n_tok, CONV_WIDTH)
    st2 = jnp.pad(state_conv, ((0, 0), (0, t_len - (CONV_K - 1)), (0, 0))).reshape(n_tok, CONV_WIDTH)
    ck = _cache_to_kernel(cache_k)
    cv = _cache_to_kernel(cache_v)
    flat = [x, p, cos, sin, st1, st2]
    n_steps = batch // SAMPLE_STEP_BATCH
    small = [weights[n] for n in SMALL_WEIGHTS]
    big = [weights[n] for n in MATMUL_WEIGHTS]
    hbm = pl.BlockSpec(memory_space=pl.ANY)
    cache_block = pl.BlockSpec((SAMPLE_STEP_BATCH, WINDOW, KV_WIDTH), lambda i: (i, 0, 0))
    in_specs = ([pl.BlockSpec(memory_space=pltpu.SMEM)] + [_resident(a, 1) for a in flat]
                + [cache_block, cache_block] + [_resident(w, 1) for w in small] + [hbm] * len(big))
    out_shape = (jax.ShapeDtypeStruct((batch, t_len, D_MODEL), F32),
                 jax.ShapeDtypeStruct((batch, WINDOW, KV_WIDTH), F32),
                 jax.ShapeDtypeStruct((batch, WINDOW, KV_WIDTH), F32),
                 jax.ShapeDtypeStruct((n_tok, CONV_WIDTH), F32)
                 ) + tuple(jax.ShapeDtypeStruct(w.shape, BF16) for w in big)
    whole = lambda s: pl.BlockSpec(s.shape, lambda i: (0,) * len(s.shape), pipeline_mode=pl.Buffered(1))
    out_specs = (whole(out_shape[0]), cache_block, cache_block, whole(out_shape[3])) + (hbm,) * len(big)
    late = [weights[n] for n in LATE_WEIGHTS]
    scratch = ([pltpu.VMEM((N_Q_HEADS, n_tok, LANES), F32),
                pltpu.VMEM((KV_WIDTH, n_tok), BF16), pltpu.VMEM((KV_WIDTH, n_tok), BF16),
                pltpu.VMEM((KV_WIDTH, n_tok), F32), pltpu.VMEM((KV_WIDTH, n_tok), F32),
                pltpu.VMEM((n_tok, ATTN_WIDTH), F32), pltpu.VMEM((n_tok, CONV_WIDTH), F32)]
               + [pltpu.VMEM(w.shape, BF16) for w in big]
               + [pltpu.VMEM((W_IN_SLOTS, W_IN_CHUNK, IN_WIDTH), F32)]
               + [pltpu.VMEM((2, w.shape[0] // n_steps, w.shape[1]), F32) for w in late]
               + [pltpu.SemaphoreType.DMA((W_IN_SLOTS,)), pltpu.SemaphoreType.DMA((len(late), 2)),
                  pltpu.SemaphoreType.DMA((len(big),))])
    y, k_new, v_new, u, *w_bf16 = pl.pallas_call(
        _sample_kernel, grid=(n_steps,), in_specs=in_specs, out_specs=out_specs,
        out_shape=out_shape, scratch_shapes=scratch, name="sample_layer",
        compiler_params=pltpu.CompilerParams(dimension_semantics=("arbitrary",),
                                             vmem_limit_bytes=SAMPLE_VMEM_LIMIT),
    )(sinks_p, *flat, ck, cv, *small, *big)
    conv_new = u.reshape(batch, t_len, CONV_WIDTH)[:, t_len - (CONV_K - 1):]
    return y, k_new, v_new, conv_new, dict(zip(MATMUL_WEIGHTS, w_bf16))


def kernel(x_prompt, x_sample, p_prompt, p_sample, cache_k, cache_v, state_conv, g_mix_norm, w_in, g_q, g_k,
           sinks, conv_w, g_attn_out, g_conv_out, w_o, g_mlp_norm, w_up, w_down, g_ple_norm, w_ple_gate, w_ple):
    depth = w_in.shape[0]
    yp, ys = x_prompt, x_sample
    outs = [[] for _ in range(6)]
    for i in range(depth):
        sinks_p, weights = _prepare_weights(
            g_mix_norm[i], w_in[i], g_q[i], g_k[i], sinks[i], conv_w[i], g_attn_out[i], g_conv_out[i],
            w_o[i], g_mlp_norm[i], w_up[i], w_down[i], g_ple_norm[i], w_ple_gate[i], w_ple[i])
        ys, ksn, vsn, csn, w_bf16 = _sample_layer(ys, p_sample[i], cache_k[i], cache_v[i], state_conv[i],
                                                  sinks_p, weights)
        yp, kp, vp, cp = _prompt_layer(yp, p_prompt[i], sinks_p,
                                       [w_bf16.get(n, weights[n]) for n in WEIGHT_NAMES])
        for lst, val in zip(outs, (_cache_from_kernel(kp), _cache_from_kernel(vp), cp,
                                   _cache_from_kernel(ksn), _cache_from_kernel(vsn), csn)):
            lst.append(val)
    return (yp, ys) + tuple(jnp.stack(o) for o in outs)
```

```python
import jax
import jax.numpy as jnp
import numpy as np
from jax import lax
from jax.experimental import pallas as pl
from jax.experimental.pallas import tpu as pltpu

D_MODEL = 1024
HEAD_DIM = 64
N_Q_HEADS = 8
N_KV_HEADS = 2
GQA_GROUP = N_Q_HEADS // N_KV_HEADS
ATTN_WIDTH = N_Q_HEADS * HEAD_DIM
KV_WIDTH = N_KV_HEADS * HEAD_DIM
CONV_WIDTH = D_MODEL - ATTN_WIDTH
CONV_K = 3
WINDOW = 128
BLOCK = 128
ROPE_THETA = 10000.0
D_FF = 4 * D_MODEL
PLE_DIM = 256
EPS = 1e-6
NEG = -1e30
LOG2_E = 1.4426950408889634
PAST_LEN = 16384
IN_WIDTH = ATTN_WIDTH + 2 * KV_WIDTH + 3 * CONV_WIDTH

LANES = 128
SUBLANES = 8
N_QCOL = ATTN_WIDTH // LANES
SEQ_TILE = 512
SUB_TILES = (256, 256)
FF_CHUNK = 1024
SAMPLE_STEP_BATCH = 8
VMEM_LIMIT = 56 * 1024 * 1024
SAMPLE_VMEM_LIMIT = 60 * 1024 * 1024

O_K = ATTN_WIDTH
O_V = O_K + KV_WIDTH
O_B = O_V + KV_WIDTH
O_C = O_B + CONV_WIDTH
O_H = O_C + CONV_WIDTH

BF16 = jnp.bfloat16
F32 = jnp.float32


def _dot(a, b):
    return jnp.dot(a, b, preferred_element_type=F32)


def _dot_t(a, b):
    return lax.dot_general(a, b, (((1,), (1,)), ((), ())), preferred_element_type=F32)


def _rms(x, g):
    return x * lax.rsqrt(jnp.mean(x * x, axis=-1, keepdims=True) + EPS) * g


def _head_norm_rope(t, ones_bd, g, cos, sin):
    ssq = _dot((t * t).astype(BF16), ones_bd)
    t = t * lax.rsqrt(ssq * (1.0 / HEAD_DIM) + EPS)
    lane = lax.broadcasted_iota(jnp.int32, (t.shape[0], LANES), 1)
    first_half = (lane & (HEAD_DIM - 1)) < HEAD_DIM // 2
    cols = []
    for m in range(t.shape[1] // LANES):
        c = t[:, m * LANES:(m + 1) * LANES] * g
        up = pltpu.roll(c, LANES - HEAD_DIM // 2, axis=1)
        dn = pltpu.roll(c, HEAD_DIM // 2, axis=1)
        cols.append(c * cos + jnp.where(first_half, up, dn) * sin)
    return cols


def _front(x, cos, sin, refs):
    h = _rms(x, refs["g_mix"][...]).astype(BF16)
    z = _dot(h, refs["w_in"][...])
    qcols = _head_norm_rope(z[:, :O_K], refs["bd_q"][...], refs["g_q"][...], cos, sin)
    (k,) = _head_norm_rope(z[:, O_K:O_V], refs["bd_k"][...], refs["g_k"][...], cos, sin)
    v = z[:, O_V:O_B]
    b_gate = z[:, O_B:O_C]
    u = z[:, O_C:O_H] * z[:, O_H:]
    return qcols, k, v, b_gate, u


def _tail_steps(x, o_attn, o_conv, p, refs, store):
    mixed = jnp.concatenate([_rms(o_attn, refs["g_attn"][...]), _rms(o_conv, refs["g_conv"][...])], axis=1)
    x = x + _dot(mixed.astype(BF16), refs["w_o"][...])
    hm = _rms(x, refs["g_mlp"][...]).astype(BF16)
    yield
    for c in range(D_FF // FF_CHUNK):
        up = _dot(hm, refs["w_up"][:, c * FF_CHUNK:(c + 1) * FF_CHUNK])
        act = jnp.square(jnp.maximum(up.astype(BF16), 0.0))
        x = x + _dot(act, refs["w_down"][c * FF_CHUNK:(c + 1) * FF_CHUNK, :])
        yield
    gate = jax.nn.sigmoid(_dot(_rms(x, refs["g_ple"][...]).astype(BF16), refs["w_gate"][...]))
    store(x + gate * _dot(p.astype(BF16), refs["w_ple"][...]))


def _run(steps):
    for _ in steps:
        pass


def _split_heads(col, c):
    group = (2 * c) // GQA_GROUP
    lo = lax.broadcasted_iota(jnp.int32, col.shape, 1) < HEAD_DIM
    swapped = pltpu.roll(col, HEAD_DIM, axis=1)
    if group == 0:
        return jnp.where(lo, col, 0.0), jnp.where(lo, swapped, 0.0)
    return jnp.where(lo, 0.0, swapped), jnp.where(lo, 0.0, col)


def _sink_column(sinks_ref, rows_per_head):
    return jnp.concatenate(
        [jnp.full((rows_per_head, 1), sinks_ref[i] * LOG2_E, F32) for i in range(N_Q_HEADS)], axis=0)


def _merge_heads(pv, rows_per_head, c):
    group = (2 * c) // GQA_GROUP
    lo = lax.broadcasted_iota(jnp.int32, (rows_per_head, LANES), 1) < HEAD_DIM
    a = pv[(2 * c) * rows_per_head:(2 * c + 1) * rows_per_head]
    b = pv[(2 * c + 1) * rows_per_head:(2 * c + 2) * rows_per_head]
    if group == 0:
        return jnp.where(lo, a, pltpu.roll(b, HEAD_DIM, axis=1))
    return jnp.where(lo, pltpu.roll(a, HEAD_DIM, axis=1), b)


WEIGHT_NAMES = ("g_mix", "w_in", "bd_q", "bd_k", "g_q", "g_k", "conv_w", "g_attn", "g_conv", "w_o",
                "g_mlp", "w_up", "w_down", "g_ple", "w_gate", "w_ple")
MATMUL_WEIGHTS = ("w_in", "w_o", "w_up", "w_down", "w_gate", "w_ple")
LATE_WEIGHTS = MATMUL_WEIGHTS[1:]
SMALL_WEIGHTS = tuple(n for n in WEIGHT_NAMES if n not in MATMUL_WEIGHTS)
W_IN_CHUNK = 64
W_IN_SLOTS = 4


def _prompt_kernel(sinks_ref, x_ref, p_ref, cos_ref, sin_ref, *rest):
    nw = len(WEIGHT_NAMES)
    refs = dict(zip(WEIGHT_NAMES, rest[:nw]))
    y_ref, kout_ref, vout_ref, convout_ref, kbuf, vbuf, ubuf = rest[nw:]
    tm = x_ref.shape[0]
    j = pl.program_id(1)

    @pl.when(j == 0)
    def _():
        kbuf[0:BLOCK, :] = jnp.zeros((BLOCK, KV_WIDTH), BF16)
        vbuf[:, 0:BLOCK] = jnp.zeros((KV_WIDTH, BLOCK), BF16)
        ubuf[0:SUBLANES, :] = jnp.zeros((SUBLANES, CONV_WIDTH), F32)

    kj = lax.broadcasted_iota(jnp.int32, (BLOCK, N_Q_HEADS * BLOCK), 0)
    qi = lax.broadcasted_iota(jnp.int32, (BLOCK, N_Q_HEADS * BLOCK), 1) & (BLOCK - 1)
    own = kj <= qi
    sink = jnp.concatenate([jnp.full((1, BLOCK), sinks_ref[i] * LOG2_E, F32) for i in range(N_Q_HEADS)],
                           axis=1)
    cw = refs["conv_w"]

    last = {}

    def sub_tile_steps(r0, n):
        rows = slice(r0, r0 + n)
        x = x_ref[rows, :]
        qcols, k, v, b_gate, u = _front(x, cos_ref[rows, :], sin_ref[rows, :], refs)
        kbuf[BLOCK + r0:BLOCK + r0 + n, :] = k.astype(BF16)
        v_t = v.T
        vbuf[:, BLOCK + r0:BLOCK + r0 + n] = v_t.astype(BF16)
        ubuf[SUBLANES + r0:SUBLANES + r0 + n, :] = u
        last.update(k=k, v_t=v_t, u=u, n=n)
        yield

        o_blocks = []
        for i in range(n // BLOCK):
            g = r0 // BLOCK + i
            qs = []
            for c in range(N_QCOL):
                qs.extend(_split_heads(qcols[c][i * BLOCK:(i + 1) * BLOCK], c))
            qstack = jnp.concatenate(qs, axis=0).astype(BF16)
            s2 = _dot_t(kbuf[g * BLOCK:(g + 2) * BLOCK, :], qstack)
            s_prev = s2[:BLOCK]
            if g == 0:
                s_prev = jnp.where(j == 0, NEG, s_prev)
            s = jnp.where(own, s2[BLOCK:], s_prev)
            mx = jnp.maximum(jnp.max(s, axis=0, keepdims=True), sink)
            e = jnp.exp2(s - mx)
            denom = jnp.sum(e, axis=0, keepdims=True) + jnp.exp2(sink - mx)
            e2 = jnp.concatenate([jnp.where(own, 0.0, e), jnp.where(own, e, 0.0)], axis=0).astype(BF16)
            pv = _dot(vbuf[:, g * BLOCK:(g + 2) * BLOCK], e2) * (1.0 / denom)
            cols = []
            for c in range(N_QCOL):
                d0 = (2 * c) // GQA_GROUP * HEAD_DIM
                col_t = jnp.concatenate([pv[d0:d0 + HEAD_DIM, (2 * c) * BLOCK:(2 * c + 1) * BLOCK],
                                         pv[d0:d0 + HEAD_DIM, (2 * c + 1) * BLOCK:(2 * c + 2) * BLOCK]], axis=0)
                cols.append(col_t.T)
            o_blocks.append(jnp.concatenate(cols, axis=1))
            yield
        o_attn = jnp.concatenate(o_blocks, axis=0)

        conv = (cw[0] * ubuf[SUBLANES - 2 + r0:SUBLANES - 2 + r0 + n, :]
                + cw[1] * ubuf[SUBLANES - 1 + r0:SUBLANES - 1 + r0 + n, :]
                + cw[2] * u)

        def store(y):
            y_ref[rows, :] = y

        yield from _tail_steps(x, o_attn, b_gate * conv, p_ref[rows, :], refs, store)

    assert sum(SUB_TILES) == tm
    starts = [sum(SUB_TILES[:t]) for t in range(len(SUB_TILES))]
    tiles = [sub_tile_steps(r0, n) for r0, n in zip(starts, SUB_TILES)]
    begin = [sum(1 + n // BLOCK for n in SUB_TILES[:t]) for t in range(len(SUB_TILES))]
    live = list(range(len(tiles)))
    rnd = 0
    while live:
        for t in reversed([t for t in live if rnd >= begin[t]]):
            if next(tiles[t], True) is True:
                live.remove(t)
        rnd += 1

    n_last = last["n"]
    kout_ref[...] = last["k"][n_last - BLOCK:, :].T
    vout_ref[...] = last["v_t"][:, n_last - BLOCK:]
    convout_ref[...] = last["u"][n_last - (CONV_K - 1):, :]
    kbuf[0:BLOCK, :] = kbuf[tm:tm + BLOCK, :]
    vbuf[:, 0:BLOCK] = vbuf[:, tm:tm + BLOCK]
    ubuf[0:SUBLANES, :] = ubuf[tm:tm + SUBLANES, :]


def _sample_kernel(sinks_ref, x_ref, p_ref, cos_ref, sin_ref, state_ref, ck_ref, cv_ref, *rest):
    ns, nm, nl = len(SMALL_WEIGHTS), len(MATMUL_WEIGHTS), len(LATE_WEIGHTS)
    refs = dict(zip(SMALL_WEIGHTS, rest[:ns]))
    w_f32 = dict(zip(MATMUL_WEIGHTS, rest[ns:ns + nm]))
    y_ref, kout_ref, vout_ref, convout_ref = rest[ns + nm:ns + nm + 4]
    w_out = dict(zip(MATMUL_WEIGHTS, rest[ns + nm + 4:ns + 2 * nm + 4]))
    scratch = rest[ns + 2 * nm + 4:]
    qbuf, kbuf, vbuf, knew, vnew, obuf, bgbuf, ubuf = scratch[:8]
    w_vmem = dict(zip(MATMUL_WEIGHTS, scratch[8:8 + nm]))
    in_stage = scratch[8 + nm]
    late_stage = dict(zip(LATE_WEIGHTS, scratch[9 + nm:9 + nm + nl]))
    in_sem, late_sem, out_sem = scratch[9 + nm + nl:]
    refs.update(w_vmem)
    t_len = x_ref.shape[1]
    n_tok = x_ref.shape[0] * t_len
    step = pl.program_id(0)
    n_steps = pl.num_programs(0)
    step_batch = ck_ref.shape[0]
    pair_rows = 2 * t_len
    chunk_pairs = BLOCK // pair_rows

    n_slots = in_stage.shape[0]

    def in_copy(i):
        return pltpu.make_async_copy(w_f32["w_in"].at[pl.ds(i * W_IN_CHUNK, W_IN_CHUNK), :],
                                     in_stage.at[i % n_slots], in_sem.at[i % n_slots])

    def late_copy(n, name, s):
        rows = late_stage[name].shape[1]
        return pltpu.make_async_copy(w_f32[name].at[pl.ds(s * rows, rows), :], late_stage[name].at[s % 2],
                                     late_sem.at[n, s % 2])

    def out_copy(n, name):
        return pltpu.make_async_copy(w_vmem[name], w_out[name], out_sem.at[n])

    @pl.when(step + 1 < n_steps)
    def _():
        for n, name in enumerate(LATE_WEIGHTS):
            late_copy(n, name, step + 1).start()

    @pl.when(step == 0)
    def _():
        for n, name in enumerate(LATE_WEIGHTS):
            late_copy(n, name, step).start()
        n_chunks = D_MODEL // W_IN_CHUNK
        for i in range(n_slots):
            in_copy(i).start()
        for i in range(n_chunks):
            in_copy(i).wait()
            w_vmem["w_in"][i * W_IN_CHUNK:(i + 1) * W_IN_CHUNK, :] = in_stage[i % n_slots].astype(BF16)
            if i + n_slots < n_chunks:
                in_copy(i + n_slots).start()
        out_copy(0, "w_in").start()
        qcols, k, v, b_gate, u = _front(x_ref[...].reshape(n_tok, D_MODEL), cos_ref[...], sin_ref[...], refs)
        for c in range(N_QCOL):
            qbuf[2 * c], qbuf[2 * c + 1] = _split_heads(qcols[c], c)
        k_t, v_t = k.T, v.T
        kbuf[...] = k_t.astype(BF16)
        vbuf[...] = v_t.astype(BF16)
        knew[...] = k_t
        vnew[...] = v_t
        bgbuf[...] = b_gate
        ubuf[...] = u
        convout_ref[...] = u.reshape(convout_ref.shape[0], t_len, CONV_WIDTH)[:, t_len - (CONV_K - 1):, :]

    n_rows = N_Q_HEADS * pair_rows
    row = lax.broadcasted_iota(jnp.int32, (n_rows, 3 * BLOCK), 0)
    col = lax.broadcasted_iota(jnp.int32, (n_rows, 3 * BLOCK), 1)
    row_b = (row % pair_rows) // t_len
    row_t = row % t_len
    cache_ok = (col < 2 * BLOCK) & (col // BLOCK == row_b) & ((col % BLOCK) > row_t)
    new_col = col - 2 * BLOCK
    sink = _sink_column(sinks_ref, pair_rows)

    def pair_stages(i):
        pair = step * (step_batch // 2) + i
        r0 = pl.multiple_of(pair * pair_rows, pair_rows)
        c0 = pl.multiple_of((pair // chunk_pairs) * BLOCK, BLOCK)
        q = qbuf[:, pl.ds(r0, pair_rows), :].reshape(n_rows, LANES).astype(BF16)
        ck = [ck_ref[2 * i + bb] for bb in range(2)]
        cv = [cv_ref[2 * i + bb] for bb in range(2)]
        keys = jnp.concatenate([c.astype(BF16) for c in ck] + [kbuf[:, pl.ds(c0, BLOCK)]], axis=1)
        vals = jnp.concatenate([c.astype(BF16) for c in cv] + [vbuf[:, pl.ds(c0, BLOCK)]], axis=1)
        s = _dot(q, keys)
        yield
        new_ok = ((col >= 2 * BLOCK) & (new_col // t_len == (pair % chunk_pairs) * 2 + row_b)
                  & (new_col % t_len <= row_t))
        ok = cache_ok | new_ok
        s = jnp.where(ok, s, NEG)
        mx = jnp.maximum(jnp.max(s, axis=-1, keepdims=True), sink)
        e = jnp.where(ok, jnp.exp2(s - mx), 0.0)
        denom = jnp.sum(e, axis=-1, keepdims=True) + jnp.exp2(sink - mx)
        yield
        pv = _dot_t(e.astype(BF16), vals) * (1.0 / denom)
        for c in range(N_QCOL):
            obuf[pl.ds(r0, pair_rows), c * LANES:(c + 1) * LANES] = _merge_heads(pv, pair_rows, c)
        yield
        k_chunk = knew[:, pl.ds(c0, BLOCK)]
        v_chunk = vnew[:, pl.ds(c0, BLOCK)]
        keep = lax.broadcasted_iota(jnp.int32, (KV_WIDTH, BLOCK), 1) < BLOCK - t_len
        for bb in range(2):
            to_tail = (BLOCK - t_len) - t_len * ((pair % chunk_pairs) * 2 + bb)
            kout_ref[2 * i + bb] = jnp.where(keep, pltpu.roll(ck[bb], BLOCK - t_len, axis=1),
                                             pltpu.roll(k_chunk, to_tail, axis=1))
            vout_ref[2 * i + bb] = jnp.where(keep, pltpu.roll(cv[bb], BLOCK - t_len, axis=1),
                                             pltpu.roll(v_chunk, to_tail, axis=1))

    pairs = [pair_stages(i) for i in range(step_batch // 2)]
    for _ in range(4):
        for stages in pairs:
            next(stages, None)

    for n, name in enumerate(LATE_WEIGHTS):
        rows = late_stage[name].shape[1]
        late_copy(n, name, step).wait()
        r0 = pl.multiple_of(step * rows, rows)
        w_vmem[name][pl.ds(r0, rows), :] = late_stage[name][step % 2].astype(BF16)

    @pl.when(step == n_steps - 1)
    def _():
        for n, name in enumerate(LATE_WEIGHTS):
            out_copy(n + 1, name).start()
        u = ubuf[...]
        tok = lax.broadcasted_iota(jnp.int32, u.shape, 0) % t_len
        per_token = lambda r: jnp.broadcast_to(state_ref[:, r:r + 1, :],
                                               (state_ref.shape[0], t_len, CONV_WIDTH)).reshape(u.shape)
        s0, s1 = per_token(0), per_token(1)
        um1 = jnp.where(tok >= 1, pltpu.roll(u, 1, axis=0), s1)
        um2 = jnp.where(tok >= 2, pltpu.roll(u, 2, axis=0), jnp.where(tok == 0, s0, s1))
        cw = refs["conv_w"]
        conv = cw[0] * um2 + cw[1] * um1 + cw[2] * u

        def store(y):
            y_ref[...] = y.reshape(y_ref.shape)

        _run(_tail_steps(x_ref[...].reshape(n_tok, D_MODEL), obuf[...], bgbuf[...] * conv,
                         p_ref[...].reshape(n_tok, PLE_DIM), refs, store))
        for n, name in enumerate(MATMUL_WEIGHTS):
            out_copy(n, name).wait()


def _rope_tables(pos):
    lane = jnp.arange(LANES, dtype=jnp.int32)
    pair = (lane % (HEAD_DIM // 2)).astype(F32)
    inv_freq = ROPE_THETA ** (-(2.0 * pair) / HEAD_DIM)
    sign = jnp.where(lane % HEAD_DIM < HEAD_DIM // 2, -1.0, 1.0).astype(F32)
    ang = pos.astype(F32)[:, None] * inv_freq[None, :]
    return jnp.cos(ang), jnp.sin(ang) * sign[None, :]


def _cache_to_kernel(c):
    batch, keys = c.shape[0], c.shape[1]
    return jnp.transpose(c, (0, 2, 3, 1)).reshape(batch, KV_WIDTH, keys)


def _cache_from_kernel(c):
    batch, _, keys = c.shape
    return jnp.transpose(c.reshape(batch, N_KV_HEADS, HEAD_DIM, keys), (0, 3, 1, 2))


def _block_diag_ones(width):
    idx = np.arange(width) // HEAD_DIM
    return jnp.asarray(idx[:, None] == idx[None, :], dtype=BF16)


def _prepare_weights(g_mix_norm, w_in, g_q, g_k, sinks, conv_w, g_attn_out, g_conv_out, w_o,
                     g_mlp_norm, w_up, w_down, g_ple_norm, w_ple_gate, w_ple):
    row = lambda g: g.reshape(1, -1).astype(F32)
    tile_head = lambda g: jnp.tile(g.reshape(1, HEAD_DIM).astype(F32), (1, LANES // HEAD_DIM))
    weights = dict(
        g_mix=row(g_mix_norm), w_in=w_in,
        bd_q=_block_diag_ones(ATTN_WIDTH), bd_k=_block_diag_ones(KV_WIDTH),
        g_q=tile_head(g_q) * (HEAD_DIM ** -0.5 * LOG2_E), g_k=tile_head(g_k),
        conv_w=conv_w.astype(F32).reshape(CONV_K, 1, CONV_WIDTH),
        g_attn=row(g_attn_out), g_conv=row(g_conv_out), w_o=w_o,
        g_mlp=row(g_mlp_norm), w_up=w_up, w_down=w_down,
        g_ple=row(g_ple_norm), w_gate=w_ple_gate, w_ple=w_ple)
    return sinks.astype(F32), weights


def _resident(a, n_grid):
    zeros = (0,) * a.ndim
    index_map = (lambda b, j: zeros) if n_grid == 2 else (lambda i: zeros)
    return pl.BlockSpec(a.shape, index_map, pipeline_mode=pl.Buffered(1))


def _prompt_layer(x, p, sinks_p, weights):
    batch, seq, _ = x.shape
    tm = SEQ_TILE
    cos, sin = _rope_tables(jnp.arange(seq, dtype=jnp.int32))
    tile = lambda w: pl.BlockSpec((None, tm, w), lambda b, j: (b, j, 0))
    per_batch = lambda r, w: pl.BlockSpec((None, r, w), lambda b, j: (b, 0, 0))
    table = pl.BlockSpec((tm, LANES), lambda b, j: (j, 0))
    in_specs = ([pl.BlockSpec(memory_space=pltpu.SMEM), tile(D_MODEL), tile(PLE_DIM), table, table]
                + [_resident(w, 2) for w in weights])
    out_shape = (jax.ShapeDtypeStruct((batch, seq, D_MODEL), F32),
                 jax.ShapeDtypeStruct((batch, BLOCK, KV_WIDTH), F32),
                 jax.ShapeDtypeStruct((batch, BLOCK, KV_WIDTH), F32),
                 jax.ShapeDtypeStruct((batch, CONV_K - 1, CONV_WIDTH), F32))
    out_specs = (tile(D_MODEL), per_batch(BLOCK, KV_WIDTH), per_batch(BLOCK, KV_WIDTH),
                 per_batch(CONV_K - 1, CONV_WIDTH))
    scratch = [pltpu.VMEM((BLOCK + tm, KV_WIDTH), BF16), pltpu.VMEM((KV_WIDTH, BLOCK + tm), BF16),
               pltpu.VMEM((SUBLANES + tm, CONV_WIDTH), F32)]
    return pl.pallas_call(
        _prompt_kernel, grid=(batch, seq // tm), in_specs=in_specs, out_specs=out_specs,
        out_shape=out_shape, scratch_shapes=scratch, name="prompt_layer",
        compiler_params=pltpu.CompilerParams(dimension_semantics=("arbitrary", "arbitrary"),
                                             vmem_limit_bytes=VMEM_LIMIT),
    )(sinks_p, x, p, cos, sin, *weights)


def _sample_layer(x, p, cache_k, cache_v, state_conv, sinks_p, weights):
    batch, t_len, _ = x.shape
    n_tok = batch * t_len
    cos, sin = _rope_tables(PAST_LEN + jnp.arange(n_tok, dtype=jnp.int32) % t_len)
    ck = _cache_to_kernel(cache_k)
    cv = _cache_to_kernel(cache_v)
    flat = [x, p, cos, sin, state_conv]
    n_steps = batch // SAMPLE_STEP_BATCH
    small = [weights[n] for n in SMALL_WEIGHTS]
    big = [weights[n] for n in MATMUL_WEIGHTS]
    hbm = pl.BlockSpec(memory_space=pl.ANY)
    cache_block = pl.BlockSpec((SAMPLE_STEP_BATCH, WINDOW, KV_WIDTH), lambda i: (i, 0, 0))
    in_specs = ([pl.BlockSpec(memory_space=pltpu.SMEM)] + [_resident(a, 1) for a in flat]
                + [cache_block, cache_block] + [_resident(w, 1) for w in small] + [hbm] * len(big))
    out_shape = (jax.ShapeDtypeStruct((batch, t_len, D_MODEL), F32),
                 jax.ShapeDtypeStruct((batch, WINDOW, KV_WIDTH), F32),
                 jax.ShapeDtypeStruct((batch, WINDOW, KV_WIDTH), F32),
                 jax.ShapeDtypeStruct((batch, CONV_K - 1, CONV_WIDTH), F32)
                 ) + tuple(jax.ShapeDtypeStruct(w.shape, BF16) for w in big)
    whole = lambda s: pl.BlockSpec(s.shape, lambda i: (0,) * len(s.shape), pipeline_mode=pl.Buffered(1))
    out_specs = (whole(out_shape[0]), cache_block, cache_block, whole(out_shape[3])) + (hbm,) * len(big)
    late = [weights[n] for n in LATE_WEIGHTS]
    scratch = ([pltpu.VMEM((N_Q_HEADS, n_tok, LANES), F32),
                pltpu.VMEM((KV_WIDTH, n_tok), BF16), pltpu.VMEM((KV_WIDTH, n_tok), BF16),
                pltpu.VMEM((KV_WIDTH, n_tok), F32), pltpu.VMEM((KV_WIDTH, n_tok), F32),
                pltpu.VMEM((n_tok, ATTN_WIDTH), F32), pltpu.VMEM((n_tok, CONV_WIDTH), F32),
                pltpu.VMEM((n_tok, CONV_WIDTH), F32)]
               + [pltpu.VMEM(w.shape, BF16) for w in big]
               + [pltpu.VMEM((W_IN_SLOTS, W_IN_CHUNK, IN_WIDTH), F32)]
               + [pltpu.VMEM((2, w.shape[0] // n_steps, w.shape[1]), F32) for w in late]
               + [pltpu.SemaphoreType.DMA((W_IN_SLOTS,)), pltpu.SemaphoreType.DMA((len(late), 2)),
                  pltpu.SemaphoreType.DMA((len(big),))])
    y, k_new, v_new, conv_new, *w_bf16 = pl.pallas_call(
        _sample_kernel, grid=(n_steps,), in_specs=in_specs, out_specs=out_specs,
        out_shape=out_shape, scratch_shapes=scratch, name="sample_layer",
        compiler_params=pltpu.CompilerParams(dimension_semantics=("arbitrary",),
                                             vmem_limit_bytes=SAMPLE_VMEM_LIMIT),
    )(sinks_p, *flat, ck, cv, *small, *big)
    return y, k_new, v_new, conv_new, dict(zip(MATMUL_WEIGHTS, w_bf16))


def kernel(x_prompt, x_sample, p_prompt, p_sample, cache_k, cache_v, state_conv, g_mix_norm, w_in, g_q, g_k,
           sinks, conv_w, g_attn_out, g_conv_out, w_o, g_mlp_norm, w_up, w_down, g_ple_norm, w_ple_gate, w_ple):
    depth = w_in.shape[0]
    yp, ys = x_prompt, x_sample
    outs = [[] for _ in range(6)]
    for i in range(depth):
        sinks_p, weights = _prepare_weights(
            g_mix_norm[i], w_in[i], g_q[i], g_k[i], sinks[i], conv_w[i], g_attn_out[i], g_conv_out[i],
            w_o[i], g_mlp_norm[i], w_up[i], w_down[i], g_ple_norm[i], w_ple_gate[i], w_ple[i])
        ys, ksn, vsn, csn, w_bf16 = _sample_layer(ys, p_sample[i], cache_k[i], cache_v[i], state_conv[i],
                                                  sinks_p, weights)
        yp, kp, vp, cp = _prompt_layer(yp, p_prompt[i], sinks_p,
                                       [w_bf16.get(n, weights[n]) for n in WEIGHT_NAMES])
        for lst, val in zip(outs, (_cache_from_kernel(kp), _cache_from_kernel(vp), cp,
                                   _cache_from_kernel(ksn), _cache_from_kernel(vsn), csn)):
            lst.append(val)
    return (yp, ys) + tuple(jnp.stack(o) for o in outs)
```

```python
import jax
import jax.numpy as jnp
import numpy as np
from jax import lax
from jax.experimental import pallas as pl
from jax.experimental.pallas import tpu as pltpu

D_MODEL = 1024
HEAD_DIM = 64
N_Q_HEADS = 8
N_KV_HEADS = 2
GQA_GROUP = N_Q_HEADS // N_KV_HEADS
ATTN_WIDTH = N_Q_HEADS * HEAD_DIM
KV_WIDTH = N_KV_HEADS * HEAD_DIM
CONV_WIDTH = D_MODEL - ATTN_WIDTH
CONV_K = 3
WINDOW = 128
BLOCK = 128
ROPE_THETA = 10000.0
D_FF = 4 * D_MODEL
PLE_DIM = 256
EPS = 1e-6
NEG = -1e30
LOG2_E = 1.4426950408889634
PAST_LEN = 16384
IN_WIDTH = ATTN_WIDTH + 2 * KV_WIDTH + 3 * CONV_WIDTH

LANES = 128
SUBLANES = 8
N_QCOL = ATTN_WIDTH // LANES
SEQ_TILE = 512
SUB_TILES = (256, 256)
FF_CHUNK = 1024
SAMPLE_STEP_BATCH = 8
VMEM_LIMIT = 56 * 1024 * 1024
SAMPLE_VMEM_LIMIT = 60 * 1024 * 1024

O_K = ATTN_WIDTH
O_V = O_K + KV_WIDTH
O_B = O_V + KV_WIDTH
O_C = O_B + CONV_WIDTH
O_H = O_C + CONV_WIDTH

BF16 = jnp.bfloat16
F32 = jnp.float32


def _dot(a, b):
    return jnp.dot(a, b, preferred_element_type=F32)


def _dot_t(a, b):
    return lax.dot_general(a, b, (((1,), (1,)), ((), ())), preferred_element_type=F32)


def _rms(x, g):
    return x * lax.rsqrt(jnp.mean(x * x, axis=-1, keepdims=True) + EPS) * g


def _head_norm_rope(t, ones_bd, g, cos, sin):
    ssq = _dot((t * t).astype(BF16), ones_bd)
    t = t * lax.rsqrt(ssq * (1.0 / HEAD_DIM) + EPS)
    lane = lax.broadcasted_iota(jnp.int32, (t.shape[0], LANES), 1)
    first_half = (lane & (HEAD_DIM - 1)) < HEAD_DIM // 2
    cols = []
    for m in range(t.shape[1] // LANES):
        c = t[:, m * LANES:(m + 1) * LANES] * g
        up = pltpu.roll(c, LANES - HEAD_DIM // 2, axis=1)
        dn = pltpu.roll(c, HEAD_DIM // 2, axis=1)
        cols.append(c * cos + jnp.where(first_half, up, dn) * sin)
    return cols


def _rope_rows(cos_t, sin_t):
    cos = jnp.concatenate([cos_t] * (LANES // cos_t.shape[0]), axis=0).T
    sin = jnp.concatenate([sin_t] * (LANES // sin_t.shape[0]), axis=0).T
    return cos, sin


def _front(x, cos, sin, refs):
    h = _rms(x, refs["g_mix"][...]).astype(BF16)
    z = _dot(h, refs["w_in"][...])
    qcols = _head_norm_rope(z[:, :O_K], refs["bd_q"][...], refs["g_q"][...], cos, sin)
    (k,) = _head_norm_rope(z[:, O_K:O_V], refs["bd_k"][...], refs["g_k"][...], cos, sin)
    v = z[:, O_V:O_B]
    b_gate = z[:, O_B:O_C]
    u = z[:, O_C:O_H] * z[:, O_H:]
    return qcols, k, v, b_gate, u


def _tail_steps(x, o_attn, o_conv, p, refs, store):
    mixed = jnp.concatenate([_rms(o_attn, refs["g_attn"][...]), _rms(o_conv, refs["g_conv"][...])], axis=1)
    x = x + _dot(mixed.astype(BF16), refs["w_o"][...])
    hm = _rms(x, refs["g_mlp"][...]).astype(BF16)
    yield
    for c in range(D_FF // FF_CHUNK):
        up = _dot(hm, refs["w_up"][:, c * FF_CHUNK:(c + 1) * FF_CHUNK])
        act = jnp.square(jnp.maximum(up.astype(BF16), 0.0))
        x = x + _dot(act, refs["w_down"][c * FF_CHUNK:(c + 1) * FF_CHUNK, :])
        yield
    gate = jax.nn.sigmoid(_dot(_rms(x, refs["g_ple"][...]).astype(BF16), refs["w_gate"][...]))
    store(x + gate * _dot(p.astype(BF16), refs["w_ple"][...]))


def _run(steps):
    for _ in steps:
        pass


def _split_heads(col, c):
    group = (2 * c) // GQA_GROUP
    lo = lax.broadcasted_iota(jnp.int32, col.shape, 1) < HEAD_DIM
    swapped = pltpu.roll(col, HEAD_DIM, axis=1)
    if group == 0:
        return jnp.where(lo, col, 0.0), jnp.where(lo, swapped, 0.0)
    return jnp.where(lo, 0.0, swapped), jnp.where(lo, 0.0, col)


def _sink_column(sinks_ref, rows_per_head):
    return jnp.concatenate(
        [jnp.full((rows_per_head, 1), sinks_ref[i] * LOG2_E, F32) for i in range(N_Q_HEADS)], axis=0)


def _merge_heads(pv, rows_per_head, c):
    group = (2 * c) // GQA_GROUP
    lo = lax.broadcasted_iota(jnp.int32, (rows_per_head, LANES), 1) < HEAD_DIM
    a = pv[(2 * c) * rows_per_head:(2 * c + 1) * rows_per_head]
    b = pv[(2 * c + 1) * rows_per_head:(2 * c + 2) * rows_per_head]
    if group == 0:
        return jnp.where(lo, a, pltpu.roll(b, HEAD_DIM, axis=1))
    return jnp.where(lo, pltpu.roll(a, HEAD_DIM, axis=1), b)


WEIGHT_NAMES = ("g_mix", "w_in", "bd_q", "bd_k", "g_q", "g_k", "conv_w", "g_attn", "g_conv", "w_o",
                "g_mlp", "w_up", "w_down", "g_ple", "w_gate", "w_ple")
MATMUL_WEIGHTS = ("w_in", "w_o", "w_up", "w_down", "w_gate", "w_ple")
LATE_WEIGHTS = MATMUL_WEIGHTS[1:]
SMALL_WEIGHTS = tuple(n for n in WEIGHT_NAMES if n not in MATMUL_WEIGHTS)
W_IN_CHUNK = 64
W_IN_SLOTS = 4


def _prompt_kernel(sinks_ref, x_ref, p_ref, cos_ref, sin_ref, *rest):
    nw = len(WEIGHT_NAMES)
    refs = dict(zip(WEIGHT_NAMES, rest[:nw]))
    y_ref, kout_ref, vout_ref, convout_ref, kbuf, vbuf, ubuf = rest[nw:]
    tm = x_ref.shape[0]
    j = pl.program_id(1)

    @pl.when(j == 0)
    def _():
        kbuf[0:BLOCK, :] = jnp.zeros((BLOCK, KV_WIDTH), BF16)
        vbuf[:, 0:BLOCK] = jnp.zeros((KV_WIDTH, BLOCK), BF16)
        ubuf[0:SUBLANES, :] = jnp.zeros((SUBLANES, CONV_WIDTH), F32)

    kj = lax.broadcasted_iota(jnp.int32, (BLOCK, N_Q_HEADS * BLOCK), 0)
    qi = lax.broadcasted_iota(jnp.int32, (BLOCK, N_Q_HEADS * BLOCK), 1) & (BLOCK - 1)
    own = kj <= qi
    sink = jnp.concatenate([jnp.full((1, BLOCK), sinks_ref[i] * LOG2_E, F32) for i in range(N_Q_HEADS)],
                           axis=1)
    cw = refs["conv_w"]

    last = {}

    def sub_tile_steps(r0, n):
        rows = slice(r0, r0 + n)
        x = x_ref[rows, :]
        qcols, k, v, b_gate, u = _front(x, *_rope_rows(cos_ref[:, rows], sin_ref[:, rows]), refs)
        kbuf[BLOCK + r0:BLOCK + r0 + n, :] = k.astype(BF16)
        v_t = v.T
        vbuf[:, BLOCK + r0:BLOCK + r0 + n] = v_t.astype(BF16)
        ubuf[SUBLANES + r0:SUBLANES + r0 + n, :] = u
        last.update(k=k, v_t=v_t, u=u, n=n)
        yield

        o_blocks = []
        for i in range(n // BLOCK):
            g = r0 // BLOCK + i
            qs = []
            for c in range(N_QCOL):
                qs.extend(_split_heads(qcols[c][i * BLOCK:(i + 1) * BLOCK], c))
            qstack = jnp.concatenate(qs, axis=0).astype(BF16)
            s2 = _dot_t(kbuf[g * BLOCK:(g + 2) * BLOCK, :], qstack)
            s_prev = s2[:BLOCK]
            if g == 0:
                s_prev = jnp.where(j == 0, NEG, s_prev)
            s = jnp.where(own, s2[BLOCK:], s_prev)
            mx = jnp.maximum(jnp.max(s, axis=0, keepdims=True), sink)
            e = jnp.exp2(s - mx)
            denom = jnp.sum(e, axis=0, keepdims=True) + jnp.exp2(sink - mx)
            e2 = jnp.concatenate([jnp.where(own, 0.0, e), jnp.where(own, e, 0.0)], axis=0).astype(BF16)
            pv = _dot(vbuf[:, g * BLOCK:(g + 2) * BLOCK], e2) * (1.0 / denom)
            cols = []
            for c in range(N_QCOL):
                d0 = (2 * c) // GQA_GROUP * HEAD_DIM
                col_t = jnp.concatenate([pv[d0:d0 + HEAD_DIM, (2 * c) * BLOCK:(2 * c + 1) * BLOCK],
                                         pv[d0:d0 + HEAD_DIM, (2 * c + 1) * BLOCK:(2 * c + 2) * BLOCK]], axis=0)
                cols.append(col_t.T)
            o_blocks.append(jnp.concatenate(cols, axis=1))
            yield
        o_attn = jnp.concatenate(o_blocks, axis=0)

        conv = (cw[0] * ubuf[SUBLANES - 2 + r0:SUBLANES - 2 + r0 + n, :]
                + cw[1] * ubuf[SUBLANES - 1 + r0:SUBLANES - 1 + r0 + n, :]
                + cw[2] * u)

        def store(y):
            y_ref[rows, :] = y

        yield from _tail_steps(x, o_attn, b_gate * conv, p_ref[rows, :], refs, store)

    assert sum(SUB_TILES) == tm
    starts = [sum(SUB_TILES[:t]) for t in range(len(SUB_TILES))]
    tiles = [sub_tile_steps(r0, n) for r0, n in zip(starts, SUB_TILES)]
    begin = [sum(1 + n // BLOCK for n in SUB_TILES[:t]) for t in range(len(SUB_TILES))]
    live = list(range(len(tiles)))
    rnd = 0
    while live:
        for t in reversed([t for t in live if rnd >= begin[t]]):
            if next(tiles[t], True) is True:
                live.remove(t)
        rnd += 1

    n_last = last["n"]
    kout_ref[...] = last["k"][n_last - BLOCK:, :].T
    vout_ref[...] = last["v_t"][:, n_last - BLOCK:]
    convout_ref[...] = last["u"][n_last - (CONV_K - 1):, :]
    kbuf[0:BLOCK, :] = kbuf[tm:tm + BLOCK, :]
    vbuf[:, 0:BLOCK] = vbuf[:, tm:tm + BLOCK]
    ubuf[0:SUBLANES, :] = ubuf[tm:tm + SUBLANES, :]


def _sample_kernel(sinks_ref, x_ref, p_ref, cos_ref, sin_ref, state_ref, ck_ref, cv_ref, *rest):
    ns, nm, nl = len(SMALL_WEIGHTS), len(MATMUL_WEIGHTS), len(LATE_WEIGHTS)
    refs = dict(zip(SMALL_WEIGHTS, rest[:ns]))
    w_f32 = dict(zip(MATMUL_WEIGHTS, rest[ns:ns + nm]))
    y_ref, kout_ref, vout_ref, convout_ref = rest[ns + nm:ns + nm + 4]
    w_out = dict(zip(MATMUL_WEIGHTS, rest[ns + nm + 4:ns + 2 * nm + 4]))
    scratch = rest[ns + 2 * nm + 4:]
    qbuf, kbuf, vbuf, knew, vnew, obuf, bgbuf, ubuf = scratch[:8]
    w_vmem = dict(zip(MATMUL_WEIGHTS, scratch[8:8 + nm]))
    in_stage = scratch[8 + nm]
    late_stage = dict(zip(LATE_WEIGHTS, scratch[9 + nm:9 + nm + nl]))
    in_sem, late_sem, out_sem = scratch[9 + nm + nl:]
    refs.update(w_vmem)
    t_len = x_ref.shape[1]
    n_tok = x_ref.shape[0] * t_len
    step = pl.program_id(0)
    n_steps = pl.num_programs(0)
    step_batch = ck_ref.shape[0]
    pair_rows = 2 * t_len
    chunk_pairs = BLOCK // pair_rows

    n_slots = in_stage.shape[0]

    def in_copy(i):
        return pltpu.make_async_copy(w_f32["w_in"].at[pl.ds(i * W_IN_CHUNK, W_IN_CHUNK), :],
                                     in_stage.at[i % n_slots], in_sem.at[i % n_slots])

    def late_copy(n, name, s):
        rows = late_stage[name].shape[1]
        return pltpu.make_async_copy(w_f32[name].at[pl.ds(s * rows, rows), :], late_stage[name].at[s % 2],
                                     late_sem.at[n, s % 2])

    def out_copy(n, name):
        return pltpu.make_async_copy(w_vmem[name], w_out[name], out_sem.at[n])

    @pl.when(step + 1 < n_steps)
    def _():
        for n, name in enumerate(LATE_WEIGHTS):
            late_copy(n, name, step + 1).start()

    @pl.when(step == 0)
    def _():
        for n, name in enumerate(LATE_WEIGHTS):
            late_copy(n, name, step).start()
        n_chunks = D_MODEL // W_IN_CHUNK
        for i in range(n_slots):
            in_copy(i).start()
        for i in range(n_chunks):
            in_copy(i).wait()
            w_vmem["w_in"][i * W_IN_CHUNK:(i + 1) * W_IN_CHUNK, :] = in_stage[i % n_slots].astype(BF16)
            if i + n_slots < n_chunks:
                in_copy(i + n_slots).start()
        out_copy(0, "w_in").start()
        qcols, k, v, b_gate, u = _front(x_ref[...].reshape(n_tok, D_MODEL),
                                        *_rope_rows(cos_ref[...], sin_ref[...]), refs)
        for c in range(N_QCOL):
            qbuf[2 * c], qbuf[2 * c + 1] = _split_heads(qcols[c], c)
        k_t, v_t = k.T, v.T
        kbuf[...] = k_t.astype(BF16)
        vbuf[...] = v_t.astype(BF16)
        knew[...] = k_t
        vnew[...] = v_t
        bgbuf[...] = b_gate
        ubuf[...] = u
        convout_ref[...] = u.reshape(convout_ref.shape[0], t_len, CONV_WIDTH)[:, t_len - (CONV_K - 1):, :]

    n_rows = N_Q_HEADS * pair_rows
    row = lax.broadcasted_iota(jnp.int32, (n_rows, 3 * BLOCK), 0)
    col = lax.broadcasted_iota(jnp.int32, (n_rows, 3 * BLOCK), 1)
    row_b = (row % pair_rows) // t_len
    row_t = row % t_len
    cache_ok = (col < 2 * BLOCK) & (col // BLOCK == row_b) & ((col % BLOCK) > row_t)
    new_col = col - 2 * BLOCK
    sink = _sink_column(sinks_ref, pair_rows)

    def pair_stages(i):
        pair = step * (step_batch // 2) + i
        r0 = pl.multiple_of(pair * pair_rows, pair_rows)
        c0 = pl.multiple_of((pair // chunk_pairs) * BLOCK, BLOCK)
        q = qbuf[:, pl.ds(r0, pair_rows), :].reshape(n_rows, LANES).astype(BF16)
        ck = [ck_ref[2 * i + bb] for bb in range(2)]
        cv = [cv_ref[2 * i + bb] for bb in range(2)]
        keys = jnp.concatenate([c.astype(BF16) for c in ck] + [kbuf[:, pl.ds(c0, BLOCK)]], axis=1)
        vals = jnp.concatenate([c.astype(BF16) for c in cv] + [vbuf[:, pl.ds(c0, BLOCK)]], axis=1)
        s = _dot(q, keys)
        yield
        new_ok = ((col >= 2 * BLOCK) & (new_col // t_len == (pair % chunk_pairs) * 2 + row_b)
                  & (new_col % t_len <= row_t))
        ok = cache_ok | new_ok
        s = jnp.where(ok, s, NEG)
        mx = jnp.maximum(jnp.max(s, axis=-1, keepdims=True), sink)
        e = jnp.where(ok, jnp.exp2(s - mx), 0.0)
        denom = jnp.sum(e, axis=-1, keepdims=True) + jnp.exp2(sink - mx)
        yield
        pv = _dot_t(e.astype(BF16), vals) * (1.0 / denom)
        for c in range(N_QCOL):
            obuf[pl.ds(r0, pair_rows), c * LANES:(c + 1) * LANES] = _merge_heads(pv, pair_rows, c)
        yield
        k_chunk = knew[:, pl.ds(c0, BLOCK)]
        v_chunk = vnew[:, pl.ds(c0, BLOCK)]
        keep = lax.broadcasted_iota(jnp.int32, (KV_WIDTH, BLOCK), 1) < BLOCK - t_len
        for bb in range(2):
            to_tail = (BLOCK - t_len) - t_len * ((pair % chunk_pairs) * 2 + bb)
            kout_ref[2 * i + bb] = jnp.where(keep, pltpu.roll(ck[bb], BLOCK - t_len, axis=1),
                                             pltpu.roll(k_chunk, to_tail, axis=1))
            vout_ref[2 * i + bb] = jnp.where(keep, pltpu.roll(cv[bb], BLOCK - t_len, axis=1),
                                             pltpu.roll(v_chunk, to_tail, axis=1))

    pairs = [pair_stages(i) for i in range(step_batch // 2)]
    for _ in range(4):
        for stages in pairs:
            next(stages, None)

    for n, name in enumerate(LATE_WEIGHTS):
        rows = late_stage[name].shape[1]
        late_copy(n, name, step).wait()
        r0 = pl.multiple_of(step * rows, rows)
        w_vmem[name][pl.ds(r0, rows), :] = late_stage[name][step % 2].astype(BF16)

    @pl.when(step == n_steps - 1)
    def _():
        for n, name in enumerate(LATE_WEIGHTS):
            out_copy(n + 1, name).start()
        u = ubuf[...]
        tok = lax.broadcasted_iota(jnp.int32, u.shape, 0) % t_len
        per_token = lambda r: jnp.broadcast_to(state_ref[:, r:r + 1, :],
                                               (state_ref.shape[0], t_len, CONV_WIDTH)).reshape(u.shape)
        s0, s1 = per_token(0), per_token(1)
        um1 = jnp.where(tok >= 1, pltpu.roll(u, 1, axis=0), s1)
        um2 = jnp.where(tok >= 2, pltpu.roll(u, 2, axis=0), jnp.where(tok == 0, s0, s1))
        cw = refs["conv_w"]
        conv = cw[0] * um2 + cw[1] * um1 + cw[2] * u

        def store(y):
            y_ref[...] = y.reshape(y_ref.shape)

        _run(_tail_steps(x_ref[...].reshape(n_tok, D_MODEL), obuf[...], bgbuf[...] * conv,
                         p_ref[...].reshape(n_tok, PLE_DIM), refs, store))
        for n, name in enumerate(MATMUL_WEIGHTS):
            out_copy(n, name).wait()


def _rope_tables(pos):
    inv_freq = ROPE_THETA ** (-jnp.arange(0, HEAD_DIM, 2, dtype=F32) / HEAD_DIM)
    ang = inv_freq[:, None] * pos.astype(F32)[None, :]
    sin = jnp.sin(ang)
    return jnp.cos(ang), jnp.concatenate([-sin, sin], axis=0)


def _cache_to_kernel(c):
    batch, keys = c.shape[0], c.shape[1]
    return jnp.transpose(c, (0, 2, 3, 1)).reshape(batch, KV_WIDTH, keys)


def _cache_from_kernel(c):
    batch, _, keys = c.shape
    return jnp.transpose(c.reshape(batch, N_KV_HEADS, HEAD_DIM, keys), (0, 3, 1, 2))


def _block_diag_ones(width):
    idx = np.arange(width) // HEAD_DIM
    return jnp.asarray(idx[:, None] == idx[None, :], dtype=BF16)


def _prepare_weights(g_mix_norm, w_in, g_q, g_k, sinks, conv_w, g_attn_out, g_conv_out, w_o,
                     g_mlp_norm, w_up, w_down, g_ple_norm, w_ple_gate, w_ple):
    row = lambda g: g.reshape(1, -1).astype(F32)
    tile_head = lambda g: jnp.tile(g.reshape(1, HEAD_DIM).astype(F32), (1, LANES // HEAD_DIM))
    weights = dict(
        g_mix=row(g_mix_norm), w_in=w_in,
        bd_q=_block_diag_ones(ATTN_WIDTH), bd_k=_block_diag_ones(KV_WIDTH),
        g_q=tile_head(g_q) * (HEAD_DIM ** -0.5 * LOG2_E), g_k=tile_head(g_k),
        conv_w=conv_w.astype(F32).reshape(CONV_K, 1, CONV_WIDTH),
        g_attn=row(g_attn_out), g_conv=row(g_conv_out), w_o=w_o,
        g_mlp=row(g_mlp_norm), w_up=w_up, w_down=w_down,
        g_ple=row(g_ple_norm), w_gate=w_ple_gate, w_ple=w_ple)
    return sinks.astype(F32), weights


def _resident(a, n_grid):
    zeros = (0,) * a.ndim
    index_map = (lambda b, j: zeros) if n_grid == 2 else (lambda i: zeros)
    return pl.BlockSpec(a.shape, index_map, pipeline_mode=pl.Buffered(1))


def _prompt_layer(x, p, sinks_p, weights):
    batch, seq, _ = x.shape
    tm = SEQ_TILE
    cos, sin = _rope_tables(jnp.arange(seq, dtype=jnp.int32))
    tile = lambda w: pl.BlockSpec((None, tm, w), lambda b, j: (b, j, 0))
    per_batch = lambda r, w: pl.BlockSpec((None, r, w), lambda b, j: (b, 0, 0))
    table = lambda a: pl.BlockSpec((a.shape[0], tm), lambda b, j: (0, j))
    in_specs = ([pl.BlockSpec(memory_space=pltpu.SMEM), tile(D_MODEL), tile(PLE_DIM), table(cos), table(sin)]
                + [_resident(w, 2) for w in weights])
    out_shape = (jax.ShapeDtypeStruct((batch, seq, D_MODEL), F32),
                 jax.ShapeDtypeStruct((batch, BLOCK, KV_WIDTH), F32),
                 jax.ShapeDtypeStruct((batch, BLOCK, KV_WIDTH), F32),
                 jax.ShapeDtypeStruct((batch, CONV_K - 1, CONV_WIDTH), F32))
    out_specs = (tile(D_MODEL), per_batch(BLOCK, KV_WIDTH), per_batch(BLOCK, KV_WIDTH),
                 per_batch(CONV_K - 1, CONV_WIDTH))
    scratch = [pltpu.VMEM((BLOCK + tm, KV_WIDTH), BF16), pltpu.VMEM((KV_WIDTH, BLOCK + tm), BF16),
               pltpu.VMEM((SUBLANES + tm, CONV_WIDTH), F32)]
    return pl.pallas_call(
        _prompt_kernel, grid=(batch, seq // tm), in_specs=in_specs, out_specs=out_specs,
        out_shape=out_shape, scratch_shapes=scratch, name="prompt_layer",
        compiler_params=pltpu.CompilerParams(dimension_semantics=("arbitrary", "arbitrary"),
                                             vmem_limit_bytes=VMEM_LIMIT),
    )(sinks_p, x, p, cos, sin, *weights)


def _sample_layer(x, p, cache_k, cache_v, state_conv, sinks_p, weights):
    batch, t_len, _ = x.shape
    n_tok = batch * t_len
    cos, sin = _rope_tables(PAST_LEN + jnp.arange(n_tok, dtype=jnp.int32) % t_len)
    ck = _cache_to_kernel(cache_k)
    cv = _cache_to_kernel(cache_v)
    flat = [x, p, cos, sin, state_conv]
    n_steps = batch // SAMPLE_STEP_BATCH
    small = [weights[n] for n in SMALL_WEIGHTS]
    big = [weights[n] for n in MATMUL_WEIGHTS]
    hbm = pl.BlockSpec(memory_space=pl.ANY)
    cache_block = pl.BlockSpec((SAMPLE_STEP_BATCH, WINDOW, KV_WIDTH), lambda i: (i, 0, 0))
    in_specs = ([pl.BlockSpec(memory_space=pltpu.SMEM)] + [_resident(a, 1) for a in flat]
                + [cache_block, cache_block] + [_resident(w, 1) for w in small] + [hbm] * len(big))
    out_shape = (jax.ShapeDtypeStruct((batch, t_len, D_MODEL), F32),
                 jax.ShapeDtypeStruct((batch, WINDOW, KV_WIDTH), F32),
                 jax.ShapeDtypeStruct((batch, WINDOW, KV_WIDTH), F32),
                 jax.ShapeDtypeStruct((batch, CONV_K - 1, CONV_WIDTH), F32)
                 ) + tuple(jax.ShapeDtypeStruct(w.shape, BF16) for w in big)
    whole = lambda s: pl.BlockSpec(s.shape, lambda i: (0,) * len(s.shape), pipeline_mode=pl.Buffered(1))
    out_specs = (whole(out_shape[0]), cache_block, cache_block, whole(out_shape[3])) + (hbm,) * len(big)
    late = [weights[n] for n in LATE_WEIGHTS]
    scratch = ([pltpu.VMEM((N_Q_HEADS, n_tok, LANES), F32),
                pltpu.VMEM((KV_WIDTH, n_tok), BF16), pltpu.VMEM((KV_WIDTH, n_tok), BF16),
                pltpu.VMEM((KV_WIDTH, n_tok), F32), pltpu.VMEM((KV_WIDTH, n_tok), F32),
                pltpu.VMEM((n_tok, ATTN_WIDTH), F32), pltpu.VMEM((n_tok, CONV_WIDTH), F32),
                pltpu.VMEM((n_tok, CONV_WIDTH), F32)]
               + [pltpu.VMEM(w.shape, BF16) for w in big]
               + [pltpu.VMEM((W_IN_SLOTS, W_IN_CHUNK, IN_WIDTH), F32)]
               + [pltpu.VMEM((2, w.shape[0] // n_steps, w.shape[1]), F32) for w in late]
               + [pltpu.SemaphoreType.DMA((W_IN_SLOTS,)), pltpu.SemaphoreType.DMA((len(late), 2)),
                  pltpu.SemaphoreType.DMA((len(big),))])
    y, k_new, v_new, conv_new, *w_bf16 = pl.pallas_call(
        _sample_kernel, grid=(n_steps,), in_specs=in_specs, out_specs=out_specs,
        out_shape=out_shape, scratch_shapes=scratch, name="sample_layer",
        compiler_params=pltpu.CompilerParams(dimension_semantics=("arbitrary",),
                                             vmem_limit_bytes=SAMPLE_VMEM_LIMIT),
    )(sinks_p, *flat, ck, cv, *small, *big)
    return y, k_new, v_new, conv_new, dict(zip(MATMUL_WEIGHTS, w_bf16))


def kernel(x_prompt, x_sample, p_prompt, p_sample, cache_k, cache_v, state_conv, g_mix_norm, w_in, g_q, g_k,
           sinks, conv_w, g_attn_out, g_conv_out, w_o, g_mlp_norm, w_up, w_down, g_ple_norm, w_ple_gate, w_ple):
    depth = w_in.shape[0]
    yp, ys = x_prompt, x_sample
    outs = [[] for _ in range(6)]
    for i in range(depth):
        sinks_p, weights = _prepare_weights(
            g_mix_norm[i], w_in[i], g_q[i], g_k[i], sinks[i], conv_w[i], g_attn_out[i], g_conv_out[i],
            w_o[i], g_mlp_norm[i], w_up[i], w_down[i], g_ple_norm[i], w_ple_gate[i], w_ple[i])
        ys, ksn, vsn, csn, w_bf16 = _sample_layer(ys, p_sample[i], cache_k[i], cache_v[i], state_conv[i],
                                                  sinks_p, weights)
        yp, kp, vp, cp = _prompt_layer(yp, p_prompt[i], sinks_p,
                                       [w_bf16.get(n, weights[n]) for n in WEIGHT_NAMES])
        for lst, val in zip(outs, (_cache_from_kernel(kp), _cache_from_kernel(vp), cp,
                                   _cache_from_kernel(ksn), _cache_from_kernel(vsn), csn)):
            lst.append(val)
    return (yp, ys) + tuple(jnp.stack(o) for o in outs)
```

```python
import jax
import jax.numpy as jnp
import numpy as np
from jax import lax
from jax.experimental import pallas as pl
from jax.experimental.pallas import tpu as pltpu

D_MODEL = 1024
HEAD_DIM = 64
N_Q_HEADS = 8
N_KV_HEADS = 2
GQA_GROUP = N_Q_HEADS // N_KV_HEADS
ATTN_WIDTH = N_Q_HEADS * HEAD_DIM
KV_WIDTH = N_KV_HEADS * HEAD_DIM
CONV_WIDTH = D_MODEL - ATTN_WIDTH
CONV_K = 3
WINDOW = 128
BLOCK = 128
ROPE_THETA = 10000.0
D_FF = 4 * D_MODEL
PLE_DIM = 256
EPS = 1e-6
NEG = -1e30
LOG2_E = 1.4426950408889634
PAST_LEN = 16384
IN_WIDTH = ATTN_WIDTH + 2 * KV_WIDTH + 3 * CONV_WIDTH

LANES = 128
SUBLANES = 8
MXU_TILE = 256
N_QCOL = ATTN_WIDTH // LANES
SEQ_TILE = 512
SUB_TILES = (256, 256)
FF_CHUNK = 1024
SAMPLE_STEP_BATCH = 8
VMEM_LIMIT = 56 * 1024 * 1024
SAMPLE_VMEM_LIMIT = 60 * 1024 * 1024

O_K = ATTN_WIDTH
O_V = O_K + KV_WIDTH
O_B = O_V + KV_WIDTH
O_C = O_B + CONV_WIDTH
O_H = O_C + CONV_WIDTH

BF16 = jnp.bfloat16
F32 = jnp.float32


def _dot(a, b):
    return jnp.dot(a, b, preferred_element_type=F32)


def _dot_t(a, b):
    return lax.dot_general(a, b, (((1,), (1,)), ((), ())), preferred_element_type=F32)


def _rms(x, g):
    return x * lax.rsqrt(jnp.mean(x * x, axis=-1, keepdims=True) + EPS) * g


def _head_norm_rope(t, ones_bd, g, cos, sin):
    sq = (t * t).astype(BF16)
    w = ones_bd.shape[0]
    ssq = jnp.concatenate([_dot(sq[:, i:i + w], ones_bd) for i in range(0, t.shape[1], w)], axis=1)
    t = t * lax.rsqrt(ssq * (1.0 / HEAD_DIM) + EPS)
    lane = lax.broadcasted_iota(jnp.int32, (t.shape[0], LANES), 1)
    first_half = (lane & (HEAD_DIM - 1)) < HEAD_DIM // 2
    cols = []
    for m in range(t.shape[1] // LANES):
        c = t[:, m * LANES:(m + 1) * LANES] * g
        up = pltpu.roll(c, LANES - HEAD_DIM // 2, axis=1)
        dn = pltpu.roll(c, HEAD_DIM // 2, axis=1)
        cols.append(c * cos + jnp.where(first_half, up, dn) * sin)
    return cols


def _rope_rows(cos_t, sin_t):
    cos = jnp.concatenate([cos_t] * (LANES // cos_t.shape[0]), axis=0).T
    sin = jnp.concatenate([sin_t] * (LANES // sin_t.shape[0]), axis=0).T
    return cos, sin


def _front(x, cos, sin, refs):
    h = _rms(x, refs["g_mix"][...]).astype(BF16)
    z = _dot(h, refs["w_in"][...])
    qcols = _head_norm_rope(z[:, :O_K], refs["bd_q"][...], refs["g_q"][...], cos, sin)
    (k,) = _head_norm_rope(z[:, O_K:O_V], refs["bd_k"][...], refs["g_k"][...], cos, sin)
    v = z[:, O_V:O_B]
    b_gate = z[:, O_B:O_C]
    u = z[:, O_C:O_H] * z[:, O_H:]
    return qcols, k, v, b_gate, u


def _tail_steps(x, o_attn, o_conv, p, refs, store):
    mixed = jnp.concatenate([_rms(o_attn, refs["g_attn"][...]), _rms(o_conv, refs["g_conv"][...])], axis=1)
    x = x + _dot(mixed.astype(BF16), refs["w_o"][...])
    hm = _rms(x, refs["g_mlp"][...]).astype(BF16)
    yield
    for c in range(D_FF // FF_CHUNK):
        up = _dot(hm, refs["w_up"][:, c * FF_CHUNK:(c + 1) * FF_CHUNK])
        act = jnp.square(jnp.maximum(up.astype(BF16), 0.0))
        x = x + _dot(act, refs["w_down"][c * FF_CHUNK:(c + 1) * FF_CHUNK, :])
        yield
    gate = jax.nn.sigmoid(_dot(_rms(x, refs["g_ple"][...]).astype(BF16), refs["w_gate"][...]))
    store(x + gate * _dot(p.astype(BF16), refs["w_ple"][...]))


def _run(steps):
    for _ in steps:
        pass


def _split_heads(col, c):
    group = (2 * c) // GQA_GROUP
    lo = lax.broadcasted_iota(jnp.int32, col.shape, 1) < HEAD_DIM
    swapped = pltpu.roll(col, HEAD_DIM, axis=1)
    if group == 0:
        return jnp.where(lo, col, 0.0), jnp.where(lo, swapped, 0.0)
    return jnp.where(lo, 0.0, swapped), jnp.where(lo, 0.0, col)


def _sink_column(sinks_ref, rows_per_head):
    return jnp.concatenate(
        [jnp.full((rows_per_head, 1), sinks_ref[i] * LOG2_E, F32) for i in range(N_Q_HEADS)], axis=0)


def _merge_heads(pv, rows_per_head, c):
    group = (2 * c) // GQA_GROUP
    lo = lax.broadcasted_iota(jnp.int32, (rows_per_head, LANES), 1) < HEAD_DIM
    a = pv[(2 * c) * rows_per_head:(2 * c + 1) * rows_per_head]
    b = pv[(2 * c + 1) * rows_per_head:(2 * c + 2) * rows_per_head]
    if group == 0:
        return jnp.where(lo, a, pltpu.roll(b, HEAD_DIM, axis=1))
    return jnp.where(lo, pltpu.roll(a, HEAD_DIM, axis=1), b)


WEIGHT_NAMES = ("g_mix", "w_in", "bd_q", "bd_k", "g_q", "g_k", "conv_w", "g_attn", "g_conv", "w_o",
                "g_mlp", "w_up", "w_down", "g_ple", "w_gate", "w_ple")
MATMUL_WEIGHTS = ("w_in", "w_o", "w_up", "w_down", "w_gate", "w_ple")
LATE_WEIGHTS = MATMUL_WEIGHTS[1:]
SMALL_WEIGHTS = tuple(n for n in WEIGHT_NAMES if n not in MATMUL_WEIGHTS)
W_IN_CHUNK = 64
W_IN_SLOTS = 4


def _prompt_kernel(sinks_ref, x_ref, p_ref, cos_ref, sin_ref, *rest):
    nw = len(WEIGHT_NAMES)
    refs = dict(zip(WEIGHT_NAMES, rest[:nw]))
    y_ref, kout_ref, vout_ref, convout_ref, kbuf, vbuf, ubuf = rest[nw:]
    tm = x_ref.shape[0]
    j = pl.program_id(1)

    @pl.when(j == 0)
    def _():
        kbuf[0:BLOCK, :] = jnp.zeros((BLOCK, KV_WIDTH), BF16)
        vbuf[:, 0:BLOCK] = jnp.zeros((KV_WIDTH, BLOCK), BF16)
        ubuf[0:SUBLANES, :] = jnp.zeros((SUBLANES, CONV_WIDTH), F32)

    kj = lax.broadcasted_iota(jnp.int32, (BLOCK, N_Q_HEADS * BLOCK), 0)
    qi = lax.broadcasted_iota(jnp.int32, (BLOCK, N_Q_HEADS * BLOCK), 1) & (BLOCK - 1)
    own = kj <= qi
    sink = jnp.concatenate([jnp.full((1, BLOCK), sinks_ref[i] * LOG2_E, F32) for i in range(N_Q_HEADS)],
                           axis=1)
    cw = refs["conv_w"]

    last = {}

    def sub_tile_steps(r0, n):
        rows = slice(r0, r0 + n)
        x = x_ref[rows, :]
        qcols, k, v, b_gate, u = _front(x, *_rope_rows(cos_ref[:, rows], sin_ref[:, rows]), refs)
        kbuf[BLOCK + r0:BLOCK + r0 + n, :] = k.astype(BF16)
        v_t = v.T
        vbuf[:, BLOCK + r0:BLOCK + r0 + n] = v_t.astype(BF16)
        ubuf[SUBLANES + r0:SUBLANES + r0 + n, :] = u
        last.update(k=k, v_t=v_t, u=u, n=n)
        yield

        o_blocks = []
        for i in range(n // BLOCK):
            g = r0 // BLOCK + i
            qs = []
            for c in range(N_QCOL):
                qs.extend(_split_heads(qcols[c][i * BLOCK:(i + 1) * BLOCK], c))
            qstack = jnp.concatenate(qs, axis=0).astype(BF16)
            s2 = _dot_t(kbuf[g * BLOCK:(g + 2) * BLOCK, :], qstack)
            s_prev = s2[:BLOCK]
            if g == 0:
                s_prev = jnp.where(j == 0, NEG, s_prev)
            s = jnp.where(own, s2[BLOCK:], s_prev)
            mx = jnp.maximum(jnp.max(s, axis=0, keepdims=True), sink)
            e = jnp.exp2(s - mx)
            e2 = jnp.concatenate([jnp.where(own, 0.0, e), jnp.where(own, e, 0.0)], axis=0).astype(BF16)
            v_ones = jnp.concatenate([vbuf[:, g * BLOCK:(g + 2) * BLOCK],
                                      jnp.ones((2 * SUBLANES, 2 * BLOCK), BF16)], axis=0)
            pv_sum = _dot(v_ones, e2)
            denom = pv_sum[KV_WIDTH:KV_WIDTH + 1] + jnp.exp2(sink - mx)
            pv = pv_sum[:KV_WIDTH] * (1.0 / denom)
            cols = []
            for c in range(N_QCOL):
                d0 = (2 * c) // GQA_GROUP * HEAD_DIM
                col_t = jnp.concatenate([pv[d0:d0 + HEAD_DIM, (2 * c) * BLOCK:(2 * c + 1) * BLOCK],
                                         pv[d0:d0 + HEAD_DIM, (2 * c + 1) * BLOCK:(2 * c + 2) * BLOCK]], axis=0)
                cols.append(col_t.T)
            o_blocks.append(jnp.concatenate(cols, axis=1))
            yield
        o_attn = jnp.concatenate(o_blocks, axis=0)

        conv = (cw[0] * ubuf[SUBLANES - 2 + r0:SUBLANES - 2 + r0 + n, :]
                + cw[1] * ubuf[SUBLANES - 1 + r0:SUBLANES - 1 + r0 + n, :]
                + cw[2] * u)

        def store(y):
            y_ref[rows, :] = y

        yield from _tail_steps(x, o_attn, b_gate * conv, p_ref[rows, :], refs, store)

    assert sum(SUB_TILES) == tm
    starts = [sum(SUB_TILES[:t]) for t in range(len(SUB_TILES))]
    tiles = [sub_tile_steps(r0, n) for r0, n in zip(starts, SUB_TILES)]
    begin = [sum(1 + n // BLOCK for n in SUB_TILES[:t]) for t in range(len(SUB_TILES))]
    live = list(range(len(tiles)))
    rnd = 0
    while live:
        for t in reversed([t for t in live if rnd >= begin[t]]):
            if next(tiles[t], True) is True:
                live.remove(t)
        rnd += 1

    n_last = last["n"]
    kout_ref[...] = last["k"][n_last - BLOCK:, :].T
    vout_ref[...] = last["v_t"][:, n_last - BLOCK:]
    convout_ref[...] = last["u"][n_last - (CONV_K - 1):, :]
    kbuf[0:BLOCK, :] = kbuf[tm:tm + BLOCK, :]
    vbuf[:, 0:BLOCK] = vbuf[:, tm:tm + BLOCK]
    ubuf[0:SUBLANES, :] = ubuf[tm:tm + SUBLANES, :]


def _sample_kernel(sinks_ref, x_ref, p_ref, cos_ref, sin_ref, state_ref, ck_ref, cv_ref, *rest):
    ns, nm, nl = len(SMALL_WEIGHTS), len(MATMUL_WEIGHTS), len(LATE_WEIGHTS)
    refs = dict(zip(SMALL_WEIGHTS, rest[:ns]))
    w_f32 = dict(zip(MATMUL_WEIGHTS, rest[ns:ns + nm]))
    y_ref, kout_ref, vout_ref, convout_ref = rest[ns + nm:ns + nm + 4]
    w_out = dict(zip(MATMUL_WEIGHTS, rest[ns + nm + 4:ns + 2 * nm + 4]))
    scratch = rest[ns + 2 * nm + 4:]
    qbuf, kbuf, vbuf, knew, vnew, obuf, bgbuf, ubuf = scratch[:8]
    w_vmem = dict(zip(MATMUL_WEIGHTS, scratch[8:8 + nm]))
    in_stage = scratch[8 + nm]
    late_stage = dict(zip(LATE_WEIGHTS, scratch[9 + nm:9 + nm + nl]))
    in_sem, late_sem, out_sem = scratch[9 + nm + nl:]
    refs.update(w_vmem)
    t_len = x_ref.shape[1]
    n_tok = x_ref.shape[0] * t_len
    step = pl.program_id(0)
    n_steps = pl.num_programs(0)
    step_batch = ck_ref.shape[0]
    pair_rows = 2 * t_len
    chunk_pairs = BLOCK // pair_rows

    n_slots = in_stage.shape[0]

    def in_copy(i):
        return pltpu.make_async_copy(w_f32["w_in"].at[pl.ds(i * W_IN_CHUNK, W_IN_CHUNK), :],
                                     in_stage.at[i % n_slots], in_sem.at[i % n_slots])

    def late_copy(n, name, s):
        rows = late_stage[name].shape[1]
        return pltpu.make_async_copy(w_f32[name].at[pl.ds(s * rows, rows), :], late_stage[name].at[s % 2],
                                     late_sem.at[n, s % 2])

    def out_copy(n, name):
        return pltpu.make_async_copy(w_vmem[name], w_out[name], out_sem.at[n])

    @pl.when(step + 1 < n_steps)
    def _():
        for n, name in enumerate(LATE_WEIGHTS):
            late_copy(n, name, step + 1).start()

    @pl.when(step == 0)
    def _():
        for n, name in enumerate(LATE_WEIGHTS):
            late_copy(n, name, step).start()
        n_chunks = D_MODEL // W_IN_CHUNK
        for i in range(n_slots):
            in_copy(i).start()
        for i in range(n_chunks):
            in_copy(i).wait()
            w_vmem["w_in"][i * W_IN_CHUNK:(i + 1) * W_IN_CHUNK, :] = in_stage[i % n_slots].astype(BF16)
            if i + n_slots < n_chunks:
                in_copy(i + n_slots).start()
        out_copy(0, "w_in").start()
        qcols, k, v, b_gate, u = _front(x_ref[...].reshape(n_tok, D_MODEL),
                                        *_rope_rows(cos_ref[...], sin_ref[...]), refs)
        for c in range(N_QCOL):
            qbuf[2 * c], qbuf[2 * c + 1] = _split_heads(qcols[c], c)
        k_t, v_t = k.T, v.T
        kbuf[...] = k_t.astype(BF16)
        vbuf[...] = v_t.astype(BF16)
        knew[...] = k_t
        vnew[...] = v_t
        bgbuf[...] = b_gate
        ubuf[...] = u
        convout_ref[...] = u.reshape(convout_ref.shape[0], t_len, CONV_WIDTH)[:, t_len - (CONV_K - 1):, :]

    n_rows = N_Q_HEADS * pair_rows
    row = lax.broadcasted_iota(jnp.int32, (n_rows, 3 * BLOCK), 0)
    col = lax.broadcasted_iota(jnp.int32, (n_rows, 3 * BLOCK), 1)
    row_b = (row % pair_rows) // t_len
    row_t = row % t_len
    cache_ok = (col < 2 * BLOCK) & (col // BLOCK == row_b) & ((col % BLOCK) > row_t)
    new_col = col - 2 * BLOCK
    sink = _sink_column(sinks_ref, pair_rows)

    def pair_stages(i):
        pair = step * (step_batch // 2) + i
        r0 = pl.multiple_of(pair * pair_rows, pair_rows)
        c0 = pl.multiple_of((pair // chunk_pairs) * BLOCK, BLOCK)
        q = qbuf[:, pl.ds(r0, pair_rows), :].reshape(n_rows, LANES).astype(BF16)
        ck = [ck_ref[2 * i + bb] for bb in range(2)]
        cv = [cv_ref[2 * i + bb] for bb in range(2)]
        keys = jnp.concatenate([c.astype(BF16) for c in ck] + [kbuf[:, pl.ds(c0, BLOCK)]], axis=1)
        vals = jnp.concatenate([c.astype(BF16) for c in cv] + [vbuf[:, pl.ds(c0, BLOCK)]], axis=1)
        s = _dot(q, keys)
        yield
        new_ok = ((col >= 2 * BLOCK) & (new_col // t_len == (pair % chunk_pairs) * 2 + row_b)
                  & (new_col % t_len <= row_t))
        ok = cache_ok | new_ok
        s = jnp.where(ok, s, NEG)
        mx = jnp.maximum(jnp.max(s, axis=-1, keepdims=True), sink)
        e = jnp.where(ok, jnp.exp2(s - mx), 0.0)
        denom = jnp.sum(e, axis=-1, keepdims=True) + jnp.exp2(sink - mx)
        yield
        pv = _dot_t(e.astype(BF16), vals) * (1.0 / denom)
        for c in range(N_QCOL):
            obuf[pl.ds(r0, pair_rows), c * LANES:(c + 1) * LANES] = _merge_heads(pv, pair_rows, c)
        yield
        k_chunk = knew[:, pl.ds(c0, BLOCK)]
        v_chunk = vnew[:, pl.ds(c0, BLOCK)]
        keep = lax.broadcasted_iota(jnp.int32, (KV_WIDTH, BLOCK), 1) < BLOCK - t_len
        for bb in range(2):
            to_tail = (BLOCK - t_len) - t_len * ((pair % chunk_pairs) * 2 + bb)
            kout_ref[2 * i + bb] = jnp.where(keep, pltpu.roll(ck[bb], BLOCK - t_len, axis=1),
                                             pltpu.roll(k_chunk, to_tail, axis=1))
            vout_ref[2 * i + bb] = jnp.where(keep, pltpu.roll(cv[bb], BLOCK - t_len, axis=1),
                                             pltpu.roll(v_chunk, to_tail, axis=1))

    pairs = [pair_stages(i) for i in range(step_batch // 2)]
    for _ in range(4):
        for stages in pairs:
            next(stages, None)

    for n, name in enumerate(LATE_WEIGHTS):
        rows = late_stage[name].shape[1]
        late_copy(n, name, step).wait()
        r0 = pl.multiple_of(step * rows, rows)
        w_vmem[name][pl.ds(r0, rows), :] = late_stage[name][step % 2].astype(BF16)

    @pl.when(step == n_steps - 1)
    def _():
        for n, name in enumerate(LATE_WEIGHTS):
            out_copy(n + 1, name).start()
        u = ubuf[...]
        tok = lax.broadcasted_iota(jnp.int32, u.shape, 0) % t_len
        per_token = lambda r: jnp.broadcast_to(state_ref[:, r:r + 1, :],
                                               (state_ref.shape[0], t_len, CONV_WIDTH)).reshape(u.shape)
        s0, s1 = per_token(0), per_token(1)
        um1 = jnp.where(tok >= 1, pltpu.roll(u, 1, axis=0), s1)
        um2 = jnp.where(tok >= 2, pltpu.roll(u, 2, axis=0), jnp.where(tok == 0, s0, s1))
        cw = refs["conv_w"]
        conv = cw[0] * um2 + cw[1] * um1 + cw[2] * u

        def store(y):
            y_ref[...] = y.reshape(y_ref.shape)

        _run(_tail_steps(x_ref[...].reshape(n_tok, D_MODEL), obuf[...], bgbuf[...] * conv,
                         p_ref[...].reshape(n_tok, PLE_DIM), refs, store))
        for n, name in enumerate(MATMUL_WEIGHTS):
            out_copy(n, name).wait()


def _rope_tables(pos):
    inv_freq = ROPE_THETA ** (-jnp.arange(0, HEAD_DIM, 2, dtype=F32) / HEAD_DIM)
    ang = inv_freq[:, None] * pos.astype(F32)[None, :]
    sin = jnp.sin(ang)
    return jnp.cos(ang), jnp.concatenate([-sin, sin], axis=0)


def _cache_to_kernel(c):
    batch, keys = c.shape[0], c.shape[1]
    return jnp.transpose(c, (0, 2, 3, 1)).reshape(batch, KV_WIDTH, keys)


def _cache_from_kernel(c):
    batch, _, keys = c.shape
    return jnp.transpose(c.reshape(batch, N_KV_HEADS, HEAD_DIM, keys), (0, 3, 1, 2))


def _block_diag_ones(width):
    idx = np.arange(width) // HEAD_DIM
    return jnp.asarray(idx[:, None] == idx[None, :], dtype=BF16)


def _prepare_weights(g_mix_norm, w_in, g_q, g_k, sinks, conv_w, g_attn_out, g_conv_out, w_o,
                     g_mlp_norm, w_up, w_down, g_ple_norm, w_ple_gate, w_ple):
    row = lambda g: g.reshape(1, -1).astype(F32)
    tile_head = lambda g: jnp.tile(g.reshape(1, HEAD_DIM).astype(F32), (1, LANES // HEAD_DIM))
    weights = dict(
        g_mix=row(g_mix_norm), w_in=w_in,
        bd_q=_block_diag_ones(MXU_TILE), bd_k=_block_diag_ones(KV_WIDTH),
        g_q=tile_head(g_q) * (HEAD_DIM ** -0.5 * LOG2_E), g_k=tile_head(g_k),
        conv_w=conv_w.astype(F32).reshape(CONV_K, 1, CONV_WIDTH),
        g_attn=row(g_attn_out), g_conv=row(g_conv_out), w_o=w_o,
        g_mlp=row(g_mlp_norm), w_up=w_up, w_down=w_down,
        g_ple=row(g_ple_norm), w_gate=w_ple_gate, w_ple=w_ple)
    return sinks.astype(F32), weights


def _resident(a, n_grid):
    zeros = (0,) * a.ndim
    index_map = (lambda b, j: zeros) if n_grid == 2 else (lambda i: zeros)
    return pl.BlockSpec(a.shape, index_map, pipeline_mode=pl.Buffered(1))


def _prompt_layer(x, p, sinks_p, weights):
    batch, seq, _ = x.shape
    tm = SEQ_TILE
    cos, sin = _rope_tables(jnp.arange(seq, dtype=jnp.int32))
    tile = lambda w: pl.BlockSpec((None, tm, w), lambda b, j: (b, j, 0))
    per_batch = lambda r, w: pl.BlockSpec((None, r, w), lambda b, j: (b, 0, 0))
    table = lambda a: pl.BlockSpec((a.shape[0], tm), lambda b, j: (0, j))
    in_specs = ([pl.BlockSpec(memory_space=pltpu.SMEM), tile(D_MODEL), tile(PLE_DIM), table(cos), table(sin)]
                + [_resident(w, 2) for w in weights])
    out_shape = (jax.ShapeDtypeStruct((batch, seq, D_MODEL), F32),
                 jax.ShapeDtypeStruct((batch, BLOCK, KV_WIDTH), F32),
                 jax.ShapeDtypeStruct((batch, BLOCK, KV_WIDTH), F32),
                 jax.ShapeDtypeStruct((batch, CONV_K - 1, CONV_WIDTH), F32))
    out_specs = (tile(D_MODEL), per_batch(BLOCK, KV_WIDTH), per_batch(BLOCK, KV_WIDTH),
                 per_batch(CONV_K - 1, CONV_WIDTH))
    scratch = [pltpu.VMEM((BLOCK + tm, KV_WIDTH), BF16), pltpu.VMEM((KV_WIDTH, BLOCK + tm), BF16),
               pltpu.VMEM((SUBLANES + tm, CONV_WIDTH), F32)]
    return pl.pallas_call(
        _prompt_kernel, grid=(batch, seq // tm), in_specs=in_specs, out_specs=out_specs,
        out_shape=out_shape, scratch_shapes=scratch, name="prompt_layer",
        compiler_params=pltpu.CompilerParams(dimension_semantics=("arbitrary", "arbitrary"),
                                             vmem_limit_bytes=VMEM_LIMIT),
    )(sinks_p, x, p, cos, sin, *weights)


def _sample_layer(x, p, cache_k, cache_v, state_conv, sinks_p, weights):
    batch, t_len, _ = x.shape
    n_tok = batch * t_len
    cos, sin = _rope_tables(PAST_LEN + jnp.arange(n_tok, dtype=jnp.int32) % t_len)
    ck = _cache_to_kernel(cache_k)
    cv = _cache_to_kernel(cache_v)
    flat = [x, p, cos, sin, state_conv]
    n_steps = batch // SAMPLE_STEP_BATCH
    small = [weights[n] for n in SMALL_WEIGHTS]
    big = [weights[n] for n in MATMUL_WEIGHTS]
    hbm = pl.BlockSpec(memory_space=pl.ANY)
    cache_block = pl.BlockSpec((SAMPLE_STEP_BATCH, WINDOW, KV_WIDTH), lambda i: (i, 0, 0))
    in_specs = ([pl.BlockSpec(memory_space=pltpu.SMEM)] + [_resident(a, 1) for a in flat]
                + [cache_block, cache_block] + [_resident(w, 1) for w in small] + [hbm] * len(big))
    out_shape = (jax.ShapeDtypeStruct((batch, t_len, D_MODEL), F32),
                 jax.ShapeDtypeStruct((batch, WINDOW, KV_WIDTH), F32),
                 jax.ShapeDtypeStruct((batch, WINDOW, KV_WIDTH), F32),
                 jax.ShapeDtypeStruct((batch, CONV_K - 1, CONV_WIDTH), F32)
                 ) + tuple(jax.ShapeDtypeStruct(w.shape, BF16) for w in big)
    whole = lambda s: pl.BlockSpec(s.shape, lambda i: (0,) * len(s.shape), pipeline_mode=pl.Buffered(1))
    out_specs = (whole(out_shape[0]), cache_block, cache_block, whole(out_shape[3])) + (hbm,) * len(big)
    late = [weights[n] for n in LATE_WEIGHTS]
    scratch = ([pltpu.VMEM((N_Q_HEADS, n_tok, LANES), F32),
                pltpu.VMEM((KV_WIDTH, n_tok), BF16), pltpu.VMEM((KV_WIDTH, n_tok), BF16),
                pltpu.VMEM((KV_WIDTH, n_tok), F32), pltpu.VMEM((KV_WIDTH, n_tok), F32),
                pltpu.VMEM((n_tok, ATTN_WIDTH), F32), pltpu.VMEM((n_tok, CONV_WIDTH), F32),
                pltpu.VMEM((n_tok, CONV_WIDTH), F32)]
               + [pltpu.VMEM(w.shape, BF16) for w in big]
               + [pltpu.VMEM((W_IN_SLOTS, W_IN_CHUNK, IN_WIDTH), F32)]
               + [pltpu.VMEM((2, w.shape[0] // n_steps, w.shape[1]), F32) for w in late]
               + [pltpu.SemaphoreType.DMA((W_IN_SLOTS,)), pltpu.SemaphoreType.DMA((len(late), 2)),
                  pltpu.SemaphoreType.DMA((len(big),))])
    y, k_new, v_new, conv_new, *w_bf16 = pl.pallas_call(
        _sample_kernel, grid=(n_steps,), in_specs=in_specs, out_specs=out_specs,
        out_shape=out_shape, scratch_shapes=scratch, name="sample_layer",
        compiler_params=pltpu.CompilerParams(dimension_semantics=("arbitrary",),
                                             vmem_limit_bytes=SAMPLE_VMEM_LIMIT),
    )(sinks_p, *flat, ck, cv, *small, *big)
    return y, k_new, v_new, conv_new, dict(zip(MATMUL_WEIGHTS, w_bf16))


def kernel(x_prompt, x_sample, p_prompt, p_sample, cache_k, cache_v, state_conv, g_mix_norm, w_in, g_q, g_k,
           sinks, conv_w, g_attn_out, g_conv_out, w_o, g_mlp_norm, w_up, w_down, g_ple_norm, w_ple_gate, w_ple):
    depth = w_in.shape[0]
    yp, ys = x_prompt, x_sample
    outs = [[] for _ in range(6)]
    for i in range(depth):
        sinks_p, weights = _prepare_weights(
            g_mix_norm[i], w_in[i], g_q[i], g_k[i], sinks[i], conv_w[i], g_attn_out[i], g_conv_out[i],
            w_o[i], g_mlp_norm[i], w_up[i], w_down[i], g_ple_norm[i], w_ple_gate[i], w_ple[i])
        ys, ksn, vsn, csn, w_bf16 = _sample_layer(ys, p_sample[i], cache_k[i], cache_v[i], state_conv[i],
                                                  sinks_p, weights)
        yp, kp, vp, cp = _prompt_layer(yp, p_prompt[i], sinks_p,
                                       [w_bf16.get(n, weights[n]) for n in WEIGHT_NAMES])
        for lst, val in zip(outs, (_cache_from_kernel(kp), _cache_from_kernel(vp), cp,
                                   _cache_from_kernel(ksn), _cache_from_kernel(vsn), csn)):
            lst.append(val)
    return (yp, ys) + tuple(jnp.stack(o) for o in outs)
```

```python
import jax
import jax.numpy as jnp
import numpy as np
from jax import lax
from jax.experimental import pallas as pl
from jax.experimental.pallas import tpu as pltpu

D_MODEL = 1024
HEAD_DIM = 64
N_Q_HEADS = 8
N_KV_HEADS = 2
GQA_GROUP = N_Q_HEADS // N_KV_HEADS
ATTN_WIDTH = N_Q_HEADS * HEAD_DIM
KV_WIDTH = N_KV_HEADS * HEAD_DIM
CONV_WIDTH = D_MODEL - ATTN_WIDTH
CONV_K = 3
WINDOW = 128
BLOCK = 128
ROPE_THETA = 10000.0
D_FF = 4 * D_MODEL
PLE_DIM = 256
EPS = 1e-6
NEG = -1e30
LOG2_E = 1.4426950408889634
PAST_LEN = 16384
IN_WIDTH = ATTN_WIDTH + 2 * KV_WIDTH + 3 * CONV_WIDTH

LANES = 128
SUBLANES = 8
N_QCOL = ATTN_WIDTH // LANES
SEQ_TILE = 512
SUB_TILES = (256, 256)
FF_CHUNK = 1024
SAMPLE_STEP_BATCH = 8
VMEM_LIMIT = 56 * 1024 * 1024
SAMPLE_VMEM_LIMIT = 60 * 1024 * 1024

O_K = ATTN_WIDTH
O_V = O_K + KV_WIDTH
O_B = O_V + KV_WIDTH
O_C = O_B + CONV_WIDTH
O_H = O_C + CONV_WIDTH

BF16 = jnp.bfloat16
F32 = jnp.float32


def _dot(a, b):
    return jnp.dot(a, b, preferred_element_type=F32)


def _dot_t(a, b):
    return lax.dot_general(a, b, (((1,), (1,)), ((), ())), preferred_element_type=F32)


def _rms(x, g):
    return x * lax.rsqrt(jnp.mean(x * x, axis=-1, keepdims=True) + EPS) * g


def _head_norm_rope(t, ones_bd, g, cos, sin):
    ssq = _dot((t * t).astype(BF16), ones_bd)
    t = t * lax.rsqrt(ssq * (1.0 / HEAD_DIM) + EPS)
    lane = lax.broadcasted_iota(jnp.int32, (t.shape[0], LANES), 1)
    first_half = (lane & (HEAD_DIM - 1)) < HEAD_DIM // 2
    cols = []
    for m in range(t.shape[1] // LANES):
        c = t[:, m * LANES:(m + 1) * LANES] * g
        up = pltpu.roll(c, LANES - HEAD_DIM // 2, axis=1)
        dn = pltpu.roll(c, HEAD_DIM // 2, axis=1)
        cols.append(c * cos + jnp.where(first_half, up, dn) * sin)
    return cols


def _rope_rows(cos_t, sin_t):
    cos = jnp.concatenate([cos_t] * (LANES // cos_t.shape[0]), axis=0).T
    sin = jnp.concatenate([sin_t] * (LANES // sin_t.shape[0]), axis=0).T
    return cos, sin


def _front(x, cos, sin, refs):
    h = _rms(x, refs["g_mix"][...]).astype(BF16)
    z = _dot(h, refs["w_in"][...])
    qcols = _head_norm_rope(z[:, :O_K], refs["bd_q"][...], refs["g_q"][...], cos, sin)
    (k,) = _head_norm_rope(z[:, O_K:O_V], refs["bd_k"][...], refs["g_k"][...], cos, sin)
    v = z[:, O_V:O_B]
    b_gate = z[:, O_B:O_C]
    u = z[:, O_C:O_H] * z[:, O_H:]
    return qcols, k, v, b_gate, u


def _tail_steps(x, o_attn, o_conv, p, refs, store):
    mixed = jnp.concatenate([_rms(o_attn, refs["g_attn"][...]), _rms(o_conv, refs["g_conv"][...])], axis=1)
    x = x + _dot(mixed.astype(BF16), refs["w_o"][...])
    hm = _rms(x, refs["g_mlp"][...]).astype(BF16)
    yield
    for c in range(D_FF // FF_CHUNK):
        up = _dot(hm, refs["w_up"][:, c * FF_CHUNK:(c + 1) * FF_CHUNK])
        act = jnp.square(jnp.maximum(up.astype(BF16), 0.0))
        yield
        x = x + _dot(act, refs["w_down"][c * FF_CHUNK:(c + 1) * FF_CHUNK, :])
        yield
    gate = jax.nn.sigmoid(_dot(_rms(x, refs["g_ple"][...]).astype(BF16), refs["w_gate"][...]))
    store(x + gate * _dot(p.astype(BF16), refs["w_ple"][...]))


def _run(steps):
    for _ in steps:
        pass


def _split_heads(col, c):
    group = (2 * c) // GQA_GROUP
    lo = lax.broadcasted_iota(jnp.int32, col.shape, 1) < HEAD_DIM
    swapped = pltpu.roll(col, HEAD_DIM, axis=1)
    if group == 0:
        return jnp.where(lo, col, 0.0), jnp.where(lo, swapped, 0.0)
    return jnp.where(lo, 0.0, swapped), jnp.where(lo, 0.0, col)


def _sink_column(sinks_ref, rows_per_head):
    return jnp.concatenate(
        [jnp.full((rows_per_head, 1), sinks_ref[i] * LOG2_E, F32) for i in range(N_Q_HEADS)], axis=0)


def _merge_heads(pv, rows_per_head, c):
    group = (2 * c) // GQA_GROUP
    lo = lax.broadcasted_iota(jnp.int32, (rows_per_head, LANES), 1) < HEAD_DIM
    a = pv[(2 * c) * rows_per_head:(2 * c + 1) * rows_per_head]
    b = pv[(2 * c + 1) * rows_per_head:(2 * c + 2) * rows_per_head]
    if group == 0:
        return jnp.where(lo, a, pltpu.roll(b, HEAD_DIM, axis=1))
    return jnp.where(lo, pltpu.roll(a, HEAD_DIM, axis=1), b)


WEIGHT_NAMES = ("g_mix", "w_in", "bd_q", "bd_k", "g_q", "g_k", "conv_w", "g_attn", "g_conv", "w_o",
                "g_mlp", "w_up", "w_down", "g_ple", "w_gate", "w_ple")
MATMUL_WEIGHTS = ("w_in", "w_o", "w_up", "w_down", "w_gate", "w_ple")
LATE_WEIGHTS = MATMUL_WEIGHTS[1:]
SMALL_WEIGHTS = tuple(n for n in WEIGHT_NAMES if n not in MATMUL_WEIGHTS)
W_IN_CHUNK = 64
W_IN_SLOTS = 4


def _prompt_kernel(sinks_ref, x_ref, p_ref, cos_ref, sin_ref, *rest):
    nw = len(WEIGHT_NAMES)
    refs = dict(zip(WEIGHT_NAMES, rest[:nw]))
    y_ref, kout_ref, vout_ref, convout_ref, kbuf, vbuf, ubuf = rest[nw:]
    tm = x_ref.shape[0]
    j = pl.program_id(1)

    @pl.when(j == 0)
    def _():
        kbuf[0:BLOCK, :] = jnp.zeros((BLOCK, KV_WIDTH), BF16)
        vbuf[:, 0:BLOCK] = jnp.zeros((KV_WIDTH, BLOCK), BF16)
        ubuf[0:SUBLANES, :] = jnp.zeros((SUBLANES, CONV_WIDTH), F32)

    kj = lax.broadcasted_iota(jnp.int32, (BLOCK, N_Q_HEADS * BLOCK), 0)
    qi = lax.broadcasted_iota(jnp.int32, (BLOCK, N_Q_HEADS * BLOCK), 1) & (BLOCK - 1)
    own = kj <= qi
    sink = jnp.concatenate([jnp.full((1, BLOCK), sinks_ref[i] * LOG2_E, F32) for i in range(N_Q_HEADS)],
                           axis=1)
    cw = refs["conv_w"]

    last = {}

    def sub_tile_steps(r0, n):
        rows = slice(r0, r0 + n)
        x = x_ref[rows, :]
        qcols, k, v, b_gate, u = _front(x, *_rope_rows(cos_ref[:, rows], sin_ref[:, rows]), refs)
        kbuf[BLOCK + r0:BLOCK + r0 + n, :] = k.astype(BF16)
        v_t = v.T
        vbuf[:, BLOCK + r0:BLOCK + r0 + n] = v_t.astype(BF16)
        ubuf[SUBLANES + r0:SUBLANES + r0 + n, :] = u
        last.update(k=k, v_t=v_t, u=u, n=n)
        yield

        o_blocks = []
        for i in range(n // BLOCK):
            g = r0 // BLOCK + i
            qs = []
            for c in range(N_QCOL):
                qs.extend(_split_heads(qcols[c][i * BLOCK:(i + 1) * BLOCK], c))
            qstack = jnp.concatenate(qs, axis=0).astype(BF16)
            s2 = _dot_t(kbuf[g * BLOCK:(g + 2) * BLOCK, :], qstack)
            yield
            s_prev = s2[:BLOCK]
            if g == 0:
                s_prev = jnp.where(j == 0, NEG, s_prev)
            s = jnp.where(own, s2[BLOCK:], s_prev)
            mx = jnp.maximum(jnp.max(s, axis=0, keepdims=True), sink)
            e = jnp.exp2(s - mx)
            denom = jnp.sum(e, axis=0, keepdims=True) + jnp.exp2(sink - mx)
            e2 = jnp.concatenate([jnp.where(own, 0.0, e), jnp.where(own, e, 0.0)], axis=0).astype(BF16)
            yield
            pv = _dot(vbuf[:, g * BLOCK:(g + 2) * BLOCK], e2) * (1.0 / denom)
            cols = []
            for c in range(N_QCOL):
                d0 = (2 * c) // GQA_GROUP * HEAD_DIM
                col_t = jnp.concatenate([pv[d0:d0 + HEAD_DIM, (2 * c) * BLOCK:(2 * c + 1) * BLOCK],
                                         pv[d0:d0 + HEAD_DIM, (2 * c + 1) * BLOCK:(2 * c + 2) * BLOCK]], axis=0)
                cols.append(col_t.T)
            o_blocks.append(jnp.concatenate(cols, axis=1))
            yield
        o_attn = jnp.concatenate(o_blocks, axis=0)

        conv = (cw[0] * ubuf[SUBLANES - 2 + r0:SUBLANES - 2 + r0 + n, :]
                + cw[1] * ubuf[SUBLANES - 1 + r0:SUBLANES - 1 + r0 + n, :]
                + cw[2] * u)

        def store(y):
            y_ref[rows, :] = y

        yield from _tail_steps(x, o_attn, b_gate * conv, p_ref[rows, :], refs, store)

    assert sum(SUB_TILES) == tm
    starts = [sum(SUB_TILES[:t]) for t in range(len(SUB_TILES))]
    tiles = [sub_tile_steps(r0, n) for r0, n in zip(starts, SUB_TILES)]
    begin = [sum(1 + 3 * (n // BLOCK) for n in SUB_TILES[:t]) for t in range(len(SUB_TILES))]
    live = list(range(len(tiles)))
    rnd = 0
    while live:
        for t in reversed([t for t in live if rnd >= begin[t]]):
            if next(tiles[t], True) is True:
                live.remove(t)
        rnd += 1

    n_last = last["n"]
    kout_ref[...] = last["k"][n_last - BLOCK:, :].T
    vout_ref[...] = last["v_t"][:, n_last - BLOCK:]
    convout_ref[...] = last["u"][n_last - (CONV_K - 1):, :]
    kbuf[0:BLOCK, :] = kbuf[tm:tm + BLOCK, :]
    vbuf[:, 0:BLOCK] = vbuf[:, tm:tm + BLOCK]
    ubuf[0:SUBLANES, :] = ubuf[tm:tm + SUBLANES, :]


def _sample_kernel(sinks_ref, x_ref, p_ref, cos_ref, sin_ref, state_ref, ck_ref, cv_ref, *rest):
    ns, nm, nl = len(SMALL_WEIGHTS), len(MATMUL_WEIGHTS), len(LATE_WEIGHTS)
    refs = dict(zip(SMALL_WEIGHTS, rest[:ns]))
    w_f32 = dict(zip(MATMUL_WEIGHTS, rest[ns:ns + nm]))
    y_ref, kout_ref, vout_ref, convout_ref = rest[ns + nm:ns + nm + 4]
    w_out = dict(zip(MATMUL_WEIGHTS, rest[ns + nm + 4:ns + 2 * nm + 4]))
    scratch = rest[ns + 2 * nm + 4:]
    qbuf, kbuf, vbuf, knew, vnew, obuf, bgbuf, ubuf = scratch[:8]
    w_vmem = dict(zip(MATMUL_WEIGHTS, scratch[8:8 + nm]))
    in_stage = scratch[8 + nm]
    late_stage = dict(zip(LATE_WEIGHTS, scratch[9 + nm:9 + nm + nl]))
    in_sem, late_sem, out_sem = scratch[9 + nm + nl:]
    refs.update(w_vmem)
    t_len = x_ref.shape[1]
    n_tok = x_ref.shape[0] * t_len
    step = pl.program_id(0)
    n_steps = pl.num_programs(0)
    step_batch = ck_ref.shape[0]
    pair_rows = 2 * t_len
    chunk_pairs = BLOCK // pair_rows

    n_slots = in_stage.shape[0]

    def in_copy(i):
        return pltpu.make_async_copy(w_f32["w_in"].at[pl.ds(i * W_IN_CHUNK, W_IN_CHUNK), :],
                                     in_stage.at[i % n_slots], in_sem.at[i % n_slots])

    def late_copy(n, name, s):
        rows = late_stage[name].shape[1]
        return pltpu.make_async_copy(w_f32[name].at[pl.ds(s * rows, rows), :], late_stage[name].at[s % 2],
                                     late_sem.at[n, s % 2])

    def out_copy(n, name):
        return pltpu.make_async_copy(w_vmem[name], w_out[name], out_sem.at[n])

    @pl.when(step + 1 < n_steps)
    def _():
        for n, name in enumerate(LATE_WEIGHTS):
            late_copy(n, name, step + 1).start()

    @pl.when(step == 0)
    def _():
        for n, name in enumerate(LATE_WEIGHTS):
            late_copy(n, name, step).start()
        n_chunks = D_MODEL // W_IN_CHUNK
        for i in range(n_slots):
            in_copy(i).start()
        for i in range(n_chunks):
            in_copy(i).wait()
            w_vmem["w_in"][i * W_IN_CHUNK:(i + 1) * W_IN_CHUNK, :] = in_stage[i % n_slots].astype(BF16)
            if i + n_slots < n_chunks:
                in_copy(i + n_slots).start()
        out_copy(0, "w_in").start()
        qcols, k, v, b_gate, u = _front(x_ref[...].reshape(n_tok, D_MODEL),
                                        *_rope_rows(cos_ref[...], sin_ref[...]), refs)
        for c in range(N_QCOL):
            qbuf[2 * c], qbuf[2 * c + 1] = _split_heads(qcols[c], c)
        k_t, v_t = k.T, v.T
        kbuf[...] = k_t.astype(BF16)
        vbuf[...] = v_t.astype(BF16)
        knew[...] = k_t
        vnew[...] = v_t
        bgbuf[...] = b_gate
        ubuf[...] = u
        convout_ref[...] = u.reshape(convout_ref.shape[0], t_len, CONV_WIDTH)[:, t_len - (CONV_K - 1):, :]

    n_rows = N_Q_HEADS * pair_rows
    row = lax.broadcasted_iota(jnp.int32, (n_rows, 3 * BLOCK), 0)
    col = lax.broadcasted_iota(jnp.int32, (n_rows, 3 * BLOCK), 1)
    row_b = (row % pair_rows) // t_len
    row_t = row % t_len
    cache_ok = (col < 2 * BLOCK) & (col // BLOCK == row_b) & ((col % BLOCK) > row_t)
    new_col = col - 2 * BLOCK
    sink = _sink_column(sinks_ref, pair_rows)

    def pair_stages(i):
        pair = step * (step_batch // 2) + i
        r0 = pl.multiple_of(pair * pair_rows, pair_rows)
        c0 = pl.multiple_of((pair // chunk_pairs) * BLOCK, BLOCK)
        q = qbuf[:, pl.ds(r0, pair_rows), :].reshape(n_rows, LANES).astype(BF16)
        ck = [ck_ref[2 * i + bb] for bb in range(2)]
        cv = [cv_ref[2 * i + bb] for bb in range(2)]
        keys = jnp.concatenate([c.astype(BF16) for c in ck] + [kbuf[:, pl.ds(c0, BLOCK)]], axis=1)
        vals = jnp.concatenate([c.astype(BF16) for c in cv] + [vbuf[:, pl.ds(c0, BLOCK)]], axis=1)
        s = _dot(q, keys)
        yield
        new_ok = ((col >= 2 * BLOCK) & (new_col // t_len == (pair % chunk_pairs) * 2 + row_b)
                  & (new_col % t_len <= row_t))
        ok = cache_ok | new_ok
        s = jnp.where(ok, s, NEG)
        mx = jnp.maximum(jnp.max(s, axis=-1, keepdims=True), sink)
        e = jnp.where(ok, jnp.exp2(s - mx), 0.0)
        denom = jnp.sum(e, axis=-1, keepdims=True) + jnp.exp2(sink - mx)
        yield
        pv = _dot_t(e.astype(BF16), vals) * (1.0 / denom)
        for c in range(N_QCOL):
            obuf[pl.ds(r0, pair_rows), c * LANES:(c + 1) * LANES] = _merge_heads(pv, pair_rows, c)
        yield
        k_chunk = knew[:, pl.ds(c0, BLOCK)]
        v_chunk = vnew[:, pl.ds(c0, BLOCK)]
        keep = lax.broadcasted_iota(jnp.int32, (KV_WIDTH, BLOCK), 1) < BLOCK - t_len
        for bb in range(2):
            to_tail = (BLOCK - t_len) - t_len * ((pair % chunk_pairs) * 2 + bb)
            kout_ref[2 * i + bb] = jnp.where(keep, pltpu.roll(ck[bb], BLOCK - t_len, axis=1),
                                             pltpu.roll(k_chunk, to_tail, axis=1))
            vout_ref[2 * i + bb] = jnp.where(keep, pltpu.roll(cv[bb], BLOCK - t_len, axis=1),
                                             pltpu.roll(v_chunk, to_tail, axis=1))

    pairs = [pair_stages(i) for i in range(step_batch // 2)]
    for _ in range(4):
        for stages in pairs:
            next(stages, None)

    for n, name in enumerate(LATE_WEIGHTS):
        rows = late_stage[name].shape[1]
        late_copy(n, name, step).wait()
        r0 = pl.multiple_of(step * rows, rows)
        w_vmem[name][pl.ds(r0, rows), :] = late_stage[name][step % 2].astype(BF16)

    @pl.when(step == n_steps - 1)
    def _():
        for n, name in enumerate(LATE_WEIGHTS):
            out_copy(n + 1, name).start()
        u = ubuf[...]
        tok = lax.broadcasted_iota(jnp.int32, u.shape, 0) % t_len
        per_token = lambda r: jnp.broadcast_to(state_ref[:, r:r + 1, :],
                                               (state_ref.shape[0], t_len, CONV_WIDTH)).reshape(u.shape)
        s0, s1 = per_token(0), per_token(1)
        um1 = jnp.where(tok >= 1, pltpu.roll(u, 1, axis=0), s1)
        um2 = jnp.where(tok >= 2, pltpu.roll(u, 2, axis=0), jnp.where(tok == 0, s0, s1))
        cw = refs["conv_w"]
        conv = cw[0] * um2 + cw[1] * um1 + cw[2] * u

        def store(y):
            y_ref[...] = y.reshape(y_ref.shape)

        _run(_tail_steps(x_ref[...].reshape(n_tok, D_MODEL), obuf[...], bgbuf[...] * conv,
                         p_ref[...].reshape(n_tok, PLE_DIM), refs, store))
        for n, name in enumerate(MATMUL_WEIGHTS):
            out_copy(n, name).wait()


def _rope_tables(pos):
    inv_freq = ROPE_THETA ** (-jnp.arange(0, HEAD_DIM, 2, dtype=F32) / HEAD_DIM)
    ang = inv_freq[:, None] * pos.astype(F32)[None, :]
    sin = jnp.sin(ang)
    return jnp.cos(ang), jnp.concatenate([-sin, sin], axis=0)


def _cache_to_kernel(c):
    batch, keys = c.shape[0], c.shape[1]
    return jnp.transpose(c, (0, 2, 3, 1)).reshape(batch, KV_WIDTH, keys)


def _cache_from_kernel(c):
    batch, _, keys = c.shape
    return jnp.transpose(c.reshape(batch, N_KV_HEADS, HEAD_DIM, keys), (0, 3, 1, 2))


def _block_diag_ones(width):
    idx = np.arange(width) // HEAD_DIM
    return jnp.asarray(idx[:, None] == idx[None, :], dtype=BF16)


def _prepare_weights(g_mix_norm, w_in, g_q, g_k, sinks, conv_w, g_attn_out, g_conv_out, w_o,
                     g_mlp_norm, w_up, w_down, g_ple_norm, w_ple_gate, w_ple):
    row = lambda g: g.reshape(1, -1).astype(F32)
    tile_head = lambda g: jnp.tile(g.reshape(1, HEAD_DIM).astype(F32), (1, LANES // HEAD_DIM))
    weights = dict(
        g_mix=row(g_mix_norm), w_in=w_in,
        bd_q=_block_diag_ones(ATTN_WIDTH), bd_k=_block_diag_ones(KV_WIDTH),
        g_q=tile_head(g_q) * (HEAD_DIM ** -0.5 * LOG2_E), g_k=tile_head(g_k),
        conv_w=conv_w.astype(F32).reshape(CONV_K, 1, CONV_WIDTH),
        g_attn=row(g_attn_out), g_conv=row(g_conv_out), w_o=w_o,
        g_mlp=row(g_mlp_norm), w_up=w_up, w_down=w_down,
        g_ple=row(g_ple_norm), w_gate=w_ple_gate, w_ple=w_ple)
    return sinks.astype(F32), weights


def _resident(a, n_grid):
    zeros = (0,) * a.ndim
    index_map = (lambda b, j: zeros) if n_grid == 2 else (lambda i: zeros)
    return pl.BlockSpec(a.shape, index_map, pipeline_mode=pl.Buffered(1))


def _prompt_layer(x, p, sinks_p, weights):
    batch, seq, _ = x.shape
    tm = SEQ_TILE
    cos, sin = _rope_tables(jnp.arange(seq, dtype=jnp.int32))
    tile = lambda w: pl.BlockSpec((None, tm, w), lambda b, j: (b, j, 0))
    per_batch = lambda r, w: pl.BlockSpec((None, r, w), lambda b, j: (b, 0, 0))
    table = lambda a: pl.BlockSpec((a.shape[0], tm), lambda b, j: (0, j))
    in_specs = ([pl.BlockSpec(memory_space=pltpu.SMEM), tile(D_MODEL), tile(PLE_DIM), table(cos), table(sin)]
                + [_resident(w, 2) for w in weights])
    out_shape = (jax.ShapeDtypeStruct((batch, seq, D_MODEL), F32),
                 jax.ShapeDtypeStruct((batch, BLOCK, KV_WIDTH), F32),
                 jax.ShapeDtypeStruct((batch, BLOCK, KV_WIDTH), F32),
                 jax.ShapeDtypeStruct((batch, CONV_K - 1, CONV_WIDTH), F32))
    out_specs = (tile(D_MODEL), per_batch(BLOCK, KV_WIDTH), per_batch(BLOCK, KV_WIDTH),
                 per_batch(CONV_K - 1, CONV_WIDTH))
    scratch = [pltpu.VMEM((BLOCK + tm, KV_WIDTH), BF16), pltpu.VMEM((KV_WIDTH, BLOCK + tm), BF16),
               pltpu.VMEM((SUBLANES + tm, CONV_WIDTH), F32)]
    return pl.pallas_call(
        _prompt_kernel, grid=(batch, seq // tm), in_specs=in_specs, out_specs=out_specs,
        out_shape=out_shape, scratch_shapes=scratch, name="prompt_layer",
        compiler_params=pltpu.CompilerParams(dimension_semantics=("arbitrary", "arbitrary"),
                                             vmem_limit_bytes=VMEM_LIMIT),
    )(sinks_p, x, p, cos, sin, *weights)


def _sample_layer(x, p, cache_k, cache_v, state_conv, sinks_p, weights):
    batch, t_len, _ = x.shape
    n_tok = batch * t_len
    cos, sin = _rope_tables(PAST_LEN + jnp.arange(n_tok, dtype=jnp.int32) % t_len)
    ck = _cache_to_kernel(cache_k)
    cv = _cache_to_kernel(cache_v)
    flat = [x, p, cos, sin, state_conv]
    n_steps = batch // SAMPLE_STEP_BATCH
    small = [weights[n] for n in SMALL_WEIGHTS]
    big = [weights[n] for n in MATMUL_WEIGHTS]
    hbm = pl.BlockSpec(memory_space=pl.ANY)
    cache_block = pl.BlockSpec((SAMPLE_STEP_BATCH, WINDOW, KV_WIDTH), lambda i: (i, 0, 0))
    in_specs = ([pl.BlockSpec(memory_space=pltpu.SMEM)] + [_resident(a, 1) for a in flat]
                + [cache_block, cache_block] + [_resident(w, 1) for w in small] + [hbm] * len(big))
    out_shape = (jax.ShapeDtypeStruct((batch, t_len, D_MODEL), F32),
                 jax.ShapeDtypeStruct((batch, WINDOW, KV_WIDTH), F32),
                 jax.ShapeDtypeStruct((batch, WINDOW, KV_WIDTH), F32),
                 jax.ShapeDtypeStruct((batch, CONV_K - 1, CONV_WIDTH), F32)
                 ) + tuple(jax.ShapeDtypeStruct(w.shape, BF16) for w in big)
    whole = lambda s: pl.BlockSpec(s.shape, lambda i: (0,) * len(s.shape), pipeline_mode=pl.Buffered(1))
    out_specs = (whole(out_shape[0]), cache_block, cache_block, whole(out_shape[3])) + (hbm,) * len(big)
    late = [weights[n] for n in LATE_WEIGHTS]
    scratch = ([pltpu.VMEM((N_Q_HEADS, n_tok, LANES), F32),
                pltpu.VMEM((KV_WIDTH, n_tok), BF16), pltpu.VMEM((KV_WIDTH, n_tok), BF16),
                pltpu.VMEM((KV_WIDTH, n_tok), F32), pltpu.VMEM((KV_WIDTH, n_tok), F32),
                pltpu.VMEM((n_tok, ATTN_WIDTH), F32), pltpu.VMEM((n_tok, CONV_WIDTH), F32),
                pltpu.VMEM((n_tok, CONV_WIDTH), F32)]
               + [pltpu.VMEM(w.shape, BF16) for w in big]
               + [pltpu.VMEM((W_IN_SLOTS, W_IN_CHUNK, IN_WIDTH), F32)]
               + [pltpu.VMEM((2, w.shape[0] // n_steps, w.shape[1]), F32) for w in late]
               + [pltpu.SemaphoreType.DMA((W_IN_SLOTS,)), pltpu.SemaphoreType.DMA((len(late), 2)),
                  pltpu.SemaphoreType.DMA((len(big),))])
    y, k_new, v_new, conv_new, *w_bf16 = pl.pallas_call(
        _sample_kernel, grid=(n_steps,), in_specs=in_specs, out_specs=out_specs,
        out_shape=out_shape, scratch_shapes=scratch, name="sample_layer",
        compiler_params=pltpu.CompilerParams(dimension_semantics=("arbitrary",),
                                             vmem_limit_bytes=SAMPLE_VMEM_LIMIT),
    )(sinks_p, *flat, ck, cv, *small, *big)
    return y, k_new, v_new, conv_new, dict(zip(MATMUL_WEIGHTS, w_bf16))


def kernel(x_prompt, x_sample, p_prompt, p_sample, cache_k, cache_v, state_conv, g_mix_norm, w_in, g_q, g_k,
           sinks, conv_w, g_attn_out, g_conv_out, w_o, g_mlp_norm, w_up, w_down, g_ple_norm, w_ple_gate, w_ple):
    depth = w_in.shape[0]
    yp, ys = x_prompt, x_sample
    outs = [[] for _ in range(6)]
    for i in range(depth):
        sinks_p, weights = _prepare_weights(
            g_mix_norm[i], w_in[i], g_q[i], g_k[i], sinks[i], conv_w[i], g_attn_out[i], g_conv_out[i],
            w_o[i], g_mlp_norm[i], w_up[i], w_down[i], g_ple_norm[i], w_ple_gate[i], w_ple[i])
        ys, ksn, vsn, csn, w_bf16 = _sample_layer(ys, p_sample[i], cache_k[i], cache_v[i], state_conv[i],
                                                  sinks_p, weights)
        yp, kp, vp, cp = _prompt_layer(yp, p_prompt[i], sinks_p,
                                       [w_bf16.get(n, weights[n]) for n in WEIGHT_NAMES])
        for lst, val in zip(outs, (_cache_from_kernel(kp), _cache_from_kernel(vp), cp,
                                   _cache_from_kernel(ksn), _cache_from_kernel(vsn), csn)):
            lst.append(val)
    return (yp, ys) + tuple(jnp.stack(o) for o in outs)
```

```python
import jax
import jax.numpy as jnp
import numpy as np
from jax import lax
from jax.experimental import pallas as pl
from jax.experimental.pallas import tpu as pltpu

D_MODEL = 1024
HEAD_DIM = 64
N_Q_HEADS = 8
N_KV_HEADS = 2
GQA_GROUP = N_Q_HEADS // N_KV_HEADS
ATTN_WIDTH = N_Q_HEADS * HEAD_DIM
KV_WIDTH = N_KV_HEADS * HEAD_DIM
CONV_WIDTH = D_MODEL - ATTN_WIDTH
CONV_K = 3
WINDOW = 128
BLOCK = 128
ROPE_THETA = 10000.0
D_FF = 4 * D_MODEL
PLE_DIM = 256
EPS = 1e-6
NEG = -1e30
LOG2_E = 1.4426950408889634
PAST_LEN = 16384
IN_WIDTH = ATTN_WIDTH + 2 * KV_WIDTH + 3 * CONV_WIDTH

LANES = 128
SUBLANES = 8
N_QCOL = ATTN_WIDTH // LANES
SEQ_TILE = 512
SUB_TILES = (256, 256)
FF_CHUNK = 1024
SAMPLE_STEP_BATCH = 8
VMEM_LIMIT = 56 * 1024 * 1024
SAMPLE_VMEM_LIMIT = 60 * 1024 * 1024

O_K = ATTN_WIDTH
O_V = O_K + KV_WIDTH
O_B = O_V + KV_WIDTH
O_C = O_B + CONV_WIDTH
O_H = O_C + CONV_WIDTH

BF16 = jnp.bfloat16
F32 = jnp.float32


def _dot(a, b):
    return jnp.dot(a, b, preferred_element_type=F32)


def _dot_t(a, b):
    return lax.dot_general(a, b, (((1,), (1,)), ((), ())), preferred_element_type=F32)


def _rms(x, g):
    return x * lax.rsqrt(jnp.mean(x * x, axis=-1, keepdims=True) + EPS) * g


def _head_norm_rope(t, ones_bd, g, cos, sin):
    ssq = _dot((t * t).astype(BF16), ones_bd)
    t = t * lax.rsqrt(ssq * (1.0 / HEAD_DIM) + EPS)
    lane = lax.broadcasted_iota(jnp.int32, (t.shape[0], LANES), 1)
    first_half = (lane & (HEAD_DIM - 1)) < HEAD_DIM // 2
    cols = []
    for m in range(t.shape[1] // LANES):
        c = t[:, m * LANES:(m + 1) * LANES] * g
        up = pltpu.roll(c, LANES - HEAD_DIM // 2, axis=1)
        dn = pltpu.roll(c, HEAD_DIM // 2, axis=1)
        cols.append(c * cos + jnp.where(first_half, up, dn) * sin)
    return cols


def _rope_rows(table):
    half = HEAD_DIM // 2
    cos = jnp.concatenate([table[:half]] * (LANES // half), axis=0).T
    sin = jnp.concatenate([table[half:]] * (LANES // HEAD_DIM), axis=0).T
    return cos, sin


def _front_steps(x, cos, sin, refs):
    h = _rms(x, refs["g_mix"][...]).astype(BF16)
    z = _dot(h, refs["w_in"][...])
    yield
    qcols = _head_norm_rope(z[:, :O_K], refs["bd_q"][...], refs["g_q"][...], cos, sin)
    (k,) = _head_norm_rope(z[:, O_K:O_V], refs["bd_k"][...], refs["g_k"][...], cos, sin)
    v = z[:, O_V:O_B]
    b_gate = z[:, O_B:O_C]
    u = z[:, O_C:O_H] * z[:, O_H:]
    return qcols, k, v, b_gate, u


def _front(x, cos, sin, refs):
    steps = _front_steps(x, cos, sin, refs)
    try:
        while True:
            next(steps)
    except StopIteration as done:
        return done.value


def _tail_steps(x, o_attn, o_conv, p, refs, store):
    mixed = jnp.concatenate([_rms(o_attn, refs["g_attn"][...]), _rms(o_conv, refs["g_conv"][...])], axis=1)
    x = x + _dot(mixed.astype(BF16), refs["w_o"][...])
    hm = _rms(x, refs["g_mlp"][...]).astype(BF16)
    yield
    for c in range(D_FF // FF_CHUNK):
        up = _dot(hm, refs["w_up"][:, c * FF_CHUNK:(c + 1) * FF_CHUNK])
        act = jnp.square(jnp.maximum(up.astype(BF16), 0.0))
        yield
        x = x + _dot(act, refs["w_down"][c * FF_CHUNK:(c + 1) * FF_CHUNK, :])
        yield
    gate = jax.nn.sigmoid(_dot(_rms(x, refs["g_ple"][...]).astype(BF16), refs["w_gate"][...]))
    store(x + gate * _dot(p.astype(BF16), refs["w_ple"][...]))


def _run(steps):
    for _ in steps:
        pass


def _split_heads(col, c):
    group = (2 * c) // GQA_GROUP
    lo = lax.broadcasted_iota(jnp.int32, col.shape, 1) < HEAD_DIM
    swapped = pltpu.roll(col, HEAD_DIM, axis=1)
    if group == 0:
        return jnp.where(lo, col, 0.0), jnp.where(lo, swapped, 0.0)
    return jnp.where(lo, 0.0, swapped), jnp.where(lo, 0.0, col)


def _sink_column(sinks_ref, rows_per_head):
    return jnp.concatenate(
        [jnp.full((rows_per_head, 1), sinks_ref[i] * LOG2_E, F32) for i in range(N_Q_HEADS)], axis=0)


def _merge_heads(pv, rows_per_head, c):
    group = (2 * c) // GQA_GROUP
    lo = lax.broadcasted_iota(jnp.int32, (rows_per_head, LANES), 1) < HEAD_DIM
    a = pv[(2 * c) * rows_per_head:(2 * c + 1) * rows_per_head]
    b = pv[(2 * c + 1) * rows_per_head:(2 * c + 2) * rows_per_head]
    if group == 0:
        return jnp.where(lo, a, pltpu.roll(b, HEAD_DIM, axis=1))
    return jnp.where(lo, pltpu.roll(a, HEAD_DIM, axis=1), b)


WEIGHT_NAMES = ("g_mix", "w_in", "bd_q", "bd_k", "g_q", "g_k", "conv_w", "g_attn", "g_conv", "w_o",
                "g_mlp", "w_up", "w_down", "g_ple", "w_gate", "w_ple")
MATMUL_WEIGHTS = ("w_in", "w_o", "w_up", "w_down", "w_gate", "w_ple")
LATE_WEIGHTS = MATMUL_WEIGHTS[1:]
SMALL_WEIGHTS = tuple(n for n in WEIGHT_NAMES if n not in MATMUL_WEIGHTS)
W_IN_CHUNK = 64
W_IN_SLOTS = 4


def _prompt_kernel(sinks_ref, x_ref, p_ref, rope_ref, *rest):
    nw = len(WEIGHT_NAMES)
    refs = dict(zip(WEIGHT_NAMES, rest[:nw]))
    y_ref, kout_ref, vout_ref, convout_ref, kbuf, vbuf, ubuf = rest[nw:]
    tm = x_ref.shape[0]
    j = pl.program_id(1)

    @pl.when(j == 0)
    def _():
        kbuf[0:BLOCK, :] = jnp.zeros((BLOCK, KV_WIDTH), BF16)
        vbuf[:, 0:BLOCK] = jnp.zeros((KV_WIDTH, BLOCK), BF16)
        ubuf[0:SUBLANES, :] = jnp.zeros((SUBLANES, CONV_WIDTH), F32)

    kj = lax.broadcasted_iota(jnp.int32, (BLOCK, N_Q_HEADS * BLOCK), 0)
    qi = lax.broadcasted_iota(jnp.int32, (BLOCK, N_Q_HEADS * BLOCK), 1) & (BLOCK - 1)
    own = kj <= qi
    sink = jnp.concatenate([jnp.full((1, BLOCK), sinks_ref[i] * LOG2_E, F32) for i in range(N_Q_HEADS)],
                           axis=1)
    cw = refs["conv_w"]

    last = {}

    def sub_tile_steps(r0, n):
        rows = slice(r0, r0 + n)
        x = x_ref[rows, :]
        qcols, k, v, b_gate, u = yield from _front_steps(
            x, *_rope_rows(rope_ref[:, rows]), refs)
        kbuf[BLOCK + r0:BLOCK + r0 + n, :] = k.astype(BF16)
        v_t = v.T
        vbuf[:, BLOCK + r0:BLOCK + r0 + n] = v_t.astype(BF16)
        ubuf[SUBLANES + r0:SUBLANES + r0 + n, :] = u
        last.update(k=k, v_t=v_t, u=u, n=n)
        yield

        o_blocks = []
        for i in range(n // BLOCK):
            g = r0 // BLOCK + i
            qs = []
            for c in range(N_QCOL):
                qs.extend(_split_heads(qcols[c][i * BLOCK:(i + 1) * BLOCK], c))
            qstack = jnp.concatenate(qs, axis=0).astype(BF16)
            s2 = _dot_t(kbuf[g * BLOCK:(g + 2) * BLOCK, :], qstack)
            yield
            s_prev = s2[:BLOCK]
            if g == 0:
                s_prev = jnp.where(j == 0, NEG, s_prev)
            s = jnp.where(own, s2[BLOCK:], s_prev)
            mx = jnp.maximum(jnp.max(s, axis=0, keepdims=True), sink)
            e = jnp.exp2(s - mx)
            denom = jnp.sum(e, axis=0, keepdims=True) + jnp.exp2(sink - mx)
            e2 = jnp.concatenate([jnp.where(own, 0.0, e), jnp.where(own, e, 0.0)], axis=0).astype(BF16)
            yield
            pv = _dot(vbuf[:, g * BLOCK:(g + 2) * BLOCK], e2) * (1.0 / denom)
            cols = []
            for c in range(N_QCOL):
                d0 = (2 * c) // GQA_GROUP * HEAD_DIM
                col_t = jnp.concatenate([pv[d0:d0 + HEAD_DIM, (2 * c) * BLOCK:(2 * c + 1) * BLOCK],
                                         pv[d0:d0 + HEAD_DIM, (2 * c + 1) * BLOCK:(2 * c + 2) * BLOCK]], axis=0)
                cols.append(col_t.T)
            o_blocks.append(jnp.concatenate(cols, axis=1))
            yield
        o_attn = jnp.concatenate(o_blocks, axis=0)

        conv = (cw[0] * ubuf[SUBLANES - 2 + r0:SUBLANES - 2 + r0 + n, :]
                + cw[1] * ubuf[SUBLANES - 1 + r0:SUBLANES - 1 + r0 + n, :]
                + cw[2] * u)

        def store(y):
            y_ref[rows, :] = y

        yield from _tail_steps(x, o_attn, b_gate * conv, p_ref[rows, :], refs, store)

    assert sum(SUB_TILES) == tm
    starts = [sum(SUB_TILES[:t]) for t in range(len(SUB_TILES))]
    tiles = [sub_tile_steps(r0, n) for r0, n in zip(starts, SUB_TILES)]
    begin = [sum(1 + 3 * (n // BLOCK) for n in SUB_TILES[:t]) for t in range(len(SUB_TILES))]
    live = list(range(len(tiles)))
    rnd = 0
    while live:
        for t in reversed([t for t in live if rnd >= begin[t]]):
            if next(tiles[t], True) is True:
                live.remove(t)
        rnd += 1

    n_last = last["n"]
    kout_ref[...] = last["k"][n_last - BLOCK:, :].T
    vout_ref[...] = last["v_t"][:, n_last - BLOCK:]
    convout_ref[...] = last["u"][n_last - (CONV_K - 1):, :]
    kbuf[0:BLOCK, :] = kbuf[tm:tm + BLOCK, :]
    vbuf[:, 0:BLOCK] = vbuf[:, tm:tm + BLOCK]
    ubuf[0:SUBLANES, :] = ubuf[tm:tm + SUBLANES, :]


def _sample_kernel(sinks_ref, x_ref, p_ref, rope_ref, state_ref, ck_ref, cv_ref, *rest):
    ns, nm, nl = len(SMALL_WEIGHTS), len(MATMUL_WEIGHTS), len(LATE_WEIGHTS)
    refs = dict(zip(SMALL_WEIGHTS, rest[:ns]))
    w_f32 = dict(zip(MATMUL_WEIGHTS, rest[ns:ns + nm]))
    y_ref, kout_ref, vout_ref, convout_ref = rest[ns + nm:ns + nm + 4]
    w_out = dict(zip(MATMUL_WEIGHTS, rest[ns + nm + 4:ns + 2 * nm + 4]))
    scratch = rest[ns + 2 * nm + 4:]
    qbuf, kbuf, vbuf, knew, vnew, obuf, bgbuf, ubuf = scratch[:8]
    w_vmem = dict(zip(MATMUL_WEIGHTS, scratch[8:8 + nm]))
    in_stage = scratch[8 + nm]
    late_stage = dict(zip(LATE_WEIGHTS, scratch[9 + nm:9 + nm + nl]))
    in_sem, late_sem, out_sem = scratch[9 + nm + nl:]
    refs.update(w_vmem)
    t_len = x_ref.shape[1]
    n_tok = x_ref.shape[0] * t_len
    step = pl.program_id(0)
    n_steps = pl.num_programs(0)
    step_batch = ck_ref.shape[0]
    pair_rows = 2 * t_len
    chunk_pairs = BLOCK // pair_rows

    n_slots = in_stage.shape[0]

    def in_copy(i):
        return pltpu.make_async_copy(w_f32["w_in"].at[pl.ds(i * W_IN_CHUNK, W_IN_CHUNK), :],
                                     in_stage.at[i % n_slots], in_sem.at[i % n_slots])

    def late_copy(n, name, s):
        rows = late_stage[name].shape[1]
        return pltpu.make_async_copy(w_f32[name].at[pl.ds(s * rows, rows), :], late_stage[name].at[s % 2],
                                     late_sem.at[n, s % 2])

    def out_copy(n, name):
        return pltpu.make_async_copy(w_vmem[name], w_out[name], out_sem.at[n])

    @pl.when(step + 1 < n_steps)
    def _():
        for n, name in enumerate(LATE_WEIGHTS):
            late_copy(n, name, step + 1).start()

    @pl.when(step == 0)
    def _():
        for n, name in enumerate(LATE_WEIGHTS):
            late_copy(n, name, step).start()
        n_chunks = D_MODEL // W_IN_CHUNK
        for i in range(n_slots):
            in_copy(i).start()
        for i in range(n_chunks):
            in_copy(i).wait()
            w_vmem["w_in"][i * W_IN_CHUNK:(i + 1) * W_IN_CHUNK, :] = in_stage[i % n_slots].astype(BF16)
            if i + n_slots < n_chunks:
                in_copy(i + n_slots).start()
        out_copy(0, "w_in").start()
        qcols, k, v, b_gate, u = _front(x_ref[...].reshape(n_tok, D_MODEL),
                                        *_rope_rows(rope_ref[...]), refs)
        for c in range(N_QCOL):
            qbuf[2 * c], qbuf[2 * c + 1] = _split_heads(qcols[c], c)
        k_t, v_t = k.T, v.T
        kbuf[...] = k_t.astype(BF16)
        vbuf[...] = v_t.astype(BF16)
        knew[...] = k_t
        vnew[...] = v_t
        bgbuf[...] = b_gate
        ubuf[...] = u
        convout_ref[...] = u.reshape(convout_ref.shape[0], t_len, CONV_WIDTH)[:, t_len - (CONV_K - 1):, :]

    n_rows = N_Q_HEADS * pair_rows
    row = lax.broadcasted_iota(jnp.int32, (n_rows, 3 * BLOCK), 0)
    col = lax.broadcasted_iota(jnp.int32, (n_rows, 3 * BLOCK), 1)
    row_b = (row % pair_rows) // t_len
    row_t = row % t_len
    cache_ok = (col < 2 * BLOCK) & (col // BLOCK == row_b) & ((col % BLOCK) > row_t)
    new_col = col - 2 * BLOCK
    sink = _sink_column(sinks_ref, pair_rows)

    def pair_stages(i):
        pair = step * (step_batch // 2) + i
        r0 = pl.multiple_of(pair * pair_rows, pair_rows)
        c0 = pl.multiple_of((pair // chunk_pairs) * BLOCK, BLOCK)
        q = qbuf[:, pl.ds(r0, pair_rows), :].reshape(n_rows, LANES).astype(BF16)
        ck = [ck_ref[2 * i + bb] for bb in range(2)]
        cv = [cv_ref[2 * i + bb] for bb in range(2)]
        keys = jnp.concatenate([c.astype(BF16) for c in ck] + [kbuf[:, pl.ds(c0, BLOCK)]], axis=1)
        vals = jnp.concatenate([c.astype(BF16) for c in cv] + [vbuf[:, pl.ds(c0, BLOCK)]], axis=1)
        s = _dot(q, keys)
        yield
        new_ok = ((col >= 2 * BLOCK) & (new_col // t_len == (pair % chunk_pairs) * 2 + row_b)
                  & (new_col % t_len <= row_t))
        ok = cache_ok | new_ok
        s = jnp.where(ok, s, NEG)
        mx = jnp.maximum(jnp.max(s, axis=-1, keepdims=True), sink)
        e = jnp.where(ok, jnp.exp2(s - mx), 0.0)
        denom = jnp.sum(e, axis=-1, keepdims=True) + jnp.exp2(sink - mx)
        yield
        pv = _dot_t(e.astype(BF16), vals) * (1.0 / denom)
        for c in range(N_QCOL):
            obuf[pl.ds(r0, pair_rows), c * LANES:(c + 1) * LANES] = _merge_heads(pv, pair_rows, c)
        yield
        k_chunk = knew[:, pl.ds(c0, BLOCK)]
        v_chunk = vnew[:, pl.ds(c0, BLOCK)]
        keep = lax.broadcasted_iota(jnp.int32, (KV_WIDTH, BLOCK), 1) < BLOCK - t_len
        for bb in range(2):
            to_tail = (BLOCK - t_len) - t_len * ((pair % chunk_pairs) * 2 + bb)
            kout_ref[2 * i + bb] = jnp.where(keep, pltpu.roll(ck[bb], BLOCK - t_len, axis=1),
                                             pltpu.roll(k_chunk, to_tail, axis=1))
            vout_ref[2 * i + bb] = jnp.where(keep, pltpu.roll(cv[bb], BLOCK - t_len, axis=1),
                                             pltpu.roll(v_chunk, to_tail, axis=1))

    pairs = [pair_stages(i) for i in range(step_batch // 2)]
    for _ in range(4):
        for stages in pairs:
            next(stages, None)

    for n, name in enumerate(LATE_WEIGHTS):
        rows = late_stage[name].shape[1]
        late_copy(n, name, step).wait()
        r0 = pl.multiple_of(step * rows, rows)
        w_vmem[name][pl.ds(r0, rows), :] = late_stage[name][step % 2].astype(BF16)

    @pl.when(step == n_steps - 1)
    def _():
        for n, name in enumerate(LATE_WEIGHTS):
            out_copy(n + 1, name).start()
        u = ubuf[...]
        tok = lax.broadcasted_iota(jnp.int32, u.shape, 0) % t_len
        per_token = lambda r: jnp.broadcast_to(state_ref[:, r:r + 1, :],
                                               (state_ref.shape[0], t_len, CONV_WIDTH)).reshape(u.shape)
        s0, s1 = per_token(0), per_token(1)
        um1 = jnp.where(tok >= 1, pltpu.roll(u, 1, axis=0), s1)
        um2 = jnp.where(tok >= 2, pltpu.roll(u, 2, axis=0), jnp.where(tok == 0, s0, s1))
        cw = refs["conv_w"]
        conv = cw[0] * um2 + cw[1] * um1 + cw[2] * u

        def store(y):
            y_ref[...] = y.reshape(y_ref.shape)

        _run(_tail_steps(x_ref[...].reshape(n_tok, D_MODEL), obuf[...], bgbuf[...] * conv,
                         p_ref[...].reshape(n_tok, PLE_DIM), refs, store))
        for n, name in enumerate(MATMUL_WEIGHTS):
            out_copy(n, name).wait()


def _rope_tables(pos):
    inv_freq = ROPE_THETA ** (-jnp.arange(0, HEAD_DIM, 2, dtype=F32) / HEAD_DIM)
    ang = inv_freq[:, None] * pos.astype(F32)[None, :]
    sin = jnp.sin(ang)
    return jnp.concatenate([jnp.cos(ang), -sin, sin], axis=0)


def _cache_to_kernel(c):
    batch, keys = c.shape[0], c.shape[1]
    return jnp.transpose(c, (0, 2, 3, 1)).reshape(batch, KV_WIDTH, keys)


def _cache_from_kernel(c):
    batch, _, keys = c.shape
    return jnp.transpose(c.reshape(batch, N_KV_HEADS, HEAD_DIM, keys), (0, 3, 1, 2))


def _block_diag_ones(width):
    idx = np.arange(width) // HEAD_DIM
    return jnp.asarray(idx[:, None] == idx[None, :], dtype=BF16)


def _prepare_weights(g_mix_norm, w_in, g_q, g_k, sinks, conv_w, g_attn_out, g_conv_out, w_o,
                     g_mlp_norm, w_up, w_down, g_ple_norm, w_ple_gate, w_ple):
    row = lambda g: g.reshape(1, -1).astype(F32)
    tile_head = lambda g: jnp.tile(g.reshape(1, HEAD_DIM).astype(F32), (1, LANES // HEAD_DIM))
    weights = dict(
        g_mix=row(g_mix_norm), w_in=w_in,
        bd_q=_block_diag_ones(ATTN_WIDTH), bd_k=_block_diag_ones(KV_WIDTH),
        g_q=tile_head(g_q) * (HEAD_DIM ** -0.5 * LOG2_E), g_k=tile_head(g_k),
        conv_w=conv_w.astype(F32).reshape(CONV_K, 1, CONV_WIDTH),
        g_attn=row(g_attn_out), g_conv=row(g_conv_out), w_o=w_o,
        g_mlp=row(g_mlp_norm), w_up=w_up, w_down=w_down,
        g_ple=row(g_ple_norm), w_gate=w_ple_gate, w_ple=w_ple)
    return sinks.astype(F32), weights


def _resident(a, n_grid):
    zeros = (0,) * a.ndim
    index_map = (lambda b, j: zeros) if n_grid == 2 else (lambda i: zeros)
    return pl.BlockSpec(a.shape, index_map, pipeline_mode=pl.Buffered(1))


def _prompt_layer(x, p, sinks_p, weights):
    batch, seq, _ = x.shape
    tm = SEQ_TILE
    rope = _rope_tables(jnp.arange(seq, dtype=jnp.int32))
    tile = lambda w: pl.BlockSpec((None, tm, w), lambda b, j: (b, j, 0))
    per_batch = lambda r, w: pl.BlockSpec((None, r, w), lambda b, j: (b, 0, 0))
    table = pl.BlockSpec((rope.shape[0], tm), lambda b, j: (0, j))
    in_specs = ([pl.BlockSpec(memory_space=pltpu.SMEM), tile(D_MODEL), tile(PLE_DIM), table]
                + [_resident(w, 2) for w in weights])
    out_shape = (jax.ShapeDtypeStruct((batch, seq, D_MODEL), F32),
                 jax.ShapeDtypeStruct((batch, BLOCK, KV_WIDTH), F32),
                 jax.ShapeDtypeStruct((batch, BLOCK, KV_WIDTH), F32),
                 jax.ShapeDtypeStruct((batch, CONV_K - 1, CONV_WIDTH), F32))
    out_specs = (tile(D_MODEL), per_batch(BLOCK, KV_WIDTH), per_batch(BLOCK, KV_WIDTH),
                 per_batch(CONV_K - 1, CONV_WIDTH))
    scratch = [pltpu.VMEM((BLOCK + tm, KV_WIDTH), BF16), pltpu.VMEM((KV_WIDTH, BLOCK + tm), BF16),
               pltpu.VMEM((SUBLANES + tm, CONV_WIDTH), F32)]
    return pl.pallas_call(
        _prompt_kernel, grid=(batch, seq // tm), in_specs=in_specs, out_specs=out_specs,
        out_shape=out_shape, scratch_shapes=scratch, name="prompt_layer",
        compiler_params=pltpu.CompilerParams(dimension_semantics=("arbitrary", "arbitrary"),
                                             vmem_limit_bytes=VMEM_LIMIT),
    )(sinks_p, x, p, rope, *weights)


def _sample_layer(x, p, cache_k, cache_v, state_conv, sinks_p, weights):
    batch, t_len, _ = x.shape
    n_tok = batch * t_len
    rope = _rope_tables(PAST_LEN + jnp.arange(n_tok, dtype=jnp.int32) % t_len)
    ck = _cache_to_kernel(cache_k)
    cv = _cache_to_kernel(cache_v)
    flat = [x, p, rope, state_conv]
    n_steps = batch // SAMPLE_STEP_BATCH
    small = [weights[n] for n in SMALL_WEIGHTS]
    big = [weights[n] for n in MATMUL_WEIGHTS]
    hbm = pl.BlockSpec(memory_space=pl.ANY)
    cache_block = pl.BlockSpec((SAMPLE_STEP_BATCH, WINDOW, KV_WIDTH), lambda i: (i, 0, 0))
    in_specs = ([pl.BlockSpec(memory_space=pltpu.SMEM)] + [_resident(a, 1) for a in flat]
                + [cache_block, cache_block] + [_resident(w, 1) for w in small] + [hbm] * len(big))
    out_shape = (jax.ShapeDtypeStruct((batch, t_len, D_MODEL), F32),
                 jax.ShapeDtypeStruct((batch, WINDOW, KV_WIDTH), F32),
                 jax.ShapeDtypeStruct((batch, WINDOW, KV_WIDTH), F32),
                 jax.ShapeDtypeStruct((batch, CONV_K - 1, CONV_WIDTH), F32)
                 ) + tuple(jax.ShapeDtypeStruct(w.shape, BF16) for w in big)
    whole = lambda s: pl.BlockSpec(s.shape, lambda i: (0,) * len(s.shape), pipeline_mode=pl.Buffered(1))
    out_specs = (whole(out_shape[0]), cache_block, cache_block, whole(out_shape[3])) + (hbm,) * len(big)
    late = [weights[n] for n in LATE_WEIGHTS]
    scratch = ([pltpu.VMEM((N_Q_HEADS, n_tok, LANES), F32),
                pltpu.VMEM((KV_WIDTH, n_tok), BF16), pltpu.VMEM((KV_WIDTH, n_tok), BF16),
                pltpu.VMEM((KV_WIDTH, n_tok), F32), pltpu.VMEM((KV_WIDTH, n_tok), F32),
                pltpu.VMEM((n_tok, ATTN_WIDTH), F32), pltpu.VMEM((n_tok, CONV_WIDTH), F32),
                pltpu.VMEM((n_tok, CONV_WIDTH), F32)]
               + [pltpu.VMEM(w.shape, BF16) for w in big]
               + [pltpu.VMEM((W_IN_SLOTS, W_IN_CHUNK, IN_WIDTH), F32)]
               + [pltpu.VMEM((2, w.shape[0] // n_steps, w.shape[1]), F32) for w in late]
               + [pltpu.SemaphoreType.DMA((W_IN_SLOTS,)), pltpu.SemaphoreType.DMA((len(late), 2)),
                  pltpu.SemaphoreType.DMA((len(big),))])
    y, k_new, v_new, conv_new, *w_bf16 = pl.pallas_call(
        _sample_kernel, grid=(n_steps,), in_specs=in_specs, out_specs=out_specs,
        out_shape=out_shape, scratch_shapes=scratch, name="sample_layer",
        compiler_params=pltpu.CompilerParams(dimension_semantics=("arbitrary",),
                                             vmem_limit_bytes=SAMPLE_VMEM_LIMIT),
    )(sinks_p, *flat, ck, cv, *small, *big)
    return y, k_new, v_new, conv_new, dict(zip(MATMUL_WEIGHTS, w_bf16))


def kernel(x_prompt, x_sample, p_prompt, p_sample, cache_k, cache_v, state_conv, g_mix_norm, w_in, g_q, g_k,
           sinks, conv_w, g_attn_out, g_conv_out, w_o, g_mlp_norm, w_up, w_down, g_ple_norm, w_ple_gate, w_ple):
    depth = w_in.shape[0]
    yp, ys = x_prompt, x_sample
    outs = [[] for _ in range(6)]
    for i in range(depth):
        sinks_p, weights = _prepare_weights(
            g_mix_norm[i], w_in[i], g_q[i], g_k[i], sinks[i], conv_w[i], g_attn_out[i], g_conv_out[i],
            w_o[i], g_mlp_norm[i], w_up[i], w_down[i], g_ple_norm[i], w_ple_gate[i], w_ple[i])
        ys, ksn, vsn, csn, w_bf16 = _sample_layer(ys, p_sample[i], cache_k[i], cache_v[i], state_conv[i],
                                                  sinks_p, weights)
        yp, kp, vp, cp = _prompt_layer(yp, p_prompt[i], sinks_p,
                                       [w_bf16.get(n, weights[n]) for n in WEIGHT_NAMES])
        for lst, val in zip(outs, (_cache_from_kernel(kp), _cache_from_kernel(vp), cp,
                                   _cache_from_kernel(ksn), _cache_from_kernel(vsn), csn)):
            lst.append(val)
    return (yp, ys) + tuple(jnp.stack(o) for o in outs)
```

```python
import jax
import jax.numpy as jnp
import numpy as np
from jax import lax
from jax.experimental import pallas as pl
from jax.experimental.pallas import tpu as pltpu

D_MODEL = 1024
HEAD_DIM = 64
N_Q_HEADS = 8
N_KV_HEADS = 2
GQA_GROUP = N_Q_HEADS // N_KV_HEADS
ATTN_WIDTH = N_Q_HEADS * HEAD_DIM
KV_WIDTH = N_KV_HEADS * HEAD_DIM
CONV_WIDTH = D_MODEL - ATTN_WIDTH
CONV_K = 3
WINDOW = 128
BLOCK = 128
ROPE_THETA = 10000.0
D_FF = 4 * D_MODEL
PLE_DIM = 256
EPS = 1e-6
NEG = -1e30
LOG2_E = 1.4426950408889634
PAST_LEN = 16384
IN_WIDTH = ATTN_WIDTH + 2 * KV_WIDTH + 3 * CONV_WIDTH

LANES = 128
SUBLANES = 8
N_QCOL = ATTN_WIDTH // LANES
SEQ_TILE = 512
SUB_TILES = (256, 256)
FF_CHUNK = 1024
SAMPLE_STEP_BATCH = 8
VMEM_LIMIT = 56 * 1024 * 1024
SAMPLE_VMEM_LIMIT = 60 * 1024 * 1024

O_K = ATTN_WIDTH
O_V = O_K + KV_WIDTH
O_B = O_V + KV_WIDTH
O_C = O_B + CONV_WIDTH
O_H = O_C + CONV_WIDTH

BF16 = jnp.bfloat16
F32 = jnp.float32


def _dot(a, b):
    return jnp.dot(a, b, preferred_element_type=F32)


def _dot_t(a, b):
    return lax.dot_general(a, b, (((1,), (1,)), ((), ())), preferred_element_type=F32)


def _rms(x, g):
    return x * lax.rsqrt(jnp.mean(x * x, axis=-1, keepdims=True) + EPS) * g


def _head_norm_rope(t, ones_bd, g, cos, sin):
    ssq = _dot((t * t).astype(BF16), ones_bd)
    t = t * lax.rsqrt(ssq * (1.0 / HEAD_DIM) + EPS)
    lane = lax.broadcasted_iota(jnp.int32, (t.shape[0], LANES), 1)
    first_half = (lane & (HEAD_DIM - 1)) < HEAD_DIM // 2
    cols = []
    for m in range(t.shape[1] // LANES):
        c = t[:, m * LANES:(m + 1) * LANES] * g
        up = pltpu.roll(c, LANES - HEAD_DIM // 2, axis=1)
        dn = pltpu.roll(c, HEAD_DIM // 2, axis=1)
        cols.append(c * cos + jnp.where(first_half, up, dn) * sin)
    return cols


def _rope_rows(table):
    half = HEAD_DIM // 2
    cos = jnp.concatenate([table[:half]] * (LANES // half), axis=0).T
    sin = jnp.concatenate([table[half:]] * (LANES // HEAD_DIM), axis=0).T
    return cos, sin


def _front_steps(x, cos, sin, refs):
    h = _rms(x, refs["g_mix"][...]).astype(BF16)
    z = _dot(h, refs["w_in"][...])
    yield
    two_heads = lambda g: jnp.concatenate([g] * (LANES // HEAD_DIM), axis=1)
    g_q = two_heads(refs["g_q"][...]) * (HEAD_DIM ** -0.5 * LOG2_E)
    qcols = _head_norm_rope(z[:, :O_K], refs["bd_q"][...], g_q, cos, sin)
    (k,) = _head_norm_rope(z[:, O_K:O_V], refs["bd_k"][...], two_heads(refs["g_k"][...]), cos, sin)
    v = z[:, O_V:O_B]
    b_gate = z[:, O_B:O_C]
    u = z[:, O_C:O_H] * z[:, O_H:]
    return qcols, k, v, b_gate, u


def _front(x, cos, sin, refs):
    steps = _front_steps(x, cos, sin, refs)
    try:
        while True:
            next(steps)
    except StopIteration as done:
        return done.value


def _tail_steps(x, o_attn, o_conv, p, refs, store):
    mixed = jnp.concatenate([_rms(o_attn, refs["g_attn"][...]), _rms(o_conv, refs["g_conv"][...])], axis=1)
    x = x + _dot(mixed.astype(BF16), refs["w_o"][...])
    hm = _rms(x, refs["g_mlp"][...]).astype(BF16)
    yield
    for c in range(D_FF // FF_CHUNK):
        up = _dot(hm, refs["w_up"][:, c * FF_CHUNK:(c + 1) * FF_CHUNK])
        act = jnp.square(jnp.maximum(up.astype(BF16), 0.0))
        yield
        x = x + _dot(act, refs["w_down"][c * FF_CHUNK:(c + 1) * FF_CHUNK, :])
        yield
    gate = jax.nn.sigmoid(_dot(_rms(x, refs["g_ple"][...]).astype(BF16), refs["w_gate"][...]))
    store(x + gate * _dot(p.astype(BF16), refs["w_ple"][...]))


def _run(steps):
    for _ in steps:
        pass


def _split_heads(col, c):
    group = (2 * c) // GQA_GROUP
    lo = lax.broadcasted_iota(jnp.int32, col.shape, 1) < HEAD_DIM
    swapped = pltpu.roll(col, HEAD_DIM, axis=1)
    if group == 0:
        return jnp.where(lo, col, 0.0), jnp.where(lo, swapped, 0.0)
    return jnp.where(lo, 0.0, swapped), jnp.where(lo, 0.0, col)


def _sink_column(sinks_ref, rows_per_head):
    return jnp.concatenate(
        [jnp.full((rows_per_head, 1), sinks_ref[i] * LOG2_E, F32) for i in range(N_Q_HEADS)], axis=0)


def _merge_heads(pv, rows_per_head, c):
    group = (2 * c) // GQA_GROUP
    lo = lax.broadcasted_iota(jnp.int32, (rows_per_head, LANES), 1) < HEAD_DIM
    a = pv[(2 * c) * rows_per_head:(2 * c + 1) * rows_per_head]
    b = pv[(2 * c + 1) * rows_per_head:(2 * c + 2) * rows_per_head]
    if group == 0:
        return jnp.where(lo, a, pltpu.roll(b, HEAD_DIM, axis=1))
    return jnp.where(lo, pltpu.roll(a, HEAD_DIM, axis=1), b)


WEIGHT_NAMES = ("g_mix", "w_in", "bd_q", "bd_k", "g_q", "g_k", "conv_w", "g_attn", "g_conv", "w_o",
                "g_mlp", "w_up", "w_down", "g_ple", "w_gate", "w_ple")
MATMUL_WEIGHTS = ("w_in", "w_o", "w_up", "w_down", "w_gate", "w_ple")
LATE_WEIGHTS = MATMUL_WEIGHTS[1:]
SMALL_WEIGHTS = tuple(n for n in WEIGHT_NAMES if n not in MATMUL_WEIGHTS)
W_IN_CHUNK = 64
W_IN_SLOTS = 4


def _prompt_kernel(sinks_ref, x_ref, p_ref, rope_ref, *rest):
    nw = len(WEIGHT_NAMES)
    refs = dict(zip(WEIGHT_NAMES, rest[:nw]))
    y_ref, kout_ref, vout_ref, convout_ref, kbuf, vbuf, ubuf = rest[nw:]
    tm = x_ref.shape[0]
    j = pl.program_id(1)

    @pl.when(j == 0)
    def _():
        kbuf[0:BLOCK, :] = jnp.zeros((BLOCK, KV_WIDTH), BF16)
        vbuf[:, 0:BLOCK] = jnp.zeros((KV_WIDTH, BLOCK), BF16)
        ubuf[0:SUBLANES, :] = jnp.zeros((SUBLANES, CONV_WIDTH), F32)

    kj = lax.broadcasted_iota(jnp.int32, (BLOCK, N_Q_HEADS * BLOCK), 0)
    qi = lax.broadcasted_iota(jnp.int32, (BLOCK, N_Q_HEADS * BLOCK), 1) & (BLOCK - 1)
    own = kj <= qi
    sink = jnp.concatenate([jnp.full((1, BLOCK), sinks_ref[i] * LOG2_E, F32) for i in range(N_Q_HEADS)],
                           axis=1)
    cw = refs["conv_w"]

    last = {}

    def sub_tile_steps(r0, n):
        rows = slice(r0, r0 + n)
        x = x_ref[rows, :]
        qcols, k, v, b_gate, u = yield from _front_steps(
            x, *_rope_rows(rope_ref[:, rows]), refs)
        kbuf[BLOCK + r0:BLOCK + r0 + n, :] = k.astype(BF16)
        v_t = v.T
        vbuf[:, BLOCK + r0:BLOCK + r0 + n] = v_t.astype(BF16)
        ubuf[SUBLANES + r0:SUBLANES + r0 + n, :] = u
        last.update(k=k, v_t=v_t, u=u, n=n)
        yield

        o_blocks = []
        for i in range(n // BLOCK):
            g = r0 // BLOCK + i
            qs = []
            for c in range(N_QCOL):
                qs.extend(_split_heads(qcols[c][i * BLOCK:(i + 1) * BLOCK], c))
            qstack = jnp.concatenate(qs, axis=0).astype(BF16)
            s2 = _dot_t(kbuf[g * BLOCK:(g + 2) * BLOCK, :], qstack)
            yield
            s_prev = s2[:BLOCK]
            if g == 0:
                s_prev = jnp.where(j == 0, NEG, s_prev)
            s = jnp.where(own, s2[BLOCK:], s_prev)
            mx = jnp.maximum(jnp.max(s, axis=0, keepdims=True), sink)
            e = jnp.exp2(s - mx)
            denom = jnp.sum(e, axis=0, keepdims=True) + jnp.exp2(sink - mx)
            e2 = jnp.concatenate([jnp.where(own, 0.0, e), jnp.where(own, e, 0.0)], axis=0).astype(BF16)
            yield
            pv = _dot(vbuf[:, g * BLOCK:(g + 2) * BLOCK], e2) * (1.0 / denom)
            cols = []
            for c in range(N_QCOL):
                d0 = (2 * c) // GQA_GROUP * HEAD_DIM
                col_t = jnp.concatenate([pv[d0:d0 + HEAD_DIM, (2 * c) * BLOCK:(2 * c + 1) * BLOCK],
                                         pv[d0:d0 + HEAD_DIM, (2 * c + 1) * BLOCK:(2 * c + 2) * BLOCK]], axis=0)
                cols.append(col_t.T)
            o_blocks.append(jnp.concatenate(cols, axis=1))
            yield
        o_attn = jnp.concatenate(o_blocks, axis=0)

        conv = (cw[0] * ubuf[SUBLANES - 2 + r0:SUBLANES - 2 + r0 + n, :]
                + cw[1] * ubuf[SUBLANES - 1 + r0:SUBLANES - 1 + r0 + n, :]
                + cw[2] * u)

        def store(y):
            y_ref[rows, :] = y

        yield from _tail_steps(x, o_attn, b_gate * conv, p_ref[rows, :], refs, store)

    assert sum(SUB_TILES) == tm
    starts = [sum(SUB_TILES[:t]) for t in range(len(SUB_TILES))]
    tiles = [sub_tile_steps(r0, n) for r0, n in zip(starts, SUB_TILES)]
    begin = [sum(1 + 3 * (n // BLOCK) for n in SUB_TILES[:t]) for t in range(len(SUB_TILES))]
    live = list(range(len(tiles)))
    rnd = 0
    while live:
        for t in reversed([t for t in live if rnd >= begin[t]]):
            if next(tiles[t], True) is True:
                live.remove(t)
        rnd += 1

    n_last = last["n"]
    kout_ref[...] = last["k"][n_last - BLOCK:, :].T
    vout_ref[...] = last["v_t"][:, n_last - BLOCK:]
    convout_ref[...] = last["u"][n_last - (CONV_K - 1):, :]
    kbuf[0:BLOCK, :] = kbuf[tm:tm + BLOCK, :]
    vbuf[:, 0:BLOCK] = vbuf[:, tm:tm + BLOCK]
    ubuf[0:SUBLANES, :] = ubuf[tm:tm + SUBLANES, :]


def _sample_kernel(sinks_ref, x_ref, p_ref, rope_ref, state_ref, ck_ref, cv_ref, *rest):
    ns, nm, nl = len(SMALL_WEIGHTS), len(MATMUL_WEIGHTS), len(LATE_WEIGHTS)
    refs = dict(zip(SMALL_WEIGHTS, rest[:ns]))
    w_f32 = dict(zip(MATMUL_WEIGHTS, rest[ns:ns + nm]))
    y_ref, kout_ref, vout_ref, convout_ref = rest[ns + nm:ns + nm + 4]
    w_out = dict(zip(MATMUL_WEIGHTS, rest[ns + nm + 4:ns + 2 * nm + 4]))
    scratch = rest[ns + 2 * nm + 4:]
    qbuf, kbuf, vbuf, knew, vnew, obuf, bgbuf, ubuf = scratch[:8]
    w_vmem = dict(zip(MATMUL_WEIGHTS, scratch[8:8 + nm]))
    in_stage = scratch[8 + nm]
    late_stage = dict(zip(LATE_WEIGHTS, scratch[9 + nm:9 + nm + nl]))
    in_sem, late_sem, out_sem = scratch[9 + nm + nl:]
    refs.update(w_vmem)
    t_len = x_ref.shape[1]
    n_tok = x_ref.shape[0] * t_len
    step = pl.program_id(0)
    n_steps = pl.num_programs(0)
    step_batch = ck_ref.shape[0]
    pair_rows = 2 * t_len
    chunk_pairs = BLOCK // pair_rows

    n_slots = in_stage.shape[0]

    def in_copy(i):
        return pltpu.make_async_copy(w_f32["w_in"].at[pl.ds(i * W_IN_CHUNK, W_IN_CHUNK), :],
                                     in_stage.at[i % n_slots], in_sem.at[i % n_slots])

    def late_copy(n, name, s):
        rows = late_stage[name].shape[1]
        return pltpu.make_async_copy(w_f32[name].at[pl.ds(s * rows, rows), :], late_stage[name].at[s % 2],
                                     late_sem.at[n, s % 2])

    def out_copy(n, name):
        return pltpu.make_async_copy(w_vmem[name], w_out[name], out_sem.at[n])

    @pl.when(step + 1 < n_steps)
    def _():
        for n, name in enumerate(LATE_WEIGHTS):
            late_copy(n, name, step + 1).start()

    @pl.when(step == 0)
    def _():
        for n, name in enumerate(LATE_WEIGHTS):
            late_copy(n, name, step).start()
        n_chunks = D_MODEL // W_IN_CHUNK
        for i in range(n_slots):
            in_copy(i).start()
        for i in range(n_chunks):
            in_copy(i).wait()
            w_vmem["w_in"][i * W_IN_CHUNK:(i + 1) * W_IN_CHUNK, :] = in_stage[i % n_slots].astype(BF16)
            if i + n_slots < n_chunks:
                in_copy(i + n_slots).start()
        out_copy(0, "w_in").start()
        qcols, k, v, b_gate, u = _front(x_ref[...].reshape(n_tok, D_MODEL),
                                        *_rope_rows(rope_ref[...]), refs)
        for c in range(N_QCOL):
            qbuf[2 * c], qbuf[2 * c + 1] = _split_heads(qcols[c], c)
        k_t, v_t = k.T, v.T
        kbuf[...] = k_t.astype(BF16)
        vbuf[...] = v_t.astype(BF16)
        knew[...] = k_t
        vnew[...] = v_t
        bgbuf[...] = b_gate
        ubuf[...] = u
        convout_ref[...] = u.reshape(convout_ref.shape[0], t_len, CONV_WIDTH)[:, t_len - (CONV_K - 1):, :]

    n_rows = N_Q_HEADS * pair_rows
    row = lax.broadcasted_iota(jnp.int32, (n_rows, 3 * BLOCK), 0)
    col = lax.broadcasted_iota(jnp.int32, (n_rows, 3 * BLOCK), 1)
    row_b = (row % pair_rows) // t_len
    row_t = row % t_len
    cache_ok = (col < 2 * BLOCK) & (col // BLOCK == row_b) & ((col % BLOCK) > row_t)
    new_col = col - 2 * BLOCK
    sink = _sink_column(sinks_ref, pair_rows)

    def pair_stages(i):
        pair = step * (step_batch // 2) + i
        r0 = pl.multiple_of(pair * pair_rows, pair_rows)
        c0 = pl.multiple_of((pair // chunk_pairs) * BLOCK, BLOCK)
        q = qbuf[:, pl.ds(r0, pair_rows), :].reshape(n_rows, LANES).astype(BF16)
        ck = [ck_ref[2 * i + bb] for bb in range(2)]
        cv = [cv_ref[2 * i + bb] for bb in range(2)]
        keys = jnp.concatenate([c.astype(BF16) for c in ck] + [kbuf[:, pl.ds(c0, BLOCK)]], axis=1)
        vals = jnp.concatenate([c.astype(BF16) for c in cv] + [vbuf[:, pl.ds(c0, BLOCK)]], axis=1)
        s = _dot(q, keys)
        yield
        new_ok = ((col >= 2 * BLOCK) & (new_col // t_len == (pair % chunk_pairs) * 2 + row_b)
                  & (new_col % t_len <= row_t))
        ok = cache_ok | new_ok
        s = jnp.where(ok, s, NEG)
        mx = jnp.maximum(jnp.max(s, axis=-1, keepdims=True), sink)
        e = jnp.where(ok, jnp.exp2(s - mx), 0.0)
        denom = jnp.sum(e, axis=-1, keepdims=True) + jnp.exp2(sink - mx)
        yield
        pv = _dot_t(e.astype(BF16), vals) * (1.0 / denom)
        for c in range(N_QCOL):
            obuf[pl.ds(r0, pair_rows), c * LANES:(c + 1) * LANES] = _merge_heads(pv, pair_rows, c)
        yield
        k_chunk = knew[:, pl.ds(c0, BLOCK)]
        v_chunk = vnew[:, pl.ds(c0, BLOCK)]
        keep = lax.broadcasted_iota(jnp.int32, (KV_WIDTH, BLOCK), 1) < BLOCK - t_len
        for bb in range(2):
            to_tail = (BLOCK - t_len) - t_len * ((pair % chunk_pairs) * 2 + bb)
            kout_ref[2 * i + bb] = jnp.where(keep, pltpu.roll(ck[bb], BLOCK - t_len, axis=1),
                                             pltpu.roll(k_chunk, to_tail, axis=1))
            vout_ref[2 * i + bb] = jnp.where(keep, pltpu.roll(cv[bb], BLOCK - t_len, axis=1),
                                             pltpu.roll(v_chunk, to_tail, axis=1))

    pairs = [pair_stages(i) for i in range(step_batch // 2)]
    for _ in range(4):
        for stages in pairs:
            next(stages, None)

    for n, name in enumerate(LATE_WEIGHTS):
        rows = late_stage[name].shape[1]
        late_copy(n, name, step).wait()
        r0 = pl.multiple_of(step * rows, rows)
        w_vmem[name][pl.ds(r0, rows), :] = late_stage[name][step % 2].astype(BF16)

    @pl.when(step == n_steps - 1)
    def _():
        for n, name in enumerate(LATE_WEIGHTS):
            out_copy(n + 1, name).start()
        u = ubuf[...]
        tok = lax.broadcasted_iota(jnp.int32, u.shape, 0) % t_len
        per_token = lambda r: jnp.broadcast_to(state_ref[:, r:r + 1, :],
                                               (state_ref.shape[0], t_len, CONV_WIDTH)).reshape(u.shape)
        s0, s1 = per_token(0), per_token(1)
        um1 = jnp.where(tok >= 1, pltpu.roll(u, 1, axis=0), s1)
        um2 = jnp.where(tok >= 2, pltpu.roll(u, 2, axis=0), jnp.where(tok == 0, s0, s1))
        cw = refs["conv_w"]
        conv = cw[0] * um2 + cw[1] * um1 + cw[2] * u

        def store(y):
            y_ref[...] = y.reshape(y_ref.shape)

        _run(_tail_steps(x_ref[...].reshape(n_tok, D_MODEL), obuf[...], bgbuf[...] * conv,
                         p_ref[...].reshape(n_tok, PLE_DIM), refs, store))
        for n, name in enumerate(MATMUL_WEIGHTS):
            out_copy(n, name).wait()


def _rope_tables(pos):
    inv_freq = ROPE_THETA ** (-jnp.arange(0, HEAD_DIM, 2, dtype=F32) / HEAD_DIM)
    ang = inv_freq[:, None] * pos.astype(F32)[None, :]
    sin = jnp.sin(ang)
    return jnp.concatenate([jnp.cos(ang), -sin, sin], axis=0)


def _cache_to_kernel(c):
    batch, keys = c.shape[0], c.shape[1]
    return jnp.transpose(c, (0, 2, 3, 1)).reshape(batch, KV_WIDTH, keys)


def _cache_from_kernel(c):
    batch, _, keys = c.shape
    return jnp.transpose(c.reshape(batch, N_KV_HEADS, HEAD_DIM, keys), (0, 3, 1, 2))


def _block_diag_ones(width):
    idx = np.arange(width) // HEAD_DIM
    return jnp.asarray(idx[:, None] == idx[None, :], dtype=BF16)


def _prepare_weights(g_mix_norm, w_in, g_q, g_k, sinks, conv_w, g_attn_out, g_conv_out, w_o,
                     g_mlp_norm, w_up, w_down, g_ple_norm, w_ple_gate, w_ple):
    row = lambda g: g.reshape(1, -1).astype(F32)
    weights = dict(
        g_mix=row(g_mix_norm), w_in=w_in,
        bd_q=_block_diag_ones(ATTN_WIDTH), bd_k=_block_diag_ones(KV_WIDTH),
        g_q=row(g_q), g_k=row(g_k),
        conv_w=conv_w.astype(F32).reshape(CONV_K, 1, CONV_WIDTH),
        g_attn=row(g_attn_out), g_conv=row(g_conv_out), w_o=w_o,
        g_mlp=row(g_mlp_norm), w_up=w_up, w_down=w_down,
        g_ple=row(g_ple_norm), w_gate=w_ple_gate, w_ple=w_ple)
    return sinks.astype(F32), weights


def _resident(a, n_grid):
    zeros = (0,) * a.ndim
    index_map = (lambda b, j: zeros) if n_grid == 2 else (lambda i: zeros)
    return pl.BlockSpec(a.shape, index_map, pipeline_mode=pl.Buffered(1))


def _prompt_layer(x, p, sinks_p, weights):
    batch, seq, _ = x.shape
    tm = SEQ_TILE
    rope = _rope_tables(jnp.arange(seq, dtype=jnp.int32))
    tile = lambda w: pl.BlockSpec((None, tm, w), lambda b, j: (b, j, 0))
    per_batch = lambda r, w: pl.BlockSpec((None, r, w), lambda b, j: (b, 0, 0))
    table = pl.BlockSpec((rope.shape[0], tm), lambda b, j: (0, j))
    in_specs = ([pl.BlockSpec(memory_space=pltpu.SMEM), tile(D_MODEL), tile(PLE_DIM), table]
                + [_resident(w, 2) for w in weights])
    out_shape = (jax.ShapeDtypeStruct((batch, seq, D_MODEL), F32),
                 jax.ShapeDtypeStruct((batch, BLOCK, KV_WIDTH), F32),
                 jax.ShapeDtypeStruct((batch, BLOCK, KV_WIDTH), F32),
                 jax.ShapeDtypeStruct((batch, CONV_K - 1, CONV_WIDTH), F32))
    out_specs = (tile(D_MODEL), per_batch(BLOCK, KV_WIDTH), per_batch(BLOCK, KV_WIDTH),
                 per_batch(CONV_K - 1, CONV_WIDTH))
    scratch = [pltpu.VMEM((BLOCK + tm, KV_WIDTH), BF16), pltpu.VMEM((KV_WIDTH, BLOCK + tm), BF16),
               pltpu.VMEM((SUBLANES + tm, CONV_WIDTH), F32)]
    return pl.pallas_call(
        _prompt_kernel, grid=(batch, seq // tm), in_specs=in_specs, out_specs=out_specs,
        out_shape=out_shape, scratch_shapes=scratch, name="prompt_layer",
        compiler_params=pltpu.CompilerParams(dimension_semantics=("arbitrary", "arbitrary"),
                                             vmem_limit_bytes=VMEM_LIMIT),
    )(sinks_p, x, p, rope, *weights)


def _sample_layer(x, p, cache_k, cache_v, state_conv, sinks_p, weights):
    batch, t_len, _ = x.shape
    n_tok = batch * t_len
    rope = _rope_tables(PAST_LEN + jnp.arange(n_tok, dtype=jnp.int32) % t_len)
    ck = _cache_to_kernel(cache_k)
    cv = _cache_to_kernel(cache_v)
    flat = [x, p, rope, state_conv]
    n_steps = batch // SAMPLE_STEP_BATCH
    small = [weights[n] for n in SMALL_WEIGHTS]
    big = [weights[n] for n in MATMUL_WEIGHTS]
    hbm = pl.BlockSpec(memory_space=pl.ANY)
    cache_block = pl.BlockSpec((SAMPLE_STEP_BATCH, WINDOW, KV_WIDTH), lambda i: (i, 0, 0))
    in_specs = ([pl.BlockSpec(memory_space=pltpu.SMEM)] + [_resident(a, 1) for a in flat]
                + [cache_block, cache_block] + [_resident(w, 1) for w in small] + [hbm] * len(big))
    out_shape = (jax.ShapeDtypeStruct((batch, t_len, D_MODEL), F32),
                 jax.ShapeDtypeStruct((batch, WINDOW, KV_WIDTH), F32),
                 jax.ShapeDtypeStruct((batch, WINDOW, KV_WIDTH), F32),
                 jax.ShapeDtypeStruct((batch, CONV_K - 1, CONV_WIDTH), F32)
                 ) + tuple(jax.ShapeDtypeStruct(w.shape, BF16) for w in big)
    whole = lambda s: pl.BlockSpec(s.shape, lambda i: (0,) * len(s.shape), pipeline_mode=pl.Buffered(1))
    out_specs = (whole(out_shape[0]), cache_block, cache_block, whole(out_shape[3])) + (hbm,) * len(big)
    late = [weights[n] for n in LATE_WEIGHTS]
    scratch = ([pltpu.VMEM((N_Q_HEADS, n_tok, LANES), F32),
                pltpu.VMEM((KV_WIDTH, n_tok), BF16), pltpu.VMEM((KV_WIDTH, n_tok), BF16),
                pltpu.VMEM((KV_WIDTH, n_tok), F32), pltpu.VMEM((KV_WIDTH, n_tok), F32),
                pltpu.VMEM((n_tok, ATTN_WIDTH), F32), pltpu.VMEM((n_tok, CONV_WIDTH), F32),
                pltpu.VMEM((n_tok, CONV_WIDTH), F32)]
               + [pltpu.VMEM(w.shape, BF16) for w in big]
               + [pltpu.VMEM((W_IN_SLOTS, W_IN_CHUNK, IN_WIDTH), F32)]
               + [pltpu.VMEM((2, w.shape[0] // n_steps, w.shape[1]), F32) for w in late]
               + [pltpu.SemaphoreType.DMA((W_IN_SLOTS,)), pltpu.SemaphoreType.DMA((len(late), 2)),
                  pltpu.SemaphoreType.DMA((len(big),))])
    y, k_new, v_new, conv_new, *w_bf16 = pl.pallas_call(
        _sample_kernel, grid=(n_steps,), in_specs=in_specs, out_specs=out_specs,
        out_shape=out_shape, scratch_shapes=scratch, name="sample_layer",
        compiler_params=pltpu.CompilerParams(dimension_semantics=("arbitrary",),
                                             vmem_limit_bytes=SAMPLE_VMEM_LIMIT),
    )(sinks_p, *flat, ck, cv, *small, *big)
    return y, k_new, v_new, conv_new, dict(zip(MATMUL_WEIGHTS, w_bf16))


def kernel(x_prompt, x_sample, p_prompt, p_sample, cache_k, cache_v, state_conv, g_mix_norm, w_in, g_q, g_k,
           sinks, conv_w, g_attn_out, g_conv_out, w_o, g_mlp_norm, w_up, w_down, g_ple_norm, w_ple_gate, w_ple):
    depth = w_in.shape[0]
    yp, ys = x_prompt, x_sample
    outs = [[] for _ in range(6)]
    for i in range(depth):
        sinks_p, weights = _prepare_weights(
            g_mix_norm[i], w_in[i], g_q[i], g_k[i], sinks[i], conv_w[i], g_attn_out[i], g_conv_out[i],
            w_o[i], g_mlp_norm[i], w_up[i], w_down[i], g_ple_norm[i], w_ple_gate[i], w_ple[i])
        ys, ksn, vsn, csn, w_bf16 = _sample_layer(ys, p_sample[i], cache_k[i], cache_v[i], state_conv[i],
                                                  sinks_p, weights)
        yp, kp, vp, cp = _prompt_layer(yp, p_prompt[i], sinks_p,
                                       [w_bf16.get(n, weights[n]) for n in WEIGHT_NAMES])
        for lst, val in zip(outs, (_cache_from_kernel(kp), _cache_from_kernel(vp), cp,
                                   _cache_from_kernel(ksn), _cache_from_kernel(vsn), csn)):
            lst.append(val)
    return (yp, ys) + tuple(jnp.stack(o) for o in outs)
```

```python
import jax
import jax.numpy as jnp
import numpy as np
from jax import lax
from jax.experimental import pallas as pl
from jax.experimental.pallas import tpu as pltpu

D_MODEL = 1024
HEAD_DIM = 64
N_Q_HEADS = 8
N_KV_HEADS = 2
GQA_GROUP = N_Q_HEADS // N_KV_HEADS
ATTN_WIDTH = N_Q_HEADS * HEAD_DIM
KV_WIDTH = N_KV_HEADS * HEAD_DIM
CONV_WIDTH = D_MODEL - ATTN_WIDTH
CONV_K = 3
WINDOW = 128
BLOCK = 128
ROPE_THETA = 10000.0
D_FF = 4 * D_MODEL
PLE_DIM = 256
EPS = 1e-6
NEG = -1e30
LOG2_E = 1.4426950408889634
PAST_LEN = 16384
IN_WIDTH = ATTN_WIDTH + 2 * KV_WIDTH + 3 * CONV_WIDTH

LANES = 128
SUBLANES = 8
N_QCOL = ATTN_WIDTH // LANES
SEQ_TILE = 512
SUB_TILES = (256, 256)
FF_CHUNK = 1024
ATTN_STAGES = 3
SAMPLE_STEP_BATCH = 8
VMEM_LIMIT = 56 * 1024 * 1024
SAMPLE_VMEM_LIMIT = 60 * 1024 * 1024

O_K = ATTN_WIDTH
O_V = O_K + KV_WIDTH
O_B = O_V + KV_WIDTH
O_C = O_B + CONV_WIDTH
O_H = O_C + CONV_WIDTH

BF16 = jnp.bfloat16
F32 = jnp.float32


def _dot(a, b):
    return jnp.dot(a, b, preferred_element_type=F32)


def _dot_t(a, b):
    return lax.dot_general(a, b, (((1,), (1,)), ((), ())), preferred_element_type=F32)


def _rms(x, g):
    return x * lax.rsqrt(jnp.mean(x * x, axis=-1, keepdims=True) + EPS) * g


def _head_norm_rope(t, ones_bd, g, cos, sin):
    ssq = _dot((t * t).astype(BF16), ones_bd)
    t = t * lax.rsqrt(ssq * (1.0 / HEAD_DIM) + EPS)
    lane = lax.broadcasted_iota(jnp.int32, (t.shape[0], LANES), 1)
    first_half = (lane & (HEAD_DIM - 1)) < HEAD_DIM // 2
    cols = []
    for m in range(t.shape[1] // LANES):
        c = t[:, m * LANES:(m + 1) * LANES] * g
        up = pltpu.roll(c, LANES - HEAD_DIM // 2, axis=1)
        dn = pltpu.roll(c, HEAD_DIM // 2, axis=1)
        cols.append(c * cos + jnp.where(first_half, up, dn) * sin)
    return cols


def _rope_rows(table):
    half = HEAD_DIM // 2
    cos = jnp.concatenate([table[:half]] * (LANES // half), axis=0).T
    sin = jnp.concatenate([table[half:]] * (LANES // HEAD_DIM), axis=0).T
    return cos, sin


def _front_steps(x, cos, sin, refs):
    h = _rms(x, refs["g_mix"][...]).astype(BF16)
    z = _dot(h, refs["w_in"][...])
    yield
    two_heads = lambda g: jnp.concatenate([g] * (LANES // HEAD_DIM), axis=1)
    g_q = two_heads(refs["g_q"][...]) * (HEAD_DIM ** -0.5 * LOG2_E)
    qcols = _head_norm_rope(z[:, :O_K], refs["bd_q"][...], g_q, cos, sin)
    (k,) = _head_norm_rope(z[:, O_K:O_V], refs["bd_k"][...], two_heads(refs["g_k"][...]), cos, sin)
    v = z[:, O_V:O_B]
    b_gate = z[:, O_B:O_C]
    u = z[:, O_C:O_H] * z[:, O_H:]
    return qcols, k, v, b_gate, u


def _front(x, cos, sin, refs):
    steps = _front_steps(x, cos, sin, refs)
    try:
        while True:
            next(steps)
    except StopIteration as done:
        return done.value


def _tail_steps(x, o_attn, o_conv, p, refs, store):
    mixed = jnp.concatenate([_rms(o_attn, refs["g_attn"][...]), _rms(o_conv, refs["g_conv"][...])], axis=1)
    x = x + _dot(mixed.astype(BF16), refs["w_o"][...])
    hm = _rms(x, refs["g_mlp"][...]).astype(BF16)
    yield
    def hidden(c):
        up = _dot(hm, refs["w_up"][:, c * FF_CHUNK:(c + 1) * FF_CHUNK])
        return jnp.square(jnp.maximum(up.astype(BF16), 0.0))

    n_chunks = D_FF // FF_CHUNK
    act = hidden(0)
    yield
    for c in range(n_chunks):
        nxt = None
        if c + 1 < n_chunks:
            nxt = hidden(c + 1)
            yield
        x = x + _dot(act, refs["w_down"][c * FF_CHUNK:(c + 1) * FF_CHUNK, :])
        act = nxt
        yield
    gate = jax.nn.sigmoid(_dot(_rms(x, refs["g_ple"][...]).astype(BF16), refs["w_gate"][...]))
    store(x + gate * _dot(p.astype(BF16), refs["w_ple"][...]))


def _run(steps):
    for _ in steps:
        pass


def _split_heads(col, c):
    group = (2 * c) // GQA_GROUP
    lo = lax.broadcasted_iota(jnp.int32, col.shape, 1) < HEAD_DIM
    swapped = pltpu.roll(col, HEAD_DIM, axis=1)
    if group == 0:
        return jnp.where(lo, col, 0.0), jnp.where(lo, swapped, 0.0)
    return jnp.where(lo, 0.0, swapped), jnp.where(lo, 0.0, col)


def _sink_column(sinks_ref, rows_per_head):
    return jnp.concatenate(
        [jnp.full((rows_per_head, 1), sinks_ref[i] * LOG2_E, F32) for i in range(N_Q_HEADS)], axis=0)


def _merge_heads(pv, rows_per_head, c):
    group = (2 * c) // GQA_GROUP
    lo = lax.broadcasted_iota(jnp.int32, (rows_per_head, LANES), 1) < HEAD_DIM
    a = pv[(2 * c) * rows_per_head:(2 * c + 1) * rows_per_head]
    b = pv[(2 * c + 1) * rows_per_head:(2 * c + 2) * rows_per_head]
    if group == 0:
        return jnp.where(lo, a, pltpu.roll(b, HEAD_DIM, axis=1))
    return jnp.where(lo, pltpu.roll(a, HEAD_DIM, axis=1), b)


WEIGHT_NAMES = ("g_mix", "w_in", "bd_q", "bd_k", "g_q", "g_k", "conv_w", "g_attn", "g_conv", "w_o",
                "g_mlp", "w_up", "w_down", "g_ple", "w_gate", "w_ple")
MATMUL_WEIGHTS = ("w_in", "w_o", "w_up", "w_down", "w_gate", "w_ple")
LATE_WEIGHTS = MATMUL_WEIGHTS[1:]
SMALL_WEIGHTS = tuple(n for n in WEIGHT_NAMES if n not in MATMUL_WEIGHTS)
W_IN_CHUNK = 64
W_IN_SLOTS = 4


def _prompt_kernel(sinks_ref, x_ref, p_ref, rope_ref, *rest):
    nw = len(WEIGHT_NAMES)
    refs = dict(zip(WEIGHT_NAMES, rest[:nw]))
    y_ref, kout_ref, vout_ref, convout_ref, kbuf, vbuf, ubuf = rest[nw:]
    tm = x_ref.shape[0]
    j = pl.program_id(1)

    @pl.when(j == 0)
    def _():
        kbuf[0:BLOCK, :] = jnp.zeros((BLOCK, KV_WIDTH), BF16)
        vbuf[:, 0:BLOCK] = jnp.zeros((KV_WIDTH, BLOCK), BF16)
        ubuf[0:SUBLANES, :] = jnp.zeros((SUBLANES, CONV_WIDTH), F32)

    kj = lax.broadcasted_iota(jnp.int32, (BLOCK, N_Q_HEADS * BLOCK), 0)
    qi = lax.broadcasted_iota(jnp.int32, (BLOCK, N_Q_HEADS * BLOCK), 1) & (BLOCK - 1)
    own = kj <= qi
    sink = jnp.concatenate([jnp.full((1, BLOCK), sinks_ref[i] * LOG2_E, F32) for i in range(N_Q_HEADS)],
                           axis=1)
    cw = refs["conv_w"]

    last = {}

    def sub_tile_steps(r0, n):
        rows = slice(r0, r0 + n)
        x = x_ref[rows, :]
        qcols, k, v, b_gate, u = yield from _front_steps(
            x, *_rope_rows(rope_ref[:, rows]), refs)
        kbuf[BLOCK + r0:BLOCK + r0 + n, :] = k.astype(BF16)
        v_t = v.T
        vbuf[:, BLOCK + r0:BLOCK + r0 + n] = v_t.astype(BF16)
        ubuf[SUBLANES + r0:SUBLANES + r0 + n, :] = u
        last.update(k=k, v_t=v_t, u=u, n=n)
        yield

        o_blocks = {}

        def block_stages(i):
            g = r0 // BLOCK + i
            qs = []
            for c in range(N_QCOL):
                qs.extend(_split_heads(qcols[c][i * BLOCK:(i + 1) * BLOCK], c))
            qstack = jnp.concatenate(qs, axis=0).astype(BF16)
            s2 = _dot_t(kbuf[g * BLOCK:(g + 2) * BLOCK, :], qstack)
            yield
            s_prev = s2[:BLOCK]
            if g == 0:
                s_prev = jnp.where(j == 0, NEG, s_prev)
            s = jnp.where(own, s2[BLOCK:], s_prev)
            mx = jnp.maximum(jnp.max(s, axis=0, keepdims=True), sink)
            e = jnp.exp2(s - mx)
            denom = jnp.sum(e, axis=0, keepdims=True) + jnp.exp2(sink - mx)
            e2 = jnp.concatenate([jnp.where(own, 0.0, e), jnp.where(own, e, 0.0)], axis=0).astype(BF16)
            yield
            pv = _dot(vbuf[:, g * BLOCK:(g + 2) * BLOCK], e2) * (1.0 / denom)
            cols = []
            for c in range(N_QCOL):
                d0 = (2 * c) // GQA_GROUP * HEAD_DIM
                col_t = jnp.concatenate([pv[d0:d0 + HEAD_DIM, (2 * c) * BLOCK:(2 * c + 1) * BLOCK],
                                         pv[d0:d0 + HEAD_DIM, (2 * c + 1) * BLOCK:(2 * c + 2) * BLOCK]], axis=0)
                cols.append(col_t.T)
            o_blocks[i] = jnp.concatenate(cols, axis=1)

        blocks = [block_stages(i) for i in range(n // BLOCK)]
        for _ in range(ATTN_STAGES):
            for stages in blocks:
                next(stages, None)
            yield
        o_attn = jnp.concatenate([o_blocks[i] for i in range(n // BLOCK)], axis=0)

        conv = (cw[0] * ubuf[SUBLANES - 2 + r0:SUBLANES - 2 + r0 + n, :]
                + cw[1] * ubuf[SUBLANES - 1 + r0:SUBLANES - 1 + r0 + n, :]
                + cw[2] * u)

        def store(y):
            y_ref[rows, :] = y

        yield from _tail_steps(x, o_attn, b_gate * conv, p_ref[rows, :], refs, store)

    assert sum(SUB_TILES) == tm
    starts = [sum(SUB_TILES[:t]) for t in range(len(SUB_TILES))]
    tiles = [sub_tile_steps(r0, n) for r0, n in zip(starts, SUB_TILES)]
    begin = [t * (1 + ATTN_STAGES) for t in range(len(SUB_TILES))]
    live = list(range(len(tiles)))
    rnd = 0
    while live:
        for t in reversed([t for t in live if rnd >= begin[t]]):
            if next(tiles[t], True) is True:
                live.remove(t)
        rnd += 1

    n_last = last["n"]
    kout_ref[...] = last["k"][n_last - BLOCK:, :].T
    vout_ref[...] = last["v_t"][:, n_last - BLOCK:]
    convout_ref[...] = last["u"][n_last - (CONV_K - 1):, :]
    kbuf[0:BLOCK, :] = kbuf[tm:tm + BLOCK, :]
    vbuf[:, 0:BLOCK] = vbuf[:, tm:tm + BLOCK]
    ubuf[0:SUBLANES, :] = ubuf[tm:tm + SUBLANES, :]


def _sample_kernel(sinks_ref, x_ref, p_ref, rope_ref, state_ref, ck_ref, cv_ref, *rest):
    ns, nm, nl = len(SMALL_WEIGHTS), len(MATMUL_WEIGHTS), len(LATE_WEIGHTS)
    refs = dict(zip(SMALL_WEIGHTS, rest[:ns]))
    w_f32 = dict(zip(MATMUL_WEIGHTS, rest[ns:ns + nm]))
    y_ref, kout_ref, vout_ref, convout_ref = rest[ns + nm:ns + nm + 4]
    w_out = dict(zip(MATMUL_WEIGHTS, rest[ns + nm + 4:ns + 2 * nm + 4]))
    scratch = rest[ns + 2 * nm + 4:]
    qbuf, kbuf, vbuf, knew, vnew, obuf, bgbuf, ubuf = scratch[:8]
    w_vmem = dict(zip(MATMUL_WEIGHTS, scratch[8:8 + nm]))
    in_stage = scratch[8 + nm]
    late_stage = dict(zip(LATE_WEIGHTS, scratch[9 + nm:9 + nm + nl]))
    in_sem, late_sem, out_sem = scratch[9 + nm + nl:]
    refs.update(w_vmem)
    t_len = x_ref.shape[1]
    n_tok = x_ref.shape[0] * t_len
    step = pl.program_id(0)
    n_steps = pl.num_programs(0)
    step_batch = ck_ref.shape[0]
    pair_rows = 2 * t_len
    chunk_pairs = BLOCK // pair_rows

    n_slots = in_stage.shape[0]

    def in_copy(i):
        return pltpu.make_async_copy(w_f32["w_in"].at[pl.ds(i * W_IN_CHUNK, W_IN_CHUNK), :],
                                     in_stage.at[i % n_slots], in_sem.at[i % n_slots])

    def late_copy(n, name, s):
        rows = late_stage[name].shape[1]
        return pltpu.make_async_copy(w_f32[name].at[pl.ds(s * rows, rows), :], late_stage[name].at[s % 2],
                                     late_sem.at[n, s % 2])

    def out_copy(n, name):
        return pltpu.make_async_copy(w_vmem[name], w_out[name], out_sem.at[n])

    @pl.when(step + 1 < n_steps)
    def _():
        for n, name in enumerate(LATE_WEIGHTS):
            late_copy(n, name, step + 1).start()

    @pl.when(step == 0)
    def _():
        for n, name in enumerate(LATE_WEIGHTS):
            late_copy(n, name, step).start()
        n_chunks = D_MODEL // W_IN_CHUNK
        for i in range(n_slots):
            in_copy(i).start()
        for i in range(n_chunks):
            in_copy(i).wait()
            w_vmem["w_in"][i * W_IN_CHUNK:(i + 1) * W_IN_CHUNK, :] = in_stage[i % n_slots].astype(BF16)
            if i + n_slots < n_chunks:
                in_copy(i + n_slots).start()
        out_copy(0, "w_in").start()
        qcols, k, v, b_gate, u = _front(x_ref[...].reshape(n_tok, D_MODEL),
                                        *_rope_rows(rope_ref[...]), refs)
        for c in range(N_QCOL):
            qbuf[2 * c], qbuf[2 * c + 1] = _split_heads(qcols[c], c)
        k_t, v_t = k.T, v.T
        kbuf[...] = k_t.astype(BF16)
        vbuf[...] = v_t.astype(BF16)
        knew[...] = k_t
        vnew[...] = v_t
        bgbuf[...] = b_gate
        ubuf[...] = u
        convout_ref[...] = u.reshape(convout_ref.shape[0], t_len, CONV_WIDTH)[:, t_len - (CONV_K - 1):, :]

    n_rows = N_Q_HEADS * pair_rows
    row = lax.broadcasted_iota(jnp.int32, (n_rows, 3 * BLOCK), 0)
    col = lax.broadcasted_iota(jnp.int32, (n_rows, 3 * BLOCK), 1)
    row_b = (row % pair_rows) // t_len
    row_t = row % t_len
    cache_ok = (col < 2 * BLOCK) & (col // BLOCK == row_b) & ((col % BLOCK) > row_t)
    new_col = col - 2 * BLOCK
    sink = _sink_column(sinks_ref, pair_rows)

    def pair_stages(i):
        pair = step * (step_batch // 2) + i
        r0 = pl.multiple_of(pair * pair_rows, pair_rows)
        c0 = pl.multiple_of((pair // chunk_pairs) * BLOCK, BLOCK)
        q = qbuf[:, pl.ds(r0, pair_rows), :].reshape(n_rows, LANES).astype(BF16)
        ck = [ck_ref[2 * i + bb] for bb in range(2)]
        cv = [cv_ref[2 * i + bb] for bb in range(2)]
        keys = jnp.concatenate([c.astype(BF16) for c in ck] + [kbuf[:, pl.ds(c0, BLOCK)]], axis=1)
        vals = jnp.concatenate([c.astype(BF16) for c in cv] + [vbuf[:, pl.ds(c0, BLOCK)]], axis=1)
        s = _dot(q, keys)
        yield
        new_ok = ((col >= 2 * BLOCK) & (new_col // t_len == (pair % chunk_pairs) * 2 + row_b)
                  & (new_col % t_len <= row_t))
        ok = cache_ok | new_ok
        s = jnp.where(ok, s, NEG)
        mx = jnp.maximum(jnp.max(s, axis=-1, keepdims=True), sink)
        e = jnp.where(ok, jnp.exp2(s - mx), 0.0)
        denom = jnp.sum(e, axis=-1, keepdims=True) + jnp.exp2(sink - mx)
        yield
        pv = _dot_t(e.astype(BF16), vals) * (1.0 / denom)
        for c in range(N_QCOL):
            obuf[pl.ds(r0, pair_rows), c * LANES:(c + 1) * LANES] = _merge_heads(pv, pair_rows, c)
        yield
        k_chunk = knew[:, pl.ds(c0, BLOCK)]
        v_chunk = vnew[:, pl.ds(c0, BLOCK)]
        keep = lax.broadcasted_iota(jnp.int32, (KV_WIDTH, BLOCK), 1) < BLOCK - t_len
        for bb in range(2):
            to_tail = (BLOCK - t_len) - t_len * ((pair % chunk_pairs) * 2 + bb)
            kout_ref[2 * i + bb] = jnp.where(keep, pltpu.roll(ck[bb], BLOCK - t_len, axis=1),
                                             pltpu.roll(k_chunk, to_tail, axis=1))
            vout_ref[2 * i + bb] = jnp.where(keep, pltpu.roll(cv[bb], BLOCK - t_len, axis=1),
                                             pltpu.roll(v_chunk, to_tail, axis=1))

    pairs = [pair_stages(i) for i in range(step_batch // 2)]
    for _ in range(4):
        for stages in pairs:
            next(stages, None)

    for n, name in enumerate(LATE_WEIGHTS):
        rows = late_stage[name].shape[1]
        late_copy(n, name, step).wait()
        r0 = pl.multiple_of(step * rows, rows)
        w_vmem[name][pl.ds(r0, rows), :] = late_stage[name][step % 2].astype(BF16)

    @pl.when(step == n_steps - 1)
    def _():
        for n, name in enumerate(LATE_WEIGHTS):
            out_copy(n + 1, name).start()
        u = ubuf[...]
        tok = lax.broadcasted_iota(jnp.int32, u.shape, 0) % t_len
        per_token = lambda r: jnp.broadcast_to(state_ref[:, r:r + 1, :],
                                               (state_ref.shape[0], t_len, CONV_WIDTH)).reshape(u.shape)
        s0, s1 = per_token(0), per_token(1)
        um1 = jnp.where(tok >= 1, pltpu.roll(u, 1, axis=0), s1)
        um2 = jnp.where(tok >= 2, pltpu.roll(u, 2, axis=0), jnp.where(tok == 0, s0, s1))
        cw = refs["conv_w"]
        conv = cw[0] * um2 + cw[1] * um1 + cw[2] * u

        def store(y):
            y_ref[...] = y.reshape(y_ref.shape)

        _run(_tail_steps(x_ref[...].reshape(n_tok, D_MODEL), obuf[...], bgbuf[...] * conv,
                         p_ref[...].reshape(n_tok, PLE_DIM), refs, store))
        for n, name in enumerate(MATMUL_WEIGHTS):
            out_copy(n, name).wait()


def _rope_tables(pos):
    inv_freq = ROPE_THETA ** (-jnp.arange(0, HEAD_DIM, 2, dtype=F32) / HEAD_DIM)
    ang = inv_freq[:, None] * pos.astype(F32)[None, :]
    sin = jnp.sin(ang)
    return jnp.concatenate([jnp.cos(ang), -sin, sin], axis=0)


def _cache_to_kernel(c):
    batch, keys = c.shape[0], c.shape[1]
    return jnp.transpose(c, (0, 2, 3, 1)).reshape(batch, KV_WIDTH, keys)


def _cache_from_kernel(c):
    batch, _, keys = c.shape
    return jnp.transpose(c.reshape(batch, N_KV_HEADS, HEAD_DIM, keys), (0, 3, 1, 2))


def _block_diag_ones(width):
    idx = np.arange(width) // HEAD_DIM
    return jnp.asarray(idx[:, None] == idx[None, :], dtype=BF16)


def _prepare_weights(g_mix_norm, w_in, g_q, g_k, sinks, conv_w, g_attn_out, g_conv_out, w_o,
                     g_mlp_norm, w_up, w_down, g_ple_norm, w_ple_gate, w_ple):
    row = lambda g: g.reshape(1, -1).astype(F32)
    weights = dict(
        g_mix=row(g_mix_norm), w_in=w_in,
        bd_q=_block_diag_ones(ATTN_WIDTH), bd_k=_block_diag_ones(KV_WIDTH),
        g_q=row(g_q), g_k=row(g_k),
        conv_w=conv_w.astype(F32).reshape(CONV_K, 1, CONV_WIDTH),
        g_attn=row(g_attn_out), g_conv=row(g_conv_out), w_o=w_o,
        g_mlp=row(g_mlp_norm), w_up=w_up, w_down=w_down,
        g_ple=row(g_ple_norm), w_gate=w_ple_gate, w_ple=w_ple)
    return sinks.astype(F32), weights


def _resident(a, n_grid):
    zeros = (0,) * a.ndim
    index_map = (lambda b, j: zeros) if n_grid == 2 else (lambda i: zeros)
    return pl.BlockSpec(a.shape, index_map, pipeline_mode=pl.Buffered(1))


def _prompt_layer(x, p, sinks_p, weights):
    batch, seq, _ = x.shape
    tm = SEQ_TILE
    rope = _rope_tables(jnp.arange(seq, dtype=jnp.int32))
    tile = lambda w: pl.BlockSpec((None, tm, w), lambda b, j: (b, j, 0))
    per_batch = lambda r, w: pl.BlockSpec((None, r, w), lambda b, j: (b, 0, 0))
    table = pl.BlockSpec((rope.shape[0], tm), lambda b, j: (0, j))
    in_specs = ([pl.BlockSpec(memory_space=pltpu.SMEM), tile(D_MODEL), tile(PLE_DIM), table]
                + [_resident(w, 2) for w in weights])
    out_shape = (jax.ShapeDtypeStruct((batch, seq, D_MODEL), F32),
                 jax.ShapeDtypeStruct((batch, BLOCK, KV_WIDTH), F32),
                 jax.ShapeDtypeStruct((batch, BLOCK, KV_WIDTH), F32),
                 jax.ShapeDtypeStruct((batch, CONV_K - 1, CONV_WIDTH), F32))
    out_specs = (tile(D_MODEL), per_batch(BLOCK, KV_WIDTH), per_batch(BLOCK, KV_WIDTH),
                 per_batch(CONV_K - 1, CONV_WIDTH))
    scratch = [pltpu.VMEM((BLOCK + tm, KV_WIDTH), BF16), pltpu.VMEM((KV_WIDTH, BLOCK + tm), BF16),
               pltpu.VMEM((SUBLANES + tm, CONV_WIDTH), F32)]
    return pl.pallas_call(
        _prompt_kernel, grid=(batch, seq // tm), in_specs=in_specs, out_specs=out_specs,
        out_shape=out_shape, scratch_shapes=scratch, name="prompt_layer",
        compiler_params=pltpu.CompilerParams(dimension_semantics=("arbitrary", "arbitrary"),
                                             vmem_limit_bytes=VMEM_LIMIT),
    )(sinks_p, x, p, rope, *weights)


def _sample_layer(x, p, cache_k, cache_v, state_conv, sinks_p, weights):
    batch, t_len, _ = x.shape
    n_tok = batch * t_len
    rope = _rope_tables(PAST_LEN + jnp.arange(n_tok, dtype=jnp.int32) % t_len)
    ck = _cache_to_kernel(cache_k)
    cv = _cache_to_kernel(cache_v)
    flat = [x, p, rope, state_conv]
    n_steps = batch // SAMPLE_STEP_BATCH
    small = [weights[n] for n in SMALL_WEIGHTS]
    big = [weights[n] for n in MATMUL_WEIGHTS]
    hbm = pl.BlockSpec(memory_space=pl.ANY)
    cache_block = pl.BlockSpec((SAMPLE_STEP_BATCH, WINDOW, KV_WIDTH), lambda i: (i, 0, 0))
    in_specs = ([pl.BlockSpec(memory_space=pltpu.SMEM)] + [_resident(a, 1) for a in flat]
                + [cache_block, cache_block] + [_resident(w, 1) for w in small] + [hbm] * len(big))
    out_shape = (jax.ShapeDtypeStruct((batch, t_len, D_MODEL), F32),
                 jax.ShapeDtypeStruct((batch, WINDOW, KV_WIDTH), F32),
                 jax.ShapeDtypeStruct((batch, WINDOW, KV_WIDTH), F32),
                 jax.ShapeDtypeStruct((batch, CONV_K - 1, CONV_WIDTH), F32)
                 ) + tuple(jax.ShapeDtypeStruct(w.shape, BF16) for w in big)
    whole = lambda s: pl.BlockSpec(s.shape, lambda i: (0,) * len(s.shape), pipeline_mode=pl.Buffered(1))
    out_specs = (whole(out_shape[0]), cache_block, cache_block, whole(out_shape[3])) + (hbm,) * len(big)
    late = [weights[n] for n in LATE_WEIGHTS]
    scratch = ([pltpu.VMEM((N_Q_HEADS, n_tok, LANES), F32),
                pltpu.VMEM((KV_WIDTH, n_tok), BF16), pltpu.VMEM((KV_WIDTH, n_tok), BF16),
                pltpu.VMEM((KV_WIDTH, n_tok), F32), pltpu.VMEM((KV_WIDTH, n_tok), F32),
                pltpu.VMEM((n_tok, ATTN_WIDTH), F32), pltpu.VMEM((n_tok, CONV_WIDTH), F32),
                pltpu.VMEM((n_tok, CONV_WIDTH), F32)]
               + [pltpu.VMEM(w.shape, BF16) for w in big]
               + [pltpu.VMEM((W_IN_SLOTS, W_IN_CHUNK, IN_WIDTH), F32)]
               + [pltpu.VMEM((2, w.shape[0] // n_steps, w.shape[1]), F32) for w in late]
               + [pltpu.SemaphoreType.DMA((W_IN_SLOTS,)), pltpu.SemaphoreType.DMA((len(late), 2)),
                  pltpu.SemaphoreType.DMA((len(big),))])
    y, k_new, v_new, conv_new, *w_bf16 = pl.pallas_call(
        _sample_kernel, grid=(n_steps,), in_specs=in_specs, out_specs=out_specs,
        out_shape=out_shape, scratch_shapes=scratch, name="sample_layer",
        compiler_params=pltpu.CompilerParams(dimension_semantics=("arbitrary",),
                                             vmem_limit_bytes=SAMPLE_VMEM_LIMIT),
    )(sinks_p, *flat, ck, cv, *small, *big)
    return y, k_new, v_new, conv_new, dict(zip(MATMUL_WEIGHTS, w_bf16))


def kernel(x_prompt, x_sample, p_prompt, p_sample, cache_k, cache_v, state_conv, g_mix_norm, w_in, g_q, g_k,
           sinks, conv_w, g_attn_out, g_conv_out, w_o, g_mlp_norm, w_up, w_down, g_ple_norm, w_ple_gate, w_ple):
    depth = w_in.shape[0]
    yp, ys = x_prompt, x_sample
    outs = [[] for _ in range(6)]
    for i in range(depth):
        sinks_p, weights = _prepare_weights(
            g_mix_norm[i], w_in[i], g_q[i], g_k[i], sinks[i], conv_w[i], g_attn_out[i], g_conv_out[i],
            w_o[i], g_mlp_norm[i], w_up[i], w_down[i], g_ple_norm[i], w_ple_gate[i], w_ple[i])
        ys, ksn, vsn, csn, w_bf16 = _sample_layer(ys, p_sample[i], cache_k[i], cache_v[i], state_conv[i],
                                                  sinks_p, weights)
        yp, kp, vp, cp = _prompt_layer(yp, p_prompt[i], sinks_p,
                                       [w_bf16.get(n, weights[n]) for n in WEIGHT_NAMES])
        for lst, val in zip(outs, (_cache_from_kernel(kp), _cache_from_kernel(vp), cp,
                                   _cache_from_kernel(ksn), _cache_from_kernel(vsn), csn)):
            lst.append(val)
    return (yp, ys) + tuple(jnp.stack(o) for o in outs)
```

```python
import jax
import jax.numpy as jnp
import numpy as np
from jax import lax
from jax.experimental import pallas as pl
from jax.experimental.pallas import tpu as pltpu

D_MODEL = 1024
HEAD_DIM = 64
N_Q_HEADS = 8
N_KV_HEADS = 2
GQA_GROUP = N_Q_HEADS // N_KV_HEADS
ATTN_WIDTH = N_Q_HEADS * HEAD_DIM
KV_WIDTH = N_KV_HEADS * HEAD_DIM
CONV_WIDTH = D_MODEL - ATTN_WIDTH
CONV_K = 3
WINDOW = 128
BLOCK = 128
ROPE_THETA = 10000.0
D_FF = 4 * D_MODEL
PLE_DIM = 256
EPS = 1e-6
NEG = -1e30
LOG2_E = 1.4426950408889634
PAST_LEN = 16384
IN_WIDTH = ATTN_WIDTH + 2 * KV_WIDTH + 3 * CONV_WIDTH

LANES = 128
SUBLANES = 8
N_QCOL = ATTN_WIDTH // LANES
SEQ_TILE = 512
SUB_TILES = (256, 256)
FF_CHUNK = 1024
MLP_LOOKAHEAD = 2
ATTN_STAGES = 3
SAMPLE_STEP_BATCH = 8
VMEM_LIMIT = 56 * 1024 * 1024
SAMPLE_VMEM_LIMIT = 60 * 1024 * 1024

assert KV_WIDTH == LANES and GQA_GROUP % 2 == 0 and WINDOW == BLOCK

O_K = ATTN_WIDTH
O_V = O_K + KV_WIDTH
O_B = O_V + KV_WIDTH
O_C = O_B + CONV_WIDTH
O_H = O_C + CONV_WIDTH

BF16 = jnp.bfloat16
F32 = jnp.float32


def _dot(a, b):
    return jnp.dot(a, b, preferred_element_type=F32)


def _dot_t(a, b):
    return lax.dot_general(a, b, (((1,), (1,)), ((), ())), preferred_element_type=F32)


def _rms(x, g):
    return x * lax.rsqrt(jnp.mean(x * x, axis=-1, keepdims=True) + EPS) * g


def _head_norm_rope(t, ones_bd, g, cos, sin):
    ssq = _dot((t * t).astype(BF16), ones_bd)
    t = t * lax.rsqrt(ssq * (1.0 / HEAD_DIM) + EPS)
    lane = lax.broadcasted_iota(jnp.int32, (t.shape[0], LANES), 1)
    first_half = (lane & (HEAD_DIM - 1)) < HEAD_DIM // 2
    cols = []
    for m in range(t.shape[1] // LANES):
        c = t[:, m * LANES:(m + 1) * LANES] * g
        up = pltpu.roll(c, LANES - HEAD_DIM // 2, axis=1)
        dn = pltpu.roll(c, HEAD_DIM // 2, axis=1)
        cols.append(c * cos + jnp.where(first_half, up, dn) * sin)
    return cols


def _rope_rows(table):
    half = HEAD_DIM // 2
    cos = jnp.concatenate([table[:half]] * (LANES // half), axis=0).T
    sin = jnp.concatenate([table[half:]] * (LANES // HEAD_DIM), axis=0).T
    return cos, sin


def _front_steps(x, cos, sin, refs):
    h = _rms(x, refs["g_mix"][...]).astype(BF16)
    z = _dot(h, refs["w_in"][...])
    yield
    two_heads = lambda g: jnp.concatenate([g] * (LANES // HEAD_DIM), axis=1)
    g_q = two_heads(refs["g_q"][...]) * (HEAD_DIM ** -0.5 * LOG2_E)
    qcols = _head_norm_rope(z[:, :O_K], refs["bd_q"][...], g_q, cos, sin)
    (k,) = _head_norm_rope(z[:, O_K:O_V], refs["bd_k"][...], two_heads(refs["g_k"][...]), cos, sin)
    v = z[:, O_V:O_B]
    b_gate = z[:, O_B:O_C]
    u = z[:, O_C:O_H] * z[:, O_H:]
    return qcols, k, v, b_gate, u


def _front(x, cos, sin, refs):
    steps = _front_steps(x, cos, sin, refs)
    try:
        while True:
            next(steps)
    except StopIteration as done:
        return done.value


def _tail_steps(x, o_attn, o_conv, p, refs, store, lookahead=1):
    emb = _dot(p.astype(BF16), refs["w_ple"][...])
    mixed = jnp.concatenate([_rms(o_attn, refs["g_attn"][...]), _rms(o_conv, refs["g_conv"][...])], axis=1)
    x = x + _dot(mixed.astype(BF16), refs["w_o"][...])
    hm = _rms(x, refs["g_mlp"][...]).astype(BF16)
    yield

    def hidden(c):
        up = _dot(hm, refs["w_up"][:, c * FF_CHUNK:(c + 1) * FF_CHUNK])
        return jnp.square(jnp.maximum(up.astype(BF16), 0.0))

    n_chunks = D_FF // FF_CHUNK
    acts = []
    for c in range(min(lookahead, n_chunks)):
        acts.append(hidden(c))
        yield
    for c in range(n_chunks):
        if c + lookahead < n_chunks:
            acts.append(hidden(c + lookahead))
            yield
        x = x + _dot(acts[c], refs["w_down"][c * FF_CHUNK:(c + 1) * FF_CHUNK, :])
        yield
    gate = jax.nn.sigmoid(_dot(_rms(x, refs["g_ple"][...]).astype(BF16), refs["w_gate"][...]))
    store(x + gate * emb)


def _run(steps):
    for _ in steps:
        pass


def _split_heads(col, c):
    group = (2 * c) // GQA_GROUP
    lo = lax.broadcasted_iota(jnp.int32, col.shape, 1) < HEAD_DIM
    swapped = pltpu.roll(col, HEAD_DIM, axis=1)
    if group == 0:
        return jnp.where(lo, col, 0.0), jnp.where(lo, swapped, 0.0)
    return jnp.where(lo, 0.0, swapped), jnp.where(lo, 0.0, col)


def _sink_column(sinks_ref, rows_per_head):
    return jnp.concatenate(
        [jnp.full((rows_per_head, 1), sinks_ref[i] * LOG2_E, F32) for i in range(N_Q_HEADS)], axis=0)


def _merge_heads(pv, rows_per_head, c):
    group = (2 * c) // GQA_GROUP
    lo = lax.broadcasted_iota(jnp.int32, (rows_per_head, LANES), 1) < HEAD_DIM
    a = pv[(2 * c) * rows_per_head:(2 * c + 1) * rows_per_head]
    b = pv[(2 * c + 1) * rows_per_head:(2 * c + 2) * rows_per_head]
    if group == 0:
        return jnp.where(lo, a, pltpu.roll(b, HEAD_DIM, axis=1))
    return jnp.where(lo, pltpu.roll(a, HEAD_DIM, axis=1), b)


WEIGHT_NAMES = ("g_mix", "w_in", "bd_q", "bd_k", "g_q", "g_k", "conv_w", "g_attn", "g_conv", "w_o",
                "g_mlp", "w_up", "w_down", "g_ple", "w_gate", "w_ple")
MATMUL_WEIGHTS = ("w_in", "w_o", "w_up", "w_down", "w_gate", "w_ple")
LATE_WEIGHTS = MATMUL_WEIGHTS[1:]
SMALL_WEIGHTS = tuple(n for n in WEIGHT_NAMES if n not in MATMUL_WEIGHTS)
W_IN_CHUNK = 64
W_IN_SLOTS = 4


def _prompt_kernel(sinks_ref, x_ref, p_ref, rope_ref, *rest):
    nw = len(WEIGHT_NAMES)
    refs = dict(zip(WEIGHT_NAMES, rest[:nw]))
    y_ref, kout_ref, vout_ref, convout_ref, kbuf, vbuf, ubuf = rest[nw:]
    tm = x_ref.shape[0]
    j = pl.program_id(1)

    @pl.when(j == 0)
    def _():
        kbuf[0:BLOCK, :] = jnp.zeros((BLOCK, KV_WIDTH), BF16)
        vbuf[:, 0:BLOCK] = jnp.zeros((KV_WIDTH, BLOCK), BF16)
        ubuf[0:SUBLANES, :] = jnp.zeros((SUBLANES, CONV_WIDTH), F32)

    kj = lax.broadcasted_iota(jnp.int32, (BLOCK, N_Q_HEADS * BLOCK), 0)
    qi = lax.broadcasted_iota(jnp.int32, (BLOCK, N_Q_HEADS * BLOCK), 1) & (BLOCK - 1)
    own = kj <= qi
    sink = jnp.concatenate([jnp.full((1, BLOCK), sinks_ref[i] * LOG2_E, F32) for i in range(N_Q_HEADS)],
                           axis=1)
    cw = refs["conv_w"]

    last = {}

    def sub_tile_steps(r0, n):
        rows = slice(r0, r0 + n)
        x = x_ref[rows, :]
        qcols, k, v, b_gate, u = yield from _front_steps(
            x, *_rope_rows(rope_ref[:, rows]), refs)
        kbuf[BLOCK + r0:BLOCK + r0 + n, :] = k.astype(BF16)
        v_t = v.T
        vbuf[:, BLOCK + r0:BLOCK + r0 + n] = v_t.astype(BF16)
        ubuf[SUBLANES + r0:SUBLANES + r0 + n, :] = u
        last.update(k=k, v_t=v_t, u=u, n=n)
        yield

        o_blocks = {}

        def block_stages(i):
            g = r0 // BLOCK + i
            qs = []
            for c in range(N_QCOL):
                qs.extend(_split_heads(qcols[c][i * BLOCK:(i + 1) * BLOCK], c))
            qstack = jnp.concatenate(qs, axis=0).astype(BF16)
            s2 = _dot_t(kbuf[g * BLOCK:(g + 2) * BLOCK, :], qstack)
            yield
            s_prev = s2[:BLOCK]
            if g == 0:
                s_prev = jnp.where(j == 0, NEG, s_prev)
            s = jnp.where(own, s2[BLOCK:], s_prev)
            mx = jnp.maximum(jnp.max(s, axis=0, keepdims=True), sink)
            e = jnp.exp2(s - mx)
            denom = jnp.sum(e, axis=0, keepdims=True) + jnp.exp2(sink - mx)
            e2 = jnp.concatenate([jnp.where(own, 0.0, e), jnp.where(own, e, 0.0)], axis=0).astype(BF16)
            yield
            pv = _dot(vbuf[:, g * BLOCK:(g + 2) * BLOCK], e2) * (1.0 / denom)
            cols = []
            for c in range(N_QCOL):
                d0 = (2 * c) // GQA_GROUP * HEAD_DIM
                col_t = jnp.concatenate([pv[d0:d0 + HEAD_DIM, (2 * c) * BLOCK:(2 * c + 1) * BLOCK],
                                         pv[d0:d0 + HEAD_DIM, (2 * c + 1) * BLOCK:(2 * c + 2) * BLOCK]], axis=0)
                cols.append(col_t.T)
            o_blocks[i] = jnp.concatenate(cols, axis=1)

        blocks = [block_stages(i) for i in range(n // BLOCK)]
        for _ in range(ATTN_STAGES):
            for stages in blocks:
                next(stages, None)
            yield
        o_attn = jnp.concatenate([o_blocks[i] for i in range(n // BLOCK)], axis=0)

        conv = (cw[0] * ubuf[SUBLANES - 2 + r0:SUBLANES - 2 + r0 + n, :]
                + cw[1] * ubuf[SUBLANES - 1 + r0:SUBLANES - 1 + r0 + n, :]
                + cw[2] * u)

        def store(y):
            y_ref[rows, :] = y

        yield from _tail_steps(x, o_attn, b_gate * conv, p_ref[rows, :], refs, store, MLP_LOOKAHEAD)

    assert sum(SUB_TILES) == tm
    starts = [sum(SUB_TILES[:t]) for t in range(len(SUB_TILES))]
    tiles = [sub_tile_steps(r0, n) for r0, n in zip(starts, SUB_TILES)]
    begin = [t * (1 + ATTN_STAGES) for t in range(len(SUB_TILES))]
    live = list(range(len(tiles)))
    rnd = 0
    while live:
        for t in reversed([t for t in live if rnd >= begin[t]]):
            if next(tiles[t], True) is True:
                live.remove(t)
        rnd += 1

    n_last = last["n"]
    kout_ref[...] = last["k"][n_last - BLOCK:, :].T
    vout_ref[...] = last["v_t"][:, n_last - BLOCK:]
    convout_ref[...] = last["u"][n_last - (CONV_K - 1):, :]
    kbuf[0:BLOCK, :] = kbuf[tm:tm + BLOCK, :]
    vbuf[:, 0:BLOCK] = vbuf[:, tm:tm + BLOCK]
    ubuf[0:SUBLANES, :] = ubuf[tm:tm + SUBLANES, :]


def _sample_kernel(sinks_ref, x_ref, p_ref, rope_ref, state_ref, ck_ref, cv_ref, *rest):
    ns, nm, nl = len(SMALL_WEIGHTS), len(MATMUL_WEIGHTS), len(LATE_WEIGHTS)
    refs = dict(zip(SMALL_WEIGHTS, rest[:ns]))
    w_f32 = dict(zip(MATMUL_WEIGHTS, rest[ns:ns + nm]))
    y_ref, kout_ref, vout_ref, convout_ref = rest[ns + nm:ns + nm + 4]
    w_out = dict(zip(MATMUL_WEIGHTS, rest[ns + nm + 4:ns + 2 * nm + 4]))
    scratch = rest[ns + 2 * nm + 4:]
    qbuf, kbuf, vbuf, knew, vnew, obuf, bgbuf, ubuf = scratch[:8]
    w_vmem = dict(zip(MATMUL_WEIGHTS, scratch[8:8 + nm]))
    in_stage = scratch[8 + nm]
    late_stage = dict(zip(LATE_WEIGHTS, scratch[9 + nm:9 + nm + nl]))
    in_sem, late_sem, out_sem = scratch[9 + nm + nl:]
    refs.update(w_vmem)
    t_len = x_ref.shape[1]
    n_tok = x_ref.shape[0] * t_len
    step = pl.program_id(0)
    n_steps = pl.num_programs(0)
    step_batch = ck_ref.shape[0]
    pair_rows = 2 * t_len
    chunk_pairs = BLOCK // pair_rows

    n_slots = in_stage.shape[0]

    def in_copy(i):
        return pltpu.make_async_copy(w_f32["w_in"].at[pl.ds(i * W_IN_CHUNK, W_IN_CHUNK), :],
                                     in_stage.at[i % n_slots], in_sem.at[i % n_slots])

    def late_copy(n, name, s):
        rows = late_stage[name].shape[1]
        return pltpu.make_async_copy(w_f32[name].at[pl.ds(s * rows, rows), :], late_stage[name].at[s % 2],
                                     late_sem.at[n, s % 2])

    def out_copy(n, name):
        return pltpu.make_async_copy(w_vmem[name], w_out[name], out_sem.at[n])

    @pl.when(step == 0)
    def _():
        n_chunks = D_MODEL // W_IN_CHUNK
        for i in range(n_slots):
            in_copy(i).start()
        for n, name in enumerate(LATE_WEIGHTS):
            late_copy(n, name, step).start()
        for i in range(n_chunks):
            in_copy(i).wait()
            w_vmem["w_in"][i * W_IN_CHUNK:(i + 1) * W_IN_CHUNK, :] = in_stage[i % n_slots].astype(BF16)
            if i + n_slots < n_chunks:
                in_copy(i + n_slots).start()
        out_copy(0, "w_in").start()
        qcols, k, v, b_gate, u = _front(x_ref[...].reshape(n_tok, D_MODEL),
                                        *_rope_rows(rope_ref[...]), refs)
        for c in range(N_QCOL):
            qbuf[2 * c], qbuf[2 * c + 1] = _split_heads(qcols[c], c)
        k_t, v_t = k.T, v.T
        kbuf[...] = k_t.astype(BF16)
        vbuf[...] = v_t.astype(BF16)
        knew[...] = k_t
        vnew[...] = v_t
        bgbuf[...] = b_gate
        ubuf[...] = u
        convout_ref[...] = u.reshape(convout_ref.shape[0], t_len, CONV_WIDTH)[:, t_len - (CONV_K - 1):, :]

    @pl.when(step + 1 < n_steps)
    def _():
        for n, name in enumerate(LATE_WEIGHTS):
            late_copy(n, name, step + 1).start()

    n_rows = N_Q_HEADS * pair_rows
    row = lax.broadcasted_iota(jnp.int32, (n_rows, 3 * BLOCK), 0)
    col = lax.broadcasted_iota(jnp.int32, (n_rows, 3 * BLOCK), 1)
    row_b = (row % pair_rows) // t_len
    row_t = row % t_len
    cache_ok = (col < 2 * BLOCK) & (col // BLOCK == row_b) & ((col % BLOCK) > row_t)
    new_col = col - 2 * BLOCK
    sink = _sink_column(sinks_ref, pair_rows)

    def pair_stages(i):
        pair = step * (step_batch // 2) + i
        r0 = pl.multiple_of(pair * pair_rows, pair_rows)
        c0 = pl.multiple_of((pair // chunk_pairs) * BLOCK, BLOCK)
        q = qbuf[:, pl.ds(r0, pair_rows), :].reshape(n_rows, LANES).astype(BF16)
        ck = [ck_ref[2 * i + bb] for bb in range(2)]
        cv = [cv_ref[2 * i + bb] for bb in range(2)]
        keys = jnp.concatenate([c.astype(BF16) for c in ck] + [kbuf[:, pl.ds(c0, BLOCK)]], axis=1)
        vals = jnp.concatenate([c.astype(BF16) for c in cv] + [vbuf[:, pl.ds(c0, BLOCK)]], axis=1)
        s = _dot(q, keys)
        yield
        new_ok = ((col >= 2 * BLOCK) & (new_col // t_len == (pair % chunk_pairs) * 2 + row_b)
                  & (new_col % t_len <= row_t))
        ok = cache_ok | new_ok
        s = jnp.where(ok, s, NEG)
        mx = jnp.maximum(jnp.max(s, axis=-1, keepdims=True), sink)
        e = jnp.where(ok, jnp.exp2(s - mx), 0.0)
        denom = jnp.sum(e, axis=-1, keepdims=True) + jnp.exp2(sink - mx)
        yield
        pv = _dot_t(e.astype(BF16), vals) * (1.0 / denom)
        for c in range(N_QCOL):
            obuf[pl.ds(r0, pair_rows), c * LANES:(c + 1) * LANES] = _merge_heads(pv, pair_rows, c)
        yield
        k_chunk = knew[:, pl.ds(c0, BLOCK)]
        v_chunk = vnew[:, pl.ds(c0, BLOCK)]
        keep = lax.broadcasted_iota(jnp.int32, (KV_WIDTH, BLOCK), 1) < BLOCK - t_len
        for bb in range(2):
            to_tail = (BLOCK - t_len) - t_len * ((pair % chunk_pairs) * 2 + bb)
            kout_ref[2 * i + bb] = jnp.where(keep, pltpu.roll(ck[bb], BLOCK - t_len, axis=1),
                                             pltpu.roll(k_chunk, to_tail, axis=1))
            vout_ref[2 * i + bb] = jnp.where(keep, pltpu.roll(cv[bb], BLOCK - t_len, axis=1),
                                             pltpu.roll(v_chunk, to_tail, axis=1))

    pairs = [pair_stages(i) for i in range(step_batch // 2)]
    for _ in range(4):
        for stages in pairs:
            next(stages, None)

    for n, name in enumerate(LATE_WEIGHTS):
        rows = late_stage[name].shape[1]
        late_copy(n, name, step).wait()
        r0 = pl.multiple_of(step * rows, rows)
        w_vmem[name][pl.ds(r0, rows), :] = late_stage[name][step % 2].astype(BF16)

    @pl.when(step == n_steps - 1)
    def _():
        for n, name in enumerate(LATE_WEIGHTS):
            out_copy(n + 1, name).start()
        u = ubuf[...]
        tok = lax.broadcasted_iota(jnp.int32, u.shape, 0) % t_len
        per_token = lambda r: jnp.broadcast_to(state_ref[:, r:r + 1, :],
                                               (state_ref.shape[0], t_len, CONV_WIDTH)).reshape(u.shape)
        s0, s1 = per_token(0), per_token(1)
        um1 = jnp.where(tok >= 1, pltpu.roll(u, 1, axis=0), s1)
        um2 = jnp.where(tok >= 2, pltpu.roll(u, 2, axis=0), jnp.where(tok == 0, s0, s1))
        cw = refs["conv_w"]
        conv = cw[0] * um2 + cw[1] * um1 + cw[2] * u

        def store(y):
            y_ref[...] = y.reshape(y_ref.shape)

        _run(_tail_steps(x_ref[...].reshape(n_tok, D_MODEL), obuf[...], bgbuf[...] * conv,
                         p_ref[...].reshape(n_tok, PLE_DIM), refs, store))
        for n, name in enumerate(MATMUL_WEIGHTS):
            out_copy(n, name).wait()


def _rope_tables(pos):
    inv_freq = ROPE_THETA ** (-jnp.arange(0, HEAD_DIM, 2, dtype=F32) / HEAD_DIM)
    ang = inv_freq[:, None] * pos.astype(F32)[None, :]
    sin = jnp.sin(ang)
    return jnp.concatenate([jnp.cos(ang), -sin, sin], axis=0)


def _cache_to_kernel(c):
    batch, keys = c.shape[0], c.shape[1]
    return jnp.transpose(c, (0, 2, 3, 1)).reshape(batch, KV_WIDTH, keys)


def _cache_from_kernel(c):
    batch, _, keys = c.shape
    return jnp.transpose(c.reshape(batch, N_KV_HEADS, HEAD_DIM, keys), (0, 3, 1, 2))


def _block_diag_ones(width):
    idx = np.arange(width) // HEAD_DIM
    return jnp.asarray(idx[:, None] == idx[None, :], dtype=BF16)


def _prepare_weights(g_mix_norm, w_in, g_q, g_k, sinks, conv_w, g_attn_out, g_conv_out, w_o,
                     g_mlp_norm, w_up, w_down, g_ple_norm, w_ple_gate, w_ple):
    row = lambda g: g.reshape(1, -1).astype(F32)
    weights = dict(
        g_mix=row(g_mix_norm), w_in=w_in,
        bd_q=_block_diag_ones(ATTN_WIDTH), bd_k=_block_diag_ones(KV_WIDTH),
        g_q=row(g_q), g_k=row(g_k),
        conv_w=conv_w.astype(F32).reshape(CONV_K, 1, CONV_WIDTH),
        g_attn=row(g_attn_out), g_conv=row(g_conv_out), w_o=w_o,
        g_mlp=row(g_mlp_norm), w_up=w_up, w_down=w_down,
        g_ple=row(g_ple_norm), w_gate=w_ple_gate, w_ple=w_ple)
    return sinks.astype(F32), weights


def _resident(a, n_grid):
    zeros = (0,) * a.ndim
    index_map = (lambda b, j: zeros) if n_grid == 2 else (lambda i: zeros)
    return pl.BlockSpec(a.shape, index_map, pipeline_mode=pl.Buffered(1))


def _prompt_layer(x, p, sinks_p, weights):
    batch, seq, _ = x.shape
    tm = SEQ_TILE
    rope = _rope_tables(jnp.arange(seq, dtype=jnp.int32))
    tile = lambda w: pl.BlockSpec((None, tm, w), lambda b, j: (b, j, 0))
    per_batch = lambda r, w: pl.BlockSpec((None, r, w), lambda b, j: (b, 0, 0))
    table = pl.BlockSpec((rope.shape[0], tm), lambda b, j: (0, j))
    in_specs = ([pl.BlockSpec(memory_space=pltpu.SMEM), tile(D_MODEL), tile(PLE_DIM), table]
                + [_resident(w, 2) for w in weights])
    out_shape = (jax.ShapeDtypeStruct((batch, seq, D_MODEL), F32),
                 jax.ShapeDtypeStruct((batch, BLOCK, KV_WIDTH), F32),
                 jax.ShapeDtypeStruct((batch, BLOCK, KV_WIDTH), F32),
                 jax.ShapeDtypeStruct((batch, CONV_K - 1, CONV_WIDTH), F32))
    out_specs = (tile(D_MODEL), per_batch(BLOCK, KV_WIDTH), per_batch(BLOCK, KV_WIDTH),
                 per_batch(CONV_K - 1, CONV_WIDTH))
    scratch = [pltpu.VMEM((BLOCK + tm, KV_WIDTH), BF16), pltpu.VMEM((KV_WIDTH, BLOCK + tm), BF16),
               pltpu.VMEM((SUBLANES + tm, CONV_WIDTH), F32)]
    return pl.pallas_call(
        _prompt_kernel, grid=(batch, seq // tm), in_specs=in_specs, out_specs=out_specs,
        out_shape=out_shape, scratch_shapes=scratch, name="prompt_layer",
        compiler_params=pltpu.CompilerParams(dimension_semantics=("arbitrary", "arbitrary"),
                                             vmem_limit_bytes=VMEM_LIMIT),
    )(sinks_p, x, p, rope, *weights)


def _sample_layer(x, p, cache_k, cache_v, state_conv, sinks_p, weights):
    batch, t_len, _ = x.shape
    n_tok = batch * t_len
    rope = _rope_tables(PAST_LEN + jnp.arange(n_tok, dtype=jnp.int32) % t_len)
    ck = _cache_to_kernel(cache_k)
    cv = _cache_to_kernel(cache_v)
    flat = [x, p, rope, state_conv]
    n_steps = batch // SAMPLE_STEP_BATCH
    small = [weights[n] for n in SMALL_WEIGHTS]
    big = [weights[n] for n in MATMUL_WEIGHTS]
    hbm = pl.BlockSpec(memory_space=pl.ANY)
    cache_block = pl.BlockSpec((SAMPLE_STEP_BATCH, WINDOW, KV_WIDTH), lambda i: (i, 0, 0))
    in_specs = ([pl.BlockSpec(memory_space=pltpu.SMEM)] + [_resident(a, 1) for a in flat]
                + [cache_block, cache_block] + [_resident(w, 1) for w in small] + [hbm] * len(big))
    out_shape = (jax.ShapeDtypeStruct((batch, t_len, D_MODEL), F32),
                 jax.ShapeDtypeStruct((batch, WINDOW, KV_WIDTH), F32),
                 jax.ShapeDtypeStruct((batch, WINDOW, KV_WIDTH), F32),
                 jax.ShapeDtypeStruct((batch, CONV_K - 1, CONV_WIDTH), F32)
                 ) + tuple(jax.ShapeDtypeStruct(w.shape, BF16) for w in big)
    whole = lambda s: pl.BlockSpec(s.shape, lambda i: (0,) * len(s.shape), pipeline_mode=pl.Buffered(1))
    out_specs = (whole(out_shape[0]), cache_block, cache_block, whole(out_shape[3])) + (hbm,) * len(big)
    late = [weights[n] for n in LATE_WEIGHTS]
    scratch = ([pltpu.VMEM((N_Q_HEADS, n_tok, LANES), F32),
                pltpu.VMEM((KV_WIDTH, n_tok), BF16), pltpu.VMEM((KV_WIDTH, n_tok), BF16),
                pltpu.VMEM((KV_WIDTH, n_tok), F32), pltpu.VMEM((KV_WIDTH, n_tok), F32),
                pltpu.VMEM((n_tok, ATTN_WIDTH), F32), pltpu.VMEM((n_tok, CONV_WIDTH), F32),
                pltpu.VMEM((n_tok, CONV_WIDTH), F32)]
               + [pltpu.VMEM(w.shape, BF16) for w in big]
               + [pltpu.VMEM((W_IN_SLOTS, W_IN_CHUNK, IN_WIDTH), F32)]
               + [pltpu.VMEM((2, w.shape[0] // n_steps, w.shape[1]), F32) for w in late]
               + [pltpu.SemaphoreType.DMA((W_IN_SLOTS,)), pltpu.SemaphoreType.DMA((len(late), 2)),
                  pltpu.SemaphoreType.DMA((len(big),))])
    y, k_new, v_new, conv_new, *w_bf16 = pl.pallas_call(
        _sample_kernel, grid=(n_steps,), in_specs=in_specs, out_specs=out_specs,
        out_shape=out_shape, scratch_shapes=scratch, name="sample_layer",
        compiler_params=pltpu.CompilerParams(dimension_semantics=("arbitrary",),
                                             vmem_limit_bytes=SAMPLE_VMEM_LIMIT),
    )(sinks_p, *flat, ck, cv, *small, *big)
    return y, k_new, v_new, conv_new, dict(zip(MATMUL_WEIGHTS, w_bf16))


def kernel(x_prompt, x_sample, p_prompt, p_sample, cache_k, cache_v, state_conv, g_mix_norm, w_in, g_q, g_k,
           sinks, conv_w, g_attn_out, g_conv_out, w_o, g_mlp_norm, w_up, w_down, g_ple_norm, w_ple_gate, w_ple):
    depth = w_in.shape[0]
    yp, ys = x_prompt, x_sample
    outs = [[] for _ in range(6)]
    for i in range(depth):
        sinks_p, weights = _prepare_weights(
            g_mix_norm[i], w_in[i], g_q[i], g_k[i], sinks[i], conv_w[i], g_attn_out[i], g_conv_out[i],
            w_o[i], g_mlp_norm[i], w_up[i], w_down[i], g_ple_norm[i], w_ple_gate[i], w_ple[i])
        ys, ksn, vsn, csn, w_bf16 = _sample_layer(ys, p_sample[i], cache_k[i], cache_v[i], state_conv[i],
                                                  sinks_p, weights)
        yp, kp, vp, cp = _prompt_layer(yp, p_prompt[i], sinks_p,
                                       [w_bf16.get(n, weights[n]) for n in WEIGHT_NAMES])
        for lst, val in zip(outs, (_cache_from_kernel(kp), _cache_from_kernel(vp), cp,
                                   _cache_from_kernel(ksn), _cache_from_kernel(vsn), csn)):
            lst.append(val)
    return (yp, ys) + tuple(jnp.stack(o) for o in outs)
```

```python
import jax
import jax.numpy as jnp
import numpy as np
from jax import lax
from jax.experimental import pallas as pl
from jax.experimental.pallas import tpu as pltpu

D_MODEL = 1024
HEAD_DIM = 64
N_Q_HEADS = 8
N_KV_HEADS = 2
GQA_GROUP = N_Q_HEADS // N_KV_HEADS
ATTN_WIDTH = N_Q_HEADS * HEAD_DIM
KV_WIDTH = N_KV_HEADS * HEAD_DIM
CONV_WIDTH = D_MODEL - ATTN_WIDTH
CONV_K = 3
WINDOW = 128
BLOCK = 128
ROPE_THETA = 10000.0
D_FF = 4 * D_MODEL
PLE_DIM = 256
EPS = 1e-6
NEG = -1e30
LOG2_E = 1.4426950408889634
PAST_LEN = 16384
IN_WIDTH = ATTN_WIDTH + 2 * KV_WIDTH + 3 * CONV_WIDTH

LANES = 128
SUBLANES = 8
N_QCOL = ATTN_WIDTH // LANES
SEQ_TILE = 512
SUB_TILES = (256, 256)
FF_CHUNK = 1024
MLP_LOOKAHEAD = 2
FRONT_STAGES = 2
ATTN_STAGES = 3
SAMPLE_STEP_BATCH = 8
VMEM_LIMIT = 56 * 1024 * 1024
SAMPLE_VMEM_LIMIT = 60 * 1024 * 1024

O_K = ATTN_WIDTH
O_V = O_K + KV_WIDTH
O_B = O_V + KV_WIDTH
O_C = O_B + CONV_WIDTH
O_H = O_C + CONV_WIDTH

BF16 = jnp.bfloat16
F32 = jnp.float32


def _dot(a, b):
    return jnp.dot(a, b, preferred_element_type=F32)


def _dot_t(a, b):
    return lax.dot_general(a, b, (((1,), (1,)), ((), ())), preferred_element_type=F32)


def _rms(x, g):
    return x * lax.rsqrt(jnp.mean(x * x, axis=-1, keepdims=True) + EPS) * g


def _head_norm_rope(t, ones_bd, g, cos, sin):
    ssq = _dot((t * t).astype(BF16), ones_bd)
    t = t * lax.rsqrt(ssq * (1.0 / HEAD_DIM) + EPS)
    lane = lax.broadcasted_iota(jnp.int32, (t.shape[0], LANES), 1)
    first_half = (lane & (HEAD_DIM - 1)) < HEAD_DIM // 2
    cols = []
    for m in range(t.shape[1] // LANES):
        c = t[:, m * LANES:(m + 1) * LANES] * g
        up = pltpu.roll(c, LANES - HEAD_DIM // 2, axis=1)
        dn = pltpu.roll(c, HEAD_DIM // 2, axis=1)
        cols.append(c * cos + jnp.where(first_half, up, dn) * sin)
    return cols


def _rope_rows(table):
    half = HEAD_DIM // 2
    cos = jnp.concatenate([table[:half]] * (LANES // half), axis=0).T
    sin = jnp.concatenate([table[half:]] * (LANES // HEAD_DIM), axis=0).T
    return cos, sin


def _front_steps(x, cos, sin, refs):
    h = _rms(x, refs["g_mix"][...]).astype(BF16)
    z = _dot(h, refs["w_in"][:, :O_B])
    yield
    zc = _dot(h, refs["w_in"][:, O_B:])
    two_heads = lambda g: jnp.concatenate([g] * (LANES // HEAD_DIM), axis=1)
    g_q = two_heads(refs["g_q"][...]) * (HEAD_DIM ** -0.5 * LOG2_E)
    qcols = _head_norm_rope(z[:, :O_K], refs["bd_q"][...], g_q, cos, sin)
    (k,) = _head_norm_rope(z[:, O_K:O_V], refs["bd_k"][...], two_heads(refs["g_k"][...]), cos, sin)
    v = z[:, O_V:O_B]
    b_gate = zc[:, :CONV_WIDTH]
    u = zc[:, CONV_WIDTH:2 * CONV_WIDTH] * zc[:, 2 * CONV_WIDTH:]
    return qcols, k, v, b_gate, u


def _front(x, cos, sin, refs):
    steps = _front_steps(x, cos, sin, refs)
    try:
        while True:
            next(steps)
    except StopIteration as done:
        return done.value


def _tail_steps(x, o_attn, o_conv, p, refs, store, lookahead=1):
    emb = _dot(p.astype(BF16), refs["w_ple"][...])
    mixed = jnp.concatenate([_rms(o_attn, refs["g_attn"][...]), _rms(o_conv, refs["g_conv"][...])], axis=1)
    x = x + _dot(mixed.astype(BF16), refs["w_o"][...])
    hm = _rms(x, refs["g_mlp"][...]).astype(BF16)
    yield

    def hidden(c):
        up = _dot(hm, refs["w_up"][:, c * FF_CHUNK:(c + 1) * FF_CHUNK])
        return jnp.square(jnp.maximum(up.astype(BF16), 0.0))

    n_chunks = D_FF // FF_CHUNK
    acts = []
    for c in range(min(lookahead, n_chunks)):
        acts.append(hidden(c))
        yield
    for c in range(n_chunks):
        if c + lookahead < n_chunks:
            acts.append(hidden(c + lookahead))
            yield
        x = x + _dot(acts[c], refs["w_down"][c * FF_CHUNK:(c + 1) * FF_CHUNK, :])
        yield
    gate = jax.nn.sigmoid(_dot(_rms(x, refs["g_ple"][...]).astype(BF16), refs["w_gate"][...]))
    store(x + gate * emb)


def _run(steps):
    for _ in steps:
        pass


def _split_heads(col, c):
    group = (2 * c) // GQA_GROUP
    lo = lax.broadcasted_iota(jnp.int32, col.shape, 1) < HEAD_DIM
    swapped = pltpu.roll(col, HEAD_DIM, axis=1)
    if group == 0:
        return jnp.where(lo, col, 0.0), jnp.where(lo, swapped, 0.0)
    return jnp.where(lo, 0.0, swapped), jnp.where(lo, 0.0, col)


def _sink_column(sinks_ref, rows_per_head):
    return jnp.concatenate(
        [jnp.full((rows_per_head, 1), sinks_ref[i] * LOG2_E, F32) for i in range(N_Q_HEADS)], axis=0)


def _merge_heads(pv, rows_per_head, c):
    group = (2 * c) // GQA_GROUP
    lo = lax.broadcasted_iota(jnp.int32, (rows_per_head, LANES), 1) < HEAD_DIM
    a = pv[(2 * c) * rows_per_head:(2 * c + 1) * rows_per_head]
    b = pv[(2 * c + 1) * rows_per_head:(2 * c + 2) * rows_per_head]
    if group == 0:
        return jnp.where(lo, a, pltpu.roll(b, HEAD_DIM, axis=1))
    return jnp.where(lo, pltpu.roll(a, HEAD_DIM, axis=1), b)


WEIGHT_NAMES = ("g_mix", "w_in", "bd_q", "bd_k", "g_q", "g_k", "conv_w", "g_attn", "g_conv", "w_o",
                "g_mlp", "w_up", "w_down", "g_ple", "w_gate", "w_ple")
MATMUL_WEIGHTS = ("w_in", "w_o", "w_up", "w_down", "w_gate", "w_ple")
LATE_WEIGHTS = MATMUL_WEIGHTS[1:]
SMALL_WEIGHTS = tuple(n for n in WEIGHT_NAMES if n not in MATMUL_WEIGHTS)
W_IN_CHUNK = 64
W_IN_SLOTS = 4


def _prompt_kernel(sinks_ref, x_ref, p_ref, rope_ref, *rest):
    nw = len(WEIGHT_NAMES)
    refs = dict(zip(WEIGHT_NAMES, rest[:nw]))
    y_ref, kout_ref, vout_ref, convout_ref, kbuf, vbuf, ubuf = rest[nw:]
    tm = x_ref.shape[0]
    j = pl.program_id(1)

    @pl.when(j == 0)
    def _():
        kbuf[0:BLOCK, :] = jnp.zeros((BLOCK, KV_WIDTH), BF16)
        vbuf[:, 0:BLOCK] = jnp.zeros((KV_WIDTH, BLOCK), BF16)
        ubuf[0:SUBLANES, :] = jnp.zeros((SUBLANES, CONV_WIDTH), F32)

    kj = lax.broadcasted_iota(jnp.int32, (BLOCK, N_Q_HEADS * BLOCK), 0)
    qi = lax.broadcasted_iota(jnp.int32, (BLOCK, N_Q_HEADS * BLOCK), 1) & (BLOCK - 1)
    own = kj <= qi
    sink = jnp.concatenate([jnp.full((1, BLOCK), sinks_ref[i] * LOG2_E, F32) for i in range(N_Q_HEADS)],
                           axis=1)
    cw = refs["conv_w"]

    last = {}

    def sub_tile_steps(r0, n):
        rows = slice(r0, r0 + n)
        x = x_ref[rows, :]
        qcols, k, v, b_gate, u = yield from _front_steps(
            x, *_rope_rows(rope_ref[:, rows]), refs)
        kbuf[BLOCK + r0:BLOCK + r0 + n, :] = k.astype(BF16)
        v_t = v.T
        vbuf[:, BLOCK + r0:BLOCK + r0 + n] = v_t.astype(BF16)
        ubuf[SUBLANES + r0:SUBLANES + r0 + n, :] = u
        last.update(k=k, v_t=v_t, u=u, n=n)
        yield

        o_blocks = {}

        def block_stages(i):
            g = r0 // BLOCK + i
            qs = []
            for c in range(N_QCOL):
                qs.extend(_split_heads(qcols[c][i * BLOCK:(i + 1) * BLOCK], c))
            qstack = jnp.concatenate(qs, axis=0).astype(BF16)
            s2 = _dot_t(kbuf[g * BLOCK:(g + 2) * BLOCK, :], qstack)
            yield
            s_prev = s2[:BLOCK]
            if g == 0:
                s_prev = jnp.where(j == 0, NEG, s_prev)
            s = jnp.where(own, s2[BLOCK:], s_prev)
            mx = jnp.maximum(jnp.max(s, axis=0, keepdims=True), sink)
            e = jnp.exp2(s - mx)
            denom = jnp.sum(e, axis=0, keepdims=True) + jnp.exp2(sink - mx)
            e2 = jnp.concatenate([jnp.where(own, 0.0, e), jnp.where(own, e, 0.0)], axis=0).astype(BF16)
            yield
            pv = _dot(vbuf[:, g * BLOCK:(g + 2) * BLOCK], e2) * (1.0 / denom)
            cols = []
            for c in range(N_QCOL):
                d0 = (2 * c) // GQA_GROUP * HEAD_DIM
                col_t = jnp.concatenate([pv[d0:d0 + HEAD_DIM, (2 * c) * BLOCK:(2 * c + 1) * BLOCK],
                                         pv[d0:d0 + HEAD_DIM, (2 * c + 1) * BLOCK:(2 * c + 2) * BLOCK]], axis=0)
                cols.append(col_t.T)
            o_blocks[i] = jnp.concatenate(cols, axis=1)

        blocks = [block_stages(i) for i in range(n // BLOCK)]
        for _ in range(ATTN_STAGES):
            for stages in blocks:
                next(stages, None)
            yield
        o_attn = jnp.concatenate([o_blocks[i] for i in range(n // BLOCK)], axis=0)

        conv = (cw[0] * ubuf[SUBLANES - 2 + r0:SUBLANES - 2 + r0 + n, :]
                + cw[1] * ubuf[SUBLANES - 1 + r0:SUBLANES - 1 + r0 + n, :]
                + cw[2] * u)

        def store(y):
            y_ref[rows, :] = y

        yield from _tail_steps(x, o_attn, b_gate * conv, p_ref[rows, :], refs, store, MLP_LOOKAHEAD)

    assert sum(SUB_TILES) == tm
    starts = [sum(SUB_TILES[:t]) for t in range(len(SUB_TILES))]
    tiles = [sub_tile_steps(r0, n) for r0, n in zip(starts, SUB_TILES)]
    first, second = tiles
    for steps in [first] * FRONT_STAGES + [second] + [first] * ATTN_STAGES:
        next(steps)
    live = [second, first]
    while live:
        for steps in list(live):
            if next(steps, True) is True:
                live.remove(steps)

    n_last = last["n"]
    kout_ref[...] = last["k"][n_last - BLOCK:, :].T
    vout_ref[...] = last["v_t"][:, n_last - BLOCK:]
    convout_ref[...] = last["u"][n_last - (CONV_K - 1):, :]
    kbuf[0:BLOCK, :] = kbuf[tm:tm + BLOCK, :]
    vbuf[:, 0:BLOCK] = vbuf[:, tm:tm + BLOCK]
    ubuf[0:SUBLANES, :] = ubuf[tm:tm + SUBLANES, :]


def _sample_kernel(sinks_ref, x_ref, p_ref, rope_ref, state_ref, ck_ref, cv_ref, *rest):
    ns, nm, nl = len(SMALL_WEIGHTS), len(MATMUL_WEIGHTS), len(LATE_WEIGHTS)
    refs = dict(zip(SMALL_WEIGHTS, rest[:ns]))
    w_f32 = dict(zip(MATMUL_WEIGHTS, rest[ns:ns + nm]))
    y_ref, kout_ref, vout_ref, convout_ref = rest[ns + nm:ns + nm + 4]
    w_out = dict(zip(MATMUL_WEIGHTS, rest[ns + nm + 4:ns + 2 * nm + 4]))
    scratch = rest[ns + 2 * nm + 4:]
    qbuf, kbuf, vbuf, knew, vnew, obuf, bgbuf, ubuf = scratch[:8]
    w_vmem = dict(zip(MATMUL_WEIGHTS, scratch[8:8 + nm]))
    in_stage = scratch[8 + nm]
    late_stage = dict(zip(LATE_WEIGHTS, scratch[9 + nm:9 + nm + nl]))
    in_sem, late_sem, out_sem = scratch[9 + nm + nl:]
    refs.update(w_vmem)
    t_len = x_ref.shape[1]
    n_tok = x_ref.shape[0] * t_len
    step = pl.program_id(0)
    n_steps = pl.num_programs(0)
    step_batch = ck_ref.shape[0]
    pair_rows = 2 * t_len
    chunk_pairs = BLOCK // pair_rows

    n_slots = in_stage.shape[0]

    def in_copy(i):
        return pltpu.make_async_copy(w_f32["w_in"].at[pl.ds(i * W_IN_CHUNK, W_IN_CHUNK), :],
                                     in_stage.at[i % n_slots], in_sem.at[i % n_slots])

    def late_copy(n, name, s):
        rows = late_stage[name].shape[1]
        return pltpu.make_async_copy(w_f32[name].at[pl.ds(s * rows, rows), :], late_stage[name].at[s % 2],
                                     late_sem.at[n, s % 2])

    def out_copy(n, name):
        return pltpu.make_async_copy(w_vmem[name], w_out[name], out_sem.at[n])

    @pl.when(step + 1 < n_steps)
    def _():
        for n, name in enumerate(LATE_WEIGHTS):
            late_copy(n, name, step + 1).start()

    @pl.when(step == 0)
    def _():
        for n, name in enumerate(LATE_WEIGHTS):
            late_copy(n, name, step).start()
        n_chunks = D_MODEL // W_IN_CHUNK
        for i in range(n_slots):
            in_copy(i).start()
        for i in range(n_chunks):
            in_copy(i).wait()
            w_vmem["w_in"][i * W_IN_CHUNK:(i + 1) * W_IN_CHUNK, :] = in_stage[i % n_slots].astype(BF16)
            if i + n_slots < n_chunks:
                in_copy(i + n_slots).start()
        out_copy(0, "w_in").start()
        qcols, k, v, b_gate, u = _front(x_ref[...].reshape(n_tok, D_MODEL),
                                        *_rope_rows(rope_ref[...]), refs)
        for c in range(N_QCOL):
            qbuf[2 * c], qbuf[2 * c + 1] = _split_heads(qcols[c], c)
        k_t, v_t = k.T, v.T
        kbuf[...] = k_t.astype(BF16)
        vbuf[...] = v_t.astype(BF16)
        knew[...] = k_t
        vnew[...] = v_t
        bgbuf[...] = b_gate
        ubuf[...] = u
        convout_ref[...] = u.reshape(convout_ref.shape[0], t_len, CONV_WIDTH)[:, t_len - (CONV_K - 1):, :]

    n_rows = N_Q_HEADS * pair_rows
    row = lax.broadcasted_iota(jnp.int32, (n_rows, 3 * BLOCK), 0)
    col = lax.broadcasted_iota(jnp.int32, (n_rows, 3 * BLOCK), 1)
    row_b = (row % pair_rows) // t_len
    row_t = row % t_len
    cache_ok = (col < 2 * BLOCK) & (col // BLOCK == row_b) & ((col % BLOCK) > row_t)
    new_col = col - 2 * BLOCK
    sink = _sink_column(sinks_ref, pair_rows)

    def pair_stages(i):
        pair = step * (step_batch // 2) + i
        r0 = pl.multiple_of(pair * pair_rows, pair_rows)
        c0 = pl.multiple_of((pair // chunk_pairs) * BLOCK, BLOCK)
        q = qbuf[:, pl.ds(r0, pair_rows), :].reshape(n_rows, LANES).astype(BF16)
        ck = [ck_ref[2 * i + bb] for bb in range(2)]
        cv = [cv_ref[2 * i + bb] for bb in range(2)]
        keys = jnp.concatenate([c.astype(BF16) for c in ck] + [kbuf[:, pl.ds(c0, BLOCK)]], axis=1)
        vals = jnp.concatenate([c.astype(BF16) for c in cv] + [vbuf[:, pl.ds(c0, BLOCK)]], axis=1)
        s = _dot(q, keys)
        yield
        new_ok = ((col >= 2 * BLOCK) & (new_col // t_len == (pair % chunk_pairs) * 2 + row_b)
                  & (new_col % t_len <= row_t))
        ok = cache_ok | new_ok
        s = jnp.where(ok, s, NEG)
        mx = jnp.maximum(jnp.max(s, axis=-1, keepdims=True), sink)
        e = jnp.where(ok, jnp.exp2(s - mx), 0.0)
        denom = jnp.sum(e, axis=-1, keepdims=True) + jnp.exp2(sink - mx)
        yield
        pv = _dot_t(e.astype(BF16), vals) * (1.0 / denom)
        for c in range(N_QCOL):
            obuf[pl.ds(r0, pair_rows), c * LANES:(c + 1) * LANES] = _merge_heads(pv, pair_rows, c)
        yield
        k_chunk = knew[:, pl.ds(c0, BLOCK)]
        v_chunk = vnew[:, pl.ds(c0, BLOCK)]
        keep = lax.broadcasted_iota(jnp.int32, (KV_WIDTH, BLOCK), 1) < BLOCK - t_len
        for bb in range(2):
            to_tail = (BLOCK - t_len) - t_len * ((pair % chunk_pairs) * 2 + bb)
            kout_ref[2 * i + bb] = jnp.where(keep, pltpu.roll(ck[bb], BLOCK - t_len, axis=1),
                                             pltpu.roll(k_chunk, to_tail, axis=1))
            vout_ref[2 * i + bb] = jnp.where(keep, pltpu.roll(cv[bb], BLOCK - t_len, axis=1),
                                             pltpu.roll(v_chunk, to_tail, axis=1))

    pairs = [pair_stages(i) for i in range(step_batch // 2)]
    for _ in range(4):
        for stages in pairs:
            next(stages, None)

    for n, name in enumerate(LATE_WEIGHTS):
        rows = late_stage[name].shape[1]
        late_copy(n, name, step).wait()
        r0 = pl.multiple_of(step * rows, rows)
        w_vmem[name][pl.ds(r0, rows), :] = late_stage[name][step % 2].astype(BF16)

    @pl.when(step == n_steps - 1)
    def _():
        for n, name in enumerate(LATE_WEIGHTS):
            out_copy(n + 1, name).start()
        u = ubuf[...]
        tok = lax.broadcasted_iota(jnp.int32, u.shape, 0) % t_len
        per_token = lambda r: jnp.broadcast_to(state_ref[:, r:r + 1, :],
                                               (state_ref.shape[0], t_len, CONV_WIDTH)).reshape(u.shape)
        s0, s1 = per_token(0), per_token(1)
        um1 = jnp.where(tok >= 1, pltpu.roll(u, 1, axis=0), s1)
        um2 = jnp.where(tok >= 2, pltpu.roll(u, 2, axis=0), jnp.where(tok == 0, s0, s1))
        cw = refs["conv_w"]
        conv = cw[0] * um2 + cw[1] * um1 + cw[2] * u

        def store(y):
            y_ref[...] = y.reshape(y_ref.shape)

        _run(_tail_steps(x_ref[...].reshape(n_tok, D_MODEL), obuf[...], bgbuf[...] * conv,
                         p_ref[...].reshape(n_tok, PLE_DIM), refs, store))
        for n, name in enumerate(MATMUL_WEIGHTS):
            out_copy(n, name).wait()


def _rope_tables(pos):
    inv_freq = ROPE_THETA ** (-jnp.arange(0, HEAD_DIM, 2, dtype=F32) / HEAD_DIM)
    ang = inv_freq[:, None] * pos.astype(F32)[None, :]
    sin = jnp.sin(ang)
    return jnp.concatenate([jnp.cos(ang), -sin, sin], axis=0)


def _cache_to_kernel(c):
    batch, keys = c.shape[0], c.shape[1]
    return jnp.transpose(c, (0, 2, 3, 1)).reshape(batch, KV_WIDTH, keys)


def _cache_from_kernel(c):
    batch, _, keys = c.shape
    return jnp.transpose(c.reshape(batch, N_KV_HEADS, HEAD_DIM, keys), (0, 3, 1, 2))


def _block_diag_ones(width):
    idx = np.arange(width) // HEAD_DIM
    return jnp.asarray(idx[:, None] == idx[None, :], dtype=BF16)


def _prepare_weights(g_mix_norm, w_in, g_q, g_k, sinks, conv_w, g_attn_out, g_conv_out, w_o,
                     g_mlp_norm, w_up, w_down, g_ple_norm, w_ple_gate, w_ple):
    row = lambda g: g.reshape(1, -1).astype(F32)
    weights = dict(
        g_mix=row(g_mix_norm), w_in=w_in,
        bd_q=_block_diag_ones(ATTN_WIDTH), bd_k=_block_diag_ones(KV_WIDTH),
        g_q=row(g_q), g_k=row(g_k),
        conv_w=conv_w.astype(F32).reshape(CONV_K, 1, CONV_WIDTH),
        g_attn=row(g_attn_out), g_conv=row(g_conv_out), w_o=w_o,
        g_mlp=row(g_mlp_norm), w_up=w_up, w_down=w_down,
        g_ple=row(g_ple_norm), w_gate=w_ple_gate, w_ple=w_ple)
    return sinks.astype(F32), weights


def _resident(a, n_grid):
    zeros = (0,) * a.ndim
    index_map = (lambda b, j: zeros) if n_grid == 2 else (lambda i: zeros)
    return pl.BlockSpec(a.shape, index_map, pipeline_mode=pl.Buffered(1))


def _prompt_layer(x, p, sinks_p, weights):
    batch, seq, _ = x.shape
    tm = SEQ_TILE
    rope = _rope_tables(jnp.arange(seq, dtype=jnp.int32))
    tile = lambda w: pl.BlockSpec((None, tm, w), lambda b, j: (b, j, 0))
    per_batch = lambda r, w: pl.BlockSpec((None, r, w), lambda b, j: (b, 0, 0))
    table = pl.BlockSpec((rope.shape[0], tm), lambda b, j: (0, j))
    in_specs = ([pl.BlockSpec(memory_space=pltpu.SMEM), tile(D_MODEL), tile(PLE_DIM), table]
                + [_resident(w, 2) for w in weights])
    out_shape = (jax.ShapeDtypeStruct((batch, seq, D_MODEL), F32),
                 jax.ShapeDtypeStruct((batch, BLOCK, KV_WIDTH), F32),
                 jax.ShapeDtypeStruct((batch, BLOCK, KV_WIDTH), F32),
                 jax.ShapeDtypeStruct((batch, CONV_K - 1, CONV_WIDTH), F32))
    out_specs = (tile(D_MODEL), per_batch(BLOCK, KV_WIDTH), per_batch(BLOCK, KV_WIDTH),
                 per_batch(CONV_K - 1, CONV_WIDTH))
    scratch = [pltpu.VMEM((BLOCK + tm, KV_WIDTH), BF16), pltpu.VMEM((KV_WIDTH, BLOCK + tm), BF16),
               pltpu.VMEM((SUBLANES + tm, CONV_WIDTH), F32)]
    return pl.pallas_call(
        _prompt_kernel, grid=(batch, seq // tm), in_specs=in_specs, out_specs=out_specs,
        out_shape=out_shape, scratch_shapes=scratch, name="prompt_layer",
        compiler_params=pltpu.CompilerParams(dimension_semantics=("arbitrary", "arbitrary"),
                                             vmem_limit_bytes=VMEM_LIMIT),
    )(sinks_p, x, p, rope, *weights)


def _sample_layer(x, p, cache_k, cache_v, state_conv, sinks_p, weights):
    batch, t_len, _ = x.shape
    n_tok = batch * t_len
    rope = _rope_tables(PAST_LEN + jnp.arange(n_tok, dtype=jnp.int32) % t_len)
    ck = _cache_to_kernel(cache_k)
    cv = _cache_to_kernel(cache_v)
    flat = [x, p, rope, state_conv]
    n_steps = batch // SAMPLE_STEP_BATCH
    small = [weights[n] for n in SMALL_WEIGHTS]
    big = [weights[n] for n in MATMUL_WEIGHTS]
    hbm = pl.BlockSpec(memory_space=pl.ANY)
    cache_block = pl.BlockSpec((SAMPLE_STEP_BATCH, WINDOW, KV_WIDTH), lambda i: (i, 0, 0))
    in_specs = ([pl.BlockSpec(memory_space=pltpu.SMEM)] + [_resident(a, 1) for a in flat]
                + [cache_block, cache_block] + [_resident(w, 1) for w in small] + [hbm] * len(big))
    out_shape = (jax.ShapeDtypeStruct((batch, t_len, D_MODEL), F32),
                 jax.ShapeDtypeStruct((batch, WINDOW, KV_WIDTH), F32),
                 jax.ShapeDtypeStruct((batch, WINDOW, KV_WIDTH), F32),
                 jax.ShapeDtypeStruct((batch, CONV_K - 1, CONV_WIDTH), F32)
                 ) + tuple(jax.ShapeDtypeStruct(w.shape, BF16) for w in big)
    whole = lambda s: pl.BlockSpec(s.shape, lambda i: (0,) * len(s.shape), pipeline_mode=pl.Buffered(1))
    out_specs = (whole(out_shape[0]), cache_block, cache_block, whole(out_shape[3])) + (hbm,) * len(big)
    late = [weights[n] for n in LATE_WEIGHTS]
    scratch = ([pltpu.VMEM((N_Q_HEADS, n_tok, LANES), F32),
                pltpu.VMEM((KV_WIDTH, n_tok), BF16), pltpu.VMEM((KV_WIDTH, n_tok), BF16),
                pltpu.VMEM((KV_WIDTH, n_tok), F32), pltpu.VMEM((KV_WIDTH, n_tok), F32),
                pltpu.VMEM((n_tok, ATTN_WIDTH), F32), pltpu.VMEM((n_tok, CONV_WIDTH), F32),
                pltpu.VMEM((n_tok, CONV_WIDTH), F32)]
               + [pltpu.VMEM(w.shape, BF16) for w in big]
               + [pltpu.VMEM((W_IN_SLOTS, W_IN_CHUNK, IN_WIDTH), F32)]
               + [pltpu.VMEM((2, w.shape[0] // n_steps, w.shape[1]), F32) for w in late]
               + [pltpu.SemaphoreType.DMA((W_IN_SLOTS,)), pltpu.SemaphoreType.DMA((len(late), 2)),
                  pltpu.SemaphoreType.DMA((len(big),))])
    y, k_new, v_new, conv_new, *w_bf16 = pl.pallas_call(
        _sample_kernel, grid=(n_steps,), in_specs=in_specs, out_specs=out_specs,
        out_shape=out_shape, scratch_shapes=scratch, name="sample_layer",
        compiler_params=pltpu.CompilerParams(dimension_semantics=("arbitrary",),
                                             vmem_limit_bytes=SAMPLE_VMEM_LIMIT),
    )(sinks_p, *flat, ck, cv, *small, *big)
    return y, k_new, v_new, conv_new, dict(zip(MATMUL_WEIGHTS, w_bf16))


def kernel(x_prompt, x_sample, p_prompt, p_sample, cache_k, cache_v, state_conv, g_mix_norm, w_in, g_q, g_k,
           sinks, conv_w, g_attn_out, g_conv_out, w_o, g_mlp_norm, w_up, w_down, g_ple_norm, w_ple_gate, w_ple):
    depth = w_in.shape[0]
    yp, ys = x_prompt, x_sample
    outs = [[] for _ in range(6)]
    for i in range(depth):
        sinks_p, weights = _prepare_weights(
            g_mix_norm[i], w_in[i], g_q[i], g_k[i], sinks[i], conv_w[i], g_attn_out[i], g_conv_out[i],
            w_o[i], g_mlp_norm[i], w_up[i], w_down[i], g_ple_norm[i], w_ple_gate[i], w_ple[i])
        ys, ksn, vsn, csn, w_bf16 = _sample_layer(ys, p_sample[i], cache_k[i], cache_v[i], state_conv[i],
                                                  sinks_p, weights)
        yp, kp, vp, cp = _prompt_layer(yp, p_prompt[i], sinks_p,
                                       [w_bf16.get(n, weights[n]) for n in WEIGHT_NAMES])
        for lst, val in zip(outs, (_cache_from_kernel(kp), _cache_from_kernel(vp), cp,
                                   _cache_from_kernel(ksn), _cache_from_kernel(vsn), csn)):
            lst.append(val)
    return (yp, ys) + tuple(jnp.stack(o) for o in outs)
```

```python
import jax
import jax.numpy as jnp
import numpy as np
from jax import lax
from jax.experimental import pallas as pl
from jax.experimental.pallas import tpu as pltpu

D_MODEL = 1024
HEAD_DIM = 64
N_Q_HEADS = 8
N_KV_HEADS = 2
GQA_GROUP = N_Q_HEADS // N_KV_HEADS
ATTN_WIDTH = N_Q_HEADS * HEAD_DIM
KV_WIDTH = N_KV_HEADS * HEAD_DIM
CONV_WIDTH = D_MODEL - ATTN_WIDTH
CONV_K = 3
WINDOW = 128
BLOCK = 128
ROPE_THETA = 10000.0
D_FF = 4 * D_MODEL
PLE_DIM = 256
EPS = 1e-6
NEG = -1e30
LOG2_E = 1.4426950408889634
PAST_LEN = 16384
IN_WIDTH = ATTN_WIDTH + 2 * KV_WIDTH + 3 * CONV_WIDTH

LANES = 128
SUBLANES = 8
N_QCOL = ATTN_WIDTH // LANES
SEQ_TILE = 512
SUB_TILES = (256, 256)
FF_CHUNK = 1024
MLP_LOOKAHEAD = 2
FRONT_STAGES = 2
ATTN_STAGES = 3
SAMPLE_STEP_BATCH = 8
VMEM_LIMIT = 56 * 1024 * 1024
SAMPLE_VMEM_LIMIT = 60 * 1024 * 1024

O_K = ATTN_WIDTH
O_V = O_K + KV_WIDTH
O_B = O_V + KV_WIDTH
O_C = O_B + CONV_WIDTH
O_H = O_C + CONV_WIDTH

BF16 = jnp.bfloat16
F32 = jnp.float32


def _dot(a, b):
    return jnp.dot(a, b, preferred_element_type=F32)


def _dot_t(a, b):
    return lax.dot_general(a, b, (((1,), (1,)), ((), ())), preferred_element_type=F32)


def _rms(x, g):
    return x * lax.rsqrt(jnp.mean(x * x, axis=-1, keepdims=True) + EPS) * g


def _head_norm_rope(t, ones_bd, g, cos, sin):
    ssq = _dot((t * t).astype(BF16), ones_bd)
    t = t * lax.rsqrt(ssq * (1.0 / HEAD_DIM) + EPS)
    lane = lax.broadcasted_iota(jnp.int32, (t.shape[0], LANES), 1)
    first_half = (lane & (HEAD_DIM - 1)) < HEAD_DIM // 2
    cols = []
    for m in range(t.shape[1] // LANES):
        c = t[:, m * LANES:(m + 1) * LANES] * g
        up = pltpu.roll(c, LANES - HEAD_DIM // 2, axis=1)
        dn = pltpu.roll(c, HEAD_DIM // 2, axis=1)
        cols.append(c * cos + jnp.where(first_half, up, dn) * sin)
    return cols


def _rope_rows(table):
    half = HEAD_DIM // 2
    cos = jnp.concatenate([table[:half]] * (LANES // half), axis=0).T
    sin = jnp.concatenate([table[half:]] * (LANES // HEAD_DIM), axis=0).T
    return cos, sin


def _front_steps(x, cos, sin, refs):
    h = _rms(x, refs["g_mix"][...]).astype(BF16)
    z = _dot(h, refs["w_in"][:, :O_B])
    yield
    zc = _dot(h, refs["w_in"][:, O_B:])
    two_heads = lambda g: jnp.concatenate([g] * (LANES // HEAD_DIM), axis=1)
    g_q = two_heads(refs["g_q"][...]) * (HEAD_DIM ** -0.5 * LOG2_E)
    qcols = _head_norm_rope(z[:, :O_K], refs["bd_q"][...], g_q, cos, sin)
    (k,) = _head_norm_rope(z[:, O_K:O_V], refs["bd_k"][...], two_heads(refs["g_k"][...]), cos, sin)
    v = z[:, O_V:O_B]
    b_gate = zc[:, :CONV_WIDTH]
    u = zc[:, CONV_WIDTH:2 * CONV_WIDTH] * zc[:, 2 * CONV_WIDTH:]
    return qcols, k, v, b_gate, u


def _front(x, cos, sin, refs):
    steps = _front_steps(x, cos, sin, refs)
    try:
        while True:
            next(steps)
    except StopIteration as done:
        return done.value


def _tail_steps(x, o_attn, o_conv, p, refs, store, lookahead=1, embed_last=False):
    embed = lambda: _dot(p.astype(BF16), refs["w_ple"][...])
    emb = None if embed_last else embed()
    mixed = jnp.concatenate([_rms(o_attn, refs["g_attn"][...]), _rms(o_conv, refs["g_conv"][...])], axis=1)
    x = x + _dot(mixed.astype(BF16), refs["w_o"][...])
    hm = _rms(x, refs["g_mlp"][...]).astype(BF16)
    yield

    def hidden(c):
        up = _dot(hm, refs["w_up"][:, c * FF_CHUNK:(c + 1) * FF_CHUNK])
        return jnp.square(jnp.maximum(up.astype(BF16), 0.0))

    n_chunks = D_FF // FF_CHUNK
    acts = []
    for c in range(min(lookahead, n_chunks)):
        acts.append(hidden(c))
        yield
    for c in range(n_chunks):
        if c + lookahead < n_chunks:
            acts.append(hidden(c + lookahead))
            yield
        x = x + _dot(acts[c], refs["w_down"][c * FF_CHUNK:(c + 1) * FF_CHUNK, :])
        yield
    if embed_last:
        emb = embed()
    gate = jax.nn.sigmoid(_dot(_rms(x, refs["g_ple"][...]).astype(BF16), refs["w_gate"][...]))
    store(x + gate * emb)


def _run(steps):
    for _ in steps:
        pass


def _split_heads(col, c):
    group = (2 * c) // GQA_GROUP
    lo = lax.broadcasted_iota(jnp.int32, col.shape, 1) < HEAD_DIM
    swapped = pltpu.roll(col, HEAD_DIM, axis=1)
    if group == 0:
        return jnp.where(lo, col, 0.0), jnp.where(lo, swapped, 0.0)
    return jnp.where(lo, 0.0, swapped), jnp.where(lo, 0.0, col)


def _sink_column(sinks_ref, rows_per_head):
    return jnp.concatenate(
        [jnp.full((rows_per_head, 1), sinks_ref[i] * LOG2_E, F32) for i in range(N_Q_HEADS)], axis=0)


def _merge_heads(pv, rows_per_head, c):
    group = (2 * c) // GQA_GROUP
    lo = lax.broadcasted_iota(jnp.int32, (rows_per_head, LANES), 1) < HEAD_DIM
    a = pv[(2 * c) * rows_per_head:(2 * c + 1) * rows_per_head]
    b = pv[(2 * c + 1) * rows_per_head:(2 * c + 2) * rows_per_head]
    if group == 0:
        return jnp.where(lo, a, pltpu.roll(b, HEAD_DIM, axis=1))
    return jnp.where(lo, pltpu.roll(a, HEAD_DIM, axis=1), b)


WEIGHT_NAMES = ("g_mix", "w_in", "bd_q", "bd_k", "g_q", "g_k", "conv_w", "g_attn", "g_conv", "w_o",
                "g_mlp", "w_up", "w_down", "g_ple", "w_gate", "w_ple")
MATMUL_WEIGHTS = ("w_in", "w_o", "w_up", "w_down", "w_gate", "w_ple")
LATE_WEIGHTS = MATMUL_WEIGHTS[1:]
SMALL_WEIGHTS = tuple(n for n in WEIGHT_NAMES if n not in MATMUL_WEIGHTS)
W_IN_CHUNK = 64
W_IN_SLOTS = 4


def _prompt_kernel(sinks_ref, x_ref, p_ref, rope_ref, *rest):
    nw = len(WEIGHT_NAMES)
    refs = dict(zip(WEIGHT_NAMES, rest[:nw]))
    y_ref, kout_ref, vout_ref, convout_ref, kbuf, vbuf, ubuf = rest[nw:]
    tm = x_ref.shape[0]
    j = pl.program_id(1)

    @pl.when(j == 0)
    def _():
        kbuf[0:BLOCK, :] = jnp.zeros((BLOCK, KV_WIDTH), BF16)
        vbuf[:, 0:BLOCK] = jnp.zeros((KV_WIDTH, BLOCK), BF16)
        ubuf[0:SUBLANES, :] = jnp.zeros((SUBLANES, CONV_WIDTH), F32)

    kj = lax.broadcasted_iota(jnp.int32, (BLOCK, N_Q_HEADS * BLOCK), 0)
    qi = lax.broadcasted_iota(jnp.int32, (BLOCK, N_Q_HEADS * BLOCK), 1) & (BLOCK - 1)
    own = kj <= qi
    sink = jnp.concatenate([jnp.full((1, BLOCK), sinks_ref[i] * LOG2_E, F32) for i in range(N_Q_HEADS)],
                           axis=1)
    cw = refs["conv_w"]

    last = {}

    def sub_tile_steps(r0, n):
        rows = slice(r0, r0 + n)
        x = x_ref[rows, :]
        qcols, k, v, b_gate, u = yield from _front_steps(
            x, *_rope_rows(rope_ref[:, rows]), refs)
        kbuf[BLOCK + r0:BLOCK + r0 + n, :] = k.astype(BF16)
        v_t = v.T
        vbuf[:, BLOCK + r0:BLOCK + r0 + n] = v_t.astype(BF16)
        ubuf[SUBLANES + r0:SUBLANES + r0 + n, :] = u
        last.update(k=k, v_t=v_t, u=u, n=n)
        yield

        o_blocks = {}

        def block_stages(i):
            g = r0 // BLOCK + i
            qs = []
            for c in range(N_QCOL):
                qs.extend(_split_heads(qcols[c][i * BLOCK:(i + 1) * BLOCK], c))
            qstack = jnp.concatenate(qs, axis=0).astype(BF16)
            s2 = _dot_t(kbuf[g * BLOCK:(g + 2) * BLOCK, :], qstack)
            yield
            s_prev = s2[:BLOCK]
            if g == 0:
                s_prev = jnp.where(j == 0, NEG, s_prev)
            s = jnp.where(own, s2[BLOCK:], s_prev)
            mx = jnp.maximum(jnp.max(s, axis=0, keepdims=True), sink)
            e = jnp.exp2(s - mx)
            denom = jnp.sum(e, axis=0, keepdims=True) + jnp.exp2(sink - mx)
            e2 = jnp.concatenate([jnp.where(own, 0.0, e), jnp.where(own, e, 0.0)], axis=0).astype(BF16)
            yield
            pv = _dot(vbuf[:, g * BLOCK:(g + 2) * BLOCK], e2) * (1.0 / denom)
            cols = []
            for c in range(N_QCOL):
                d0 = (2 * c) // GQA_GROUP * HEAD_DIM
                col_t = jnp.concatenate([pv[d0:d0 + HEAD_DIM, (2 * c) * BLOCK:(2 * c + 1) * BLOCK],
                                         pv[d0:d0 + HEAD_DIM, (2 * c + 1) * BLOCK:(2 * c + 2) * BLOCK]], axis=0)
                cols.append(col_t.T)
            o_blocks[i] = jnp.concatenate(cols, axis=1)

        blocks = [block_stages(i) for i in range(n // BLOCK)]
        for _ in range(ATTN_STAGES):
            for stages in blocks:
                next(stages, None)
            yield
        o_attn = jnp.concatenate([o_blocks[i] for i in range(n // BLOCK)], axis=0)

        conv = (cw[0] * ubuf[SUBLANES - 2 + r0:SUBLANES - 2 + r0 + n, :]
                + cw[1] * ubuf[SUBLANES - 1 + r0:SUBLANES - 1 + r0 + n, :]
                + cw[2] * u)

        def store(y):
            y_ref[rows, :] = y

        yield from _tail_steps(x, o_attn, b_gate * conv, p_ref[rows, :], refs, store, MLP_LOOKAHEAD,
                               embed_last=r0 + n == tm)

    assert sum(SUB_TILES) == tm
    starts = [sum(SUB_TILES[:t]) for t in range(len(SUB_TILES))]
    tiles = [sub_tile_steps(r0, n) for r0, n in zip(starts, SUB_TILES)]
    first, second = tiles
    n_stages = FRONT_STAGES + ATTN_STAGES + 2 + 2 * (D_FF // FF_CHUNK)
    order = [first] * FRONT_STAGES + [second] + [first] * ATTN_STAGES
    alternating = n_stages - FRONT_STAGES - ATTN_STAGES - 1
    order += [second, first] * alternating
    order += [second] * (n_stages - 2 - alternating) + [first, second]
    for steps in order:
        next(steps, None)

    n_last = last["n"]
    kout_ref[...] = last["k"][n_last - BLOCK:, :].T
    vout_ref[...] = last["v_t"][:, n_last - BLOCK:]
    convout_ref[...] = last["u"][n_last - (CONV_K - 1):, :]
    kbuf[0:BLOCK, :] = kbuf[tm:tm + BLOCK, :]
    vbuf[:, 0:BLOCK] = vbuf[:, tm:tm + BLOCK]
    ubuf[0:SUBLANES, :] = ubuf[tm:tm + SUBLANES, :]


def _sample_kernel(sinks_ref, x_ref, p_ref, rope_ref, state_ref, ck_ref, cv_ref, *rest):
    ns, nm, nl = len(SMALL_WEIGHTS), len(MATMUL_WEIGHTS), len(LATE_WEIGHTS)
    refs = dict(zip(SMALL_WEIGHTS, rest[:ns]))
    w_f32 = dict(zip(MATMUL_WEIGHTS, rest[ns:ns + nm]))
    y_ref, kout_ref, vout_ref, convout_ref = rest[ns + nm:ns + nm + 4]
    w_out = dict(zip(MATMUL_WEIGHTS, rest[ns + nm + 4:ns + 2 * nm + 4]))
    scratch = rest[ns + 2 * nm + 4:]
    qbuf, kbuf, vbuf, knew, vnew, obuf, bgbuf, ubuf = scratch[:8]
    w_vmem = dict(zip(MATMUL_WEIGHTS, scratch[8:8 + nm]))
    in_stage = scratch[8 + nm]
    late_stage = dict(zip(LATE_WEIGHTS, scratch[9 + nm:9 + nm + nl]))
    in_sem, late_sem, out_sem = scratch[9 + nm + nl:]
    refs.update(w_vmem)
    t_len = x_ref.shape[1]
    n_tok = x_ref.shape[0] * t_len
    step = pl.program_id(0)
    n_steps = pl.num_programs(0)
    step_batch = ck_ref.shape[0]
    pair_rows = 2 * t_len
    chunk_pairs = BLOCK // pair_rows

    n_slots = in_stage.shape[0]

    def in_copy(i):
        return pltpu.make_async_copy(w_f32["w_in"].at[pl.ds(i * W_IN_CHUNK, W_IN_CHUNK), :],
                                     in_stage.at[i % n_slots], in_sem.at[i % n_slots])

    def late_copy(n, name, s):
        rows = late_stage[name].shape[1]
        return pltpu.make_async_copy(w_f32[name].at[pl.ds(s * rows, rows), :], late_stage[name].at[s % 2],
                                     late_sem.at[n, s % 2])

    def out_copy(n, name):
        return pltpu.make_async_copy(w_vmem[name], w_out[name], out_sem.at[n])

    @pl.when(step + 1 < n_steps)
    def _():
        for n, name in enumerate(LATE_WEIGHTS):
            late_copy(n, name, step + 1).start()

    @pl.when(step == 0)
    def _():
        for n, name in enumerate(LATE_WEIGHTS):
            late_copy(n, name, step).start()
        n_chunks = D_MODEL // W_IN_CHUNK
        for i in range(n_slots):
            in_copy(i).start()
        for i in range(n_chunks):
            in_copy(i).wait()
            w_vmem["w_in"][i * W_IN_CHUNK:(i + 1) * W_IN_CHUNK, :] = in_stage[i % n_slots].astype(BF16)
            if i + n_slots < n_chunks:
                in_copy(i + n_slots).start()
        out_copy(0, "w_in").start()
        qcols, k, v, b_gate, u = _front(x_ref[...].reshape(n_tok, D_MODEL),
                                        *_rope_rows(rope_ref[...]), refs)
        for c in range(N_QCOL):
            qbuf[2 * c], qbuf[2 * c + 1] = _split_heads(qcols[c], c)
        k_t, v_t = k.T, v.T
        kbuf[...] = k_t.astype(BF16)
        vbuf[...] = v_t.astype(BF16)
        knew[...] = k_t
        vnew[...] = v_t
        bgbuf[...] = b_gate
        ubuf[...] = u
        convout_ref[...] = u.reshape(convout_ref.shape[0], t_len, CONV_WIDTH)[:, t_len - (CONV_K - 1):, :]

    n_rows = N_Q_HEADS * pair_rows
    row = lax.broadcasted_iota(jnp.int32, (n_rows, 3 * BLOCK), 0)
    col = lax.broadcasted_iota(jnp.int32, (n_rows, 3 * BLOCK), 1)
    row_b = (row % pair_rows) // t_len
    row_t = row % t_len
    cache_ok = (col < 2 * BLOCK) & (col // BLOCK == row_b) & ((col % BLOCK) > row_t)
    new_col = col - 2 * BLOCK
    sink = _sink_column(sinks_ref, pair_rows)

    def pair_stages(i):
        pair = step * (step_batch // 2) + i
        r0 = pl.multiple_of(pair * pair_rows, pair_rows)
        c0 = pl.multiple_of((pair // chunk_pairs) * BLOCK, BLOCK)
        q = qbuf[:, pl.ds(r0, pair_rows), :].reshape(n_rows, LANES).astype(BF16)
        ck = [ck_ref[2 * i + bb] for bb in range(2)]
        cv = [cv_ref[2 * i + bb] for bb in range(2)]
        keys = jnp.concatenate([c.astype(BF16) for c in ck] + [kbuf[:, pl.ds(c0, BLOCK)]], axis=1)
        vals = jnp.concatenate([c.astype(BF16) for c in cv] + [vbuf[:, pl.ds(c0, BLOCK)]], axis=1)
        s = _dot(q, keys)
        yield
        new_ok = ((col >= 2 * BLOCK) & (new_col // t_len == (pair % chunk_pairs) * 2 + row_b)
                  & (new_col % t_len <= row_t))
        ok = cache_ok | new_ok
        s = jnp.where(ok, s, NEG)
        mx = jnp.maximum(jnp.max(s, axis=-1, keepdims=True), sink)
        e = jnp.where(ok, jnp.exp2(s - mx), 0.0)
        denom = jnp.sum(e, axis=-1, keepdims=True) + jnp.exp2(sink - mx)
        yield
        pv = _dot_t(e.astype(BF16), vals) * (1.0 / denom)
        for c in range(N_QCOL):
            obuf[pl.ds(r0, pair_rows), c * LANES:(c + 1) * LANES] = _merge_heads(pv, pair_rows, c)
        yield
        k_chunk = knew[:, pl.ds(c0, BLOCK)]
        v_chunk = vnew[:, pl.ds(c0, BLOCK)]
        keep = lax.broadcasted_iota(jnp.int32, (KV_WIDTH, BLOCK), 1) < BLOCK - t_len
        for bb in range(2):
            to_tail = (BLOCK - t_len) - t_len * ((pair % chunk_pairs) * 2 + bb)
            kout_ref[2 * i + bb] = jnp.where(keep, pltpu.roll(ck[bb], BLOCK - t_len, axis=1),
                                             pltpu.roll(k_chunk, to_tail, axis=1))
            vout_ref[2 * i + bb] = jnp.where(keep, pltpu.roll(cv[bb], BLOCK - t_len, axis=1),
                                             pltpu.roll(v_chunk, to_tail, axis=1))

    pairs = [pair_stages(i) for i in range(step_batch // 2)]
    for _ in range(4):
        for stages in pairs:
            next(stages, None)

    for n, name in enumerate(LATE_WEIGHTS):
        rows = late_stage[name].shape[1]
        late_copy(n, name, step).wait()
        r0 = pl.multiple_of(step * rows, rows)
        w_vmem[name][pl.ds(r0, rows), :] = late_stage[name][step % 2].astype(BF16)

    @pl.when(step == n_steps - 1)
    def _():
        for n, name in enumerate(LATE_WEIGHTS):
            out_copy(n + 1, name).start()
        u = ubuf[...]
        tok = lax.broadcasted_iota(jnp.int32, u.shape, 0) % t_len
        per_token = lambda r: jnp.broadcast_to(state_ref[:, r:r + 1, :],
                                               (state_ref.shape[0], t_len, CONV_WIDTH)).reshape(u.shape)
        s0, s1 = per_token(0), per_token(1)
        um1 = jnp.where(tok >= 1, pltpu.roll(u, 1, axis=0), s1)
        um2 = jnp.where(tok >= 2, pltpu.roll(u, 2, axis=0), jnp.where(tok == 0, s0, s1))
        cw = refs["conv_w"]
        conv = cw[0] * um2 + cw[1] * um1 + cw[2] * u

        def store(y):
            y_ref[...] = y.reshape(y_ref.shape)

        _run(_tail_steps(x_ref[...].reshape(n_tok, D_MODEL), obuf[...], bgbuf[...] * conv,
                         p_ref[...].reshape(n_tok, PLE_DIM), refs, store))
        for n, name in enumerate(MATMUL_WEIGHTS):
            out_copy(n, name).wait()


def _rope_tables(pos):
    inv_freq = ROPE_THETA ** (-jnp.arange(0, HEAD_DIM, 2, dtype=F32) / HEAD_DIM)
    ang = inv_freq[:, None] * pos.astype(F32)[None, :]
    sin = jnp.sin(ang)
    return jnp.concatenate([jnp.cos(ang), -sin, sin], axis=0)


def _cache_to_kernel(c):
    batch, keys = c.shape[0], c.shape[1]
    return jnp.transpose(c, (0, 2, 3, 1)).reshape(batch, KV_WIDTH, keys)


def _cache_from_kernel(c):
    batch, _, keys = c.shape
    return jnp.transpose(c.reshape(batch, N_KV_HEADS, HEAD_DIM, keys), (0, 3, 1, 2))


def _block_diag_ones(width):
    idx = np.arange(width) // HEAD_DIM
    return jnp.asarray(idx[:, None] == idx[None, :], dtype=BF16)


def _prepare_weights(g_mix_norm, w_in, g_q, g_k, sinks, conv_w, g_attn_out, g_conv_out, w_o,
                     g_mlp_norm, w_up, w_down, g_ple_norm, w_ple_gate, w_ple):
    row = lambda g: g.reshape(1, -1).astype(F32)
    weights = dict(
        g_mix=row(g_mix_norm), w_in=w_in,
        bd_q=_block_diag_ones(ATTN_WIDTH), bd_k=_block_diag_ones(KV_WIDTH),
        g_q=row(g_q), g_k=row(g_k),
        conv_w=conv_w.astype(F32).reshape(CONV_K, 1, CONV_WIDTH),
        g_attn=row(g_attn_out), g_conv=row(g_conv_out), w_o=w_o,
        g_mlp=row(g_mlp_norm), w_up=w_up, w_down=w_down,
        g_ple=row(g_ple_norm), w_gate=w_ple_gate, w_ple=w_ple)
    return sinks.astype(F32), weights


def _resident(a, n_grid):
    zeros = (0,) * a.ndim
    index_map = (lambda b, j: zeros) if n_grid == 2 else (lambda i: zeros)
    return pl.BlockSpec(a.shape, index_map, pipeline_mode=pl.Buffered(1))


def _prompt_layer(x, p, sinks_p, weights):
    batch, seq, _ = x.shape
    tm = SEQ_TILE
    rope = _rope_tables(jnp.arange(seq, dtype=jnp.int32))
    tile = lambda w: pl.BlockSpec((None, tm, w), lambda b, j: (b, j, 0))
    per_batch = lambda r, w: pl.BlockSpec((None, r, w), lambda b, j: (b, 0, 0))
    table = pl.BlockSpec((rope.shape[0], tm), lambda b, j: (0, j))
    in_specs = ([pl.BlockSpec(memory_space=pltpu.SMEM), tile(D_MODEL), tile(PLE_DIM), table]
                + [_resident(w, 2) for w in weights])
    out_shape = (jax.ShapeDtypeStruct((batch, seq, D_MODEL), F32),
                 jax.ShapeDtypeStruct((batch, BLOCK, KV_WIDTH), F32),
                 jax.ShapeDtypeStruct((batch, BLOCK, KV_WIDTH), F32),
                 jax.ShapeDtypeStruct((batch, CONV_K - 1, CONV_WIDTH), F32))
    out_specs = (tile(D_MODEL), per_batch(BLOCK, KV_WIDTH), per_batch(BLOCK, KV_WIDTH),
                 per_batch(CONV_K - 1, CONV_WIDTH))
    scratch = [pltpu.VMEM((BLOCK + tm, KV_WIDTH), BF16), pltpu.VMEM((KV_WIDTH, BLOCK + tm), BF16),
               pltpu.VMEM((SUBLANES + tm, CONV_WIDTH), F32)]
    return pl.pallas_call(
        _prompt_kernel, grid=(batch, seq // tm), in_specs=in_specs, out_specs=out_specs,
        out_shape=out_shape, scratch_shapes=scratch, name="prompt_layer",
        compiler_params=pltpu.CompilerParams(dimension_semantics=("arbitrary", "arbitrary"),
                                             vmem_limit_bytes=VMEM_LIMIT),
    )(sinks_p, x, p, rope, *weights)


def _sample_layer(x, p, cache_k, cache_v, state_conv, sinks_p, weights):
    batch, t_len, _ = x.shape
    n_tok = batch * t_len
    rope = _rope_tables(PAST_LEN + jnp.arange(n_tok, dtype=jnp.int32) % t_len)
    ck = _cache_to_kernel(cache_k)
    cv = _cache_to_kernel(cache_v)
    flat = [x, p, rope, state_conv]
    n_steps = batch // SAMPLE_STEP_BATCH
    small = [weights[n] for n in SMALL_WEIGHTS]
    big = [weights[n] for n in MATMUL_WEIGHTS]
    hbm = pl.BlockSpec(memory_space=pl.ANY)
    cache_block = pl.BlockSpec((SAMPLE_STEP_BATCH, WINDOW, KV_WIDTH), lambda i: (i, 0, 0))
    in_specs = ([pl.BlockSpec(memory_space=pltpu.SMEM)] + [_resident(a, 1) for a in flat]
                + [cache_block, cache_block] + [_resident(w, 1) for w in small] + [hbm] * len(big))
    out_shape = (jax.ShapeDtypeStruct((batch, t_len, D_MODEL), F32),
                 jax.ShapeDtypeStruct((batch, WINDOW, KV_WIDTH), F32),
                 jax.ShapeDtypeStruct((batch, WINDOW, KV_WIDTH), F32),
                 jax.ShapeDtypeStruct((batch, CONV_K - 1, CONV_WIDTH), F32)
                 ) + tuple(jax.ShapeDtypeStruct(w.shape, BF16) for w in big)
    whole = lambda s: pl.BlockSpec(s.shape, lambda i: (0,) * len(s.shape), pipeline_mode=pl.Buffered(1))
    out_specs = (whole(out_shape[0]), cache_block, cache_block, whole(out_shape[3])) + (hbm,) * len(big)
    late = [weights[n] for n in LATE_WEIGHTS]
    scratch = ([pltpu.VMEM((N_Q_HEADS, n_tok, LANES), F32),
                pltpu.VMEM((KV_WIDTH, n_tok), BF16), pltpu.VMEM((KV_WIDTH, n_tok), BF16),
                pltpu.VMEM((KV_WIDTH, n_tok), F32), pltpu.VMEM((KV_WIDTH, n_tok), F32),
                pltpu.VMEM((n_tok, ATTN_WIDTH), F32), pltpu.VMEM((n_tok, CONV_WIDTH), F32),
                pltpu.VMEM((n_tok, CONV_WIDTH), F32)]
               + [pltpu.VMEM(w.shape, BF16) for w in big]
               + [pltpu.VMEM((W_IN_SLOTS, W_IN_CHUNK, IN_WIDTH), F32)]
               + [pltpu.VMEM((2, w.shape[0] // n_steps, w.shape[1]), F32) for w in late]
               + [pltpu.SemaphoreType.DMA((W_IN_SLOTS,)), pltpu.SemaphoreType.DMA((len(late), 2)),
                  pltpu.SemaphoreType.DMA((len(big),))])
    y, k_new, v_new, conv_new, *w_bf16 = pl.pallas_call(
        _sample_kernel, grid=(n_steps,), in_specs=in_specs, out_specs=out_specs,
        out_shape=out_shape, scratch_shapes=scratch, name="sample_layer",
        compiler_params=pltpu.CompilerParams(dimension_semantics=("arbitrary",),
                                             vmem_limit_bytes=SAMPLE_VMEM_LIMIT),
    )(sinks_p, *flat, ck, cv, *small, *big)
    return y, k_new, v_new, conv_new, dict(zip(MATMUL_WEIGHTS, w_bf16))


def kernel(x_prompt, x_sample, p_prompt, p_sample, cache_k, cache_v, state_conv, g_mix_norm, w_in, g_q, g_k,
           sinks, conv_w, g_attn_out, g_conv_out, w_o, g_mlp_norm, w_up, w_down, g_ple_norm, w_ple_gate, w_ple):
    depth = w_in.shape[0]
    yp, ys = x_prompt, x_sample
    outs = [[] for _ in range(6)]
    for i in range(depth):
        sinks_p, weights = _prepare_weights(
            g_mix_norm[i], w_in[i], g_q[i], g_k[i], sinks[i], conv_w[i], g_attn_out[i], g_conv_out[i],
            w_o[i], g_mlp_norm[i], w_up[i], w_down[i], g_ple_norm[i], w_ple_gate[i], w_ple[i])
        ys, ksn, vsn, csn, w_bf16 = _sample_layer(ys, p_sample[i], cache_k[i], cache_v[i], state_conv[i],
                                                  sinks_p, weights)
        yp, kp, vp, cp = _prompt_layer(yp, p_prompt[i], sinks_p,
                                       [w_bf16.get(n, weights[n]) for n in WEIGHT_NAMES])
        for lst, val in zip(outs, (_cache_from_kernel(kp), _cache_from_kernel(vp), cp,
                                   _cache_from_kernel(ksn), _cache_from_kernel(vsn), csn)):
            lst.append(val)
    return (yp, ys) + tuple(jnp.stack(o) for o in outs)
```

```python
import jax
import jax.numpy as jnp
import numpy as np
from jax import lax
from jax.experimental import pallas as pl
from jax.experimental.pallas import tpu as pltpu

D_MODEL = 1024
HEAD_DIM = 64
N_Q_HEADS = 8
N_KV_HEADS = 2
GQA_GROUP = N_Q_HEADS // N_KV_HEADS
ATTN_WIDTH = N_Q_HEADS * HEAD_DIM
KV_WIDTH = N_KV_HEADS * HEAD_DIM
CONV_WIDTH = D_MODEL - ATTN_WIDTH
CONV_K = 3
WINDOW = 128
BLOCK = 128
ROPE_THETA = 10000.0
D_FF = 4 * D_MODEL
PLE_DIM = 256
EPS = 1e-6
NEG = -1e30
LOG2_E = 1.4426950408889634
PAST_LEN = 16384
IN_WIDTH = ATTN_WIDTH + 2 * KV_WIDTH + 3 * CONV_WIDTH

LANES = 128
SUBLANES = 8
N_QCOL = ATTN_WIDTH // LANES
SEQ_TILE = 512
SUB_TILES = (256, 256)
FF_CHUNK = 1024
MLP_LOOKAHEAD = 2
FRONT_STAGES = 2
ATTN_STAGES = 3
SAMPLE_STEP_BATCH = 8
VMEM_LIMIT = 56 * 1024 * 1024
SAMPLE_VMEM_LIMIT = 60 * 1024 * 1024

O_K = ATTN_WIDTH
O_V = O_K + KV_WIDTH
O_B = O_V + KV_WIDTH
O_C = O_B + CONV_WIDTH
O_H = O_C + CONV_WIDTH

BF16 = jnp.bfloat16
F32 = jnp.float32


def _dot(a, b):
    return jnp.dot(a, b, preferred_element_type=F32)


def _dot_t(a, b):
    return lax.dot_general(a, b, (((1,), (1,)), ((), ())), preferred_element_type=F32)


def _rms(x, g):
    return x * lax.rsqrt(jnp.mean(x * x, axis=-1, keepdims=True) + EPS) * g


def _head_norm_rope(t, ones_bd, g, cos, sin):
    ssq = _dot((t * t).astype(BF16), ones_bd)
    t = t * lax.rsqrt(ssq * (1.0 / HEAD_DIM) + EPS)
    lane = lax.broadcasted_iota(jnp.int32, (t.shape[0], LANES), 1)
    first_half = (lane & (HEAD_DIM - 1)) < HEAD_DIM // 2
    cols = []
    for m in range(t.shape[1] // LANES):
        c = t[:, m * LANES:(m + 1) * LANES] * g
        up = pltpu.roll(c, LANES - HEAD_DIM // 2, axis=1)
        dn = pltpu.roll(c, HEAD_DIM // 2, axis=1)
        cols.append(c * cos + jnp.where(first_half, up, dn) * sin)
    return cols


def _rope_rows(table):
    half = HEAD_DIM // 2
    cos = jnp.concatenate([table[:half]] * (LANES // half), axis=0).T
    sin = jnp.concatenate([table[half:]] * (LANES // HEAD_DIM), axis=0).T
    return cos, sin


def _front_steps(x, cos, sin, refs):
    h = _rms(x, refs["g_mix"][...]).astype(BF16)
    z = _dot(h, refs["w_in"][:, :O_B])
    yield
    zc = _dot(h, refs["w_in"][:, O_B:])
    two_heads = lambda g: jnp.concatenate([g] * (LANES // HEAD_DIM), axis=1)
    g_q = two_heads(refs["g_q"][...]) * (HEAD_DIM ** -0.5 * LOG2_E)
    qcols = _head_norm_rope(z[:, :O_K], refs["bd_q"][...], g_q, cos, sin)
    (k,) = _head_norm_rope(z[:, O_K:O_V], refs["bd_k"][...], two_heads(refs["g_k"][...]), cos, sin)
    v = z[:, O_V:O_B]
    b_gate = zc[:, :CONV_WIDTH]
    u = zc[:, CONV_WIDTH:2 * CONV_WIDTH] * zc[:, 2 * CONV_WIDTH:]
    return qcols, k, v, b_gate, u


def _front(x, cos, sin, refs):
    steps = _front_steps(x, cos, sin, refs)
    try:
        while True:
            next(steps)
    except StopIteration as done:
        return done.value


def _tail_steps(x, o_attn, o_conv, p, refs, store, lookahead=1, embed_last=False, emb=None):
    embed = lambda: _dot(p.astype(BF16), refs["w_ple"][...])
    if emb is None and not embed_last:
        emb = embed()
    mixed = jnp.concatenate([_rms(o_attn, refs["g_attn"][...]), _rms(o_conv, refs["g_conv"][...])], axis=1)
    x = x + _dot(mixed.astype(BF16), refs["w_o"][...])
    hm = _rms(x, refs["g_mlp"][...]).astype(BF16)
    yield

    def hidden(c):
        up = _dot(hm, refs["w_up"][:, c * FF_CHUNK:(c + 1) * FF_CHUNK])
        return jnp.square(jnp.maximum(up.astype(BF16), 0.0))

    n_chunks = D_FF // FF_CHUNK
    acts = []
    for c in range(min(lookahead, n_chunks)):
        acts.append(hidden(c))
        yield
    for c in range(n_chunks):
        if c + lookahead < n_chunks:
            acts.append(hidden(c + lookahead))
            yield
        x = x + _dot(acts[c], refs["w_down"][c * FF_CHUNK:(c + 1) * FF_CHUNK, :])
        yield
    if emb is None:
        emb = embed()
    gate = jax.nn.sigmoid(_dot(_rms(x, refs["g_ple"][...]).astype(BF16), refs["w_gate"][...]))
    store(x + gate * emb)


def _run(steps):
    for _ in steps:
        pass


def _split_heads(col, c):
    group = (2 * c) // GQA_GROUP
    lo = lax.broadcasted_iota(jnp.int32, col.shape, 1) < HEAD_DIM
    swapped = pltpu.roll(col, HEAD_DIM, axis=1)
    if group == 0:
        return jnp.where(lo, col, 0.0), jnp.where(lo, swapped, 0.0)
    return jnp.where(lo, 0.0, swapped), jnp.where(lo, 0.0, col)


def _sink_column(sinks_ref, rows_per_head):
    return jnp.concatenate(
        [jnp.full((rows_per_head, 1), sinks_ref[i] * LOG2_E, F32) for i in range(N_Q_HEADS)], axis=0)


def _merge_heads(pv, rows_per_head, c):
    group = (2 * c) // GQA_GROUP
    lo = lax.broadcasted_iota(jnp.int32, (rows_per_head, LANES), 1) < HEAD_DIM
    a = pv[(2 * c) * rows_per_head:(2 * c + 1) * rows_per_head]
    b = pv[(2 * c + 1) * rows_per_head:(2 * c + 2) * rows_per_head]
    if group == 0:
        return jnp.where(lo, a, pltpu.roll(b, HEAD_DIM, axis=1))
    return jnp.where(lo, pltpu.roll(a, HEAD_DIM, axis=1), b)


WEIGHT_NAMES = ("g_mix", "w_in", "bd_q", "bd_k", "g_q", "g_k", "conv_w", "g_attn", "g_conv", "w_o",
                "g_mlp", "w_up", "w_down", "g_ple", "w_gate", "w_ple")
MATMUL_WEIGHTS = ("w_in", "w_o", "w_up", "w_down", "w_gate", "w_ple")
LATE_WEIGHTS = MATMUL_WEIGHTS[1:]
SMALL_WEIGHTS = tuple(n for n in WEIGHT_NAMES if n not in MATMUL_WEIGHTS)
W_IN_CHUNK = 64
W_IN_SLOTS = 4


def _prompt_kernel(sinks_ref, x_ref, p_ref, rope_ref, *rest):
    nw = len(WEIGHT_NAMES)
    refs = dict(zip(WEIGHT_NAMES, rest[:nw]))
    y_ref, kout_ref, vout_ref, convout_ref, kbuf, vbuf, ubuf = rest[nw:]
    tm = x_ref.shape[0]
    j = pl.program_id(1)

    @pl.when(j == 0)
    def _():
        kbuf[0:BLOCK, :] = jnp.zeros((BLOCK, KV_WIDTH), BF16)
        vbuf[:, 0:BLOCK] = jnp.zeros((KV_WIDTH, BLOCK), BF16)
        ubuf[0:SUBLANES, :] = jnp.zeros((SUBLANES, CONV_WIDTH), F32)

    kj = lax.broadcasted_iota(jnp.int32, (BLOCK, N_Q_HEADS * BLOCK), 0)
    qi = lax.broadcasted_iota(jnp.int32, (BLOCK, N_Q_HEADS * BLOCK), 1) & (BLOCK - 1)
    own = kj <= qi
    sink = jnp.concatenate([jnp.full((1, BLOCK), sinks_ref[i] * LOG2_E, F32) for i in range(N_Q_HEADS)],
                           axis=1)
    cw = refs["conv_w"]

    last = {}

    def sub_tile_steps(r0, n):
        rows = slice(r0, r0 + n)
        emb = _dot(p_ref[rows, :].astype(BF16), refs["w_ple"][...]) if r0 == 0 else None
        x = x_ref[rows, :]
        qcols, k, v, b_gate, u = yield from _front_steps(
            x, *_rope_rows(rope_ref[:, rows]), refs)
        kbuf[BLOCK + r0:BLOCK + r0 + n, :] = k.astype(BF16)
        v_t = v.T
        vbuf[:, BLOCK + r0:BLOCK + r0 + n] = v_t.astype(BF16)
        ubuf[SUBLANES + r0:SUBLANES + r0 + n, :] = u
        last.update(k=k, v_t=v_t, u=u, n=n)
        yield

        o_blocks = {}

        def block_stages(i):
            g = r0 // BLOCK + i
            qs = []
            for c in range(N_QCOL):
                qs.extend(_split_heads(qcols[c][i * BLOCK:(i + 1) * BLOCK], c))
            qstack = jnp.concatenate(qs, axis=0).astype(BF16)
            s2 = _dot_t(kbuf[g * BLOCK:(g + 2) * BLOCK, :], qstack)
            yield
            s_prev = s2[:BLOCK]
            if g == 0:
                s_prev = jnp.where(j == 0, NEG, s_prev)
            s = jnp.where(own, s2[BLOCK:], s_prev)
            mx = jnp.maximum(jnp.max(s, axis=0, keepdims=True), sink)
            e = jnp.exp2(s - mx)
            denom = jnp.sum(e, axis=0, keepdims=True) + jnp.exp2(sink - mx)
            e2 = jnp.concatenate([jnp.where(own, 0.0, e), jnp.where(own, e, 0.0)], axis=0).astype(BF16)
            yield
            pv = _dot(vbuf[:, g * BLOCK:(g + 2) * BLOCK], e2) * (1.0 / denom)
            cols = []
            for c in range(N_QCOL):
                d0 = (2 * c) // GQA_GROUP * HEAD_DIM
                col_t = jnp.concatenate([pv[d0:d0 + HEAD_DIM, (2 * c) * BLOCK:(2 * c + 1) * BLOCK],
                                         pv[d0:d0 + HEAD_DIM, (2 * c + 1) * BLOCK:(2 * c + 2) * BLOCK]], axis=0)
                cols.append(col_t.T)
            o_blocks[i] = jnp.concatenate(cols, axis=1)

        blocks = [block_stages(i) for i in range(n // BLOCK)]
        for _ in range(ATTN_STAGES):
            for stages in blocks:
                next(stages, None)
            yield
        o_attn = jnp.concatenate([o_blocks[i] for i in range(n // BLOCK)], axis=0)

        conv = (cw[0] * ubuf[SUBLANES - 2 + r0:SUBLANES - 2 + r0 + n, :]
                + cw[1] * ubuf[SUBLANES - 1 + r0:SUBLANES - 1 + r0 + n, :]
                + cw[2] * u)

        def store(y):
            y_ref[rows, :] = y

        yield from _tail_steps(x, o_attn, b_gate * conv, p_ref[rows, :], refs, store, MLP_LOOKAHEAD,
                               embed_last=r0 + n == tm, emb=emb)

    assert sum(SUB_TILES) == tm
    starts = [sum(SUB_TILES[:t]) for t in range(len(SUB_TILES))]
    tiles = [sub_tile_steps(r0, n) for r0, n in zip(starts, SUB_TILES)]
    first, second = tiles
    n_stages = FRONT_STAGES + ATTN_STAGES + 2 + 2 * (D_FF // FF_CHUNK)
    order = [first] * FRONT_STAGES + [second] + [first] * ATTN_STAGES
    alternating = n_stages - FRONT_STAGES - ATTN_STAGES - 1
    order += [second, first] * alternating
    order += [second] * (n_stages - 2 - alternating) + [first, second]
    for steps in order:
        next(steps, None)

    n_last = last["n"]
    kout_ref[...] = last["k"][n_last - BLOCK:, :].T
    vout_ref[...] = last["v_t"][:, n_last - BLOCK:]
    convout_ref[...] = last["u"][n_last - (CONV_K - 1):, :]
    kbuf[0:BLOCK, :] = kbuf[tm:tm + BLOCK, :]
    vbuf[:, 0:BLOCK] = vbuf[:, tm:tm + BLOCK]
    ubuf[0:SUBLANES, :] = ubuf[tm:tm + SUBLANES, :]


def _sample_kernel(sinks_ref, x_ref, p_ref, rope_ref, state_ref, ck_ref, cv_ref, *rest):
    ns, nm, nl = len(SMALL_WEIGHTS), len(MATMUL_WEIGHTS), len(LATE_WEIGHTS)
    refs = dict(zip(SMALL_WEIGHTS, rest[:ns]))
    w_f32 = dict(zip(MATMUL_WEIGHTS, rest[ns:ns + nm]))
    y_ref, kout_ref, vout_ref, convout_ref = rest[ns + nm:ns + nm + 4]
    w_out = dict(zip(MATMUL_WEIGHTS, rest[ns + nm + 4:ns + 2 * nm + 4]))
    scratch = rest[ns + 2 * nm + 4:]
    qbuf, kbuf, vbuf, knew, vnew, obuf, bgbuf, ubuf = scratch[:8]
    w_vmem = dict(zip(MATMUL_WEIGHTS, scratch[8:8 + nm]))
    in_stage = scratch[8 + nm]
    late_stage = dict(zip(LATE_WEIGHTS, scratch[9 + nm:9 + nm + nl]))
    in_sem, late_sem, out_sem = scratch[9 + nm + nl:]
    refs.update(w_vmem)
    t_len = x_ref.shape[1]
    n_tok = x_ref.shape[0] * t_len
    step = pl.program_id(0)
    n_steps = pl.num_programs(0)
    step_batch = ck_ref.shape[0]
    pair_rows = 2 * t_len
    chunk_pairs = BLOCK // pair_rows

    n_slots = in_stage.shape[0]

    def in_copy(i):
        return pltpu.make_async_copy(w_f32["w_in"].at[pl.ds(i * W_IN_CHUNK, W_IN_CHUNK), :],
                                     in_stage.at[i % n_slots], in_sem.at[i % n_slots])

    def late_copy(n, name, s):
        rows = late_stage[name].shape[1]
        return pltpu.make_async_copy(w_f32[name].at[pl.ds(s * rows, rows), :], late_stage[name].at[s % 2],
                                     late_sem.at[n, s % 2])

    def out_copy(n, name):
        return pltpu.make_async_copy(w_vmem[name], w_out[name], out_sem.at[n])

    @pl.when(step + 1 < n_steps)
    def _():
        for n, name in enumerate(LATE_WEIGHTS):
            late_copy(n, name, step + 1).start()

    @pl.when(step == 0)
    def _():
        for n, name in enumerate(LATE_WEIGHTS):
            late_copy(n, name, step).start()
        n_chunks = D_MODEL // W_IN_CHUNK
        for i in range(n_slots):
            in_copy(i).start()
        for i in range(n_chunks):
            in_copy(i).wait()
            w_vmem["w_in"][i * W_IN_CHUNK:(i + 1) * W_IN_CHUNK, :] = in_stage[i % n_slots].astype(BF16)
            if i + n_slots < n_chunks:
                in_copy(i + n_slots).start()
        out_copy(0, "w_in").start()
        qcols, k, v, b_gate, u = _front(x_ref[...].reshape(n_tok, D_MODEL),
                                        *_rope_rows(rope_ref[...]), refs)
        for c in range(N_QCOL):
            qbuf[2 * c], qbuf[2 * c + 1] = _split_heads(qcols[c], c)
        k_t, v_t = k.T, v.T
        kbuf[...] = k_t.astype(BF16)
        vbuf[...] = v_t.astype(BF16)
        knew[...] = k_t
        vnew[...] = v_t
        bgbuf[...] = b_gate
        ubuf[...] = u
        convout_ref[...] = u.reshape(convout_ref.shape[0], t_len, CONV_WIDTH)[:, t_len - (CONV_K - 1):, :]

    n_rows = N_Q_HEADS * pair_rows
    row = lax.broadcasted_iota(jnp.int32, (n_rows, 3 * BLOCK), 0)
    col = lax.broadcasted_iota(jnp.int32, (n_rows, 3 * BLOCK), 1)
    row_b = (row % pair_rows) // t_len
    row_t = row % t_len
    cache_ok = (col < 2 * BLOCK) & (col // BLOCK == row_b) & ((col % BLOCK) > row_t)
    new_col = col - 2 * BLOCK
    sink = _sink_column(sinks_ref, pair_rows)

    def pair_stages(i):
        pair = step * (step_batch // 2) + i
        r0 = pl.multiple_of(pair * pair_rows, pair_rows)
        c0 = pl.multiple_of((pair // chunk_pairs) * BLOCK, BLOCK)
        q = qbuf[:, pl.ds(r0, pair_rows), :].reshape(n_rows, LANES).astype(BF16)
        ck = [ck_ref[2 * i + bb] for bb in range(2)]
        cv = [cv_ref[2 * i + bb] for bb in range(2)]
        keys = jnp.concatenate([c.astype(BF16) for c in ck] + [kbuf[:, pl.ds(c0, BLOCK)]], axis=1)
        vals = jnp.concatenate([c.astype(BF16) for c in cv] + [vbuf[:, pl.ds(c0, BLOCK)]], axis=1)
        s = _dot(q, keys)
        yield
        new_ok = ((col >= 2 * BLOCK) & (new_col // t_len == (pair % chunk_pairs) * 2 + row_b)
                  & (new_col % t_len <= row_t))
        ok = cache_ok | new_ok
        s = jnp.where(ok, s, NEG)
        mx = jnp.maximum(jnp.max(s, axis=-1, keepdims=True), sink)
        e = jnp.where(ok, jnp.exp2(s - mx), 0.0)
        denom = jnp.sum(e, axis=-1, keepdims=True) + jnp.exp2(sink - mx)
        yield
        pv = _dot_t(e.astype(BF16), vals) * (1.0 / denom)
        for c in range(N_QCOL):
            obuf[pl.ds(r0, pair_rows), c * LANES:(c + 1) * LANES] = _merge_heads(pv, pair_rows, c)
        yield
        k_chunk = knew[:, pl.ds(c0, BLOCK)]
        v_chunk = vnew[:, pl.ds(c0, BLOCK)]
        keep = lax.broadcasted_iota(jnp.int32, (KV_WIDTH, BLOCK), 1) < BLOCK - t_len
        for bb in range(2):
            to_tail = (BLOCK - t_len) - t_len * ((pair % chunk_pairs) * 2 + bb)
            kout_ref[2 * i + bb] = jnp.where(keep, pltpu.roll(ck[bb], BLOCK - t_len, axis=1),
                                             pltpu.roll(k_chunk, to_tail, axis=1))
            vout_ref[2 * i + bb] = jnp.where(keep, pltpu.roll(cv[bb], BLOCK - t_len, axis=1),
                                             pltpu.roll(v_chunk, to_tail, axis=1))

    pairs = [pair_stages(i) for i in range(step_batch // 2)]
    for _ in range(4):
        for stages in pairs:
            next(stages, None)

    for n, name in enumerate(LATE_WEIGHTS):
        rows = late_stage[name].shape[1]
        late_copy(n, name, step).wait()
        r0 = pl.multiple_of(step * rows, rows)
        w_vmem[name][pl.ds(r0, rows), :] = late_stage[name][step % 2].astype(BF16)

    @pl.when(step == n_steps - 1)
    def _():
        for n, name in enumerate(LATE_WEIGHTS):
            out_copy(n + 1, name).start()
        u = ubuf[...]
        tok = lax.broadcasted_iota(jnp.int32, u.shape, 0) % t_len
        per_token = lambda r: jnp.broadcast_to(state_ref[:, r:r + 1, :],
                                               (state_ref.shape[0], t_len, CONV_WIDTH)).reshape(u.shape)
        s0, s1 = per_token(0), per_token(1)
        um1 = jnp.where(tok >= 1, pltpu.roll(u, 1, axis=0), s1)
        um2 = jnp.where(tok >= 2, pltpu.roll(u, 2, axis=0), jnp.where(tok == 0, s0, s1))
        cw = refs["conv_w"]
        conv = cw[0] * um2 + cw[1] * um1 + cw[2] * u

        def store(y):
            y_ref[...] = y.reshape(y_ref.shape)

        _run(_tail_steps(x_ref[...].reshape(n_tok, D_MODEL), obuf[...], bgbuf[...] * conv,
                         p_ref[...].reshape(n_tok, PLE_DIM), refs, store))
        for n, name in enumerate(MATMUL_WEIGHTS):
            out_copy(n, name).wait()


def _rope_tables(pos):
    inv_freq = ROPE_THETA ** (-jnp.arange(0, HEAD_DIM, 2, dtype=F32) / HEAD_DIM)
    ang = inv_freq[:, None] * pos.astype(F32)[None, :]
    sin = jnp.sin(ang)
    return jnp.concatenate([jnp.cos(ang), -sin, sin], axis=0)


def _cache_to_kernel(c):
    batch, keys = c.shape[0], c.shape[1]
    return jnp.transpose(c, (0, 2, 3, 1)).reshape(batch, KV_WIDTH, keys)


def _cache_from_kernel(c):
    batch, _, keys = c.shape
    return jnp.transpose(c.reshape(batch, N_KV_HEADS, HEAD_DIM, keys), (0, 3, 1, 2))


def _block_diag_ones(width):
    idx = np.arange(width) // HEAD_DIM
    return jnp.asarray(idx[:, None] == idx[None, :], dtype=BF16)


def _prepare_weights(g_mix_norm, w_in, g_q, g_k, sinks, conv_w, g_attn_out, g_conv_out, w_o,
                     g_mlp_norm, w_up, w_down, g_ple_norm, w_ple_gate, w_ple):
    row = lambda g: g.reshape(1, -1).astype(F32)
    weights = dict(
        g_mix=row(g_mix_norm), w_in=w_in,
        bd_q=_block_diag_ones(ATTN_WIDTH), bd_k=_block_diag_ones(KV_WIDTH),
        g_q=row(g_q), g_k=row(g_k),
        conv_w=conv_w.astype(F32).reshape(CONV_K, 1, CONV_WIDTH),
        g_attn=row(g_attn_out), g_conv=row(g_conv_out), w_o=w_o,
        g_mlp=row(g_mlp_norm), w_up=w_up, w_down=w_down,
        g_ple=row(g_ple_norm), w_gate=w_ple_gate, w_ple=w_ple)
    return sinks.astype(F32), weights


def _resident(a, n_grid):
    zeros = (0,) * a.ndim
    index_map = (lambda b, j: zeros) if n_grid == 2 else (lambda i: zeros)
    return pl.BlockSpec(a.shape, index_map, pipeline_mode=pl.Buffered(1))


def _prompt_layer(x, p, sinks_p, weights):
    batch, seq, _ = x.shape
    tm = SEQ_TILE
    rope = _rope_tables(jnp.arange(seq, dtype=jnp.int32))
    tile = lambda w: pl.BlockSpec((None, tm, w), lambda b, j: (b, j, 0))
    per_batch = lambda r, w: pl.BlockSpec((None, r, w), lambda b, j: (b, 0, 0))
    table = pl.BlockSpec((rope.shape[0], tm), lambda b, j: (0, j))
    in_specs = ([pl.BlockSpec(memory_space=pltpu.SMEM), tile(D_MODEL), tile(PLE_DIM), table]
                + [_resident(w, 2) for w in weights])
    out_shape = (jax.ShapeDtypeStruct((batch, seq, D_MODEL), F32),
                 jax.ShapeDtypeStruct((batch, BLOCK, KV_WIDTH), F32),
                 jax.ShapeDtypeStruct((batch, BLOCK, KV_WIDTH), F32),
                 jax.ShapeDtypeStruct((batch, CONV_K - 1, CONV_WIDTH), F32))
    out_specs = (tile(D_MODEL), per_batch(BLOCK, KV_WIDTH), per_batch(BLOCK, KV_WIDTH),
                 per_batch(CONV_K - 1, CONV_WIDTH))
    scratch = [pltpu.VMEM((BLOCK + tm, KV_WIDTH), BF16), pltpu.VMEM((KV_WIDTH, BLOCK + tm), BF16),
               pltpu.VMEM((SUBLANES + tm, CONV_WIDTH), F32)]
    return pl.pallas_call(
        _prompt_kernel, grid=(batch, seq // tm), in_specs=in_specs, out_specs=out_specs,
        out_shape=out_shape, scratch_shapes=scratch, name="prompt_layer",
        compiler_params=pltpu.CompilerParams(dimension_semantics=("arbitrary", "arbitrary"),
                                             vmem_limit_bytes=VMEM_LIMIT),
    )(sinks_p, x, p, rope, *weights)


def _sample_layer(x, p, cache_k, cache_v, state_conv, sinks_p, weights):
    batch, t_len, _ = x.shape
    n_tok = batch * t_len
    rope = _rope_tables(PAST_LEN + jnp.arange(n_tok, dtype=jnp.int32) % t_len)
    ck = _cache_to_kernel(cache_k)
    cv = _cache_to_kernel(cache_v)
    flat = [x, p, rope, state_conv]
    n_steps = batch // SAMPLE_STEP_BATCH
    small = [weights[n] for n in SMALL_WEIGHTS]
    big = [weights[n] for n in MATMUL_WEIGHTS]
    hbm = pl.BlockSpec(memory_space=pl.ANY)
    cache_block = pl.BlockSpec((SAMPLE_STEP_BATCH, WINDOW, KV_WIDTH), lambda i: (i, 0, 0))
    in_specs = ([pl.BlockSpec(memory_space=pltpu.SMEM)] + [_resident(a, 1) for a in flat]
                + [cache_block, cache_block] + [_resident(w, 1) for w in small] + [hbm] * len(big))
    out_shape = (jax.ShapeDtypeStruct((batch, t_len, D_MODEL), F32),
                 jax.ShapeDtypeStruct((batch, WINDOW, KV_WIDTH), F32),
                 jax.ShapeDtypeStruct((batch, WINDOW, KV_WIDTH), F32),
                 jax.ShapeDtypeStruct((batch, CONV_K - 1, CONV_WIDTH), F32)
                 ) + tuple(jax.ShapeDtypeStruct(w.shape, BF16) for w in big)
    whole = lambda s: pl.BlockSpec(s.shape, lambda i: (0,) * len(s.shape), pipeline_mode=pl.Buffered(1))
    out_specs = (whole(out_shape[0]), cache_block, cache_block, whole(out_shape[3])) + (hbm,) * len(big)
    late = [weights[n] for n in LATE_WEIGHTS]
    scratch = ([pltpu.VMEM((N_Q_HEADS, n_tok, LANES), F32),
                pltpu.VMEM((KV_WIDTH, n_tok), BF16), pltpu.VMEM((KV_WIDTH, n_tok), BF16),
                pltpu.VMEM((KV_WIDTH, n_tok), F32), pltpu.VMEM((KV_WIDTH, n_tok), F32),
                pltpu.VMEM((n_tok, ATTN_WIDTH), F32), pltpu.VMEM((n_tok, CONV_WIDTH), F32),
                pltpu.VMEM((n_tok, CONV_WIDTH), F32)]
               + [pltpu.VMEM(w.shape, BF16) for w in big]
               + [pltpu.VMEM((W_IN_SLOTS, W_IN_CHUNK, IN_WIDTH), F32)]
               + [pltpu.VMEM((2, w.shape[0] // n_steps, w.shape[1]), F32) for w in late]
               + [pltpu.SemaphoreType.DMA((W_IN_SLOTS,)), pltpu.SemaphoreType.DMA((len(late), 2)),
                  pltpu.SemaphoreType.DMA((len(big),))])
    y, k_new, v_new, conv_new, *w_bf16 = pl.pallas_call(
        _sample_kernel, grid=(n_steps,), in_specs=in_specs, out_specs=out_specs,
        out_shape=out_shape, scratch_shapes=scratch, name="sample_layer",
        compiler_params=pltpu.CompilerParams(dimension_semantics=("arbitrary",),
                                             vmem_limit_bytes=SAMPLE_VMEM_LIMIT),
    )(sinks_p, *flat, ck, cv, *small, *big)
    return y, k_new, v_new, conv_new, dict(zip(MATMUL_WEIGHTS, w_bf16))


def kernel(x_prompt, x_sample, p_prompt, p_sample, cache_k, cache_v, state_conv, g_mix_norm, w_in, g_q, g_k,
           sinks, conv_w, g_attn_out, g_conv_out, w_o, g_mlp_norm, w_up, w_down, g_ple_norm, w_ple_gate, w_ple):
    depth = w_in.shape[0]
    yp, ys = x_prompt, x_sample
    outs = [[] for _ in range(6)]
    for i in range(depth):
        sinks_p, weights = _prepare_weights(
            g_mix_norm[i], w_in[i], g_q[i], g_k[i], sinks[i], conv_w[i], g_attn_out[i], g_conv_out[i],
            w_o[i], g_mlp_norm[i], w_up[i], w_down[i], g_ple_norm[i], w_ple_gate[i], w_ple[i])
        ys, ksn, vsn, csn, w_bf16 = _sample_layer(ys, p_sample[i], cache_k[i], cache_v[i], state_conv[i],
                                                  sinks_p, weights)
        yp, kp, vp, cp = _prompt_layer(yp, p_prompt[i], sinks_p,
                                       [w_bf16.get(n, weights[n]) for n in WEIGHT_NAMES])
        for lst, val in zip(outs, (_cache_from_kernel(kp), _cache_from_kernel(vp), cp,
                                   _cache_from_kernel(ksn), _cache_from_kernel(vsn), csn)):
            lst.append(val)
    return (yp, ys) + tuple(jnp.stack(o) for o in outs)
```

```python
import jax
import jax.numpy as jnp
import numpy as np
from jax import lax
from jax.experimental import pallas as pl
from jax.experimental.pallas import tpu as pltpu

D_MODEL = 1024
HEAD_DIM = 64
N_Q_HEADS = 8
N_KV_HEADS = 2
GQA_GROUP = N_Q_HEADS // N_KV_HEADS
ATTN_WIDTH = N_Q_HEADS * HEAD_DIM
KV_WIDTH = N_KV_HEADS * HEAD_DIM
CONV_WIDTH = D_MODEL - ATTN_WIDTH
CONV_K = 3
WINDOW = 128
BLOCK = 128
ROPE_THETA = 10000.0
D_FF = 4 * D_MODEL
PLE_DIM = 256
EPS = 1e-6
NEG = -1e30
LOG2_E = 1.4426950408889634
PAST_LEN = 16384
IN_WIDTH = ATTN_WIDTH + 2 * KV_WIDTH + 3 * CONV_WIDTH

LANES = 128
SUBLANES = 8
N_QCOL = ATTN_WIDTH // LANES
SEQ_TILE = 512
SUB_TILES = (256, 256)
FF_CHUNK = 1024
MLP_LOOKAHEAD = 2
FRONT_STAGES = 2
ATTN_STAGES = 3
SAMPLE_STEP_BATCH = 8
VMEM_LIMIT = 56 * 1024 * 1024
SAMPLE_VMEM_LIMIT = 60 * 1024 * 1024

O_K = ATTN_WIDTH
O_V = O_K + KV_WIDTH
O_B = O_V + KV_WIDTH
O_C = O_B + CONV_WIDTH
O_H = O_C + CONV_WIDTH

BF16 = jnp.bfloat16
F32 = jnp.float32


def _dot(a, b):
    return jnp.dot(a, b, preferred_element_type=F32)


def _dot_t(a, b):
    return lax.dot_general(a, b, (((1,), (1,)), ((), ())), preferred_element_type=F32)


def _rms(x, g):
    return x * lax.rsqrt(jnp.mean(x * x, axis=-1, keepdims=True) + EPS) * g


def _head_norm_rope(t, ones_bd, g, cos, sin):
    ssq = _dot((t * t).astype(BF16), ones_bd)
    t = t * lax.rsqrt(ssq * (1.0 / HEAD_DIM) + EPS)
    lane = lax.broadcasted_iota(jnp.int32, (t.shape[0], LANES), 1)
    first_half = (lane & (HEAD_DIM - 1)) < HEAD_DIM // 2
    cols = []
    for m in range(t.shape[1] // LANES):
        c = t[:, m * LANES:(m + 1) * LANES] * g
        up = pltpu.roll(c, LANES - HEAD_DIM // 2, axis=1)
        dn = pltpu.roll(c, HEAD_DIM // 2, axis=1)
        cols.append(c * cos + jnp.where(first_half, up, dn) * sin)
    return cols


def _rope_rows(table):
    half = HEAD_DIM // 2
    cos = jnp.concatenate([table[:half]] * (LANES // half), axis=0).T
    sin = jnp.concatenate([table[half:]] * (LANES // HEAD_DIM), axis=0).T
    return cos, sin


def _front_steps(x, cos, sin, refs):
    h = _rms(x, refs["g_mix"][...]).astype(BF16)
    z = _dot(h, refs["w_in"][:, :O_B])
    yield
    zc = _dot(h, refs["w_in"][:, O_B:])
    two_heads = lambda g: jnp.concatenate([g] * (LANES // HEAD_DIM), axis=1)
    g_q = two_heads(refs["g_q"][...]) * (HEAD_DIM ** -0.5 * LOG2_E)
    qcols = _head_norm_rope(z[:, :O_K], refs["bd_q"][...], g_q, cos, sin)
    (k,) = _head_norm_rope(z[:, O_K:O_V], refs["bd_k"][...], two_heads(refs["g_k"][...]), cos, sin)
    v = z[:, O_V:O_B]
    b_gate = zc[:, :CONV_WIDTH]
    u = zc[:, CONV_WIDTH:2 * CONV_WIDTH] * zc[:, 2 * CONV_WIDTH:]
    return qcols, k, v, b_gate, u


def _front(x, cos, sin, refs):
    steps = _front_steps(x, cos, sin, refs)
    try:
        while True:
            next(steps)
    except StopIteration as done:
        return done.value


def _tail_steps(x, o_attn, o_conv, p, refs, store, lookahead=1, embed_last=False, emb=None):
    embed = lambda: _dot(p.astype(BF16), refs["w_ple"][...])
    if emb is None and not embed_last:
        emb = embed()
    mixed = jnp.concatenate([_rms(o_attn, refs["g_attn"][...]), _rms(o_conv, refs["g_conv"][...])], axis=1)
    x = x + _dot(mixed.astype(BF16), refs["w_o"][...])
    hm = _rms(x, refs["g_mlp"][...]).astype(BF16)
    yield

    def hidden(c):
        up = _dot(hm, refs["w_up"][:, c * FF_CHUNK:(c + 1) * FF_CHUNK])
        return jnp.square(jnp.maximum(up.astype(BF16), 0.0))

    n_chunks = D_FF // FF_CHUNK
    acts = []
    for c in range(min(lookahead, n_chunks)):
        acts.append(hidden(c))
        yield
    for c in range(n_chunks):
        if c + lookahead < n_chunks:
            acts.append(hidden(c + lookahead))
            yield
        x = x + _dot(acts[c], refs["w_down"][c * FF_CHUNK:(c + 1) * FF_CHUNK, :])
        yield
    if emb is None:
        emb = embed()
    gate = jax.nn.sigmoid(_dot(_rms(x, refs["g_ple"][...]).astype(BF16), refs["w_gate"][...]))
    store(x + gate * emb)


def _run(steps):
    for _ in steps:
        pass


def _split_heads(col, c):
    group = (2 * c) // GQA_GROUP
    lo = lax.broadcasted_iota(jnp.int32, col.shape, 1) < HEAD_DIM
    swapped = pltpu.roll(col, HEAD_DIM, axis=1)
    if group == 0:
        return jnp.where(lo, col, 0.0), jnp.where(lo, swapped, 0.0)
    return jnp.where(lo, 0.0, swapped), jnp.where(lo, 0.0, col)


def _sink_column(sinks_ref, rows_per_head):
    return jnp.concatenate(
        [jnp.full((rows_per_head, 1), sinks_ref[i] * LOG2_E, F32) for i in range(N_Q_HEADS)], axis=0)


def _merge_heads(pv, rows_per_head, c):
    group = (2 * c) // GQA_GROUP
    lo = lax.broadcasted_iota(jnp.int32, (rows_per_head, LANES), 1) < HEAD_DIM
    a = pv[(2 * c) * rows_per_head:(2 * c + 1) * rows_per_head]
    b = pv[(2 * c + 1) * rows_per_head:(2 * c + 2) * rows_per_head]
    if group == 0:
        return jnp.where(lo, a, pltpu.roll(b, HEAD_DIM, axis=1))
    return jnp.where(lo, pltpu.roll(a, HEAD_DIM, axis=1), b)


WEIGHT_NAMES = ("g_mix", "w_in", "bd_q", "bd_k", "g_q", "g_k", "conv_w", "g_attn", "g_conv", "w_o",
                "g_mlp", "w_up", "w_down", "g_ple", "w_gate", "w_ple")
MATMUL_WEIGHTS = ("w_in", "w_o", "w_up", "w_down", "w_gate", "w_ple")
LATE_WEIGHTS = MATMUL_WEIGHTS[1:]
SMALL_WEIGHTS = tuple(n for n in WEIGHT_NAMES if n not in MATMUL_WEIGHTS)
W_IN_CHUNK = 64
W_IN_SLOTS = 4
LATE_DMA_PRIORITY = 1


def _prompt_kernel(sinks_ref, x_ref, p_ref, rope_ref, *rest):
    nw = len(WEIGHT_NAMES)
    refs = dict(zip(WEIGHT_NAMES, rest[:nw]))
    y_ref, kout_ref, vout_ref, convout_ref, kbuf, vbuf, ubuf = rest[nw:]
    tm = x_ref.shape[0]
    j = pl.program_id(1)

    @pl.when(j == 0)
    def _():
        kbuf[0:BLOCK, :] = jnp.zeros((BLOCK, KV_WIDTH), BF16)
        vbuf[:, 0:BLOCK] = jnp.zeros((KV_WIDTH, BLOCK), BF16)
        ubuf[0:SUBLANES, :] = jnp.zeros((SUBLANES, CONV_WIDTH), F32)

    kj = lax.broadcasted_iota(jnp.int32, (BLOCK, N_Q_HEADS * BLOCK), 0)
    qi = lax.broadcasted_iota(jnp.int32, (BLOCK, N_Q_HEADS * BLOCK), 1) & (BLOCK - 1)
    own = kj <= qi
    sink = jnp.concatenate([jnp.full((1, BLOCK), sinks_ref[i] * LOG2_E, F32) for i in range(N_Q_HEADS)],
                           axis=1)
    cw = refs["conv_w"]

    last = {}

    def sub_tile_steps(r0, n):
        rows = slice(r0, r0 + n)
        emb = _dot(p_ref[rows, :].astype(BF16), refs["w_ple"][...]) if r0 == 0 else None
        x = x_ref[rows, :]
        qcols, k, v, b_gate, u = yield from _front_steps(
            x, *_rope_rows(rope_ref[:, rows]), refs)
        kbuf[BLOCK + r0:BLOCK + r0 + n, :] = k.astype(BF16)
        v_t = v.T
        vbuf[:, BLOCK + r0:BLOCK + r0 + n] = v_t.astype(BF16)
        ubuf[SUBLANES + r0:SUBLANES + r0 + n, :] = u
        last.update(k=k, v_t=v_t, u=u, n=n)
        yield

        o_blocks = {}

        def block_stages(i):
            g = r0 // BLOCK + i
            qs = []
            for c in range(N_QCOL):
                qs.extend(_split_heads(qcols[c][i * BLOCK:(i + 1) * BLOCK], c))
            qstack = jnp.concatenate(qs, axis=0).astype(BF16)
            s2 = _dot_t(kbuf[g * BLOCK:(g + 2) * BLOCK, :], qstack)
            yield
            s_prev = s2[:BLOCK]
            if g == 0:
                s_prev = jnp.where(j == 0, NEG, s_prev)
            s = jnp.where(own, s2[BLOCK:], s_prev)
            mx = jnp.maximum(jnp.max(s, axis=0, keepdims=True), sink)
            e = jnp.exp2(s - mx)
            denom = jnp.sum(e, axis=0, keepdims=True) + jnp.exp2(sink - mx)
            e2 = jnp.concatenate([jnp.where(own, 0.0, e), jnp.where(own, e, 0.0)], axis=0).astype(BF16)
            yield
            pv = _dot(vbuf[:, g * BLOCK:(g + 2) * BLOCK], e2) * (1.0 / denom)
            cols = []
            for c in range(N_QCOL):
                d0 = (2 * c) // GQA_GROUP * HEAD_DIM
                col_t = jnp.concatenate([pv[d0:d0 + HEAD_DIM, (2 * c) * BLOCK:(2 * c + 1) * BLOCK],
                                         pv[d0:d0 + HEAD_DIM, (2 * c + 1) * BLOCK:(2 * c + 2) * BLOCK]], axis=0)
                cols.append(col_t.T)
            o_blocks[i] = jnp.concatenate(cols, axis=1)

        blocks = [block_stages(i) for i in range(n // BLOCK)]
        for _ in range(ATTN_STAGES):
            for stages in blocks:
                next(stages, None)
            yield
        o_attn = jnp.concatenate([o_blocks[i] for i in range(n // BLOCK)], axis=0)

        conv = (cw[0] * ubuf[SUBLANES - 2 + r0:SUBLANES - 2 + r0 + n, :]
                + cw[1] * ubuf[SUBLANES - 1 + r0:SUBLANES - 1 + r0 + n, :]
                + cw[2] * u)

        def store(y):
            y_ref[rows, :] = y

        yield from _tail_steps(x, o_attn, b_gate * conv, p_ref[rows, :], refs, store, MLP_LOOKAHEAD,
                               embed_last=r0 + n == tm, emb=emb)

    assert sum(SUB_TILES) == tm
    starts = [sum(SUB_TILES[:t]) for t in range(len(SUB_TILES))]
    tiles = [sub_tile_steps(r0, n) for r0, n in zip(starts, SUB_TILES)]
    first, second = tiles
    n_stages = FRONT_STAGES + ATTN_STAGES + 2 + 2 * (D_FF // FF_CHUNK)
    order = [first] * FRONT_STAGES + [second] + [first] * ATTN_STAGES
    alternating = n_stages - FRONT_STAGES - ATTN_STAGES - 1
    order += [second, first] * alternating
    order += [second] * (n_stages - 2 - alternating) + [first, second]
    for steps in order:
        next(steps, None)

    n_last = last["n"]
    kout_ref[...] = last["k"][n_last - BLOCK:, :].T
    vout_ref[...] = last["v_t"][:, n_last - BLOCK:]
    convout_ref[...] = last["u"][n_last - (CONV_K - 1):, :]
    kbuf[0:BLOCK, :] = kbuf[tm:tm + BLOCK, :]
    vbuf[:, 0:BLOCK] = vbuf[:, tm:tm + BLOCK]
    ubuf[0:SUBLANES, :] = ubuf[tm:tm + SUBLANES, :]


def _sample_kernel(sinks_ref, x_ref, p_ref, rope_ref, state_ref, ck_ref, cv_ref, *rest):
    ns, nm, nl = len(SMALL_WEIGHTS), len(MATMUL_WEIGHTS), len(LATE_WEIGHTS)
    refs = dict(zip(SMALL_WEIGHTS, rest[:ns]))
    w_f32 = dict(zip(MATMUL_WEIGHTS, rest[ns:ns + nm]))
    y_ref, kout_ref, vout_ref, convout_ref = rest[ns + nm:ns + nm + 4]
    w_out = dict(zip(MATMUL_WEIGHTS, rest[ns + nm + 4:ns + 2 * nm + 4]))
    scratch = rest[ns + 2 * nm + 4:]
    qbuf, kbuf, vbuf, knew, vnew, obuf, bgbuf, ubuf = scratch[:8]
    w_vmem = dict(zip(MATMUL_WEIGHTS, scratch[8:8 + nm]))
    in_stage = scratch[8 + nm]
    late_stage = dict(zip(LATE_WEIGHTS, scratch[9 + nm:9 + nm + nl]))
    in_sem, late_sem, out_sem = scratch[9 + nm + nl:]
    refs.update(w_vmem)
    t_len = x_ref.shape[1]
    n_tok = x_ref.shape[0] * t_len
    step = pl.program_id(0)
    n_steps = pl.num_programs(0)
    step_batch = ck_ref.shape[0]
    pair_rows = 2 * t_len
    chunk_pairs = BLOCK // pair_rows

    n_slots = in_stage.shape[0]

    def in_copy(i):
        return pltpu.make_async_copy(w_f32["w_in"].at[pl.ds(i * W_IN_CHUNK, W_IN_CHUNK), :],
                                     in_stage.at[i % n_slots], in_sem.at[i % n_slots])

    def late_copy(n, name, s):
        rows = late_stage[name].shape[1]
        return pltpu.make_async_copy(w_f32[name].at[pl.ds(s * rows, rows), :], late_stage[name].at[s % 2],
                                     late_sem.at[n, s % 2])

    def out_copy(n, name):
        return pltpu.make_async_copy(w_vmem[name], w_out[name], out_sem.at[n])

    @pl.when(step + 1 < n_steps)
    def _():
        for n, name in enumerate(LATE_WEIGHTS):
            late_copy(n, name, step + 1).start(priority=LATE_DMA_PRIORITY)

    @pl.when(step == 0)
    def _():
        for n, name in enumerate(LATE_WEIGHTS):
            late_copy(n, name, step).start(priority=LATE_DMA_PRIORITY)
        n_chunks = D_MODEL // W_IN_CHUNK
        for i in range(n_slots):
            in_copy(i).start()
        for i in range(n_chunks):
            in_copy(i).wait()
            w_vmem["w_in"][i * W_IN_CHUNK:(i + 1) * W_IN_CHUNK, :] = in_stage[i % n_slots].astype(BF16)
            if i + n_slots < n_chunks:
                in_copy(i + n_slots).start()
        out_copy(0, "w_in").start()
        qcols, k, v, b_gate, u = _front(x_ref[...].reshape(n_tok, D_MODEL),
                                        *_rope_rows(rope_ref[...]), refs)
        for c in range(N_QCOL):
            qbuf[2 * c], qbuf[2 * c + 1] = _split_heads(qcols[c], c)
        k_t, v_t = k.T, v.T
        kbuf[...] = k_t.astype(BF16)
        vbuf[...] = v_t.astype(BF16)
        knew[...] = k_t
        vnew[...] = v_t
        bgbuf[...] = b_gate
        ubuf[...] = u
        convout_ref[...] = u.reshape(convout_ref.shape[0], t_len, CONV_WIDTH)[:, t_len - (CONV_K - 1):, :]

    n_rows = N_Q_HEADS * pair_rows
    row = lax.broadcasted_iota(jnp.int32, (n_rows, 3 * BLOCK), 0)
    col = lax.broadcasted_iota(jnp.int32, (n_rows, 3 * BLOCK), 1)
    row_b = (row % pair_rows) // t_len
    row_t = row % t_len
    cache_ok = (col < 2 * BLOCK) & (col // BLOCK == row_b) & ((col % BLOCK) > row_t)
    new_col = col - 2 * BLOCK
    sink = _sink_column(sinks_ref, pair_rows)

    def pair_stages(i):
        pair = step * (step_batch // 2) + i
        r0 = pl.multiple_of(pair * pair_rows, pair_rows)
        c0 = pl.multiple_of((pair // chunk_pairs) * BLOCK, BLOCK)
        q = qbuf[:, pl.ds(r0, pair_rows), :].reshape(n_rows, LANES).astype(BF16)
        ck = [ck_ref[2 * i + bb] for bb in range(2)]
        cv = [cv_ref[2 * i + bb] for bb in range(2)]
        keys = jnp.concatenate([c.astype(BF16) for c in ck] + [kbuf[:, pl.ds(c0, BLOCK)]], axis=1)
        vals = jnp.concatenate([c.astype(BF16) for c in cv] + [vbuf[:, pl.ds(c0, BLOCK)]], axis=1)
        s = _dot(q, keys)
        yield
        new_ok = ((col >= 2 * BLOCK) & (new_col // t_len == (pair % chunk_pairs) * 2 + row_b)
                  & (new_col % t_len <= row_t))
        ok = cache_ok | new_ok
        s = jnp.where(ok, s, NEG)
        mx = jnp.maximum(jnp.max(s, axis=-1, keepdims=True), sink)
        e = jnp.where(ok, jnp.exp2(s - mx), 0.0)
        denom = jnp.sum(e, axis=-1, keepdims=True) + jnp.exp2(sink - mx)
        yield
        pv = _dot_t(e.astype(BF16), vals) * (1.0 / denom)
        for c in range(N_QCOL):
            obuf[pl.ds(r0, pair_rows), c * LANES:(c + 1) * LANES] = _merge_heads(pv, pair_rows, c)
        yield
        k_chunk = knew[:, pl.ds(c0, BLOCK)]
        v_chunk = vnew[:, pl.ds(c0, BLOCK)]
        keep = lax.broadcasted_iota(jnp.int32, (KV_WIDTH, BLOCK), 1) < BLOCK - t_len
        for bb in range(2):
            to_tail = (BLOCK - t_len) - t_len * ((pair % chunk_pairs) * 2 + bb)
            kout_ref[2 * i + bb] = jnp.where(keep, pltpu.roll(ck[bb], BLOCK - t_len, axis=1),
                                             pltpu.roll(k_chunk, to_tail, axis=1))
            vout_ref[2 * i + bb] = jnp.where(keep, pltpu.roll(cv[bb], BLOCK - t_len, axis=1),
                                             pltpu.roll(v_chunk, to_tail, axis=1))

    pairs = [pair_stages(i) for i in range(step_batch // 2)]
    for _ in range(4):
        for stages in pairs:
            next(stages, None)

    for n, name in enumerate(LATE_WEIGHTS):
        rows = late_stage[name].shape[1]
        late_copy(n, name, step).wait()
        r0 = pl.multiple_of(step * rows, rows)
        w_vmem[name][pl.ds(r0, rows), :] = late_stage[name][step % 2].astype(BF16)

    @pl.when(step == n_steps - 1)
    def _():
        for n, name in enumerate(LATE_WEIGHTS):
            out_copy(n + 1, name).start()
        u = ubuf[...]
        tok = lax.broadcasted_iota(jnp.int32, u.shape, 0) % t_len
        per_token = lambda r: jnp.broadcast_to(state_ref[:, r:r + 1, :],
                                               (state_ref.shape[0], t_len, CONV_WIDTH)).reshape(u.shape)
        s0, s1 = per_token(0), per_token(1)
        um1 = jnp.where(tok >= 1, pltpu.roll(u, 1, axis=0), s1)
        um2 = jnp.where(tok >= 2, pltpu.roll(u, 2, axis=0), jnp.where(tok == 0, s0, s1))
        cw = refs["conv_w"]
        conv = cw[0] * um2 + cw[1] * um1 + cw[2] * u

        def store(y):
            y_ref[...] = y.reshape(y_ref.shape)

        _run(_tail_steps(x_ref[...].reshape(n_tok, D_MODEL), obuf[...], bgbuf[...] * conv,
                         p_ref[...].reshape(n_tok, PLE_DIM), refs, store))
        for n, name in enumerate(MATMUL_WEIGHTS):
            out_copy(n, name).wait()


def _rope_tables(pos):
    inv_freq = ROPE_THETA ** (-jnp.arange(0, HEAD_DIM, 2, dtype=F32) / HEAD_DIM)
    ang = inv_freq[:, None] * pos.astype(F32)[None, :]
    sin = jnp.sin(ang)
    return jnp.concatenate([jnp.cos(ang), -sin, sin], axis=0)


def _cache_to_kernel(c):
    batch, keys = c.shape[0], c.shape[1]
    return jnp.transpose(c, (0, 2, 3, 1)).reshape(batch, KV_WIDTH, keys)


def _cache_from_kernel(c):
    batch, _, keys = c.shape
    return jnp.transpose(c.reshape(batch, N_KV_HEADS, HEAD_DIM, keys), (0, 3, 1, 2))


def _block_diag_ones(width):
    idx = np.arange(width) // HEAD_DIM
    return jnp.asarray(idx[:, None] == idx[None, :], dtype=BF16)


def _prepare_weights(g_mix_norm, w_in, g_q, g_k, sinks, conv_w, g_attn_out, g_conv_out, w_o,
                     g_mlp_norm, w_up, w_down, g_ple_norm, w_ple_gate, w_ple):
    row = lambda g: g.reshape(1, -1).astype(F32)
    weights = dict(
        g_mix=row(g_mix_norm), w_in=w_in,
        bd_q=_block_diag_ones(ATTN_WIDTH), bd_k=_block_diag_ones(KV_WIDTH),
        g_q=row(g_q), g_k=row(g_k),
        conv_w=conv_w.astype(F32).reshape(CONV_K, 1, CONV_WIDTH),
        g_attn=row(g_attn_out), g_conv=row(g_conv_out), w_o=w_o,
        g_mlp=row(g_mlp_norm), w_up=w_up, w_down=w_down,
        g_ple=row(g_ple_norm), w_gate=w_ple_gate, w_ple=w_ple)
    return sinks.astype(F32), weights


def _resident(a, n_grid):
    zeros = (0,) * a.ndim
    index_map = (lambda b, j: zeros) if n_grid == 2 else (lambda i: zeros)
    return pl.BlockSpec(a.shape, index_map, pipeline_mode=pl.Buffered(1))


def _prompt_layer(x, p, sinks_p, weights):
    batch, seq, _ = x.shape
    tm = SEQ_TILE
    rope = _rope_tables(jnp.arange(seq, dtype=jnp.int32))
    tile = lambda w: pl.BlockSpec((None, tm, w), lambda b, j: (b, j, 0))
    per_batch = lambda r, w: pl.BlockSpec((None, r, w), lambda b, j: (b, 0, 0))
    table = pl.BlockSpec((rope.shape[0], tm), lambda b, j: (0, j))
    in_specs = ([pl.BlockSpec(memory_space=pltpu.SMEM), tile(D_MODEL), tile(PLE_DIM), table]
                + [_resident(w, 2) for w in weights])
    out_shape = (jax.ShapeDtypeStruct((batch, seq, D_MODEL), F32),
                 jax.ShapeDtypeStruct((batch, BLOCK, KV_WIDTH), F32),
                 jax.ShapeDtypeStruct((batch, BLOCK, KV_WIDTH), F32),
                 jax.ShapeDtypeStruct((batch, CONV_K - 1, CONV_WIDTH), F32))
    out_specs = (tile(D_MODEL), per_batch(BLOCK, KV_WIDTH), per_batch(BLOCK, KV_WIDTH),
                 per_batch(CONV_K - 1, CONV_WIDTH))
    scratch = [pltpu.VMEM((BLOCK + tm, KV_WIDTH), BF16), pltpu.VMEM((KV_WIDTH, BLOCK + tm), BF16),
               pltpu.VMEM((SUBLANES + tm, CONV_WIDTH), F32)]
    return pl.pallas_call(
        _prompt_kernel, grid=(batch, seq // tm), in_specs=in_specs, out_specs=out_specs,
        out_shape=out_shape, scratch_shapes=scratch, name="prompt_layer",
        compiler_params=pltpu.CompilerParams(dimension_semantics=("arbitrary", "arbitrary"),
                                             vmem_limit_bytes=VMEM_LIMIT),
    )(sinks_p, x, p, rope, *weights)


def _sample_layer(x, p, cache_k, cache_v, state_conv, sinks_p, weights):
    batch, t_len, _ = x.shape
    n_tok = batch * t_len
    rope = _rope_tables(PAST_LEN + jnp.arange(n_tok, dtype=jnp.int32) % t_len)
    ck = _cache_to_kernel(cache_k)
    cv = _cache_to_kernel(cache_v)
    flat = [x, p, rope, state_conv]
    n_steps = batch // SAMPLE_STEP_BATCH
    small = [weights[n] for n in SMALL_WEIGHTS]
    big = [weights[n] for n in MATMUL_WEIGHTS]
    hbm = pl.BlockSpec(memory_space=pl.ANY)
    cache_block = pl.BlockSpec((SAMPLE_STEP_BATCH, WINDOW, KV_WIDTH), lambda i: (i, 0, 0))
    in_specs = ([pl.BlockSpec(memory_space=pltpu.SMEM)] + [_resident(a, 1) for a in flat]
                + [cache_block, cache_block] + [_resident(w, 1) for w in small] + [hbm] * len(big))
    out_shape = (jax.ShapeDtypeStruct((batch, t_len, D_MODEL), F32),
                 jax.ShapeDtypeStruct((batch, WINDOW, KV_WIDTH), F32),
                 jax.ShapeDtypeStruct((batch, WINDOW, KV_WIDTH), F32),
                 jax.ShapeDtypeStruct((batch, CONV_K - 1, CONV_WIDTH), F32)
                 ) + tuple(jax.ShapeDtypeStruct(w.shape, BF16) for w in big)
    whole = lambda s: pl.BlockSpec(s.shape, lambda i: (0,) * len(s.shape), pipeline_mode=pl.Buffered(1))
    out_specs = (whole(out_shape[0]), cache_block, cache_block, whole(out_shape[3])) + (hbm,) * len(big)
    late = [weights[n] for n in LATE_WEIGHTS]
    scratch = ([pltpu.VMEM((N_Q_HEADS, n_tok, LANES), F32),
                pltpu.VMEM((KV_WIDTH, n_tok), BF16), pltpu.VMEM((KV_WIDTH, n_tok), BF16),
                pltpu.VMEM((KV_WIDTH, n_tok), F32), pltpu.VMEM((KV_WIDTH, n_tok), F32),
                pltpu.VMEM((n_tok, ATTN_WIDTH), F32), pltpu.VMEM((n_tok, CONV_WIDTH), F32),
                pltpu.VMEM((n_tok, CONV_WIDTH), F32)]
               + [pltpu.VMEM(w.shape, BF16) for w in big]
               + [pltpu.VMEM((W_IN_SLOTS, W_IN_CHUNK, IN_WIDTH), F32)]
               + [pltpu.VMEM((2, w.shape[0] // n_steps, w.shape[1]), F32) for w in late]
               + [pltpu.SemaphoreType.DMA((W_IN_SLOTS,)), pltpu.SemaphoreType.DMA((len(late), 2)),
                  pltpu.SemaphoreType.DMA((len(big),))])
    y, k_new, v_new, conv_new, *w_bf16 = pl.pallas_call(
        _sample_kernel, grid=(n_steps,), in_specs=in_specs, out_specs=out_specs,
        out_shape=out_shape, scratch_shapes=scratch, name="sample_layer",
        compiler_params=pltpu.CompilerParams(dimension_semantics=("arbitrary",),
                                             vmem_limit_bytes=SAMPLE_VMEM_LIMIT),
    )(sinks_p, *flat, ck, cv, *small, *big)
    return y, k_new, v_new, conv_new, dict(zip(MATMUL_WEIGHTS, w_bf16))


def kernel(x_prompt, x_sample, p_prompt, p_sample, cache_k, cache_v, state_conv, g_mix_norm, w_in, g_q, g_k,
           sinks, conv_w, g_attn_out, g_conv_out, w_o, g_mlp_norm, w_up, w_down, g_ple_norm, w_ple_gate, w_ple):
    depth = w_in.shape[0]
    yp, ys = x_prompt, x_sample
    outs = [[] for _ in range(6)]
    for i in range(depth):
        sinks_p, weights = _prepare_weights(
            g_mix_norm[i], w_in[i], g_q[i], g_k[i], sinks[i], conv_w[i], g_attn_out[i], g_conv_out[i],
            w_o[i], g_mlp_norm[i], w_up[i], w_down[i], g_ple_norm[i], w_ple_gate[i], w_ple[i])
        ys, ksn, vsn, csn, w_bf16 = _sample_layer(ys, p_sample[i], cache_k[i], cache_v[i], state_conv[i],
                                                  sinks_p, weights)
        yp, kp, vp, cp = _prompt_layer(yp, p_prompt[i], sinks_p,
                                       [w_bf16.get(n, weights[n]) for n in WEIGHT_NAMES])
        for lst, val in zip(outs, (_cache_from_kernel(kp), _cache_from_kernel(vp), cp,
                                   _cache_from_kernel(ksn), _cache_from_kernel(vsn), csn)):
            lst.append(val)
    return (yp, ys) + tuple(jnp.stack(o) for o in outs)
```

```python
import jax
import jax.numpy as jnp
import numpy as np
from jax import lax
from jax.experimental import pallas as pl
from jax.experimental.pallas import tpu as pltpu

D_MODEL = 1024
HEAD_DIM = 64
N_Q_HEADS = 8
N_KV_HEADS = 2
GQA_GROUP = N_Q_HEADS // N_KV_HEADS
ATTN_WIDTH = N_Q_HEADS * HEAD_DIM
KV_WIDTH = N_KV_HEADS * HEAD_DIM
CONV_WIDTH = D_MODEL - ATTN_WIDTH
CONV_K = 3
WINDOW = 128
BLOCK = 128
ROPE_THETA = 10000.0
D_FF = 4 * D_MODEL
PLE_DIM = 256
EPS = 1e-6
NEG = -1e30
LOG2_E = 1.4426950408889634
PAST_LEN = 16384
IN_WIDTH = ATTN_WIDTH + 2 * KV_WIDTH + 3 * CONV_WIDTH

LANES = 128
SUBLANES = 8
N_QCOL = ATTN_WIDTH // LANES
SEQ_TILE = 512
SUB_TILES = (256, 256)
FF_CHUNK = 1024
MLP_LOOKAHEAD = 2
FRONT_STAGES = 2
ATTN_STAGES = 3
SAMPLE_STEP_BATCH = 8
VMEM_LIMIT = 56 * 1024 * 1024
SAMPLE_VMEM_LIMIT = 60 * 1024 * 1024

O_K = ATTN_WIDTH
O_V = O_K + KV_WIDTH
O_B = O_V + KV_WIDTH
O_C = O_B + CONV_WIDTH
O_H = O_C + CONV_WIDTH

BF16 = jnp.bfloat16
F32 = jnp.float32


def _dot(a, b):
    return jnp.dot(a, b, preferred_element_type=F32)


def _dot_t(a, b):
    return lax.dot_general(a, b, (((1,), (1,)), ((), ())), preferred_element_type=F32)


def _rms(x, g):
    return x * lax.rsqrt(jnp.mean(x * x, axis=-1, keepdims=True) + EPS) * g


def _head_norm_rope(t, ones_bd, g, cos, sin):
    ssq = _dot((t * t).astype(BF16), ones_bd)
    t = t * lax.rsqrt(ssq * (1.0 / HEAD_DIM) + EPS)
    lane = lax.broadcasted_iota(jnp.int32, (t.shape[0], LANES), 1)
    first_half = (lane & (HEAD_DIM - 1)) < HEAD_DIM // 2
    cols = []
    for m in range(t.shape[1] // LANES):
        c = t[:, m * LANES:(m + 1) * LANES] * g
        up = pltpu.roll(c, LANES - HEAD_DIM // 2, axis=1)
        dn = pltpu.roll(c, HEAD_DIM // 2, axis=1)
        cols.append(c * cos + jnp.where(first_half, up, dn) * sin)
    return cols


def _rope_rows(table):
    half = HEAD_DIM // 2
    cos = jnp.concatenate([table[:half]] * (LANES // half), axis=0).T
    sin = jnp.concatenate([table[half:]] * (LANES // HEAD_DIM), axis=0).T
    return cos, sin


def _front_steps(x, cos, sin, refs):
    h = _rms(x, refs["g_mix"][...]).astype(BF16)
    z = _dot(h, refs["w_in"][:, :O_B])
    yield
    zc = _dot(h, refs["w_in"][:, O_B:])
    two_heads = lambda g: jnp.concatenate([g] * (LANES // HEAD_DIM), axis=1)
    g_q = two_heads(refs["g_q"][...]) * (HEAD_DIM ** -0.5 * LOG2_E)
    qcols = _head_norm_rope(z[:, :O_K], refs["bd_q"][...], g_q, cos, sin)
    (k,) = _head_norm_rope(z[:, O_K:O_V], refs["bd_k"][...], two_heads(refs["g_k"][...]), cos, sin)
    v = z[:, O_V:O_B]
    b_gate = zc[:, :CONV_WIDTH]
    u = zc[:, CONV_WIDTH:2 * CONV_WIDTH] * zc[:, 2 * CONV_WIDTH:]
    return qcols, k, v, b_gate, u


def _front(x, cos, sin, refs):
    steps = _front_steps(x, cos, sin, refs)
    try:
        while True:
            next(steps)
    except StopIteration as done:
        return done.value


def _tail_steps(x, o_attn, o_conv, p, refs, store, lookahead=1, embed_last=False, emb=None):
    embed = lambda: _dot(p.astype(BF16), refs["w_ple"][...])
    if emb is None and not embed_last:
        emb = embed()
    mixed = jnp.concatenate([_rms(o_attn, refs["g_attn"][...]), _rms(o_conv, refs["g_conv"][...])], axis=1)
    x = x + _dot(mixed.astype(BF16), refs["w_o"][...])
    hm = _rms(x, refs["g_mlp"][...]).astype(BF16)
    yield

    def hidden(c):
        up = _dot(hm, refs["w_up"][:, c * FF_CHUNK:(c + 1) * FF_CHUNK])
        return jnp.square(jnp.maximum(up.astype(BF16), 0.0))

    n_chunks = D_FF // FF_CHUNK
    acts = []
    for c in range(min(lookahead, n_chunks)):
        acts.append(hidden(c))
        yield
    for c in range(n_chunks):
        if c + lookahead < n_chunks:
            acts.append(hidden(c + lookahead))
            yield
        x = x + _dot(acts[c], refs["w_down"][c * FF_CHUNK:(c + 1) * FF_CHUNK, :])
        yield
    if emb is None:
        emb = embed()
    gate = jax.nn.sigmoid(_dot(_rms(x, refs["g_ple"][...]).astype(BF16), refs["w_gate"][...]))
    store(x + gate * emb)


def _run(steps):
    for _ in steps:
        pass


def _split_heads(col, c):
    group = (2 * c) // GQA_GROUP
    lo = lax.broadcasted_iota(jnp.int32, col.shape, 1) < HEAD_DIM
    swapped = pltpu.roll(col, HEAD_DIM, axis=1)
    if group == 0:
        return jnp.where(lo, col, 0.0), jnp.where(lo, swapped, 0.0)
    return jnp.where(lo, 0.0, swapped), jnp.where(lo, 0.0, col)


def _sink_column(sinks_ref, rows_per_head):
    return jnp.concatenate(
        [jnp.full((rows_per_head, 1), sinks_ref[i] * LOG2_E, F32) for i in range(N_Q_HEADS)], axis=0)


def _merge_heads(pv, rows_per_head, c):
    group = (2 * c) // GQA_GROUP
    lo = lax.broadcasted_iota(jnp.int32, (rows_per_head, LANES), 1) < HEAD_DIM
    a = pv[(2 * c) * rows_per_head:(2 * c + 1) * rows_per_head]
    b = pv[(2 * c + 1) * rows_per_head:(2 * c + 2) * rows_per_head]
    if group == 0:
        return jnp.where(lo, a, pltpu.roll(b, HEAD_DIM, axis=1))
    return jnp.where(lo, pltpu.roll(a, HEAD_DIM, axis=1), b)


WEIGHT_NAMES = ("g_mix", "w_in", "bd_q", "bd_k", "g_q", "g_k", "conv_w", "g_attn", "g_conv", "w_o",
                "g_mlp", "w_up", "w_down", "g_ple", "w_gate", "w_ple")
MATMUL_WEIGHTS = ("w_in", "w_o", "w_up", "w_down", "w_gate", "w_ple")
LATE_WEIGHTS = MATMUL_WEIGHTS[1:]
SMALL_WEIGHTS = tuple(n for n in WEIGHT_NAMES if n not in MATMUL_WEIGHTS)
W_IN_CHUNK = 32
W_IN_SLOTS = 8
LATE_DMA_PRIORITY = 1


def _prompt_kernel(sinks_ref, x_ref, p_ref, rope_ref, *rest):
    nw = len(WEIGHT_NAMES)
    refs = dict(zip(WEIGHT_NAMES, rest[:nw]))
    y_ref, kout_ref, vout_ref, convout_ref, kbuf, vbuf, ubuf = rest[nw:]
    tm = x_ref.shape[0]
    j = pl.program_id(1)

    @pl.when(j == 0)
    def _():
        kbuf[0:BLOCK, :] = jnp.zeros((BLOCK, KV_WIDTH), BF16)
        vbuf[:, 0:BLOCK] = jnp.zeros((KV_WIDTH, BLOCK), BF16)
        ubuf[0:SUBLANES, :] = jnp.zeros((SUBLANES, CONV_WIDTH), F32)

    kj = lax.broadcasted_iota(jnp.int32, (BLOCK, N_Q_HEADS * BLOCK), 0)
    qi = lax.broadcasted_iota(jnp.int32, (BLOCK, N_Q_HEADS * BLOCK), 1) & (BLOCK - 1)
    own = kj <= qi
    sink = jnp.concatenate([jnp.full((1, BLOCK), sinks_ref[i] * LOG2_E, F32) for i in range(N_Q_HEADS)],
                           axis=1)
    cw = refs["conv_w"]

    last = {}

    def sub_tile_steps(r0, n):
        rows = slice(r0, r0 + n)
        emb = _dot(p_ref[rows, :].astype(BF16), refs["w_ple"][...]) if r0 == 0 else None
        x = x_ref[rows, :]
        qcols, k, v, b_gate, u = yield from _front_steps(
            x, *_rope_rows(rope_ref[:, rows]), refs)
        kbuf[BLOCK + r0:BLOCK + r0 + n, :] = k.astype(BF16)
        v_t = v.T
        vbuf[:, BLOCK + r0:BLOCK + r0 + n] = v_t.astype(BF16)
        ubuf[SUBLANES + r0:SUBLANES + r0 + n, :] = u
        last.update(k=k, v_t=v_t, u=u, n=n)
        yield

        o_blocks = {}

        def block_stages(i):
            g = r0 // BLOCK + i
            qs = []
            for c in range(N_QCOL):
                qs.extend(_split_heads(qcols[c][i * BLOCK:(i + 1) * BLOCK], c))
            qstack = jnp.concatenate(qs, axis=0).astype(BF16)
            s2 = _dot_t(kbuf[g * BLOCK:(g + 2) * BLOCK, :], qstack)
            yield
            s_prev = s2[:BLOCK]
            if g == 0:
                s_prev = jnp.where(j == 0, NEG, s_prev)
            s = jnp.where(own, s2[BLOCK:], s_prev)
            mx = jnp.maximum(jnp.max(s, axis=0, keepdims=True), sink)
            e = jnp.exp2(s - mx)
            denom = jnp.sum(e, axis=0, keepdims=True) + jnp.exp2(sink - mx)
            e2 = jnp.concatenate([jnp.where(own, 0.0, e), jnp.where(own, e, 0.0)], axis=0).astype(BF16)
            yield
            pv = _dot(vbuf[:, g * BLOCK:(g + 2) * BLOCK], e2) * (1.0 / denom)
            cols = []
            for c in range(N_QCOL):
                d0 = (2 * c) // GQA_GROUP * HEAD_DIM
                col_t = jnp.concatenate([pv[d0:d0 + HEAD_DIM, (2 * c) * BLOCK:(2 * c + 1) * BLOCK],
                                         pv[d0:d0 + HEAD_DIM, (2 * c + 1) * BLOCK:(2 * c + 2) * BLOCK]], axis=0)
                cols.append(col_t.T)
            o_blocks[i] = jnp.concatenate(cols, axis=1)

        blocks = [block_stages(i) for i in range(n // BLOCK)]
        for _ in range(ATTN_STAGES):
            for stages in blocks:
                next(stages, None)
            yield
        o_attn = jnp.concatenate([o_blocks[i] for i in range(n // BLOCK)], axis=0)

        conv = (cw[0] * ubuf[SUBLANES - 2 + r0:SUBLANES - 2 + r0 + n, :]
                + cw[1] * ubuf[SUBLANES - 1 + r0:SUBLANES - 1 + r0 + n, :]
                + cw[2] * u)

        def store(y):
            y_ref[rows, :] = y

        yield from _tail_steps(x, o_attn, b_gate * conv, p_ref[rows, :], refs, store, MLP_LOOKAHEAD,
                               embed_last=r0 + n == tm, emb=emb)

    assert sum(SUB_TILES) == tm
    starts = [sum(SUB_TILES[:t]) for t in range(len(SUB_TILES))]
    tiles = [sub_tile_steps(r0, n) for r0, n in zip(starts, SUB_TILES)]
    first, second = tiles
    n_stages = FRONT_STAGES + ATTN_STAGES + 2 + 2 * (D_FF // FF_CHUNK)
    order = [first] * FRONT_STAGES + [second] + [first] * ATTN_STAGES
    alternating = n_stages - FRONT_STAGES - ATTN_STAGES - 1
    order += [second, first] * alternating
    order += [second] * (n_stages - 2 - alternating) + [first, second]
    for steps in order:
        next(steps, None)

    n_last = last["n"]
    kout_ref[...] = last["k"][n_last - BLOCK:, :].T
    vout_ref[...] = last["v_t"][:, n_last - BLOCK:]
    convout_ref[...] = last["u"][n_last - (CONV_K - 1):, :]
    kbuf[0:BLOCK, :] = kbuf[tm:tm + BLOCK, :]
    vbuf[:, 0:BLOCK] = vbuf[:, tm:tm + BLOCK]
    ubuf[0:SUBLANES, :] = ubuf[tm:tm + SUBLANES, :]


def _sample_kernel(sinks_ref, x_ref, p_ref, rope_ref, state_ref, ck_ref, cv_ref, *rest):
    ns, nm, nl = len(SMALL_WEIGHTS), len(MATMUL_WEIGHTS), len(LATE_WEIGHTS)
    refs = dict(zip(SMALL_WEIGHTS, rest[:ns]))
    w_f32 = dict(zip(MATMUL_WEIGHTS, rest[ns:ns + nm]))
    y_ref, kout_ref, vout_ref, convout_ref = rest[ns + nm:ns + nm + 4]
    w_out = dict(zip(MATMUL_WEIGHTS, rest[ns + nm + 4:ns + 2 * nm + 4]))
    scratch = rest[ns + 2 * nm + 4:]
    qbuf, kbuf, vbuf, knew, vnew, obuf, bgbuf, ubuf = scratch[:8]
    w_vmem = dict(zip(MATMUL_WEIGHTS, scratch[8:8 + nm]))
    in_stage = scratch[8 + nm]
    late_stage = dict(zip(LATE_WEIGHTS, scratch[9 + nm:9 + nm + nl]))
    in_sem, late_sem, out_sem = scratch[9 + nm + nl:]
    refs.update(w_vmem)
    t_len = x_ref.shape[1]
    n_tok = x_ref.shape[0] * t_len
    step = pl.program_id(0)
    n_steps = pl.num_programs(0)
    step_batch = ck_ref.shape[0]
    pair_rows = 2 * t_len
    chunk_pairs = BLOCK // pair_rows

    n_slots = in_stage.shape[0]

    def in_copy(i):
        return pltpu.make_async_copy(w_f32["w_in"].at[pl.ds(i * W_IN_CHUNK, W_IN_CHUNK), :],
                                     in_stage.at[i % n_slots], in_sem.at[i % n_slots])

    def late_copy(n, name, s):
        rows = late_stage[name].shape[1]
        return pltpu.make_async_copy(w_f32[name].at[pl.ds(s * rows, rows), :], late_stage[name].at[s % 2],
                                     late_sem.at[n, s % 2])

    def out_copy(n, name):
        return pltpu.make_async_copy(w_vmem[name], w_out[name], out_sem.at[n])

    @pl.when(step + 1 < n_steps)
    def _():
        for n, name in enumerate(LATE_WEIGHTS):
            late_copy(n, name, step + 1).start(priority=LATE_DMA_PRIORITY)

    @pl.when(step == 0)
    def _():
        for n, name in enumerate(LATE_WEIGHTS):
            late_copy(n, name, step).start(priority=LATE_DMA_PRIORITY)
        n_chunks = D_MODEL // W_IN_CHUNK
        for i in range(n_slots):
            in_copy(i).start()
        for i in range(n_chunks):
            in_copy(i).wait()
            w_vmem["w_in"][i * W_IN_CHUNK:(i + 1) * W_IN_CHUNK, :] = in_stage[i % n_slots].astype(BF16)
            if i + n_slots < n_chunks:
                in_copy(i + n_slots).start()
        out_copy(0, "w_in").start()
        qcols, k, v, b_gate, u = _front(x_ref[...].reshape(n_tok, D_MODEL),
                                        *_rope_rows(rope_ref[...]), refs)
        for c in range(N_QCOL):
            qbuf[2 * c], qbuf[2 * c + 1] = _split_heads(qcols[c], c)
        k_t, v_t = k.T, v.T
        kbuf[...] = k_t.astype(BF16)
        vbuf[...] = v_t.astype(BF16)
        knew[...] = k_t
        vnew[...] = v_t
        bgbuf[...] = b_gate
        ubuf[...] = u
        convout_ref[...] = u.reshape(convout_ref.shape[0], t_len, CONV_WIDTH)[:, t_len - (CONV_K - 1):, :]

    n_rows = N_Q_HEADS * pair_rows
    row = lax.broadcasted_iota(jnp.int32, (n_rows, 3 * BLOCK), 0)
    col = lax.broadcasted_iota(jnp.int32, (n_rows, 3 * BLOCK), 1)
    row_b = (row % pair_rows) // t_len
    row_t = row % t_len
    cache_ok = (col < 2 * BLOCK) & (col // BLOCK == row_b) & ((col % BLOCK) > row_t)
    new_col = col - 2 * BLOCK
    sink = _sink_column(sinks_ref, pair_rows)

    def pair_stages(i):
        pair = step * (step_batch // 2) + i
        r0 = pl.multiple_of(pair * pair_rows, pair_rows)
        c0 = pl.multiple_of((pair // chunk_pairs) * BLOCK, BLOCK)
        q = qbuf[:, pl.ds(r0, pair_rows), :].reshape(n_rows, LANES).astype(BF16)
        ck = [ck_ref[2 * i + bb] for bb in range(2)]
        cv = [cv_ref[2 * i + bb] for bb in range(2)]
        keys = jnp.concatenate([c.astype(BF16) for c in ck] + [kbuf[:, pl.ds(c0, BLOCK)]], axis=1)
        vals = jnp.concatenate([c.astype(BF16) for c in cv] + [vbuf[:, pl.ds(c0, BLOCK)]], axis=1)
        s = _dot(q, keys)
        yield
        new_ok = ((col >= 2 * BLOCK) & (new_col // t_len == (pair % chunk_pairs) * 2 + row_b)
                  & (new_col % t_len <= row_t))
        ok = cache_ok | new_ok
        s = jnp.where(ok, s, NEG)
        mx = jnp.maximum(jnp.max(s, axis=-1, keepdims=True), sink)
        e = jnp.where(ok, jnp.exp2(s - mx), 0.0)
        denom = jnp.sum(e, axis=-1, keepdims=True) + jnp.exp2(sink - mx)
        yield
        pv = _dot_t(e.astype(BF16), vals) * (1.0 / denom)
        for c in range(N_QCOL):
            obuf[pl.ds(r0, pair_rows), c * LANES:(c + 1) * LANES] = _merge_heads(pv, pair_rows, c)
        yield
        k_chunk = knew[:, pl.ds(c0, BLOCK)]
        v_chunk = vnew[:, pl.ds(c0, BLOCK)]
        keep = lax.broadcasted_iota(jnp.int32, (KV_WIDTH, BLOCK), 1) < BLOCK - t_len
        for bb in range(2):
            to_tail = (BLOCK - t_len) - t_len * ((pair % chunk_pairs) * 2 + bb)
            kout_ref[2 * i + bb] = jnp.where(keep, pltpu.roll(ck[bb], BLOCK - t_len, axis=1),
                                             pltpu.roll(k_chunk, to_tail, axis=1))
            vout_ref[2 * i + bb] = jnp.where(keep, pltpu.roll(cv[bb], BLOCK - t_len, axis=1),
                                             pltpu.roll(v_chunk, to_tail, axis=1))

    pairs = [pair_stages(i) for i in range(step_batch // 2)]
    for _ in range(4):
        for stages in pairs:
            next(stages, None)

    for n, name in enumerate(LATE_WEIGHTS):
        rows = late_stage[name].shape[1]
        late_copy(n, name, step).wait()
        r0 = pl.multiple_of(step * rows, rows)
        w_vmem[name][pl.ds(r0, rows), :] = late_stage[name][step % 2].astype(BF16)

    @pl.when(step == n_steps - 1)
    def _():
        for n, name in enumerate(LATE_WEIGHTS):
            out_copy(n + 1, name).start()
        u = ubuf[...]
        tok = lax.broadcasted_iota(jnp.int32, u.shape, 0) % t_len
        per_token = lambda r: jnp.broadcast_to(state_ref[:, r:r + 1, :],
                                               (state_ref.shape[0], t_len, CONV_WIDTH)).reshape(u.shape)
        s0, s1 = per_token(0), per_token(1)
        um1 = jnp.where(tok >= 1, pltpu.roll(u, 1, axis=0), s1)
        um2 = jnp.where(tok >= 2, pltpu.roll(u, 2, axis=0), jnp.where(tok == 0, s0, s1))
        cw = refs["conv_w"]
        conv = cw[0] * um2 + cw[1] * um1 + cw[2] * u

        def store(y):
            y_ref[...] = y.reshape(y_ref.shape)

        _run(_tail_steps(x_ref[...].reshape(n_tok, D_MODEL), obuf[...], bgbuf[...] * conv,
                         p_ref[...].reshape(n_tok, PLE_DIM), refs, store))
        for n, name in enumerate(MATMUL_WEIGHTS):
            out_copy(n, name).wait()


def _rope_tables(pos):
    inv_freq = ROPE_THETA ** (-jnp.arange(0, HEAD_DIM, 2, dtype=F32) / HEAD_DIM)
    ang = inv_freq[:, None] * pos.astype(F32)[None, :]
    sin = jnp.sin(ang)
    return jnp.concatenate([jnp.cos(ang), -sin, sin], axis=0)


def _cache_to_kernel(c):
    batch, keys = c.shape[0], c.shape[1]
    return jnp.transpose(c, (0, 2, 3, 1)).reshape(batch, KV_WIDTH, keys)


def _cache_from_kernel(c):
    batch, _, keys = c.shape
    return jnp.transpose(c.reshape(batch, N_KV_HEADS, HEAD_DIM, keys), (0, 3, 1, 2))


def _block_diag_ones(width):
    idx = np.arange(width) // HEAD_DIM
    return jnp.asarray(idx[:, None] == idx[None, :], dtype=BF16)


def _prepare_weights(g_mix_norm, w_in, g_q, g_k, sinks, conv_w, g_attn_out, g_conv_out, w_o,
                     g_mlp_norm, w_up, w_down, g_ple_norm, w_ple_gate, w_ple):
    row = lambda g: g.reshape(1, -1).astype(F32)
    weights = dict(
        g_mix=row(g_mix_norm), w_in=w_in,
        bd_q=_block_diag_ones(ATTN_WIDTH), bd_k=_block_diag_ones(KV_WIDTH),
        g_q=row(g_q), g_k=row(g_k),
        conv_w=conv_w.astype(F32).reshape(CONV_K, 1, CONV_WIDTH),
        g_attn=row(g_attn_out), g_conv=row(g_conv_out), w_o=w_o,
        g_mlp=row(g_mlp_norm), w_up=w_up, w_down=w_down,
        g_ple=row(g_ple_norm), w_gate=w_ple_gate, w_ple=w_ple)
    return sinks.astype(F32), weights


def _resident(a, n_grid):
    zeros = (0,) * a.ndim
    index_map = (lambda b, j: zeros) if n_grid == 2 else (lambda i: zeros)
    return pl.BlockSpec(a.shape, index_map, pipeline_mode=pl.Buffered(1))


def _prompt_layer(x, p, sinks_p, weights):
    batch, seq, _ = x.shape
    tm = SEQ_TILE
    rope = _rope_tables(jnp.arange(seq, dtype=jnp.int32))
    tile = lambda w: pl.BlockSpec((None, tm, w), lambda b, j: (b, j, 0))
    per_batch = lambda r, w: pl.BlockSpec((None, r, w), lambda b, j: (b, 0, 0))
    table = pl.BlockSpec((rope.shape[0], tm), lambda b, j: (0, j))
    in_specs = ([pl.BlockSpec(memory_space=pltpu.SMEM), tile(D_MODEL), tile(PLE_DIM), table]
                + [_resident(w, 2) for w in weights])
    out_shape = (jax.ShapeDtypeStruct((batch, seq, D_MODEL), F32),
                 jax.ShapeDtypeStruct((batch, BLOCK, KV_WIDTH), F32),
                 jax.ShapeDtypeStruct((batch, BLOCK, KV_WIDTH), F32),
                 jax.ShapeDtypeStruct((batch, CONV_K - 1, CONV_WIDTH), F32))
    out_specs = (tile(D_MODEL), per_batch(BLOCK, KV_WIDTH), per_batch(BLOCK, KV_WIDTH),
                 per_batch(CONV_K - 1, CONV_WIDTH))
    scratch = [pltpu.VMEM((BLOCK + tm, KV_WIDTH), BF16), pltpu.VMEM((KV_WIDTH, BLOCK + tm), BF16),
               pltpu.VMEM((SUBLANES + tm, CONV_WIDTH), F32)]
    return pl.pallas_call(
        _prompt_kernel, grid=(batch, seq // tm), in_specs=in_specs, out_specs=out_specs,
        out_shape=out_shape, scratch_shapes=scratch, name="prompt_layer",
        compiler_params=pltpu.CompilerParams(dimension_semantics=("arbitrary", "arbitrary"),
                                             vmem_limit_bytes=VMEM_LIMIT),
    )(sinks_p, x, p, rope, *weights)


def _sample_layer(x, p, cache_k, cache_v, state_conv, sinks_p, weights):
    batch, t_len, _ = x.shape
    n_tok = batch * t_len
    rope = _rope_tables(PAST_LEN + jnp.arange(n_tok, dtype=jnp.int32) % t_len)
    ck = _cache_to_kernel(cache_k)
    cv = _cache_to_kernel(cache_v)
    flat = [x, p, rope, state_conv]
    n_steps = batch // SAMPLE_STEP_BATCH
    small = [weights[n] for n in SMALL_WEIGHTS]
    big = [weights[n] for n in MATMUL_WEIGHTS]
    hbm = pl.BlockSpec(memory_space=pl.ANY)
    cache_block = pl.BlockSpec((SAMPLE_STEP_BATCH, WINDOW, KV_WIDTH), lambda i: (i, 0, 0))
    in_specs = ([pl.BlockSpec(memory_space=pltpu.SMEM)] + [_resident(a, 1) for a in flat]
                + [cache_block, cache_block] + [_resident(w, 1) for w in small] + [hbm] * len(big))
    out_shape = (jax.ShapeDtypeStruct((batch, t_len, D_MODEL), F32),
                 jax.ShapeDtypeStruct((batch, WINDOW, KV_WIDTH), F32),
                 jax.ShapeDtypeStruct((batch, WINDOW, KV_WIDTH), F32),
                 jax.ShapeDtypeStruct((batch, CONV_K - 1, CONV_WIDTH), F32)
                 ) + tuple(jax.ShapeDtypeStruct(w.shape, BF16) for w in big)
    whole = lambda s: pl.BlockSpec(s.shape, lambda i: (0,) * len(s.shape), pipeline_mode=pl.Buffered(1))
    out_specs = (whole(out_shape[0]), cache_block, cache_block, whole(out_shape[3])) + (hbm,) * len(big)
    late = [weights[n] for n in LATE_WEIGHTS]
    scratch = ([pltpu.VMEM((N_Q_HEADS, n_tok, LANES), F32),
                pltpu.VMEM((KV_WIDTH, n_tok), BF16), pltpu.VMEM((KV_WIDTH, n_tok), BF16),
                pltpu.VMEM((KV_WIDTH, n_tok), F32), pltpu.VMEM((KV_WIDTH, n_tok), F32),
                pltpu.VMEM((n_tok, ATTN_WIDTH), F32), pltpu.VMEM((n_tok, CONV_WIDTH), F32),
                pltpu.VMEM((n_tok, CONV_WIDTH), F32)]
               + [pltpu.VMEM(w.shape, BF16) for w in big]
               + [pltpu.VMEM((W_IN_SLOTS, W_IN_CHUNK, IN_WIDTH), F32)]
               + [pltpu.VMEM((2, w.shape[0] // n_steps, w.shape[1]), F32) for w in late]
               + [pltpu.SemaphoreType.DMA((W_IN_SLOTS,)), pltpu.SemaphoreType.DMA((len(late), 2)),
                  pltpu.SemaphoreType.DMA((len(big),))])
    y, k_new, v_new, conv_new, *w_bf16 = pl.pallas_call(
        _sample_kernel, grid=(n_steps,), in_specs=in_specs, out_specs=out_specs,
        out_shape=out_shape, scratch_shapes=scratch, name="sample_layer",
        compiler_params=pltpu.CompilerParams(dimension_semantics=("arbitrary",),
                                             vmem_limit_bytes=SAMPLE_VMEM_LIMIT),
    )(sinks_p, *flat, ck, cv, *small, *big)
    return y, k_new, v_new, conv_new, dict(zip(MATMUL_WEIGHTS, w_bf16))


def kernel(x_prompt, x_sample, p_prompt, p_sample, cache_k, cache_v, state_conv, g_mix_norm, w_in, g_q, g_k,
           sinks, conv_w, g_attn_out, g_conv_out, w_o, g_mlp_norm, w_up, w_down, g_ple_norm, w_ple_gate, w_ple):
    depth = w_in.shape[0]
    yp, ys = x_prompt, x_sample
    outs = [[] for _ in range(6)]
    for i in range(depth):
        sinks_p, weights = _prepare_weights(
            g_mix_norm[i], w_in[i], g_q[i], g_k[i], sinks[i], conv_w[i], g_attn_out[i], g_conv_out[i],
            w_o[i], g_mlp_norm[i], w_up[i], w_down[i], g_ple_norm[i], w_ple_gate[i], w_ple[i])
        ys, ksn, vsn, csn, w_bf16 = _sample_layer(ys, p_sample[i], cache_k[i], cache_v[i], state_conv[i],
                                                  sinks_p, weights)
        yp, kp, vp, cp = _prompt_layer(yp, p_prompt[i], sinks_p,
                                       [w_bf16.get(n, weights[n]) for n in WEIGHT_NAMES])
        for lst, val in zip(outs, (_cache_from_kernel(kp), _cache_from_kernel(vp), cp,
                                   _cache_from_kernel(ksn), _cache_from_kernel(vsn), csn)):
            lst.append(val)
    return (yp, ys) + tuple(jnp.stack(o) for o in outs)
```

```python
import jax
import jax.numpy as jnp
import numpy as np
from jax import lax
from jax.experimental import pallas as pl
from jax.experimental.pallas import tpu as pltpu

D_MODEL = 1024
HEAD_DIM = 64
N_Q_HEADS = 8
N_KV_HEADS = 2
GQA_GROUP = N_Q_HEADS // N_KV_HEADS
ATTN_WIDTH = N_Q_HEADS * HEAD_DIM
KV_WIDTH = N_KV_HEADS * HEAD_DIM
CONV_WIDTH = D_MODEL - ATTN_WIDTH
CONV_K = 3
WINDOW = 128
BLOCK = 128
ROPE_THETA = 10000.0
D_FF = 4 * D_MODEL
PLE_DIM = 256
EPS = 1e-6
NEG = -1e30
LOG2_E = 1.4426950408889634
PAST_LEN = 16384
IN_WIDTH = ATTN_WIDTH + 2 * KV_WIDTH + 3 * CONV_WIDTH

LANES = 128
SUBLANES = 8
N_QCOL = ATTN_WIDTH // LANES
SEQ_TILE = 512
SUB_TILES = (256, 256)
FF_CHUNK = 1024
MLP_LOOKAHEAD = 2
FRONT_STAGES = 2
ATTN_STAGES = 3
SAMPLE_STEP_BATCH = 8
VMEM_LIMIT = 56 * 1024 * 1024
SAMPLE_VMEM_LIMIT = 60 * 1024 * 1024

O_K = ATTN_WIDTH
O_V = O_K + KV_WIDTH
O_B = O_V + KV_WIDTH
O_C = O_B + CONV_WIDTH
O_H = O_C + CONV_WIDTH

BF16 = jnp.bfloat16
F32 = jnp.float32


def _dot(a, b):
    return jnp.dot(a, b, preferred_element_type=F32)


def _dot_t(a, b):
    return lax.dot_general(a, b, (((1,), (1,)), ((), ())), preferred_element_type=F32)


def _rms(x, g):
    return x * lax.rsqrt(jnp.mean(x * x, axis=-1, keepdims=True) + EPS) * g


def _head_norm_rope(t, ones_bd, g, cos, sin):
    ssq = _dot((t * t).astype(BF16), ones_bd)
    t = t * lax.rsqrt(ssq * (1.0 / HEAD_DIM) + EPS)
    lane = lax.broadcasted_iota(jnp.int32, (t.shape[0], LANES), 1)
    first_half = (lane & (HEAD_DIM - 1)) < HEAD_DIM // 2
    cols = []
    for m in range(t.shape[1] // LANES):
        c = t[:, m * LANES:(m + 1) * LANES] * g
        up = pltpu.roll(c, LANES - HEAD_DIM // 2, axis=1)
        dn = pltpu.roll(c, HEAD_DIM // 2, axis=1)
        cols.append(c * cos + jnp.where(first_half, up, dn) * sin)
    return cols


def _rope_rows(table):
    half = HEAD_DIM // 2
    cos = jnp.concatenate([table[:half]] * (LANES // half), axis=0).T
    sin = jnp.concatenate([table[half:]] * (LANES // HEAD_DIM), axis=0).T
    return cos, sin


def _front_steps(x, cos, sin, refs):
    h = _rms(x, refs["g_mix"][...]).astype(BF16)
    z = _dot(h, refs["w_in"][:, :O_B])
    yield
    zc = _dot(h, refs["w_in"][:, O_B:])
    two_heads = lambda g: jnp.concatenate([g] * (LANES // HEAD_DIM), axis=1)
    g_q = two_heads(refs["g_q"][...]) * (HEAD_DIM ** -0.5 * LOG2_E)
    qcols = _head_norm_rope(z[:, :O_K], refs["bd_q"][...], g_q, cos, sin)
    (k,) = _head_norm_rope(z[:, O_K:O_V], refs["bd_k"][...], two_heads(refs["g_k"][...]), cos, sin)
    v = z[:, O_V:O_B]
    b_gate = zc[:, :CONV_WIDTH]
    u = zc[:, CONV_WIDTH:2 * CONV_WIDTH] * zc[:, 2 * CONV_WIDTH:]
    return qcols, k, v, b_gate, u


def _front(x, cos, sin, refs):
    steps = _front_steps(x, cos, sin, refs)
    try:
        while True:
            next(steps)
    except StopIteration as done:
        return done.value


def _tail_steps(x, o_attn, o_conv, p, refs, store, lookahead=1, embed_last=False, emb=None):
    embed = lambda: _dot(p.astype(BF16), refs["w_ple"][...])
    if emb is None and not embed_last:
        emb = embed()
    mixed = jnp.concatenate([_rms(o_attn, refs["g_attn"][...]), _rms(o_conv, refs["g_conv"][...])], axis=1)
    x = x + _dot(mixed.astype(BF16), refs["w_o"][...])
    hm = _rms(x, refs["g_mlp"][...]).astype(BF16)
    yield

    def hidden(c):
        up = _dot(hm, refs["w_up"][:, c * FF_CHUNK:(c + 1) * FF_CHUNK])
        return jnp.square(jnp.maximum(up.astype(BF16), 0.0))

    n_chunks = D_FF // FF_CHUNK
    acts = []
    for c in range(min(lookahead, n_chunks)):
        acts.append(hidden(c))
        yield
    for c in range(n_chunks):
        if c + lookahead < n_chunks:
            acts.append(hidden(c + lookahead))
            yield
        x = x + _dot(acts[c], refs["w_down"][c * FF_CHUNK:(c + 1) * FF_CHUNK, :])
        yield
    if emb is None:
        emb = embed()
    gate = jax.nn.sigmoid(_dot(_rms(x, refs["g_ple"][...]).astype(BF16), refs["w_gate"][...]))
    store(x + gate * emb)


def _run(steps):
    for _ in steps:
        pass


def _split_heads(col, c):
    group = (2 * c) // GQA_GROUP
    lo = lax.broadcasted_iota(jnp.int32, col.shape, 1) < HEAD_DIM
    swapped = pltpu.roll(col, HEAD_DIM, axis=1)
    if group == 0:
        return jnp.where(lo, col, 0.0), jnp.where(lo, swapped, 0.0)
    return jnp.where(lo, 0.0, swapped), jnp.where(lo, 0.0, col)


def _sink_column(sinks_ref, rows_per_head):
    return jnp.concatenate(
        [jnp.full((rows_per_head, 1), sinks_ref[i] * LOG2_E, F32) for i in range(N_Q_HEADS)], axis=0)


def _merge_heads(pv, rows_per_head, c):
    group = (2 * c) // GQA_GROUP
    lo = lax.broadcasted_iota(jnp.int32, (rows_per_head, LANES), 1) < HEAD_DIM
    a = pv[(2 * c) * rows_per_head:(2 * c + 1) * rows_per_head]
    b = pv[(2 * c + 1) * rows_per_head:(2 * c + 2) * rows_per_head]
    if group == 0:
        return jnp.where(lo, a, pltpu.roll(b, HEAD_DIM, axis=1))
    return jnp.where(lo, pltpu.roll(a, HEAD_DIM, axis=1), b)


WEIGHT_NAMES = ("g_mix", "w_in", "bd_q", "bd_k", "g_q", "g_k", "conv_w", "g_attn", "g_conv", "w_o",
                "g_mlp", "w_up", "w_down", "g_ple", "w_gate", "w_ple")
MATMUL_WEIGHTS = ("w_in", "w_o", "w_up", "w_down", "w_gate", "w_ple")
LATE_WEIGHTS = MATMUL_WEIGHTS[1:]
SMALL_WEIGHTS = tuple(n for n in WEIGHT_NAMES if n not in MATMUL_WEIGHTS)
W_IN_CHUNK = 16
W_IN_SLOTS = 16
LATE_DMA_PRIORITY = 1


def _prompt_kernel(sinks_ref, x_ref, p_ref, rope_ref, *rest):
    nw = len(WEIGHT_NAMES)
    refs = dict(zip(WEIGHT_NAMES, rest[:nw]))
    y_ref, kout_ref, vout_ref, convout_ref, kbuf, vbuf, ubuf = rest[nw:]
    tm = x_ref.shape[0]
    j = pl.program_id(1)

    @pl.when(j == 0)
    def _():
        kbuf[0:BLOCK, :] = jnp.zeros((BLOCK, KV_WIDTH), BF16)
        vbuf[:, 0:BLOCK] = jnp.zeros((KV_WIDTH, BLOCK), BF16)
        ubuf[0:SUBLANES, :] = jnp.zeros((SUBLANES, CONV_WIDTH), F32)

    kj = lax.broadcasted_iota(jnp.int32, (BLOCK, N_Q_HEADS * BLOCK), 0)
    qi = lax.broadcasted_iota(jnp.int32, (BLOCK, N_Q_HEADS * BLOCK), 1) & (BLOCK - 1)
    own = kj <= qi
    sink = jnp.concatenate([jnp.full((1, BLOCK), sinks_ref[i] * LOG2_E, F32) for i in range(N_Q_HEADS)],
                           axis=1)
    cw = refs["conv_w"]

    last = {}

    def sub_tile_steps(r0, n):
        rows = slice(r0, r0 + n)
        emb = _dot(p_ref[rows, :].astype(BF16), refs["w_ple"][...]) if r0 == 0 else None
        x = x_ref[rows, :]
        qcols, k, v, b_gate, u = yield from _front_steps(
            x, *_rope_rows(rope_ref[:, rows]), refs)
        kbuf[BLOCK + r0:BLOCK + r0 + n, :] = k.astype(BF16)
        v_t = v.T
        vbuf[:, BLOCK + r0:BLOCK + r0 + n] = v_t.astype(BF16)
        ubuf[SUBLANES + r0:SUBLANES + r0 + n, :] = u
        last.update(k=k, v_t=v_t, u=u, n=n)
        yield

        o_blocks = {}

        def block_stages(i):
            g = r0 // BLOCK + i
            qs = []
            for c in range(N_QCOL):
                qs.extend(_split_heads(qcols[c][i * BLOCK:(i + 1) * BLOCK], c))
            qstack = jnp.concatenate(qs, axis=0).astype(BF16)
            s2 = _dot_t(kbuf[g * BLOCK:(g + 2) * BLOCK, :], qstack)
            yield
            s_prev = s2[:BLOCK]
            if g == 0:
                s_prev = jnp.where(j == 0, NEG, s_prev)
            s = jnp.where(own, s2[BLOCK:], s_prev)
            mx = jnp.maximum(jnp.max(s, axis=0, keepdims=True), sink)
            e = jnp.exp2(s - mx)
            denom = jnp.sum(e, axis=0, keepdims=True) + jnp.exp2(sink - mx)
            e2 = jnp.concatenate([jnp.where(own, 0.0, e), jnp.where(own, e, 0.0)], axis=0).astype(BF16)
            yield
            pv = _dot(vbuf[:, g * BLOCK:(g + 2) * BLOCK], e2) * (1.0 / denom)
            cols = []
            for c in range(N_QCOL):
                d0 = (2 * c) // GQA_GROUP * HEAD_DIM
                col_t = jnp.concatenate([pv[d0:d0 + HEAD_DIM, (2 * c) * BLOCK:(2 * c + 1) * BLOCK],
                                         pv[d0:d0 + HEAD_DIM, (2 * c + 1) * BLOCK:(2 * c + 2) * BLOCK]], axis=0)
                cols.append(col_t.T)
            o_blocks[i] = jnp.concatenate(cols, axis=1)

        blocks = [block_stages(i) for i in range(n // BLOCK)]
        for _ in range(ATTN_STAGES):
            for stages in blocks:
                next(stages, None)
            yield
        o_attn = jnp.concatenate([o_blocks[i] for i in range(n // BLOCK)], axis=0)

        conv = (cw[0] * ubuf[SUBLANES - 2 + r0:SUBLANES - 2 + r0 + n, :]
                + cw[1] * ubuf[SUBLANES - 1 + r0:SUBLANES - 1 + r0 + n, :]
                + cw[2] * u)

        def store(y):
            y_ref[rows, :] = y

        yield from _tail_steps(x, o_attn, b_gate * conv, p_ref[rows, :], refs, store, MLP_LOOKAHEAD,
                               embed_last=r0 + n == tm, emb=emb)

    assert sum(SUB_TILES) == tm
    starts = [sum(SUB_TILES[:t]) for t in range(len(SUB_TILES))]
    tiles = [sub_tile_steps(r0, n) for r0, n in zip(starts, SUB_TILES)]
    first, second = tiles
    n_stages = FRONT_STAGES + ATTN_STAGES + 2 + 2 * (D_FF // FF_CHUNK)
    order = [first] * FRONT_STAGES + [second] + [first] * ATTN_STAGES
    alternating = n_stages - FRONT_STAGES - ATTN_STAGES - 1
    order += [second, first] * alternating
    order += [second] * (n_stages - 2 - alternating) + [first, second]
    for steps in order:
        next(steps, None)

    n_last = last["n"]
    kout_ref[...] = last["k"][n_last - BLOCK:, :].T
    vout_ref[...] = last["v_t"][:, n_last - BLOCK:]
    convout_ref[...] = last["u"][n_last - (CONV_K - 1):, :]
    kbuf[0:BLOCK, :] = kbuf[tm:tm + BLOCK, :]
    vbuf[:, 0:BLOCK] = vbuf[:, tm:tm + BLOCK]
    ubuf[0:SUBLANES, :] = ubuf[tm:tm + SUBLANES, :]


def _sample_kernel(sinks_ref, x_ref, p_ref, rope_ref, state_ref, ck_ref, cv_ref, *rest):
    ns, nm, nl = len(SMALL_WEIGHTS), len(MATMUL_WEIGHTS), len(LATE_WEIGHTS)
    refs = dict(zip(SMALL_WEIGHTS, rest[:ns]))
    w_f32 = dict(zip(MATMUL_WEIGHTS, rest[ns:ns + nm]))
    y_ref, kout_ref, vout_ref, convout_ref = rest[ns + nm:ns + nm + 4]
    w_out = dict(zip(MATMUL_WEIGHTS, rest[ns + nm + 4:ns + 2 * nm + 4]))
    scratch = rest[ns + 2 * nm + 4:]
    qbuf, kbuf, vbuf, knew, vnew, obuf, bgbuf, ubuf = scratch[:8]
    w_vmem = dict(zip(MATMUL_WEIGHTS, scratch[8:8 + nm]))
    in_stage = scratch[8 + nm]
    late_stage = dict(zip(LATE_WEIGHTS, scratch[9 + nm:9 + nm + nl]))
    in_sem, late_sem, out_sem = scratch[9 + nm + nl:]
    refs.update(w_vmem)
    t_len = x_ref.shape[1]
    n_tok = x_ref.shape[0] * t_len
    step = pl.program_id(0)
    n_steps = pl.num_programs(0)
    step_batch = ck_ref.shape[0]
    pair_rows = 2 * t_len
    chunk_pairs = BLOCK // pair_rows

    n_slots = in_stage.shape[0]

    def in_copy(i):
        return pltpu.make_async_copy(w_f32["w_in"].at[pl.ds(i * W_IN_CHUNK, W_IN_CHUNK), :],
                                     in_stage.at[i % n_slots], in_sem.at[i % n_slots])

    def late_copy(n, name, s):
        rows = late_stage[name].shape[1]
        return pltpu.make_async_copy(w_f32[name].at[pl.ds(s * rows, rows), :], late_stage[name].at[s % 2],
                                     late_sem.at[n, s % 2])

    def out_copy(n, name):
        return pltpu.make_async_copy(w_vmem[name], w_out[name], out_sem.at[n])

    @pl.when(step + 1 < n_steps)
    def _():
        for n, name in enumerate(LATE_WEIGHTS):
            late_copy(n, name, step + 1).start(priority=LATE_DMA_PRIORITY)

    @pl.when(step == 0)
    def _():
        for n, name in enumerate(LATE_WEIGHTS):
            late_copy(n, name, step).start(priority=LATE_DMA_PRIORITY)
        n_chunks = D_MODEL // W_IN_CHUNK
        for i in range(n_slots):
            in_copy(i).start()
        for i in range(n_chunks):
            in_copy(i).wait()
            w_vmem["w_in"][i * W_IN_CHUNK:(i + 1) * W_IN_CHUNK, :] = in_stage[i % n_slots].astype(BF16)
            if i + n_slots < n_chunks:
                in_copy(i + n_slots).start()
        out_copy(0, "w_in").start()
        qcols, k, v, b_gate, u = _front(x_ref[...].reshape(n_tok, D_MODEL),
                                        *_rope_rows(rope_ref[...]), refs)
        for c in range(N_QCOL):
            qbuf[2 * c], qbuf[2 * c + 1] = _split_heads(qcols[c], c)
        k_t, v_t = k.T, v.T
        kbuf[...] = k_t.astype(BF16)
        vbuf[...] = v_t.astype(BF16)
        knew[...] = k_t
        vnew[...] = v_t
        bgbuf[...] = b_gate
        ubuf[...] = u
        convout_ref[...] = u.reshape(convout_ref.shape[0], t_len, CONV_WIDTH)[:, t_len - (CONV_K - 1):, :]

    n_rows = N_Q_HEADS * pair_rows
    row = lax.broadcasted_iota(jnp.int32, (n_rows, 3 * BLOCK), 0)
    col = lax.broadcasted_iota(jnp.int32, (n_rows, 3 * BLOCK), 1)
    row_b = (row % pair_rows) // t_len
    row_t = row % t_len
    cache_ok = (col < 2 * BLOCK) & (col // BLOCK == row_b) & ((col % BLOCK) > row_t)
    new_col = col - 2 * BLOCK
    sink = _sink_column(sinks_ref, pair_rows)

    def pair_stages(i):
        pair = step * (step_batch // 2) + i
        r0 = pl.multiple_of(pair * pair_rows, pair_rows)
        c0 = pl.multiple_of((pair // chunk_pairs) * BLOCK, BLOCK)
        q = qbuf[:, pl.ds(r0, pair_rows), :].reshape(n_rows, LANES).astype(BF16)
        ck = [ck_ref[2 * i + bb] for bb in range(2)]
        cv = [cv_ref[2 * i + bb] for bb in range(2)]
        keys = jnp.concatenate([c.astype(BF16) for c in ck] + [kbuf[:, pl.ds(c0, BLOCK)]], axis=1)
        vals = jnp.concatenate([c.astype(BF16) for c in cv] + [vbuf[:, pl.ds(c0, BLOCK)]], axis=1)
        s = _dot(q, keys)
        yield
        new_ok = ((col >= 2 * BLOCK) & (new_col // t_len == (pair % chunk_pairs) * 2 + row_b)
                  & (new_col % t_len <= row_t))
        ok = cache_ok | new_ok
        s = jnp.where(ok, s, NEG)
        mx = jnp.maximum(jnp.max(s, axis=-1, keepdims=True), sink)
        e = jnp.where(ok, jnp.exp2(s - mx), 0.0)
        denom = jnp.sum(e, axis=-1, keepdims=True) + jnp.exp2(sink - mx)
        yield
        pv = _dot_t(e.astype(BF16), vals) * (1.0 / denom)
        for c in range(N_QCOL):
            obuf[pl.ds(r0, pair_rows), c * LANES:(c + 1) * LANES] = _merge_heads(pv, pair_rows, c)
        yield
        k_chunk = knew[:, pl.ds(c0, BLOCK)]
        v_chunk = vnew[:, pl.ds(c0, BLOCK)]
        keep = lax.broadcasted_iota(jnp.int32, (KV_WIDTH, BLOCK), 1) < BLOCK - t_len
        for bb in range(2):
            to_tail = (BLOCK - t_len) - t_len * ((pair % chunk_pairs) * 2 + bb)
            kout_ref[2 * i + bb] = jnp.where(keep, pltpu.roll(ck[bb], BLOCK - t_len, axis=1),
                                             pltpu.roll(k_chunk, to_tail, axis=1))
            vout_ref[2 * i + bb] = jnp.where(keep, pltpu.roll(cv[bb], BLOCK - t_len, axis=1),
                                             pltpu.roll(v_chunk, to_tail, axis=1))

    pairs = [pair_stages(i) for i in range(step_batch // 2)]
    for _ in range(4):
        for stages in pairs:
            next(stages, None)

    for n, name in enumerate(LATE_WEIGHTS):
        rows = late_stage[name].shape[1]
        late_copy(n, name, step).wait()
        r0 = pl.multiple_of(step * rows, rows)
        w_vmem[name][pl.ds(r0, rows), :] = late_stage[name][step % 2].astype(BF16)

    @pl.when(step == n_steps - 1)
    def _():
        for n, name in enumerate(LATE_WEIGHTS):
            out_copy(n + 1, name).start()
        u = ubuf[...]
        tok = lax.broadcasted_iota(jnp.int32, u.shape, 0) % t_len
        per_token = lambda r: jnp.broadcast_to(state_ref[:, r:r + 1, :],
                                               (state_ref.shape[0], t_len, CONV_WIDTH)).reshape(u.shape)
        s0, s1 = per_token(0), per_token(1)
        um1 = jnp.where(tok >= 1, pltpu.roll(u, 1, axis=0), s1)
        um2 = jnp.where(tok >= 2, pltpu.roll(u, 2, axis=0), jnp.where(tok == 0, s0, s1))
        cw = refs["conv_w"]
        conv = cw[0] * um2 + cw[1] * um1 + cw[2] * u

        def store(y):
            y_ref[...] = y.reshape(y_ref.shape)

        _run(_tail_steps(x_ref[...].reshape(n_tok, D_MODEL), obuf[...], bgbuf[...] * conv,
                         p_ref[...].reshape(n_tok, PLE_DIM), refs, store))
        for n, name in enumerate(MATMUL_WEIGHTS):
            out_copy(n, name).wait()


def _rope_tables(pos):
    inv_freq = ROPE_THETA ** (-jnp.arange(0, HEAD_DIM, 2, dtype=F32) / HEAD_DIM)
    ang = inv_freq[:, None] * pos.astype(F32)[None, :]
    sin = jnp.sin(ang)
    return jnp.concatenate([jnp.cos(ang), -sin, sin], axis=0)


def _cache_to_kernel(c):
    batch, keys = c.shape[0], c.shape[1]
    return jnp.transpose(c, (0, 2, 3, 1)).reshape(batch, KV_WIDTH, keys)


def _cache_from_kernel(c):
    batch, _, keys = c.shape
    return jnp.transpose(c.reshape(batch, N_KV_HEADS, HEAD_DIM, keys), (0, 3, 1, 2))


def _block_diag_ones(width):
    idx = np.arange(width) // HEAD_DIM
    return jnp.asarray(idx[:, None] == idx[None, :], dtype=BF16)


def _prepare_weights(g_mix_norm, w_in, g_q, g_k, sinks, conv_w, g_attn_out, g_conv_out, w_o,
                     g_mlp_norm, w_up, w_down, g_ple_norm, w_ple_gate, w_ple):
    row = lambda g: g.reshape(1, -1).astype(F32)
    weights = dict(
        g_mix=row(g_mix_norm), w_in=w_in,
        bd_q=_block_diag_ones(ATTN_WIDTH), bd_k=_block_diag_ones(KV_WIDTH),
        g_q=row(g_q), g_k=row(g_k),
        conv_w=conv_w.astype(F32).reshape(CONV_K, 1, CONV_WIDTH),
        g_attn=row(g_attn_out), g_conv=row(g_conv_out), w_o=w_o,
        g_mlp=row(g_mlp_norm), w_up=w_up, w_down=w_down,
        g_ple=row(g_ple_norm), w_gate=w_ple_gate, w_ple=w_ple)
    return sinks.astype(F32), weights


def _resident(a, n_grid):
    zeros = (0,) * a.ndim
    index_map = (lambda b, j: zeros) if n_grid == 2 else (lambda i: zeros)
    return pl.BlockSpec(a.shape, index_map, pipeline_mode=pl.Buffered(1))


def _prompt_layer(x, p, sinks_p, weights):
    batch, seq, _ = x.shape
    tm = SEQ_TILE
    rope = _rope_tables(jnp.arange(seq, dtype=jnp.int32))
    tile = lambda w: pl.BlockSpec((None, tm, w), lambda b, j: (b, j, 0))
    per_batch = lambda r, w: pl.BlockSpec((None, r, w), lambda b, j: (b, 0, 0))
    table = pl.BlockSpec((rope.shape[0], tm), lambda b, j: (0, j))
    in_specs = ([pl.BlockSpec(memory_space=pltpu.SMEM), tile(D_MODEL), tile(PLE_DIM), table]
                + [_resident(w, 2) for w in weights])
    out_shape = (jax.ShapeDtypeStruct((batch, seq, D_MODEL), F32),
                 jax.ShapeDtypeStruct((batch, BLOCK, KV_WIDTH), F32),
                 jax.ShapeDtypeStruct((batch, BLOCK, KV_WIDTH), F32),
                 jax.ShapeDtypeStruct((batch, CONV_K - 1, CONV_WIDTH), F32))
    out_specs = (tile(D_MODEL), per_batch(BLOCK, KV_WIDTH), per_batch(BLOCK, KV_WIDTH),
                 per_batch(CONV_K - 1, CONV_WIDTH))
    scratch = [pltpu.VMEM((BLOCK + tm, KV_WIDTH), BF16), pltpu.VMEM((KV_WIDTH, BLOCK + tm), BF16),
               pltpu.VMEM((SUBLANES + tm, CONV_WIDTH), F32)]
    return pl.pallas_call(
        _prompt_kernel, grid=(batch, seq // tm), in_specs=in_specs, out_specs=out_specs,
        out_shape=out_shape, scratch_shapes=scratch, name="prompt_layer",
        compiler_params=pltpu.CompilerParams(dimension_semantics=("arbitrary", "arbitrary"),
                                             vmem_limit_bytes=VMEM_LIMIT),
    )(sinks_p, x, p, rope, *weights)


def _sample_layer(x, p, cache_k, cache_v, state_conv, sinks_p, weights):
    batch, t_len, _ = x.shape
    n_tok = batch * t_len
    rope = _rope_tables(PAST_LEN + jnp.arange(n_tok, dtype=jnp.int32) % t_len)
    ck = _cache_to_kernel(cache_k)
    cv = _cache_to_kernel(cache_v)
    flat = [x, p, rope, state_conv]
    n_steps = batch // SAMPLE_STEP_BATCH
    small = [weights[n] for n in SMALL_WEIGHTS]
    big = [weights[n] for n in MATMUL_WEIGHTS]
    hbm = pl.BlockSpec(memory_space=pl.ANY)
    cache_block = pl.BlockSpec((SAMPLE_STEP_BATCH, WINDOW, KV_WIDTH), lambda i: (i, 0, 0))
    in_specs = ([pl.BlockSpec(memory_space=pltpu.SMEM)] + [_resident(a, 1) for a in flat]
                + [cache_block, cache_block] + [_resident(w, 1) for w in small] + [hbm] * len(big))
    out_shape = (jax.ShapeDtypeStruct((batch, t_len, D_MODEL), F32),
                 jax.ShapeDtypeStruct((batch, WINDOW, KV_WIDTH), F32),
                 jax.ShapeDtypeStruct((batch, WINDOW, KV_WIDTH), F32),
                 jax.ShapeDtypeStruct((batch, CONV_K - 1, CONV_WIDTH), F32)
                 ) + tuple(jax.ShapeDtypeStruct(w.shape, BF16) for w in big)
    whole = lambda s: pl.BlockSpec(s.shape, lambda i: (0,) * len(s.shape), pipeline_mode=pl.Buffered(1))
    out_specs = (whole(out_shape[0]), cache_block, cache_block, whole(out_shape[3])) + (hbm,) * len(big)
    late = [weights[n] for n in LATE_WEIGHTS]
    scratch = ([pltpu.VMEM((N_Q_HEADS, n_tok, LANES), F32),
                pltpu.VMEM((KV_WIDTH, n_tok), BF16), pltpu.VMEM((KV_WIDTH, n_tok), BF16),
                pltpu.VMEM((KV_WIDTH, n_tok), F32), pltpu.VMEM((KV_WIDTH, n_tok), F32),
                pltpu.VMEM((n_tok, ATTN_WIDTH), F32), pltpu.VMEM((n_tok, CONV_WIDTH), F32),
                pltpu.VMEM((n_tok, CONV_WIDTH), F32)]
               + [pltpu.VMEM(w.shape, BF16) for w in big]
               + [pltpu.VMEM((W_IN_SLOTS, W_IN_CHUNK, IN_WIDTH), F32)]
               + [pltpu.VMEM((2, w.shape[0] // n_steps, w.shape[1]), F32) for w in late]
               + [pltpu.SemaphoreType.DMA((W_IN_SLOTS,)), pltpu.SemaphoreType.DMA((len(late), 2)),
                  pltpu.SemaphoreType.DMA((len(big),))])
    y, k_new, v_new, conv_new, *w_bf16 = pl.pallas_call(
        _sample_kernel, grid=(n_steps,), in_specs=in_specs, out_specs=out_specs,
        out_shape=out_shape, scratch_shapes=scratch, name="sample_layer",
        compiler_params=pltpu.CompilerParams(dimension_semantics=("arbitrary",),
                                             vmem_limit_bytes=SAMPLE_VMEM_LIMIT),
    )(sinks_p, *flat, ck, cv, *small, *big)
    return y, k_new, v_new, conv_new, dict(zip(MATMUL_WEIGHTS, w_bf16))


def kernel(x_prompt, x_sample, p_prompt, p_sample, cache_k, cache_v, state_conv, g_mix_norm, w_in, g_q, g_k,
           sinks, conv_w, g_attn_out, g_conv_out, w_o, g_mlp_norm, w_up, w_down, g_ple_norm, w_ple_gate, w_ple):
    depth = w_in.shape[0]
    yp, ys = x_prompt, x_sample
    outs = [[] for _ in range(6)]
    for i in range(depth):
        sinks_p, weights = _prepare_weights(
            g_mix_norm[i], w_in[i], g_q[i], g_k[i], sinks[i], conv_w[i], g_attn_out[i], g_conv_out[i],
            w_o[i], g_mlp_norm[i], w_up[i], w_down[i], g_ple_norm[i], w_ple_gate[i], w_ple[i])
        ys, ksn, vsn, csn, w_bf16 = _sample_layer(ys, p_sample[i], cache_k[i], cache_v[i], state_conv[i],
                                                  sinks_p, weights)
        yp, kp, vp, cp = _prompt_layer(yp, p_prompt[i], sinks_p,
                                       [w_bf16.get(n, weights[n]) for n in WEIGHT_NAMES])
        for lst, val in zip(outs, (_cache_from_kernel(kp), _cache_from_kernel(vp), cp,
                                   _cache_from_kernel(ksn), _cache_from_kernel(vsn), csn)):
            lst.append(val)
    return (yp, ys) + tuple(jnp.stack(o) for o in outs)
```
